```python
import jax
import jax.numpy as jnp
from jax import lax
import numpy as np


D_MODEL = 2048
BATCH = 4
SEQ = 4096
DEPTH = 1

EPS = 1e-6
ROPE_THETA = 500000.0
HEAD_DIM = 128
ROPE_DIM = HEAD_DIM // 4

GDN_HEADS = 8
GDN_DK = HEAD_DIM
GDN_DV = HEAD_DIM
GDN_CONV = 4
GDN_CHUNK = 64

NSA_HEADS = 16
NSA_GROUPS = 2
NSA_HPG = NSA_HEADS // NSA_GROUPS
NSA_DK = HEAD_DIM
CMP_LEN = 32
CMP_STRIDE = 16
CMP_HIDDEN = 256
SLC_LEN = 64
SLC_TOPK = 16
WIN = 512
NSA_QBLOCK = 64

PEER_HEADS = 8
PEER_NKEYS = 128
PEER_EXPERTS = PEER_NKEYS * PEER_NKEYS
PEER_QDIM = 256
PEER_TOPK = 16
PEER_TOKBLOCK = 128

GDN_QK_W = GDN_HEADS * GDN_DK
GDN_V_W = GDN_HEADS * GDN_DV
NSA_Q_W = NSA_HEADS * NSA_DK
NSA_KV_W = NSA_GROUPS * NSA_DK
IN_WIDTHS = (GDN_QK_W, GDN_QK_W, GDN_V_W, GDN_V_W, GDN_HEADS, GDN_HEADS,
             NSA_Q_W, NSA_KV_W, NSA_KV_W, NSA_KV_W, NSA_KV_W, NSA_KV_W, NSA_KV_W, 3 * NSA_HEADS,
             D_MODEL, D_MODEL)
D_IN = 2 * GDN_QK_W + 2 * GDN_V_W + 2 * GDN_HEADS + NSA_Q_W + 6 * NSA_KV_W + 3 * NSA_HEADS + 2 * D_MODEL

NEG = -1e30
BIG = 1e9

kernel_name = 'hybrid_gdn_nsa_peer_block'


def _rmsnorm(x, w):
    xf = x.astype(jnp.float32)
    y = xf * lax.rsqrt(jnp.mean(xf * xf, axis=-1, keepdims=True) + EPS)
    return (y * w.astype(jnp.float32)).astype(x.dtype)


def _l2norm(x):
    return x * lax.rsqrt(jnp.sum(x * x, axis=-1, keepdims=True) + EPS)


def _partial_rope(x, positions):
    half = ROPE_DIM // 2
    inv_freq = ROPE_THETA ** (-jnp.arange(half, dtype=jnp.float32) / half)
    ang = positions.astype(jnp.float32)[:, :, None] * inv_freq
    cos = jnp.cos(ang)[:, :, None, :]
    sin = jnp.sin(ang)[:, :, None, :]
    x1 = x[..., :half].astype(jnp.float32)
    x2 = x[..., half:ROPE_DIM].astype(jnp.float32)
    rot = jnp.concatenate([x1 * cos - x2 * sin, x2 * cos + x1 * sin], axis=-1).astype(x.dtype)
    return jnp.concatenate([rot, x[..., ROPE_DIM:]], axis=-1)


def _causal_conv(x, w):
    C = x.shape[-1]
    return lax.conv_general_dilated(x, w[:, None, :].astype(x.dtype), window_strides=(1,),
                                    padding=[(GDN_CONV - 1, 0)],
                                    dimension_numbers=('NWC', 'WIO', 'NWC'),
                                    feature_group_count=C)


def _gated_deltanet(q, k, v, z, a, b, conv_w, A_log, dt_bias, norm_w):
    f32 = jnp.float32
    Bn, S, _ = q.shape
    H, dk, dv, C = GDN_HEADS, GDN_DK, GDN_DV, GDN_CHUNK
    N = S // C
    qkv = jax.nn.silu(_causal_conv(jnp.concatenate([q, k, v], axis=-1), conv_w))
    q, k, v = jnp.split(qkv, [GDN_QK_W, 2 * GDN_QK_W], axis=-1)
    q = _l2norm(q.reshape(Bn, S, H, dk).astype(f32)) * (dk ** -0.5)
    k = _l2norm(k.reshape(Bn, S, H, dk).astype(f32))
    v = v.reshape(Bn, S, H, dv).astype(f32)
    beta = jax.nn.sigmoid(b.astype(f32))
    g = -jnp.exp(A_log.astype(f32)) * jax.nn.softplus(a.astype(f32) + dt_bias.astype(f32))

    def chunks(t):
        t = t.reshape((Bn, N, C, H) + t.shape[3:])
        return jnp.swapaxes(jnp.swapaxes(t, 0, 1), 2, 3)

    qc, kc, vc = chunks(q), chunks(k), chunks(v)
    bc = chunks(beta)
    Gc = jnp.cumsum(chunks(g), axis=-1)
    ids = jnp.arange(C)
    incl = ids[:, None] >= ids[None, :]
    strict = ids[:, None] > ids[None, :]
    decay = jnp.where(incl, jnp.exp(jnp.where(incl, Gc[..., :, None] - Gc[..., None, :], 0.0)), 0.0)
    gamma = jnp.exp(Gc)
    L = jnp.where(strict, decay * jnp.einsum('nbhid,nbhjd->nbhij', kc, kc), 0.0) * bc[..., :, None]
    A = L + jnp.eye(C, dtype=f32)
    W = lax.linalg.triangular_solve(A, (bc * gamma)[..., None] * kc, left_side=True, lower=True, unit_diagonal=True)
    U0 = lax.linalg.triangular_solve(A, bc[..., None] * vc, left_side=True, lower=True, unit_diagonal=True)
    QK = decay * jnp.einsum('nbhid,nbhjd->nbhij', qc, kc)
    Qg = gamma[..., None] * qc
    Kd = jnp.exp(Gc[..., -1:] - Gc)[..., None] * kc
    gend = gamma[..., -1]

    def step(state, inp):
        W_, U0_, QK_, Qg_, Kd_, ge_ = inp
        U = U0_ - jnp.einsum('bhck,bhkv->bhcv', W_, state)
        O = jnp.einsum('bhck,bhkv->bhcv', Qg_, state) + jnp.einsum('bhij,bhjv->bhiv', QK_, U)
        state = ge_[..., None, None] * state + jnp.einsum('bhck,bhcv->bhkv', Kd_, U)
        return state, O

    s0 = jnp.zeros((Bn, H, dk, dv), f32)
    _, O = lax.scan(step, s0, (W, U0, QK, Qg, Kd, gend))
    o = jnp.swapaxes(jnp.swapaxes(O, 2, 3), 0, 1).reshape(Bn, S, H, dv)
    o = _rmsnorm(o, norm_w) * jax.nn.silu(z.reshape(Bn, S, H, dv).astype(f32))
    return o.reshape(Bn, S, H * dv).astype(z.dtype)


def _compress(t, pos, w1, w2):
    Bn, S = t.shape[:2]
    nc = (S - CMP_LEN) // CMP_STRIDE + 1
    idx = np.arange(nc)[:, None] * CMP_STRIDE + np.arange(CMP_LEN)[None, :]
    blocks = t[:, idx] + pos[None, None, :, None, :]
    flat = jnp.swapaxes(blocks, 2, 3).reshape(Bn, nc, NSA_GROUPS, CMP_LEN * NSA_DK)
    return jax.nn.gelu(flat @ w1) @ w2


def _nsa(q, k_c, v_c, k_s, v_s, k_w, v_w, gates, positions,
         cmp_pos_k, cmp_w1_k, cmp_w2_k, cmp_pos_v, cmp_w1_v, cmp_w2_v):
    f32 = jnp.float32
    Bn, S, _ = q.shape
    G, HPG, d = NSA_GROUPS, NSA_HPG, NSA_DK
    q = _partial_rope(q.reshape(Bn, S, NSA_HEADS, d), positions) * (d ** -0.5)
    q = q.reshape(Bn, S, G, HPG, d)
    k_c = _partial_rope(k_c.reshape(Bn, S, G, d), positions)
    k_s = _partial_rope(k_s.reshape(Bn, S, G, d), positions)
    k_w = _partial_rope(k_w.reshape(Bn, S, G, d), positions)
    v_c = v_c.reshape(Bn, S, G, d)
    v_s = v_s.reshape(Bn, S, G, d)
    v_w = v_w.reshape(Bn, S, G, d)
    gates = jax.nn.sigmoid(gates.astype(f32)).reshape(Bn, S, G, HPG, 3)

    nc = (S - CMP_LEN) // CMP_STRIDE + 1
    k_cmp = _compress(k_c, cmp_pos_k, cmp_w1_k, cmp_w2_k)
    v_cmp = _compress(v_c, cmp_pos_v, cmp_w1_v, cmp_w2_v)
    cmp_start = np.arange(nc) * CMP_STRIDE
    cmp_end = jnp.asarray(cmp_start + CMP_LEN - 1)
    nsb = S // SLC_LEN
    slc_start_np = np.arange(nsb) * SLC_LEN
    overlap = jnp.asarray(((cmp_start[:, None] < slc_start_np[None, :] + SLC_LEN)
                           & (cmp_start[:, None] + CMP_LEN > slc_start_np[None, :])).astype(np.float32))
    slc_start = jnp.asarray(slc_start_np)
    n_sel = min(SLC_TOPK, nsb)
    ks_blocks = jnp.transpose(k_s.reshape(Bn, nsb, SLC_LEN, G, d), (0, 3, 1, 2, 4))
    vs_blocks = jnp.transpose(v_s.reshape(Bn, nsb, SLC_LEN, G, d), (0, 3, 1, 2, 4))
    gather = jax.vmap(jax.vmap(lambda blk, idx: blk[idx]))
    k_w_pad = jnp.pad(k_w, ((0, 0), (WIN, 0), (0, 0), (0, 0)))
    v_w_pad = jnp.pad(v_w, ((0, 0), (WIN, 0), (0, 0), (0, 0)))
    blk_ids = jnp.arange(nsb)
    offs = jnp.arange(SLC_LEN)

    def q_block(i):
        s0 = i * NSA_QBLOCK
        qi = lax.dynamic_slice_in_dim(q, s0, NSA_QBLOCK, axis=1)
        gi = lax.dynamic_slice_in_dim(gates, s0, NSA_QBLOCK, axis=1)
        tq = s0 + jnp.arange(NSA_QBLOCK)
        sc = jnp.einsum('bqghd,bngd->bghqn', qi, k_cmp).astype(f32)
        valid_c = cmp_end[None, :] <= tq[:, None]
        p_c = jax.nn.softmax(jnp.where(valid_c, sc, NEG), axis=-1) * valid_c
        o_c = jnp.einsum('bghqn,bngd->bqghd', p_c.astype(v_cmp.dtype), v_cmp)
        imp = jnp.einsum('bghqn,ns->bgqs', p_c, overlap)
        cur = tq // SLC_LEN
        forced = (blk_ids[None, :] == 0) | (blk_ids[None, :] == cur[:, None]) | (blk_ids[None, :] == cur[:, None] - 1)
        imp = jnp.where(forced, BIG, imp)
        imp = jnp.where(slc_start[None, :] <= tq[:, None], imp, NEG)
        _, sel = lax.top_k(imp, n_sel)
        k_sel = gather(ks_blocks, sel)
        v_sel = gather(vs_blocks, sel)
        key_pos = sel[..., None] * SLC_LEN + offs
        mask_s = (key_pos <= tq[:, None, None])[:, :, None]
        ss = jnp.einsum('bqghd,bgqnld->bghqnl', qi, k_sel).astype(f32)
        ss = jnp.where(mask_s, ss, NEG)
        p_s = jax.nn.softmax(ss.reshape(ss.shape[:4] + (-1,)), axis=-1).reshape(ss.shape)
        o_s = jnp.einsum('bghqnl,bgqnld->bqghd', p_s.astype(v_sel.dtype), v_sel)
        kwi = lax.dynamic_slice_in_dim(k_w_pad, s0, NSA_QBLOCK + WIN, axis=1)
        vwi = lax.dynamic_slice_in_dim(v_w_pad, s0, NSA_QBLOCK + WIN, axis=1)
        kpos = s0 - WIN + jnp.arange(NSA_QBLOCK + WIN)
        mask_w = (kpos[None, :] <= tq[:, None]) & (kpos[None, :] > tq[:, None] - WIN) & (kpos[None, :] >= 0)
        sw = jnp.einsum('bqghd,bkgd->bghqk', qi, kwi).astype(f32)
        p_w = jax.nn.softmax(jnp.where(mask_w, sw, NEG), axis=-1)
        o_w = jnp.einsum('bghqk,bkgd->bqghd', p_w.astype(vwi.dtype), vwi)
        o = gi[..., 0:1] * o_c + gi[..., 1:2] * o_s + gi[..., 2:3] * o_w
        return o.astype(q.dtype)

    out = lax.map(q_block, jnp.arange(S // NSA_QBLOCK))
    return jnp.swapaxes(out, 0, 1).reshape(Bn, S, NSA_HEADS * d)


def _peer(h, wq, keys1, keys2, u, v):
    f32 = jnp.float32
    Bn, S, D = h.shape
    ht = h.reshape(Bn * S, D)
    half = PEER_QDIM // 2
    TB = PEER_TOKBLOCK

    def tok_block(i):
        xb = lax.dynamic_slice_in_dim(ht, i * TB, TB, axis=0)
        qh = (xb @ wq).reshape(TB, PEER_HEADS, PEER_QDIM).astype(f32)
        s1 = jnp.einsum('thd,hkd->thk', qh[..., :half], keys1.astype(f32))
        s2 = jnp.einsum('thd,hkd->thk', qh[..., half:], keys2.astype(f32))
        v1, i1 = lax.top_k(s1, PEER_TOPK)
        v2, i2 = lax.top_k(s2, PEER_TOPK)
        cand = (v1[..., :, None] + v2[..., None, :]).reshape(TB, PEER_HEADS, PEER_TOPK * PEER_TOPK)
        cidx = (i1[..., :, None] * PEER_NKEYS + i2[..., None, :]).reshape(TB, PEER_HEADS, PEER_TOPK * PEER_TOPK)
        sc, pos = lax.top_k(cand, PEER_TOPK)
        eidx = jnp.take_along_axis(cidx, pos, axis=-1)
        gw = jax.nn.softmax(sc, axis=-1)
        ue = u[eidx]
        ve = v[eidx]
        act = jax.nn.gelu(jnp.einsum('td,thkd->thk', xb, ue).astype(f32))
        return jnp.einsum('thk,thkd->td', (gw * act).astype(ve.dtype), ve)

    out = lax.map(tok_block, jnp.arange(Bn * S // TB))
    return out.reshape(Bn, S, D)


def setup_inputs(seed: int = 0) -> dict:
    key = jax.random.key(seed)
    ks = jax.random.split(key, 28)
    L, D = DEPTH, D_MODEL

    def nrm(k, shape, scale):
        return jax.random.normal(k, shape, jnp.float32) * scale

    dt = jnp.exp(jax.random.uniform(ks[9], (L, GDN_HEADS), jnp.float32, float(np.log(1e-3)), float(np.log(1e-1))))
    return {
        'x': nrm(ks[0], (BATCH, SEQ, D), 1.0),
        'c': nrm(ks[1], (BATCH, D), 1.0),
        'positions': jnp.broadcast_to(jnp.arange(SEQ, dtype=jnp.int32), (BATCH, SEQ)),
        'ada_w': nrm(ks[2], (L, D, 6 * D), D ** -0.5),
        'ada_b': nrm(ks[3], (L, 6 * D), 0.01),
        'norm1_w': 1.0 + nrm(ks[4], (L, D), 0.01),
        'norm2_w': 1.0 + nrm(ks[5], (L, D), 0.01),
        'w_in': nrm(ks[6], (L, D, D_IN), D ** -0.5),
        'gdn_conv_w': nrm(ks[7], (L, GDN_CONV, 2 * GDN_QK_W + GDN_V_W), GDN_CONV ** -0.5),
        'gdn_A_log': jnp.log(jax.random.uniform(ks[8], (L, GDN_HEADS), jnp.float32, 1.0, 16.0)),
        'gdn_dt_bias': dt + jnp.log(-jnp.expm1(-dt)),
        'gdn_norm_w': 1.0 + nrm(ks[10], (L, GDN_DV), 0.01),
        'cmp_pos_k': nrm(ks[11], (L, CMP_LEN, NSA_DK), 0.1),
        'cmp_w1_k': nrm(ks[12], (L, CMP_LEN * NSA_DK, CMP_HIDDEN), (CMP_LEN * NSA_DK) ** -0.5),
        'cmp_w2_k': nrm(ks[13], (L, CMP_HIDDEN, NSA_DK), CMP_HIDDEN ** -0.5),
        'cmp_pos_v': nrm(ks[14], (L, CMP_LEN, NSA_DK), 0.1),
        'cmp_w1_v': nrm(ks[15], (L, CMP_LEN * NSA_DK, CMP_HIDDEN), (CMP_LEN * NSA_DK) ** -0.5),
        'cmp_w2_v': nrm(ks[16], (L, CMP_HIDDEN, NSA_DK), CMP_HIDDEN ** -0.5),
        'w_branch_gdn': nrm(ks[17], (L, GDN_V_W, D), GDN_V_W ** -0.5),
        'w_branch_nsa': nrm(ks[18], (L, NSA_Q_W, D), NSA_Q_W ** -0.5),
        'w_out': nrm(ks[19], (L, D, D), D ** -0.5),
        'peer_wq': nrm(ks[20], (L, D, PEER_HEADS * PEER_QDIM), D ** -0.5),
        'peer_keys1': nrm(ks[21], (L, PEER_HEADS, PEER_NKEYS, PEER_QDIM // 2), (PEER_QDIM // 2) ** -0.5),
        'peer_keys2': nrm(ks[22], (L, PEER_HEADS, PEER_NKEYS, PEER_QDIM // 2), (PEER_QDIM // 2) ** -0.5),
        'peer_u': nrm(ks[23], (L, PEER_EXPERTS, D), D ** -0.5),
        'peer_v': nrm(ks[24], (L, PEER_EXPERTS, D), PEER_HEADS ** -0.5),
        'final_norm_w': 1.0 + nrm(ks[25], (D,), 0.01),
    }


def reference(x, c, positions, ada_w, ada_b, norm1_w, norm2_w, w_in, gdn_conv_w, gdn_A_log,
              gdn_dt_bias, gdn_norm_w, cmp_pos_k, cmp_w1_k, cmp_w2_k, cmp_pos_v, cmp_w1_v,
              cmp_w2_v, w_branch_gdn, w_branch_nsa, w_out, peer_wq, peer_keys1, peer_keys2,
              peer_u, peer_v, final_norm_w):
    offsets = np.cumsum(IN_WIDTHS)[:-1].tolist()
    cs = jax.nn.silu(c)
    for l in range(DEPTH):
        mod = cs @ ada_w[l] + ada_b[l]
        sh1, sc1, g1, sh2, sc2, g2 = [m[:, None, :] for m in jnp.split(mod, 6, axis=-1)]
        h = _rmsnorm(x, norm1_w[l]) * (1.0 + sc1) + sh1
        (a_q, a_k, a_v, a_z, a_a, a_b, b_q, b_kc, b_vc, b_ks, b_vs, b_kw, b_vw, b_g,
         m_a, m_b) = jnp.split(h @ w_in[l], offsets, axis=-1)
        o_a = _gated_deltanet(a_q, a_k, a_v, a_z, a_a, a_b, gdn_conv_w[l], gdn_A_log[l],
                              gdn_dt_bias[l], gdn_norm_w[l])
        o_b = _nsa(b_q, b_kc, b_vc, b_ks, b_vs, b_kw, b_vw, b_g, positions,
                   cmp_pos_k[l], cmp_w1_k[l], cmp_w2_k[l], cmp_pos_v[l], cmp_w1_v[l], cmp_w2_v[l])
        y = jax.nn.sigmoid(m_a) * (o_a @ w_branch_gdn[l]) + jax.nn.sigmoid(m_b) * (o_b @ w_branch_nsa[l])
        x = x + g1 * (y @ w_out[l])
        h2 = _rmsnorm(x, norm2_w[l]) * (1.0 + sc2) + sh2
        x = x + g2 * _peer(h2, peer_wq[l], peer_keys1[l], peer_keys2[l], peer_u[l], peer_v[l])
    return _rmsnorm(x, final_norm_w)
```

```python
import functools
import math

import numpy as np
import jax
import jax.numpy as jnp
from jax import lax
from jax.experimental import pallas as pl
from jax.experimental.pallas import tpu as pltpu

F32 = jnp.float32
BF16 = jnp.bfloat16
HI = lax.Precision.HIGHEST

LANES = 128
VMEM_LIMIT = 56 * 1024 * 1024

EPS = 1e-6
ROPE_THETA = 500000.0
HEAD_DIM = 128
ROPE_DIM = HEAD_DIM // 4
ROPE_HALF = ROPE_DIM // 2

GDN_HEADS = 8
GDN_CONV = 4
GDN_CHUNK = 64

NSA_HEADS = 16
NSA_GROUPS = 2
NSA_HPG = NSA_HEADS // NSA_GROUPS
CMP_LEN = 32
CMP_STRIDE = 16
CMP_HIDDEN = 256
SLC_LEN = 64
SLC_TOPK = 16
WIN = 512

PEER_HEADS = 8
PEER_NKEYS = 128
PEER_QDIM = 256
PEER_TOPK = 16

NEG = -1e30
BIG = 1e9

J_AQ, J_AK, J_AV, J_AZ = 0, 8, 16, 24
J_BQ = 32
N_NSA_BLOCKS = 28
J_MA, J_MB = 60, 76
J_SMALL = 92
J_GATE = 93
NJ = 96


def _cparams(sem):
    return pltpu.CompilerParams(dimension_semantics=sem, vmem_limit_bytes=VMEM_LIMIT)


def _bdot(a, b):
    return jnp.dot(a.astype(BF16), b.astype(BF16), preferred_element_type=F32)


def _bdot_nt(a, b):
    return lax.dot_general(a.astype(BF16), b.astype(BF16), (((1,), (1,)), ((), ())),
                           preferred_element_type=F32)


def _fdot(a, b):
    return jnp.dot(a, b, precision=HI, preferred_element_type=F32)


def _fdot_nt(a, b):
    return lax.dot_general(a, b, (((1,), (1,)), ((), ())), precision=HI,
                           preferred_element_type=F32)


def _sigmoid(x):
    return 1.0 / (1.0 + jnp.exp(-x))


def _silu(x):
    return x * _sigmoid(x)


def _gelu(x):
    return 0.5 * x * (1.0 + jnp.tanh(math.sqrt(2.0 / math.pi) * (x + 0.044715 * (x * x * x))))


def _softplus(x):
    return jnp.maximum(x, 0.0) + jnp.log(1.0 + jnp.exp(-jnp.abs(x)))


def _ada_kernel(c_ref, w_ref, b_ref, o_ref):
    o_ref[...] = _fdot(_silu(c_ref[...]), w_ref[...]) + b_ref[...]


def _ada(c_pad, ada_w, ada_b):
    m, d = c_pad.shape
    n = ada_w.shape[1]
    tn = 1024
    return pl.pallas_call(
        _ada_kernel,
        grid=(n // tn,),
        in_specs=[pl.BlockSpec((m, d), lambda j: (0, 0)),
                  pl.BlockSpec((d, tn), lambda j: (0, j)),
                  pl.BlockSpec((1, tn), lambda j: (0, j))],
        out_specs=pl.BlockSpec((m, tn), lambda j: (0, j)),
        out_shape=jax.ShapeDtypeStruct((m, n), F32),
        compiler_params=_cparams(("parallel",)),
        name="ada_mod",
    )(c_pad, ada_w, ada_b.reshape(1, n))


def _normproj_kernel(x_ref, nw_ref, sc_ref, sh_ref, w_ref, *rest, nk, emit_h):
    if emit_h:
        o_ref, h_out_ref, h_scr = rest
    else:
        o_ref, h_scr = rest

    @pl.when(pl.program_id(2) == 0)
    def _():
        x = x_ref[0]
        ms = jnp.mean(x * x, axis=-1, keepdims=True)
        y = x * lax.rsqrt(ms + EPS) * nw_ref[...]
        h = y * (1.0 + sc_ref[0]) + sh_ref[0]
        h_scr[...] = h.astype(BF16)
        if emit_h:
            h_out_ref[0] = h

    acc = jnp.dot(h_scr[...], w_ref[...], preferred_element_type=F32)
    for k in range(nk):
        o_ref[0, k] = acc[:, k * LANES:(k + 1) * LANES]


def _normproj(x, nw, sc, sh, w_bf16, *, tm, tn, emit_h):
    b, s, d = x.shape
    n = w_bf16.shape[1]
    tm = min(tm, s)
    nk = tn // LANES
    out_shape = [jax.ShapeDtypeStruct((b, n // LANES, s, LANES), F32)]
    out_specs = [pl.BlockSpec((1, nk, tm, LANES), lambda bi, i, j: (bi, j, i, 0))]
    if emit_h:
        out_shape.append(jax.ShapeDtypeStruct((b, s, d), F32))
        out_specs.append(pl.BlockSpec((1, tm, d), lambda bi, i, j: (bi, i, 0)))
    res = pl.pallas_call(
        functools.partial(_normproj_kernel, nk=nk, emit_h=emit_h),
        grid=(b, s // tm, n // tn),
        in_specs=[pl.BlockSpec((1, tm, d), lambda bi, i, j: (bi, i, 0)),
                  pl.BlockSpec((1, d), lambda bi, i, j: (0, 0)),
                  pl.BlockSpec((1, 1, d), lambda bi, i, j: (bi, 0, 0)),
                  pl.BlockSpec((1, 1, d), lambda bi, i, j: (bi, 0, 0)),
                  pl.BlockSpec((d, tn), lambda bi, i, j: (0, j))],
        out_specs=out_specs,
        out_shape=out_shape,
        scratch_shapes=[pltpu.VMEM((tm, d), BF16)],
        compiler_params=_cparams(("parallel", "parallel", "arbitrary")),
        name="normproj_h" if emit_h else "normproj",
    )(x, nw.reshape(1, d), sc.reshape(b, 1, d), sh.reshape(b, 1, d), w_bf16)
    return res if emit_h else res[0]


GDN_CB = 256


def _gdn_local_kernel(q_ref, qh_ref, k_ref, kh_ref, v_ref, vh_ref, sm_ref,
                      cwq_ref, cwk_ref, cwv_ref, alog_ref, dtb_ref,
                      w_ref, u0_ref, qk_ref, qg_ref, kd_ref, ge_ref):
    h = pl.program_id(1)
    i = pl.program_id(2)
    C = GDN_CHUNK
    first = (i == 0)

    def conv_act(main_ref, halo_ref, cw_ref, c):
        if c == 0:
            prev = jnp.where(first, 0.0, halo_ref[0, 0])
        else:
            prev = main_ref[0, 0, c * C - 8:c * C, :]
        ext = jnp.concatenate([prev, main_ref[0, 0, c * C:(c + 1) * C, :]], axis=0)
        w = cw_ref[0]
        y = w[0:1, :] * ext[5:5 + C, :]
        for j in range(1, GDN_CONV):
            y = y + w[j:j + 1, :] * ext[5 + j:5 + j + C, :]
        return _silu(y)

    rid = lax.broadcasted_iota(jnp.int32, (C, C), 0)
    cid = lax.broadcasted_iota(jnp.int32, (C, C), 1)
    incl = rid >= cid
    strict = rid > cid
    eye = rid == cid
    tri = jnp.where(incl, 1.0, 0.0).astype(F32)
    eye_f = jnp.where(eye, 1.0, 0.0).astype(F32)
    lane = lax.broadcasted_iota(jnp.int32, (C, LANES), 1)
    neg_a = -jnp.exp(alog_ref[...])
    dtb = dtb_ref[...]

    for c in range(GDN_CB // C):
        q = conv_act(q_ref, qh_ref, cwq_ref, c)
        k = conv_act(k_ref, kh_ref, cwk_ref, c)
        v = conv_act(v_ref, vh_ref, cwv_ref, c)
        q = q * lax.rsqrt(jnp.sum(q * q, axis=-1, keepdims=True) + EPS) * (HEAD_DIM ** -0.5)
        k = k * lax.rsqrt(jnp.sum(k * k, axis=-1, keepdims=True) + EPS)

        sm = sm_ref[0, 0, c * C:(c + 1) * C, :]
        g_all = neg_a * _softplus(sm + dtb)
        gc_all = _fdot(tri, g_all)
        gc = jnp.sum(jnp.where(lane == h, gc_all, 0.0), axis=1, keepdims=True)
        beta = jnp.sum(jnp.where(lane == GDN_HEADS + h, _sigmoid(sm), 0.0), axis=1, keepdims=True)
        gc_row = jnp.sum(jnp.where(eye, gc, 0.0), axis=0, keepdims=True)
        gc_last = gc[C - 1:C, :]
        decay = jnp.where(incl, jnp.exp(jnp.where(incl, gc - gc_row, 0.0)), 0.0)
        gamma = jnp.exp(gc)

        kk = _fdot_nt(k, k)
        lmat = jnp.where(strict, decay * kk, 0.0) * beta
        tinv = eye_f - lmat
        pw = lmat
        for _ in range(int(math.log2(C)) - 1):
            pw = _fdot(pw, pw)
            tinv = _fdot(tinv, eye_f + pw)
        rhs = jnp.concatenate([(beta * gamma) * k, beta * v], axis=1)
        wu = _fdot(tinv, rhs)
        qk = decay * _fdot_nt(q, k)

        rows = slice(c * C, (c + 1) * C)
        w_ref[0, 0, rows, :] = wu[:, :HEAD_DIM].astype(BF16)
        u0_ref[0, 0, rows, :] = wu[:, HEAD_DIM:]
        qk_ref[0, 0, rows, :] = qk.astype(BF16)
        qg_ref[0, 0, rows, :] = (gamma * q).astype(BF16)
        kd_ref[0, 0, rows, :] = (jnp.exp(gc_last - gc) * k).astype(BF16)
        ge_ref[0, 0, pl.ds(i * (GDN_CB // C) + c, 1), :] = jnp.broadcast_to(jnp.exp(gc_last), (1, LANES))


def _gdn_local(P, cw, alog_pad, dtb_pad):
    b, _, s, _ = P.shape
    H, CB, C = GDN_HEADS, GDN_CB, GDN_CHUNK
    n = s // C

    def main(j0):
        return pl.BlockSpec((1, 1, CB, LANES), lambda bi, h, i: (bi, j0 + h, i, 0))

    def halo(j0):
        return pl.BlockSpec((1, 1, 8, LANES),
                            lambda bi, h, i: (bi, j0 + h, jnp.maximum(i * (CB // 8) - 1, 0), 0))

    def cws(j0):
        return pl.BlockSpec((1, GDN_CONV, LANES), lambda bi, h, i: (j0 + h, 0, 0))

    row = pl.BlockSpec((1, LANES), lambda bi, h, i: (0, 0))
    hs = lambda width: pl.BlockSpec((1, 1, CB, width), lambda bi, h, i: (bi, h, i, 0))
    return pl.pallas_call(
        _gdn_local_kernel,
        grid=(b, H, s // CB),
        in_specs=[main(J_AQ), halo(J_AQ), main(J_AK), halo(J_AK), main(J_AV), halo(J_AV),
                  pl.BlockSpec((1, 1, CB, LANES), lambda bi, h, i: (bi, J_SMALL, i, 0)),
                  cws(0), cws(8), cws(16), row, row],
        out_specs=[hs(LANES), hs(LANES), hs(C), hs(LANES), hs(LANES),
                   pl.BlockSpec((1, 1, n, LANES), lambda bi, h, i: (bi, h, 0, 0))],
        out_shape=[jax.ShapeDtypeStruct((b, H, s, LANES), BF16),
                   jax.ShapeDtypeStruct((b, H, s, LANES), F32),
                   jax.ShapeDtypeStruct((b, H, s, C), BF16),
                   jax.ShapeDtypeStruct((b, H, s, LANES), BF16),
                   jax.ShapeDtypeStruct((b, H, s, LANES), BF16),
                   jax.ShapeDtypeStruct((b, H, n, LANES), F32)],
        compiler_params=_cparams(("parallel", "parallel", "arbitrary")),
        name="gdn_local",
    )(P, P, P, P, P, P, P, cw, cw, cw, alog_pad, dtb_pad)


GDN_HB = 4


def _gdn_scan_kernel(w_ref, u0_ref, qk_ref, qg_ref, kd_ref, ge_ref, z_ref, nw_ref, o_ref, s_scr,
                     *, sb):
    C = GDN_CHUNK

    @pl.when(pl.program_id(2) == 0)
    def _():
        s_scr[...] = jnp.zeros_like(s_scr)

    nw = nw_ref[...]

    def body(n, carry):
        r = pl.multiple_of(n * C, C)
        for hh in range(GDN_HB):
            st = s_scr[hh]
            stb = st.astype(BF16)
            u = u0_ref[0, hh, pl.ds(r, C), :] - jnp.dot(w_ref[0, hh, pl.ds(r, C), :], stb,
                                                       preferred_element_type=F32)
            ub = u.astype(BF16)
            o = (jnp.dot(qg_ref[0, hh, pl.ds(r, C), :], stb, preferred_element_type=F32)
                 + jnp.dot(qk_ref[0, hh, pl.ds(r, C), :], ub, preferred_element_type=F32))
            ge = ge_ref[0, hh, pl.ds(n, 1), :]
            s_scr[hh] = ge * st + lax.dot_general(kd_ref[0, hh, pl.ds(r, C), :], ub,
                                                  (((0,), (0,)), ((), ())),
                                                  preferred_element_type=F32)
            on = o * lax.rsqrt(jnp.mean(o * o, axis=-1, keepdims=True) + EPS) * nw
            z = z_ref[0, hh, pl.ds(r, C), :]
            o_ref[0, pl.ds(r, C), hh * LANES:(hh + 1) * LANES] = (on * _silu(z)).astype(BF16)
        return carry

    lax.fori_loop(0, sb // C, body, 0)


def _gdn_scan(w, u0, qk, qg, kd, ge, P, norm_w):
    b, H, s, _ = w.shape
    C, HB = GDN_CHUNK, GDN_HB
    sb = min(1024, s)
    hs = lambda width: pl.BlockSpec((1, HB, sb, width), lambda bi, hb, i: (bi, hb, i, 0))
    return pl.pallas_call(
        functools.partial(_gdn_scan_kernel, sb=sb),
        grid=(b, H // HB, s // sb),
        in_specs=[hs(LANES), hs(LANES), hs(C), hs(LANES), hs(LANES),
                  pl.BlockSpec((1, HB, sb // C, LANES), lambda bi, hb, i: (bi, hb, i, 0)),
                  pl.BlockSpec((1, HB, sb, LANES), lambda bi, hb, i: (bi, J_AZ // HB + hb, i, 0)),
                  pl.BlockSpec((1, LANES), lambda bi, hb, i: (0, 0))],
        out_specs=pl.BlockSpec((1, sb, HB * LANES), lambda bi, hb, i: (bi, i, hb)),
        out_shape=jax.ShapeDtypeStruct((b, s, H * LANES), BF16),
        scratch_shapes=[pltpu.VMEM((HB, HEAD_DIM, HEAD_DIM), F32)],
        compiler_params=_cparams(("parallel", "parallel", "arbitrary")),
        name="gdn_scan",
    )(w, u0, qk, qg, kd, ge, P, norm_w.reshape(1, LANES))


def _pad_lanes_row(v):
    return jnp.pad(v.astype(F32), (0, LANES - v.shape[0])).reshape(1, LANES)


def _gdn(P, conv_w, a_log, dt_bias, norm_w):
    cw = jnp.transpose(conv_w.reshape(GDN_CONV, 3 * GDN_HEADS, LANES), (1, 0, 2))
    w, u0, qk, qg, kd, ge = _gdn_local(P, cw, _pad_lanes_row(a_log), _pad_lanes_row(dt_bias))
    return _gdn_scan(w, u0, qk, qg, kd, ge, P, norm_w)


NSA_TQ = 128
R_Q, R_KC, R_VC, R_KS, R_VS, R_KW, R_VW = 0, 16, 18, 20, 22, 24, 26


def _rope_kernel(x_ref, pos_ref, invf_ref, o_ref, cos_scr, sin_scr):
    j = pl.program_id(2)
    lane = lax.broadcasted_iota(jnp.int32, cos_scr.shape, 1)

    @pl.when(j == 0)
    def _():
        ang = pos_ref[0] * invf_ref[...]
        sn = jnp.sin(ang)
        cos_scr[...] = jnp.where(lane < ROPE_DIM, jnp.cos(ang), 1.0)
        sin_scr[...] = jnp.where(lane < ROPE_HALF, -sn, jnp.where(lane < ROPE_DIM, sn, 0.0))

    x = x_ref[0, 0]
    swapped = jnp.where(lane < ROPE_HALF, pltpu.roll(x, LANES - ROPE_HALF, axis=1),
                        pltpu.roll(x, ROPE_HALF, axis=1))
    rot = x * cos_scr[...] + swapped * sin_scr[...]
    is_q = j < NSA_HEADS
    is_rope = jnp.logical_or(is_q, ((j - NSA_HEADS) % 4) < 2)
    out = jnp.where(is_rope, rot, x) * jnp.where(is_q, HEAD_DIM ** -0.5, 1.0)
    o_ref[0, 0] = out.astype(BF16)


def _rope(P, pos_f32, invf):
    b, _, s, _ = P.shape
    tr = min(512, s)
    return pl.pallas_call(
        _rope_kernel,
        grid=(b, s // tr, N_NSA_BLOCKS),
        in_specs=[pl.BlockSpec((1, 1, tr, LANES), lambda bi, i, j: (bi, J_BQ + j, i, 0)),
                  pl.BlockSpec((1, tr, 1), lambda bi, i, j: (bi, i, 0)),
                  pl.BlockSpec((1, LANES), lambda bi, i, j: (0, 0))],
        out_specs=pl.BlockSpec((1, 1, tr, LANES), lambda bi, i, j: (bi, j, i, 0)),
        out_shape=jax.ShapeDtypeStruct((b, N_NSA_BLOCKS, s, LANES), BF16),
        scratch_shapes=[pltpu.VMEM((tr, LANES), F32), pltpu.VMEM((tr, LANES), F32)],
        compiler_params=_cparams(("parallel", "parallel", "arbitrary")),
        name="nsa_rope",
    )(P, pos_f32, invf)


def _compress_kernel(r_ref, w1_ref, w2_ref, pos_ref, o_ref):
    r = r_ref[0, 0]
    nr = r.shape[0]
    half = CMP_STRIDE * HEAD_DIM
    a = jnp.dot(r, w1_ref[0, :half, :], preferred_element_type=F32)
    bm = jnp.dot(r, w1_ref[0, half:, :], preferred_element_type=F32)
    pos8 = jnp.broadcast_to(pos_ref[0], (8, CMP_LEN * HEAD_DIM)).astype(BF16)
    pb = jnp.dot(pos8, w1_ref[0], preferred_element_type=F32)[0:1, :]
    hid = a + pltpu.roll(bm, nr - 1, axis=0) + pb
    out = jnp.dot(_gelu(hid).astype(BF16), w2_ref[0], preferred_element_type=F32)
    row = lax.broadcasted_iota(jnp.int32, out.shape, 0)
    o_ref[0, 0] = jnp.where(row < nr - 1, out, 0.0).astype(BF16)


def _compress(rows, w1, w2, pos):
    b, _, nr, width = rows.shape
    return pl.pallas_call(
        _compress_kernel,
        grid=(b, 4),
        in_specs=[pl.BlockSpec((1, 1, nr, width), lambda bi, j: (bi, j, 0, 0)),
                  pl.BlockSpec((1, CMP_LEN * HEAD_DIM, CMP_HIDDEN), lambda bi, j: (j // 2, 0, 0)),
                  pl.BlockSpec((1, CMP_HIDDEN, HEAD_DIM), lambda bi, j: (j // 2, 0, 0)),
                  pl.BlockSpec((1, 1, CMP_LEN * HEAD_DIM), lambda bi, j: (j // 2, 0, 0))],
        out_specs=pl.BlockSpec((1, 1, nr, HEAD_DIM), lambda bi, j: (bi, j, 0, 0)),
        out_shape=jax.ShapeDtypeStruct((b, 4, nr, HEAD_DIM), BF16),
        compiler_params=_cparams(("parallel", "arbitrary")),
        name="nsa_compress",
    )(rows, w1, w2, pos)


def _nsa_kernel(q_ref, kc_ref, vc_ref, ks_ref, vs_ref, kw_ref, vw_ref, gate_ref, ovt_ref, e_ref,
                o_ref, m_scr, l_scr, acc_scr, os_scr, *, s_len):
    i = pl.program_id(2)
    TQ, HP = NSA_TQ, NSA_HPG
    TK = TQ
    R = HP * TQ
    nsb = s_len // SLC_LEN
    ncp = s_len // CMP_STRIDE
    n_sel = min(SLC_TOPK, nsb)
    s0 = i * TQ
    q2 = q_ref[0].reshape(R, HEAD_DIM)

    sc = _bdot_nt(q2, kc_ref[0, 0])
    tq_r = s0 + (lax.broadcasted_iota(jnp.int32, (R, ncp), 0) & (TQ - 1))
    ncol = lax.broadcasted_iota(jnp.int32, (R, ncp), 1)
    valid = jnp.logical_and(ncol * CMP_STRIDE + (CMP_LEN - 1) <= tq_r, ncol < ncp - 1)
    scm = jnp.where(valid, sc, NEG)
    e = jnp.where(valid, jnp.exp(scm - jnp.max(scm, axis=1, keepdims=True)), 0.0)
    p_c = e / jnp.maximum(jnp.sum(e, axis=1, keepdims=True), 1e-30)
    o_c = _bdot(p_c, vc_ref[0, 0])
    psum = jnp.sum(p_c.reshape(HP, TQ, ncp), axis=0)
    imp = _fdot_nt(ovt_ref[...], psum)
    blk = lax.broadcasted_iota(jnp.int32, (nsb, TQ), 0)
    tq_l = s0 + lax.broadcasted_iota(jnp.int32, (nsb, TQ), 1)
    cur = tq_l // SLC_LEN
    forced = jnp.logical_or(blk == 0, jnp.logical_or(blk == cur, blk == cur - 1))
    imp = jnp.where(forced, BIG, imp)
    imp = jnp.where(blk * SLC_LEN <= tq_l, imp, NEG)
    rank = jnp.zeros((nsb, TQ), jnp.int32)
    for j in range(nsb):
        rj = imp[j:j + 1, :]
        beats = jnp.logical_or(rj > imp, jnp.logical_and(rj == imp, blk > j))
        rank = rank + jnp.where(beats, 1, 0)
    sel_t = jnp.where(rank < n_sel, 1.0, 0.0).astype(F32)
    if nsb < LANES:
        sel_t = jnp.concatenate([sel_t, jnp.zeros((LANES - nsb, TQ), F32)], axis=0)
    sel = jnp.transpose(sel_t).astype(BF16)

    qpos = s0 + lax.broadcasted_iota(jnp.int32, (TQ, TK), 0)
    kofs = lax.broadcasted_iota(jnp.int32, (TQ, TK), 1)

    def flash_init():
        m_scr[...] = jnp.full(m_scr.shape, NEG, F32)
        l_scr[...] = jnp.zeros(l_scr.shape, F32)
        acc_scr[...] = jnp.zeros(acc_scr.shape, F32)

    def flash_step(k_ref, v_ref, j, bias):
        r = pl.multiple_of(j * TK, TK)
        s = _bdot_nt(q2, k_ref[0, 0, pl.ds(r, TK), :])
        s = (s.reshape(HP, TQ, TK) + bias[None]).reshape(R, TK)
        m_prev = m_scr[...]
        m_new = jnp.maximum(m_prev, jnp.max(s, axis=1, keepdims=True))
        alpha = jnp.exp(m_prev - m_new)
        p = jnp.exp(s - m_new)
        l_scr[...] = alpha * l_scr[...] + jnp.sum(p, axis=1, keepdims=True)
        acc_scr[...] = alpha * acc_scr[...] + _bdot(p, v_ref[0, 0, pl.ds(r, TK), :])
        m_scr[...] = m_new

    flash_init()

    def sel_body(j, carry):
        kpos = j * TK + kofs
        r = pl.multiple_of(j * TK, TK)
        chosen = jnp.dot(sel, e_ref[:, pl.ds(r, TK)], preferred_element_type=F32)
        ok = jnp.logical_and(chosen > 0.5, kpos <= qpos)
        flash_step(ks_ref, vs_ref, j, jnp.where(ok, 0.0, NEG))
        return carry

    lax.fori_loop(0, i + 1, sel_body, 0)
    os_scr[...] = acc_scr[...] / l_scr[...]

    flash_init()

    def win_body(j, carry):
        kpos = j * TK + kofs
        ok = jnp.logical_and(kpos <= qpos, kpos > qpos - WIN)
        flash_step(kw_ref, vw_ref, j, jnp.where(ok, 0.0, NEG))
        return carry

    lax.fori_loop(jnp.maximum(i - WIN // TK, 0), i + 1, win_body, 0)

    gates = _sigmoid(gate_ref[0, 0])
    for hh in range(HP):
        rows = slice(hh * TQ, (hh + 1) * TQ)
        o_w = acc_scr[rows, :] / l_scr[rows, :]
        out = (gates[:, 3 * hh:3 * hh + 1] * o_c[rows, :]
               + gates[:, 3 * hh + 1:3 * hh + 2] * os_scr[rows, :]
               + gates[:, 3 * hh + 2:3 * hh + 3] * o_w)
        o_ref[0, :, hh * HEAD_DIM:(hh + 1) * HEAD_DIM] = out.astype(BF16)


def _nsa_attend(rp, cmp_kv, P, ovt, expand):
    b, _, s, _ = rp.shape
    TQ, HP, G = NSA_TQ, NSA_HPG, NSA_GROUPS
    ncp = cmp_kv.shape[2]
    R = HP * TQ
    full = lambda j0: pl.BlockSpec((1, 1, s, HEAD_DIM), lambda bi, g, i: (bi, j0 + g, 0, 0))
    cmp_spec = lambda j0: pl.BlockSpec((1, 1, ncp, HEAD_DIM), lambda bi, g, i: (bi, j0 + g, 0, 0))
    return pl.pallas_call(
        functools.partial(_nsa_kernel, s_len=s),
        grid=(b, G, s // TQ),
        in_specs=[pl.BlockSpec((1, HP, TQ, HEAD_DIM), lambda bi, g, i: (bi, g, i, 0)),
                  cmp_spec(0), cmp_spec(2),
                  full(R_KS), full(R_VS), full(R_KW), full(R_VW),
                  pl.BlockSpec((1, 1, TQ, LANES), lambda bi, g, i: (bi, J_GATE + g, i, 0)),
                  pl.BlockSpec(ovt.shape, lambda bi, g, i: (0, 0)),
                  pl.BlockSpec(expand.shape, lambda bi, g, i: (0, 0))],
        out_specs=pl.BlockSpec((1, TQ, HP * HEAD_DIM), lambda bi, g, i: (bi, i, g)),
        out_shape=jax.ShapeDtypeStruct((b, s, NSA_HEADS * HEAD_DIM), BF16),
        scratch_shapes=[pltpu.VMEM((R, LANES), F32), pltpu.VMEM((R, LANES), F32),
                        pltpu.VMEM((R, HEAD_DIM), F32), pltpu.VMEM((R, HEAD_DIM), F32)],
        compiler_params=_cparams(("parallel", "parallel", "arbitrary")),
        name="nsa_attend",
    )(rp, cmp_kv, cmp_kv, rp, rp, rp, rp, P, ovt, expand)


def _nsa(P, positions, cmp_pos_k, cmp_w1_k, cmp_w2_k, cmp_pos_v, cmp_w1_v, cmp_w2_v):
    b, _, s, _ = P.shape
    lanes = np.arange(LANES)
    invf = np.where(lanes < ROPE_DIM, ROPE_THETA ** (-(lanes % ROPE_HALF) / ROPE_HALF), 0.0)
    rp = _rope(P, positions.astype(F32).reshape(b, s, 1), jnp.asarray(invf, F32).reshape(1, LANES))

    ncp = s // CMP_STRIDE
    rows = rp[:, R_KC:R_KC + 4].reshape(b, 4, ncp, CMP_STRIDE * HEAD_DIM)
    w1 = jnp.stack([cmp_w1_k, cmp_w1_v]).astype(BF16)
    w2 = jnp.stack([cmp_w2_k, cmp_w2_v]).astype(BF16)
    pos = jnp.stack([cmp_pos_k, cmp_pos_v]).reshape(2, 1, CMP_LEN * HEAD_DIM)
    cmp_kv = _compress(rows, w1, w2, pos)

    nsb = s // SLC_LEN
    cmp_start = np.arange(ncp) * CMP_STRIDE
    slc_start = np.arange(nsb) * SLC_LEN
    ov = ((cmp_start[:, None] < slc_start[None, :] + SLC_LEN)
          & (cmp_start[:, None] + CMP_LEN > slc_start[None, :])
          & (np.arange(ncp)[:, None] < ncp - 1))
    ovt = jnp.asarray(ov.T.astype(np.float32))
    expand = (np.arange(LANES)[:, None] == (np.arange(s)[None, :] // SLC_LEN))
    expand = jnp.asarray(expand.astype(np.float32), BF16)
    return _nsa_attend(rp, cmp_kv, P, ovt, expand)


MERGE_TN = 512


def _merge_kernel(oa_ref, ob_ref, wg_ref, wn_ref, ma_ref, mb_ref, y_ref):
    ya = jnp.dot(oa_ref[0], wg_ref[...], preferred_element_type=F32)
    yb = jnp.dot(ob_ref[0], wn_ref[...], preferred_element_type=F32)
    for k in range(MERGE_TN // LANES):
        cols = slice(k * LANES, (k + 1) * LANES)
        y = _sigmoid(ma_ref[0, k]) * ya[:, cols] + _sigmoid(mb_ref[0, k]) * yb[:, cols]
        y_ref[0, :, cols] = y.astype(BF16)


def _merge(o_a, o_b, wg, wn, P):
    b, s, da = o_a.shape
    db = o_b.shape[2]
    d = wg.shape[1]
    tm, tn = min(512, s), MERGE_TN
    nk = tn // LANES
    return pl.pallas_call(
        _merge_kernel,
        grid=(b, s // tm, d // tn),
        in_specs=[pl.BlockSpec((1, tm, da), lambda bi, i, j: (bi, i, 0)),
                  pl.BlockSpec((1, tm, db), lambda bi, i, j: (bi, i, 0)),
                  pl.BlockSpec((da, tn), lambda bi, i, j: (0, j)),
                  pl.BlockSpec((db, tn), lambda bi, i, j: (0, j)),
                  pl.BlockSpec((1, nk, tm, LANES), lambda bi, i, j: (bi, J_MA // nk + j, i, 0)),
                  pl.BlockSpec((1, nk, tm, LANES), lambda bi, i, j: (bi, J_MB // nk + j, i, 0))],
        out_specs=pl.BlockSpec((1, tm, tn), lambda bi, i, j: (bi, i, j)),
        out_shape=jax.ShapeDtypeStruct((b, s, d), BF16),
        compiler_params=_cparams(("parallel", "parallel", "arbitrary")),
        name="merge",
    )(o_a, o_b, wg, wn, P, P)


def _outproj_kernel(y_ref, w_ref, x_ref, g_ref, o_ref):
    o_ref[0] = x_ref[0] + g_ref[0] * jnp.dot(y_ref[0], w_ref[...], preferred_element_type=F32)


def _outproj(y, w, x, g):
    b, s, d = x.shape
    tm, tn = min(512, s), 512
    return pl.pallas_call(
        _outproj_kernel,
        grid=(b, s // tm, d // tn),
        in_specs=[pl.BlockSpec((1, tm, d), lambda bi, i, j: (bi, i, 0)),
                  pl.BlockSpec((d, tn), lambda bi, i, j: (0, j)),
                  pl.BlockSpec((1, tm, tn), lambda bi, i, j: (bi, i, j)),
                  pl.BlockSpec((1, 1, tn), lambda bi, i, j: (bi, 0, j))],
        out_specs=pl.BlockSpec((1, tm, tn), lambda bi, i, j: (bi, i, j)),
        out_shape=jax.ShapeDtypeStruct((b, s, d), F32),
        compiler_params=_cparams(("parallel", "parallel", "arbitrary")),
        name="outproj",
    )(y, w, x, g.reshape(b, 1, d))


PEER_TB = 256
PEER_TG = 8
PEER_SEL = PEER_HEADS * PEER_TOPK
MIN_F32 = -3.0e38


def _topk_rows(vals, payload=None):
    nrow = vals.shape[0]
    rid = lax.broadcasted_iota(jnp.int32, vals.shape, 0)
    out_v, out_i = [], []
    for _ in range(PEER_TOPK):
        m = jnp.max(vals, axis=0, keepdims=True)
        idx = jnp.min(jnp.where(vals == m, rid, nrow), axis=0, keepdims=True)
        hit = rid == idx
        out_v.append(m)
        if payload is None:
            out_i.append(idx)
        else:
            out_i.append(jnp.sum(jnp.where(hit, payload, 0), axis=0, keepdims=True))
        vals = jnp.where(hit, MIN_F32, vals)
    return jnp.concatenate(out_v, axis=0), jnp.concatenate(out_i, axis=0)


def _peer_topk_kernel(qh_ref, k1_ref, k2_ref, eidx_ref, gw_ref):
    def head(h, carry):
        s1 = _fdot_nt(k1_ref[h], qh_ref[0, 2 * h])
        s2 = _fdot_nt(k2_ref[h], qh_ref[0, 2 * h + 1])
        v1, i1 = _topk_rows(s1)
        v2, i2 = _topk_rows(s2)
        cand = jnp.concatenate([v1[a:a + 1, :] + v2 for a in range(PEER_TOPK)], axis=0)
        cidx = jnp.concatenate([i1[a:a + 1, :] * PEER_NKEYS + i2 for a in range(PEER_TOPK)], axis=0)
        sc, eidx = _topk_rows(cand, cidx)
        ex = jnp.exp(sc - sc[0:1, :])
        eidx_ref[h] = eidx
        gw_ref[h] = ex / jnp.sum(ex, axis=0, keepdims=True)
        return carry

    lax.fori_loop(0, PEER_HEADS, head, 0)


def _peer_topk(qh, keys1, keys2):
    b, nb, s, _ = qh.shape
    tb = min(PEER_TB, s)
    nt = s // tb
    kspec = pl.BlockSpec(keys1.shape, lambda bi, i: (0, 0, 0))
    ospec = pl.BlockSpec((PEER_HEADS, PEER_TOPK, tb), lambda bi, i: (0, 0, bi * nt + i))
    return pl.pallas_call(
        _peer_topk_kernel,
        grid=(b, nt),
        in_specs=[pl.BlockSpec((1, nb, tb, LANES), lambda bi, i: (bi, 0, i, 0)), kspec, kspec],
        out_specs=[ospec, ospec],
        out_shape=[jax.ShapeDtypeStruct((PEER_HEADS, PEER_TOPK, b * s), jnp.int32),
                   jax.ShapeDtypeStruct((PEER_HEADS, PEER_TOPK, b * s), F32)],
        compiler_params=_cparams(("parallel", "parallel")),
        name="peer_topk",
    )(qh, keys1, keys2)


def _peer_gather_kernel(idx_ref, gw_ref, x_ref, u_hbm, v_hbm, o_ref, ubuf, vbuf, a_scr, c_scr, sem):
    NS = PEER_SEL

    def issue(t, slot):
        def body(j, carry):
            e = idx_ref[t, j]
            pltpu.make_async_copy(u_hbm.at[e], ubuf.at[slot, j], sem.at[0, slot]).start()
            pltpu.make_async_copy(v_hbm.at[e], vbuf.at[slot, j], sem.at[1, slot]).start()
            return carry
        lax.fori_loop(0, NS, body, 0)

    def wait(slot):
        pltpu.make_async_copy(u_hbm.at[pl.ds(0, NS)], ubuf.at[slot], sem.at[0, slot]).wait()
        pltpu.make_async_copy(v_hbm.at[pl.ds(0, NS)], vbuf.at[slot], sem.at[1, slot]).wait()

    rid = lax.broadcasted_iota(jnp.int32, (NS, NS), 0)
    cid = lax.broadcasted_iota(jnp.int32, (NS, NS), 1)
    eye = rid == cid
    issue(0, 0)

    def token(t, carry):
        slot = t % 2

        @pl.when(t + 1 < PEER_TG)
        def _():
            issue(t + 1, 1 - slot)

        wait(slot)
        x = x_ref[t]
        xl, xh = x[0:8, :], x[8:16, :]

        def dot_body(j, carry):
            uj = ubuf[slot, j]
            part = uj[0:8, :] * xl + uj[8:16, :] * xh
            a_scr[pl.ds(j, 1), :] = jnp.sum(part, axis=0, keepdims=True)
            return carry
        lax.fori_loop(0, NS, dot_body, 0, unroll=8)

        act = jnp.sum(a_scr[...], axis=1, keepdims=True)
        gw_col = jnp.sum(jnp.where(eye, gw_ref[pl.ds(t, 1), :], 0.0), axis=1, keepdims=True)
        c_scr[...] = jnp.broadcast_to(gw_col * _gelu(act), (NS, LANES))

        def sum_body(j, acc):
            return acc + c_scr[pl.ds(j, 1), :] * vbuf[slot, j]
        o_ref[t] = lax.fori_loop(0, NS, sum_body, jnp.zeros((16, LANES), F32), unroll=8)
        return carry

    lax.fori_loop(0, PEER_TG, token, 0)


def _peer_gather(eidx, gw, h2r, u3, v3):
    t = eidx.shape[0]
    tg = PEER_TG
    sub = u3.shape[1]
    return pl.pallas_call(
        _peer_gather_kernel,
        grid=(t // tg,),
        in_specs=[pl.BlockSpec((tg, PEER_SEL), lambda i: (i, 0), memory_space=pltpu.SMEM),
                  pl.BlockSpec((tg, PEER_SEL), lambda i: (i, 0)),
                  pl.BlockSpec((tg, sub, LANES), lambda i: (i, 0, 0)),
                  pl.BlockSpec(memory_space=pl.ANY),
                  pl.BlockSpec(memory_space=pl.ANY)],
        out_specs=pl.BlockSpec((tg, sub, LANES), lambda i: (i, 0, 0)),
        out_shape=jax.ShapeDtypeStruct((t, sub, LANES), F32),
        scratch_shapes=[pltpu.VMEM((2, PEER_SEL, sub, LANES), F32),
                        pltpu.VMEM((2, PEER_SEL, sub, LANES), F32),
                        pltpu.VMEM((PEER_SEL, LANES), F32),
                        pltpu.VMEM((PEER_SEL, LANES), F32),
                        pltpu.SemaphoreType.DMA((2, 2))],
        compiler_params=_cparams(("arbitrary",)),
        name="peer_gather",
    )(eidx, gw, h2r, u3, v3)


def _peer(x1, norm_w, sc, sh, wq, keys1, keys2, u, v):
    b, s, d = x1.shape
    qh, h2 = _normproj(x1, norm_w, sc, sh, wq.astype(BF16), tm=512, tn=512, emit_h=True)
    eidx, gw = _peer_topk(qh, keys1, keys2)
    eidx = jnp.transpose(eidx, (2, 0, 1)).reshape(b * s, PEER_SEL)
    gw = jnp.transpose(gw, (2, 0, 1)).reshape(b * s, PEER_SEL)
    sub = d // LANES
    out = _peer_gather(eidx, gw, h2.reshape(b * s, sub, LANES),
                       u.reshape(-1, sub, LANES), v.reshape(-1, sub, LANES))
    return out.reshape(b, s, d)


def _final_kernel(x_ref, p_ref, g_ref, w_ref, o_ref):
    x = x_ref[0] + g_ref[0] * p_ref[0]
    o_ref[0] = x * lax.rsqrt(jnp.mean(x * x, axis=-1, keepdims=True) + EPS) * w_ref[...]


def _final(x1, peer, g2, wf):
    b, s, d = x1.shape
    tm = min(512, s)
    blk = pl.BlockSpec((1, tm, d), lambda bi, i: (bi, i, 0))
    return pl.pallas_call(
        _final_kernel,
        grid=(b, s // tm),
        in_specs=[blk, blk, pl.BlockSpec((1, 1, d), lambda bi, i: (bi, 0, 0)),
                  pl.BlockSpec((1, d), lambda bi, i: (0, 0))],
        out_specs=blk,
        out_shape=jax.ShapeDtypeStruct((b, s, d), F32),
        compiler_params=_cparams(("parallel", "parallel")),
        name="final_norm",
    )(x1, peer, g2.reshape(b, 1, d), wf.reshape(1, d))


def _pad_rows(a, mult=8):
    pad = (-a.shape[0]) % mult
    return jnp.pad(a, ((0, pad), (0, 0)))


def _permute_w_in(w):
    o_aa, o_bq, o_bg, o_ma, o_end = 4096, 4112, 7696, 7744, 11840
    d = w.shape[0]
    z = lambda n: jnp.zeros((d, n), w.dtype)
    hg = 3 * NSA_HPG
    cols = [w[:, 0:o_aa], w[:, o_bq:o_bg], w[:, o_ma:o_end],
            w[:, o_aa:o_bq], z(LANES - 16),
            w[:, o_bg:o_bg + hg], z(LANES - hg),
            w[:, o_bg + hg:o_ma], z(LANES - hg),
            z(LANES)]
    return jnp.concatenate(cols, axis=1).astype(BF16)


def kernel(x, c, positions, ada_w, ada_b, norm1_w, norm2_w, w_in, gdn_conv_w, gdn_A_log, gdn_dt_bias, gdn_norm_w, cmp_pos_k, cmp_w1_k, cmp_w2_k, cmp_pos_v, cmp_w1_v, cmp_w2_v, w_branch_gdn, w_branch_nsa, w_out, peer_wq, peer_keys1, peer_keys2, peer_u, peer_v, final_norm_w):
    b, s, d = x.shape
    l = 0
    mod = _ada(_pad_rows(c), ada_w[l], ada_b[l])[:b]
    sh1, sc1, g1, sh2, sc2, g2 = jnp.split(mod, 6, axis=-1)
    P = _normproj(x, norm1_w[l], sc1, sh1, _permute_w_in(w_in[l]), tm=1024, tn=512, emit_h=False)
    o_a = _gdn(P, gdn_conv_w[l], gdn_A_log[l], gdn_dt_bias[l], gdn_norm_w[l])
    o_b = _nsa(P, positions, cmp_pos_k[l], cmp_w1_k[l], cmp_w2_k[l], cmp_pos_v[l], cmp_w1_v[l], cmp_w2_v[l])
    y = _merge(o_a, o_b, w_branch_gdn[l].astype(BF16), w_branch_nsa[l].astype(BF16), P)
    x1 = _outproj(y, w_out[l].astype(BF16), x, g1)
    peer = _peer(x1, norm2_w[l], sc2, sh2, peer_wq[l], peer_keys1[l], peer_keys2[l], peer_u[l], peer_v[l])
    return _final(x1, peer, g2, final_norm_w)
```

```python
import functools
import math

import numpy as np
import jax
import jax.numpy as jnp
from jax import lax
from jax.experimental import pallas as pl
from jax.experimental.pallas import tpu as pltpu

F32 = jnp.float32
BF16 = jnp.bfloat16
HI = lax.Precision.HIGHEST

LANES = 128
VMEM_LIMIT = 56 * 1024 * 1024

EPS = 1e-6
ROPE_THETA = 500000.0
HEAD_DIM = 128
ROPE_DIM = HEAD_DIM // 4
ROPE_HALF = ROPE_DIM // 2

GDN_HEADS = 8
GDN_CONV = 4
GDN_CHUNK = 64

NSA_HEADS = 16
NSA_GROUPS = 2
NSA_HPG = NSA_HEADS // NSA_GROUPS
CMP_LEN = 32
CMP_STRIDE = 16
CMP_HIDDEN = 256
SLC_LEN = 64
SLC_TOPK = 16
WIN = 512

PEER_HEADS = 8
PEER_NKEYS = 128
PEER_QDIM = 256
PEER_TOPK = 16

NEG = -1e30
BIG = 1e9

J_AQ, J_AK, J_AV, J_AZ = 0, 8, 16, 24
J_BQ = 32
N_NSA_BLOCKS = 28
J_MA, J_MB = 60, 76
J_SMALL = 92
J_GATE = 93
NJ = 96


def _cparams(sem):
    return pltpu.CompilerParams(dimension_semantics=sem, vmem_limit_bytes=VMEM_LIMIT)


def _bdot(a, b):
    return jnp.dot(a.astype(BF16), b.astype(BF16), preferred_element_type=F32)


def _bdot_nt(a, b):
    return lax.dot_general(a.astype(BF16), b.astype(BF16), (((1,), (1,)), ((), ())),
                           preferred_element_type=F32)


def _split_bf16(a):
    hi = a.astype(BF16)
    return hi, (a - hi.astype(F32)).astype(BF16)


def _dot3(a, b, exact_a=False):
    bh, bl = _split_bf16(b)
    dot = functools.partial(jnp.dot, preferred_element_type=F32)
    if exact_a:
        return dot(a, bh) + dot(a, bl)
    ah, al = _split_bf16(a)
    return dot(ah, bh) + dot(al, bh) + dot(ah, bl)


def _dot3_nt(a, b):
    ah, al = _split_bf16(a)
    bh, bl = _split_bf16(b)
    dot = functools.partial(lax.dot_general, dimension_numbers=(((1,), (1,)), ((), ())),
                            preferred_element_type=F32)
    return dot(ah, bh) + dot(al, bh) + dot(ah, bl)


def _fdot(a, b):
    return jnp.dot(a, b, precision=HI, preferred_element_type=F32)


def _fdot_nt(a, b):
    return lax.dot_general(a, b, (((1,), (1,)), ((), ())), precision=HI,
                           preferred_element_type=F32)


def _sigmoid(x):
    return 1.0 / (1.0 + jnp.exp(-x))


def _silu(x):
    return x * _sigmoid(x)


def _gelu(x):
    return 0.5 * x * (1.0 + jnp.tanh(math.sqrt(2.0 / math.pi) * (x + 0.044715 * (x * x * x))))


def _softplus(x):
    return jnp.maximum(x, 0.0) + jnp.log(1.0 + jnp.exp(-jnp.abs(x)))


def _ada_kernel(c_ref, w_ref, b_ref, o_ref):
    o_ref[...] = _fdot(_silu(c_ref[...]), w_ref[...]) + b_ref[...]


def _ada(c_pad, ada_w, ada_b):
    m, d = c_pad.shape
    n = ada_w.shape[1]
    tn = 1024
    return pl.pallas_call(
        _ada_kernel,
        grid=(n // tn,),
        in_specs=[pl.BlockSpec((m, d), lambda j: (0, 0)),
                  pl.BlockSpec((d, tn), lambda j: (0, j)),
                  pl.BlockSpec((1, tn), lambda j: (0, j))],
        out_specs=pl.BlockSpec((m, tn), lambda j: (0, j)),
        out_shape=jax.ShapeDtypeStruct((m, n), F32),
        compiler_params=_cparams(("parallel",)),
        name="ada_mod",
    )(c_pad, ada_w, ada_b.reshape(1, n))


def _normproj_kernel(x_ref, nw_ref, sc_ref, sh_ref, w_ref, *rest, nk, emit_h):
    if emit_h:
        o_ref, h_out_ref, h_scr = rest
    else:
        o_ref, h_scr = rest

    @pl.when(pl.program_id(2) == 0)
    def _():
        x = x_ref[0]
        ms = jnp.mean(x * x, axis=-1, keepdims=True)
        y = x * lax.rsqrt(ms + EPS) * nw_ref[...]
        h = y * (1.0 + sc_ref[0]) + sh_ref[0]
        h_scr[...] = h.astype(BF16)
        if emit_h:
            h_out_ref[0] = h

    acc = jnp.dot(h_scr[...], w_ref[...], preferred_element_type=F32)
    for k in range(nk):
        o_ref[0, k] = acc[:, k * LANES:(k + 1) * LANES]


def _normproj(x, nw, sc, sh, w_bf16, *, tm, tn, emit_h):
    b, s, d = x.shape
    n = w_bf16.shape[1]
    tm = min(tm, s)
    nk = tn // LANES
    out_shape = [jax.ShapeDtypeStruct((b, n // LANES, s, LANES), F32)]
    out_specs = [pl.BlockSpec((1, nk, tm, LANES), lambda bi, i, j: (bi, j, i, 0))]
    if emit_h:
        out_shape.append(jax.ShapeDtypeStruct((b, s, d), F32))
        out_specs.append(pl.BlockSpec((1, tm, d), lambda bi, i, j: (bi, i, 0)))
    res = pl.pallas_call(
        functools.partial(_normproj_kernel, nk=nk, emit_h=emit_h),
        grid=(b, s // tm, n // tn),
        in_specs=[pl.BlockSpec((1, tm, d), lambda bi, i, j: (bi, i, 0)),
                  pl.BlockSpec((1, d), lambda bi, i, j: (0, 0)),
                  pl.BlockSpec((1, 1, d), lambda bi, i, j: (bi, 0, 0)),
                  pl.BlockSpec((1, 1, d), lambda bi, i, j: (bi, 0, 0)),
                  pl.BlockSpec((d, tn), lambda bi, i, j: (0, j))],
        out_specs=out_specs,
        out_shape=out_shape,
        scratch_shapes=[pltpu.VMEM((tm, d), BF16)],
        compiler_params=_cparams(("parallel", "parallel", "arbitrary")),
        name="normproj_h" if emit_h else "normproj",
    )(x, nw.reshape(1, d), sc.reshape(b, 1, d), sh.reshape(b, 1, d), w_bf16)
    return res if emit_h else res[0]


GDN_CB = 512
GDN_GB = 256


def _gdn_local_kernel(q_ref, qh_ref, k_ref, kh_ref, v_ref, vh_ref, sm_ref,
                      cwq_ref, cwk_ref, cwv_ref, alog_ref, dtb_ref,
                      w_ref, u0_ref, qk_ref, qg_ref, kd_ref, ge_ref):
    h = pl.program_id(1)
    i = pl.program_id(2)
    C, G = GDN_CHUNK, GDN_GB
    first = (i == 0)

    def conv_act(main_ref, halo_ref, cw_ref, r0):
        if r0 == 0:
            prev = jnp.where(first, 0.0, halo_ref[0, 0])
        else:
            prev = main_ref[0, 0, r0 - 8:r0, :]
        ext = jnp.concatenate([prev, main_ref[0, 0, r0:r0 + G, :]], axis=0)
        w = cw_ref[0]
        y = w[0:1, :] * ext[5:5 + G, :]
        for j in range(1, GDN_CONV):
            y = y + w[j:j + 1, :] * ext[5 + j:5 + j + G, :]
        return _silu(y)

    rid = lax.broadcasted_iota(jnp.int32, (G, G), 0)
    cid = lax.broadcasted_iota(jnp.int32, (G, G), 1)
    same = (rid // C) == (cid // C)
    incl = jnp.logical_and(same, rid >= cid)
    strict = jnp.logical_and(same, rid > cid)
    eye = rid == cid
    is_last = cid == (rid // C) * C + (C - 1)
    tri = jnp.where(incl, 1.0, 0.0).astype(BF16)
    eye_f = jnp.where(eye, 1.0, 0.0).astype(F32)
    lane = lax.broadcasted_iota(jnp.int32, (G, LANES), 1)
    neg_a = -jnp.exp(alog_ref[...])
    dtb = dtb_ref[...]

    for grp in range(GDN_CB // G):
        r0 = grp * G
        q = conv_act(q_ref, qh_ref, cwq_ref, r0)
        k = conv_act(k_ref, kh_ref, cwk_ref, r0)
        v = conv_act(v_ref, vh_ref, cwv_ref, r0)
        q = q * lax.rsqrt(jnp.sum(q * q, axis=-1, keepdims=True) + EPS) * (HEAD_DIM ** -0.5)
        k = k * lax.rsqrt(jnp.sum(k * k, axis=-1, keepdims=True) + EPS)

        sm = sm_ref[0, 0, r0:r0 + G, :]
        g_all = neg_a * _softplus(sm + dtb)
        gc_all = _dot3(tri, g_all, exact_a=True)
        gc = jnp.sum(jnp.where(lane == h, gc_all, 0.0), axis=1, keepdims=True)
        beta = jnp.sum(jnp.where(lane == GDN_HEADS + h, _sigmoid(sm), 0.0), axis=1, keepdims=True)
        gc_row = jnp.sum(jnp.where(eye, gc, 0.0), axis=0, keepdims=True)
        gc_last = jnp.sum(jnp.where(is_last, gc_row, 0.0), axis=1, keepdims=True)
        decay = jnp.where(incl, jnp.exp(jnp.where(incl, gc - gc_row, 0.0)), 0.0)
        gamma = jnp.exp(gc)

        kk = _dot3_nt(k, k)
        lmat = jnp.where(strict, decay * kk, 0.0) * beta
        tinv = eye_f - lmat
        pw = lmat
        for _ in range(int(math.log2(C)) - 1):
            pw = _dot3(pw, pw)
            tinv = _dot3(tinv, eye_f + pw)
        rhs = jnp.concatenate([(beta * gamma) * k, beta * v], axis=1)
        wu = _dot3(tinv, rhs)
        qk = decay * _bdot_nt(q, k)

        rows = slice(r0, r0 + G)
        w_ref[0, 0, rows, :] = wu[:, :HEAD_DIM].astype(BF16)
        u0_ref[0, 0, rows, :] = wu[:, HEAD_DIM:]
        qg_ref[0, 0, rows, :] = (gamma * q).astype(BF16)
        kd_ref[0, 0, rows, :] = (jnp.exp(gc_last - gc) * k).astype(BF16)
        ge_all = jnp.broadcast_to(jnp.exp(gc_last), (G, LANES))
        for c in range(G // C):
            cr = slice(c * C, (c + 1) * C)
            qk_ref[0, 0, r0 + c * C:r0 + (c + 1) * C, :] = qk[cr, cr].astype(BF16)
            ge_ref[0, 0, pl.ds((i * (GDN_CB // G) + grp) * (G // C) + c, 1), :] = ge_all[c * C:c * C + 1, :]


def _gdn_local(P, cw, alog_pad, dtb_pad):
    b, _, s, _ = P.shape
    H, CB, C = GDN_HEADS, GDN_CB, GDN_CHUNK
    n = s // C

    def main(j0):
        return pl.BlockSpec((1, 1, CB, LANES), lambda bi, h, i: (bi, j0 + h, i, 0))

    def halo(j0):
        return pl.BlockSpec((1, 1, 8, LANES),
                            lambda bi, h, i: (bi, j0 + h, jnp.maximum(i * (CB // 8) - 1, 0), 0))

    def cws(j0):
        return pl.BlockSpec((1, GDN_CONV, LANES), lambda bi, h, i: (j0 + h, 0, 0))

    row = pl.BlockSpec((1, LANES), lambda bi, h, i: (0, 0))
    hs = lambda width: pl.BlockSpec((1, 1, CB, width), lambda bi, h, i: (bi, h, i, 0))
    return pl.pallas_call(
        _gdn_local_kernel,
        grid=(b, H, s // CB),
        in_specs=[main(J_AQ), halo(J_AQ), main(J_AK), halo(J_AK), main(J_AV), halo(J_AV),
                  pl.BlockSpec((1, 1, CB, LANES), lambda bi, h, i: (bi, J_SMALL, i, 0)),
                  cws(0), cws(8), cws(16), row, row],
        out_specs=[hs(LANES), hs(LANES), hs(C), hs(LANES), hs(LANES),
                   pl.BlockSpec((1, 1, n, LANES), lambda bi, h, i: (bi, h, 0, 0))],
        out_shape=[jax.ShapeDtypeStruct((b, H, s, LANES), BF16),
                   jax.ShapeDtypeStruct((b, H, s, LANES), F32),
                   jax.ShapeDtypeStruct((b, H, s, C), BF16),
                   jax.ShapeDtypeStruct((b, H, s, LANES), BF16),
                   jax.ShapeDtypeStruct((b, H, s, LANES), BF16),
                   jax.ShapeDtypeStruct((b, H, n, LANES), F32)],
        compiler_params=_cparams(("parallel", "parallel", "arbitrary")),
        name="gdn_local",
    )(P, P, P, P, P, P, P, cw, cw, cw, alog_pad, dtb_pad)


GDN_HB = 4


def _gdn_scan_kernel(w_ref, u0_ref, qk_ref, qg_ref, kd_ref, ge_ref, z_ref, nw_ref, o_ref, s_scr,
                     *, sb):
    C = GDN_CHUNK

    @pl.when(pl.program_id(2) == 0)
    def _():
        s_scr[...] = jnp.zeros_like(s_scr)

    nw = nw_ref[...]

    def body(n, carry):
        r = pl.multiple_of(n * C, C)
        for hh in range(GDN_HB):
            st = s_scr[hh]
            stb = st.astype(BF16)
            u = u0_ref[0, hh, pl.ds(r, C), :] - jnp.dot(w_ref[0, hh, pl.ds(r, C), :], stb,
                                                       preferred_element_type=F32)
            ub = u.astype(BF16)
            o = (jnp.dot(qg_ref[0, hh, pl.ds(r, C), :], stb, preferred_element_type=F32)
                 + jnp.dot(qk_ref[0, hh, pl.ds(r, C), :], ub, preferred_element_type=F32))
            ge = ge_ref[0, hh, pl.ds(n, 1), :]
            s_scr[hh] = ge * st + lax.dot_general(kd_ref[0, hh, pl.ds(r, C), :], ub,
                                                  (((0,), (0,)), ((), ())),
                                                  preferred_element_type=F32)
            on = o * lax.rsqrt(jnp.mean(o * o, axis=-1, keepdims=True) + EPS) * nw
            z = z_ref[0, hh, pl.ds(r, C), :]
            o_ref[0, pl.ds(r, C), hh * LANES:(hh + 1) * LANES] = (on * _silu(z)).astype(BF16)
        return carry

    lax.fori_loop(0, sb // C, body, 0)


def _gdn_scan(w, u0, qk, qg, kd, ge, P, norm_w):
    b, H, s, _ = w.shape
    C, HB = GDN_CHUNK, GDN_HB
    sb = min(1024, s)
    hs = lambda width: pl.BlockSpec((1, HB, sb, width), lambda bi, hb, i: (bi, hb, i, 0))
    return pl.pallas_call(
        functools.partial(_gdn_scan_kernel, sb=sb),
        grid=(b, H // HB, s // sb),
        in_specs=[hs(LANES), hs(LANES), hs(C), hs(LANES), hs(LANES),
                  pl.BlockSpec((1, HB, sb // C, LANES), lambda bi, hb, i: (bi, hb, i, 0)),
                  pl.BlockSpec((1, HB, sb, LANES), lambda bi, hb, i: (bi, J_AZ // HB + hb, i, 0)),
                  pl.BlockSpec((1, LANES), lambda bi, hb, i: (0, 0))],
        out_specs=pl.BlockSpec((1, sb, HB * LANES), lambda bi, hb, i: (bi, i, hb)),
        out_shape=jax.ShapeDtypeStruct((b, s, H * LANES), BF16),
        scratch_shapes=[pltpu.VMEM((HB, HEAD_DIM, HEAD_DIM), F32)],
        compiler_params=_cparams(("parallel", "parallel", "arbitrary")),
        name="gdn_scan",
    )(w, u0, qk, qg, kd, ge, P, norm_w.reshape(1, LANES))


def _pad_lanes_row(v):
    return jnp.pad(v.astype(F32), (0, LANES - v.shape[0])).reshape(1, LANES)


def _gdn(P, conv_w, a_log, dt_bias, norm_w):
    cw = jnp.transpose(conv_w.reshape(GDN_CONV, 3 * GDN_HEADS, LANES), (1, 0, 2))
    w, u0, qk, qg, kd, ge = _gdn_local(P, cw, _pad_lanes_row(a_log), _pad_lanes_row(dt_bias))
    return _gdn_scan(w, u0, qk, qg, kd, ge, P, norm_w)


NSA_TQ = 128
R_Q, R_KC, R_VC, R_KS, R_VS, R_KW, R_VW = 0, 16, 18, 20, 22, 24, 26


def _rope_kernel(x_ref, pos_ref, invf_ref, o_ref, cos_scr, sin_scr):
    j = pl.program_id(2)
    lane = lax.broadcasted_iota(jnp.int32, cos_scr.shape, 1)

    @pl.when(j == 0)
    def _():
        ang = pos_ref[0] * invf_ref[...]
        sn = jnp.sin(ang)
        cos_scr[...] = jnp.where(lane < ROPE_DIM, jnp.cos(ang), 1.0)
        sin_scr[...] = jnp.where(lane < ROPE_HALF, -sn, jnp.where(lane < ROPE_DIM, sn, 0.0))

    x = x_ref[0, 0]
    swapped = jnp.where(lane < ROPE_HALF, pltpu.roll(x, LANES - ROPE_HALF, axis=1),
                        pltpu.roll(x, ROPE_HALF, axis=1))
    rot = x * cos_scr[...] + swapped * sin_scr[...]
    is_q = j < NSA_HEADS
    is_rope = jnp.logical_or(is_q, ((j - NSA_HEADS) % 4) < 2)
    out = jnp.where(is_rope, rot, x) * jnp.where(is_q, HEAD_DIM ** -0.5, 1.0)
    o_ref[0, 0] = out.astype(BF16)


def _rope(P, pos_f32, invf):
    b, _, s, _ = P.shape
    tr = min(512, s)
    return pl.pallas_call(
        _rope_kernel,
        grid=(b, s // tr, N_NSA_BLOCKS),
        in_specs=[pl.BlockSpec((1, 1, tr, LANES), lambda bi, i, j: (bi, J_BQ + j, i, 0)),
                  pl.BlockSpec((1, tr, 1), lambda bi, i, j: (bi, i, 0)),
                  pl.BlockSpec((1, LANES), lambda bi, i, j: (0, 0))],
        out_specs=pl.BlockSpec((1, 1, tr, LANES), lambda bi, i, j: (bi, j, i, 0)),
        out_shape=jax.ShapeDtypeStruct((b, N_NSA_BLOCKS, s, LANES), BF16),
        scratch_shapes=[pltpu.VMEM((tr, LANES), F32), pltpu.VMEM((tr, LANES), F32)],
        compiler_params=_cparams(("parallel", "parallel", "arbitrary")),
        name="nsa_rope",
    )(P, pos_f32, invf)


def _compress_kernel(r_ref, w1_ref, w2_ref, pos_ref, o_ref):
    r = r_ref[0, 0]
    nr = r.shape[0]
    half = CMP_STRIDE * HEAD_DIM
    a = jnp.dot(r, w1_ref[0, :half, :], preferred_element_type=F32)
    bm = jnp.dot(r, w1_ref[0, half:, :], preferred_element_type=F32)
    pos8 = jnp.broadcast_to(pos_ref[0], (8, CMP_LEN * HEAD_DIM)).astype(BF16)
    pb = jnp.dot(pos8, w1_ref[0], preferred_element_type=F32)[0:1, :]
    hid = a + pltpu.roll(bm, nr - 1, axis=0) + pb
    out = jnp.dot(_gelu(hid).astype(BF16), w2_ref[0], preferred_element_type=F32)
    row = lax.broadcasted_iota(jnp.int32, out.shape, 0)
    o_ref[0, 0] = jnp.where(row < nr - 1, out, 0.0).astype(BF16)


def _compress(rows, w1, w2, pos):
    b, _, nr, width = rows.shape
    return pl.pallas_call(
        _compress_kernel,
        grid=(b, 4),
        in_specs=[pl.BlockSpec((1, 1, nr, width), lambda bi, j: (bi, j, 0, 0)),
                  pl.BlockSpec((1, CMP_LEN * HEAD_DIM, CMP_HIDDEN), lambda bi, j: (j // 2, 0, 0)),
                  pl.BlockSpec((1, CMP_HIDDEN, HEAD_DIM), lambda bi, j: (j // 2, 0, 0)),
                  pl.BlockSpec((1, 1, CMP_LEN * HEAD_DIM), lambda bi, j: (j // 2, 0, 0))],
        out_specs=pl.BlockSpec((1, 1, nr, HEAD_DIM), lambda bi, j: (bi, j, 0, 0)),
        out_shape=jax.ShapeDtypeStruct((b, 4, nr, HEAD_DIM), BF16),
        compiler_params=_cparams(("parallel", "arbitrary")),
        name="nsa_compress",
    )(rows, w1, w2, pos)


def _nsa_kernel(q_ref, kc_ref, vc_ref, ks_ref, vs_ref, kw_ref, vw_ref, gate_ref, ovt_ref, e_ref,
                o_ref, m_scr, l_scr, acc_scr, os_scr, *, s_len):
    i = pl.program_id(2)
    TQ, HP = NSA_TQ, NSA_HPG
    TK = TQ
    R = HP * TQ
    nsb = s_len // SLC_LEN
    ncp = s_len // CMP_STRIDE
    n_sel = min(SLC_TOPK, nsb)
    s0 = i * TQ
    q2 = q_ref[0].reshape(R, HEAD_DIM)

    sc = _bdot_nt(q2, kc_ref[0, 0])
    tq_r = s0 + (lax.broadcasted_iota(jnp.int32, (R, ncp), 0) & (TQ - 1))
    ncol = lax.broadcasted_iota(jnp.int32, (R, ncp), 1)
    valid = jnp.logical_and(ncol * CMP_STRIDE + (CMP_LEN - 1) <= tq_r, ncol < ncp - 1)
    scm = jnp.where(valid, sc, NEG)
    e = jnp.where(valid, jnp.exp(scm - jnp.max(scm, axis=1, keepdims=True)), 0.0)
    p_c = e / jnp.maximum(jnp.sum(e, axis=1, keepdims=True), 1e-30)
    o_c = _bdot(p_c, vc_ref[0, 0])
    psum = jnp.sum(p_c.reshape(HP, TQ, ncp), axis=0)
    imp = _fdot_nt(ovt_ref[...], psum)
    blk = lax.broadcasted_iota(jnp.int32, (nsb, TQ), 0)
    tq_l = s0 + lax.broadcasted_iota(jnp.int32, (nsb, TQ), 1)
    cur = tq_l // SLC_LEN
    forced = jnp.logical_or(blk == 0, jnp.logical_or(blk == cur, blk == cur - 1))
    imp = jnp.where(forced, BIG, imp)
    imp = jnp.where(blk * SLC_LEN <= tq_l, imp, NEG)
    rank = jnp.zeros((nsb, TQ), jnp.int32)
    for j in range(nsb):
        rj = imp[j:j + 1, :]
        beats = jnp.logical_or(rj > imp, jnp.logical_and(rj == imp, blk > j))
        rank = rank + jnp.where(beats, 1, 0)
    sel_t = jnp.where(rank < n_sel, 1.0, 0.0).astype(F32)
    if nsb < LANES:
        sel_t = jnp.concatenate([sel_t, jnp.zeros((LANES - nsb, TQ), F32)], axis=0)
    sel = jnp.transpose(sel_t).astype(BF16)

    qpos = s0 + lax.broadcasted_iota(jnp.int32, (TQ, TK), 0)
    kofs = lax.broadcasted_iota(jnp.int32, (TQ, TK), 1)

    def flash_init():
        m_scr[...] = jnp.full(m_scr.shape, NEG, F32)
        l_scr[...] = jnp.zeros(l_scr.shape, F32)
        acc_scr[...] = jnp.zeros(acc_scr.shape, F32)

    def flash_step(k_ref, v_ref, j, bias):
        r = pl.multiple_of(j * TK, TK)
        s = _bdot_nt(q2, k_ref[0, 0, pl.ds(r, TK), :])
        s = (s.reshape(HP, TQ, TK) + bias[None]).reshape(R, TK)
        m_prev = m_scr[...]
        m_new = jnp.maximum(m_prev, jnp.max(s, axis=1, keepdims=True))
        alpha = jnp.exp(m_prev - m_new)
        p = jnp.exp(s - m_new)
        l_scr[...] = alpha * l_scr[...] + jnp.sum(p, axis=1, keepdims=True)
        acc_scr[...] = alpha * acc_scr[...] + _bdot(p, v_ref[0, 0, pl.ds(r, TK), :])
        m_scr[...] = m_new

    flash_init()

    def sel_body(j, carry):
        kpos = j * TK + kofs
        r = pl.multiple_of(j * TK, TK)
        chosen = jnp.dot(sel, e_ref[:, pl.ds(r, TK)], preferred_element_type=F32)
        ok = jnp.logical_and(chosen > 0.5, kpos <= qpos)
        flash_step(ks_ref, vs_ref, j, jnp.where(ok, 0.0, NEG))
        return carry

    lax.fori_loop(0, i + 1, sel_body, 0)
    os_scr[...] = acc_scr[...] / l_scr[...]

    flash_init()

    def win_body(j, carry):
        kpos = j * TK + kofs
        ok = jnp.logical_and(kpos <= qpos, kpos > qpos - WIN)
        flash_step(kw_ref, vw_ref, j, jnp.where(ok, 0.0, NEG))
        return carry

    lax.fori_loop(jnp.maximum(i - WIN // TK, 0), i + 1, win_body, 0)

    gates = _sigmoid(gate_ref[0, 0])
    for hh in range(HP):
        rows = slice(hh * TQ, (hh + 1) * TQ)
        o_w = acc_scr[rows, :] / l_scr[rows, :]
        out = (gates[:, 3 * hh:3 * hh + 1] * o_c[rows, :]
               + gates[:, 3 * hh + 1:3 * hh + 2] * os_scr[rows, :]
               + gates[:, 3 * hh + 2:3 * hh + 3] * o_w)
        o_ref[0, :, hh * HEAD_DIM:(hh + 1) * HEAD_DIM] = out.astype(BF16)


def _nsa_attend(rp, cmp_kv, P, ovt, expand):
    b, _, s, _ = rp.shape
    TQ, HP, G = NSA_TQ, NSA_HPG, NSA_GROUPS
    ncp = cmp_kv.shape[2]
    R = HP * TQ
    full = lambda j0: pl.BlockSpec((1, 1, s, HEAD_DIM), lambda bi, g, i: (bi, j0 + g, 0, 0))
    cmp_spec = lambda j0: pl.BlockSpec((1, 1, ncp, HEAD_DIM), lambda bi, g, i: (bi, j0 + g, 0, 0))
    return pl.pallas_call(
        functools.partial(_nsa_kernel, s_len=s),
        grid=(b, G, s // TQ),
        in_specs=[pl.BlockSpec((1, HP, TQ, HEAD_DIM), lambda bi, g, i: (bi, g, i, 0)),
                  cmp_spec(0), cmp_spec(2),
                  full(R_KS), full(R_VS), full(R_KW), full(R_VW),
                  pl.BlockSpec((1, 1, TQ, LANES), lambda bi, g, i: (bi, J_GATE + g, i, 0)),
                  pl.BlockSpec(ovt.shape, lambda bi, g, i: (0, 0)),
                  pl.BlockSpec(expand.shape, lambda bi, g, i: (0, 0))],
        out_specs=pl.BlockSpec((1, TQ, HP * HEAD_DIM), lambda bi, g, i: (bi, i, g)),
        out_shape=jax.ShapeDtypeStruct((b, s, NSA_HEADS * HEAD_DIM), BF16),
        scratch_shapes=[pltpu.VMEM((R, LANES), F32), pltpu.VMEM((R, LANES), F32),
                        pltpu.VMEM((R, HEAD_DIM), F32), pltpu.VMEM((R, HEAD_DIM), F32)],
        compiler_params=_cparams(("parallel", "parallel", "arbitrary")),
        name="nsa_attend",
    )(rp, cmp_kv, cmp_kv, rp, rp, rp, rp, P, ovt, expand)


def _nsa(P, positions, cmp_pos_k, cmp_w1_k, cmp_w2_k, cmp_pos_v, cmp_w1_v, cmp_w2_v):
    b, _, s, _ = P.shape
    lanes = np.arange(LANES)
    invf = np.where(lanes < ROPE_DIM, ROPE_THETA ** (-(lanes % ROPE_HALF) / ROPE_HALF), 0.0)
    rp = _rope(P, positions.astype(F32).reshape(b, s, 1), jnp.asarray(invf, F32).reshape(1, LANES))

    ncp = s // CMP_STRIDE
    rows = rp[:, R_KC:R_KC + 4].reshape(b, 4, ncp, CMP_STRIDE * HEAD_DIM)
    w1 = jnp.stack([cmp_w1_k, cmp_w1_v]).astype(BF16)
    w2 = jnp.stack([cmp_w2_k, cmp_w2_v]).astype(BF16)
    pos = jnp.stack([cmp_pos_k, cmp_pos_v]).reshape(2, 1, CMP_LEN * HEAD_DIM)
    cmp_kv = _compress(rows, w1, w2, pos)

    nsb = s // SLC_LEN
    cmp_start = np.arange(ncp) * CMP_STRIDE
    slc_start = np.arange(nsb) * SLC_LEN
    ov = ((cmp_start[:, None] < slc_start[None, :] + SLC_LEN)
          & (cmp_start[:, None] + CMP_LEN > slc_start[None, :])
          & (np.arange(ncp)[:, None] < ncp - 1))
    ovt = jnp.asarray(ov.T.astype(np.float32))
    expand = (np.arange(LANES)[:, None] == (np.arange(s)[None, :] // SLC_LEN))
    expand = jnp.asarray(expand.astype(np.float32), BF16)
    return _nsa_attend(rp, cmp_kv, P, ovt, expand)


MERGE_TN = 512


def _merge_kernel(oa_ref, ob_ref, wg_ref, wn_ref, ma_ref, mb_ref, y_ref):
    ya = jnp.dot(oa_ref[0], wg_ref[...], preferred_element_type=F32)
    yb = jnp.dot(ob_ref[0], wn_ref[...], preferred_element_type=F32)
    for k in range(MERGE_TN // LANES):
        cols = slice(k * LANES, (k + 1) * LANES)
        y = _sigmoid(ma_ref[0, k]) * ya[:, cols] + _sigmoid(mb_ref[0, k]) * yb[:, cols]
        y_ref[0, :, cols] = y.astype(BF16)


def _merge(o_a, o_b, wg, wn, P):
    b, s, da = o_a.shape
    db = o_b.shape[2]
    d = wg.shape[1]
    tm, tn = min(512, s), MERGE_TN
    nk = tn // LANES
    return pl.pallas_call(
        _merge_kernel,
        grid=(b, s // tm, d // tn),
        in_specs=[pl.BlockSpec((1, tm, da), lambda bi, i, j: (bi, i, 0)),
                  pl.BlockSpec((1, tm, db), lambda bi, i, j: (bi, i, 0)),
                  pl.BlockSpec((da, tn), lambda bi, i, j: (0, j)),
                  pl.BlockSpec((db, tn), lambda bi, i, j: (0, j)),
                  pl.BlockSpec((1, nk, tm, LANES), lambda bi, i, j: (bi, J_MA // nk + j, i, 0)),
                  pl.BlockSpec((1, nk, tm, LANES), lambda bi, i, j: (bi, J_MB // nk + j, i, 0))],
        out_specs=pl.BlockSpec((1, tm, tn), lambda bi, i, j: (bi, i, j)),
        out_shape=jax.ShapeDtypeStruct((b, s, d), BF16),
        compiler_params=_cparams(("parallel", "parallel", "arbitrary")),
        name="merge",
    )(o_a, o_b, wg, wn, P, P)


def _outproj_kernel(y_ref, w_ref, x_ref, g_ref, o_ref):
    o_ref[0] = x_ref[0] + g_ref[0] * jnp.dot(y_ref[0], w_ref[...], preferred_element_type=F32)


def _outproj(y, w, x, g):
    b, s, d = x.shape
    tm, tn = min(512, s), 512
    return pl.pallas_call(
        _outproj_kernel,
        grid=(b, s // tm, d // tn),
        in_specs=[pl.BlockSpec((1, tm, d), lambda bi, i, j: (bi, i, 0)),
                  pl.BlockSpec((d, tn), lambda bi, i, j: (0, j)),
                  pl.BlockSpec((1, tm, tn), lambda bi, i, j: (bi, i, j)),
                  pl.BlockSpec((1, 1, tn), lambda bi, i, j: (bi, 0, j))],
        out_specs=pl.BlockSpec((1, tm, tn), lambda bi, i, j: (bi, i, j)),
        out_shape=jax.ShapeDtypeStruct((b, s, d), F32),
        compiler_params=_cparams(("parallel", "parallel", "arbitrary")),
        name="outproj",
    )(y, w, x, g.reshape(b, 1, d))


PEER_TB = 256
PEER_TG = 64
PEER_AHEAD = 3
PEER_SLOTS = PEER_AHEAD + 2
PEER_SEL = PEER_HEADS * PEER_TOPK
MIN_F32 = -3.0e38


def _topk_rows(vals, payload=None):
    nrow = vals.shape[0]
    rid = lax.broadcasted_iota(jnp.int32, vals.shape, 0)
    out_v, out_i = [], []
    for _ in range(PEER_TOPK):
        m = jnp.max(vals, axis=0, keepdims=True)
        idx = jnp.min(jnp.where(vals == m, rid, nrow), axis=0, keepdims=True)
        hit = rid == idx
        out_v.append(m)
        if payload is None:
            out_i.append(idx)
        else:
            out_i.append(jnp.sum(jnp.where(hit, payload, 0), axis=0, keepdims=True))
        vals = jnp.where(hit, MIN_F32, vals)
    return jnp.concatenate(out_v, axis=0), jnp.concatenate(out_i, axis=0)


def _peer_topk_kernel(qh_ref, k1_ref, k2_ref, eidx_ref, gw_ref):
    def head(h, carry):
        s1 = _fdot_nt(k1_ref[h], qh_ref[0, 2 * h])
        s2 = _fdot_nt(k2_ref[h], qh_ref[0, 2 * h + 1])
        v1, i1 = _topk_rows(s1)
        v2, i2 = _topk_rows(s2)
        cand = jnp.concatenate([v1[a:a + 1, :] + v2 for a in range(PEER_TOPK)], axis=0)
        cidx = jnp.concatenate([i1[a:a + 1, :] * PEER_NKEYS + i2 for a in range(PEER_TOPK)], axis=0)
        sc, eidx = _topk_rows(cand, cidx)
        ex = jnp.exp(sc - sc[0:1, :])
        eidx_ref[h] = eidx
        gw_ref[h] = ex / jnp.sum(ex, axis=0, keepdims=True)
        return carry

    lax.fori_loop(0, PEER_HEADS, head, 0)


def _peer_topk(qh, keys1, keys2):
    b, nb, s, _ = qh.shape
    tb = min(PEER_TB, s)
    nt = s // tb
    kspec = pl.BlockSpec(keys1.shape, lambda bi, i: (0, 0, 0))
    ospec = pl.BlockSpec((PEER_HEADS, PEER_TOPK, tb), lambda bi, i: (0, 0, bi * nt + i))
    return pl.pallas_call(
        _peer_topk_kernel,
        grid=(b, nt),
        in_specs=[pl.BlockSpec((1, nb, tb, LANES), lambda bi, i: (bi, 0, i, 0)), kspec, kspec],
        out_specs=[ospec, ospec],
        out_shape=[jax.ShapeDtypeStruct((PEER_HEADS, PEER_TOPK, b * s), jnp.int32),
                   jax.ShapeDtypeStruct((PEER_HEADS, PEER_TOPK, b * s), F32)],
        compiler_params=_cparams(("parallel", "parallel")),
        name="peer_topk",
    )(qh, keys1, keys2)


def _peer_gather_kernel(idx_ref, gw_ref, x_ref, uv_hbm, o_ref, buf, a_scr, c_scr, sem, *, tg):
    NS, D, NSLOT = PEER_SEL, PEER_AHEAD, PEER_SLOTS
    rid = lax.broadcasted_iota(jnp.int32, (NS, NS), 0)
    cid = lax.broadcasted_iota(jnp.int32, (NS, NS), 1)
    eye = rid == cid

    def start_row(t, j):
        slot = t % NSLOT
        pltpu.make_async_copy(uv_hbm.at[idx_ref[t, j]], buf.at[slot, j], sem.at[slot]).start()

    def wait_rows(t):
        slot = t % NSLOT
        pltpu.make_async_copy(uv_hbm.at[pl.ds(0, NS)], buf.at[slot], sem.at[slot]).wait()

    def phase(t, do_issue, do_dot, do_sum):
        if do_dot:
            wait_rows(t)
            x = x_ref[t]
            xl, xh = x[0:8, :], x[8:16, :]
            slot_d = t % NSLOT
        if do_sum:
            slot_s = (t - 1) % NSLOT
            par_s = (t - 1) % 2

        def body(j, acc):
            if do_issue:
                start_row(t + D, j)
            if do_dot:
                uj = buf[slot_d, j, 0]
                a_scr[pl.ds(j, 1), :] = jnp.sum(uj[0:8, :] * xl + uj[8:16, :] * xh, axis=0, keepdims=True)
            if do_sum:
                acc = acc + c_scr[par_s, pl.ds(j, 1), :] * buf[slot_s, j, 1]
            return acc

        acc = lax.fori_loop(0, NS, body, jnp.zeros((x_ref.shape[1], LANES), F32), unroll=8)
        if do_sum:
            o_ref[t - 1] = acc
        if do_dot:
            act = jnp.sum(a_scr[...], axis=1, keepdims=True)
            gw_col = jnp.sum(jnp.where(eye, gw_ref[pl.ds(t, 1), :], 0.0), axis=1, keepdims=True)
            c_scr[t % 2] = jnp.broadcast_to(gw_col * _gelu(act), (NS, LANES))

    def ramp(t, carry):
        def body(j, c):
            start_row(t, j)
            return c
        return lax.fori_loop(0, NS, body, carry)

    def looped(do_issue):
        def f(t, carry):
            phase(t, do_issue, True, True)
            return carry
        return f

    lax.fori_loop(0, D, ramp, 0)
    phase(0, True, True, False)
    lax.fori_loop(1, tg - D, looped(True), 0)
    lax.fori_loop(tg - D, tg, looped(False), 0)
    phase(tg, False, False, True)


def _peer_gather(eidx, gw, h2r, uv):
    t = eidx.shape[0]
    tg = min(PEER_TG, t)
    sub = h2r.shape[1]
    return pl.pallas_call(
        functools.partial(_peer_gather_kernel, tg=tg),
        grid=(t // tg,),
        in_specs=[pl.BlockSpec((tg, PEER_SEL), lambda i: (i, 0), memory_space=pltpu.SMEM),
                  pl.BlockSpec((tg, PEER_SEL), lambda i: (i, 0)),
                  pl.BlockSpec((tg, sub, LANES), lambda i: (i, 0, 0)),
                  pl.BlockSpec(memory_space=pl.ANY)],
        out_specs=pl.BlockSpec((tg, sub, LANES), lambda i: (i, 0, 0)),
        out_shape=jax.ShapeDtypeStruct((t, sub, LANES), F32),
        scratch_shapes=[pltpu.VMEM((PEER_SLOTS, PEER_SEL, 2, sub, LANES), F32),
                        pltpu.VMEM((PEER_SEL, LANES), F32),
                        pltpu.VMEM((2, PEER_SEL, LANES), F32),
                        pltpu.SemaphoreType.DMA((PEER_SLOTS,))],
        compiler_params=_cparams(("arbitrary",)),
        name="peer_gather",
    )(eidx, gw, h2r, uv)


def _peer(x1, norm_w, sc, sh, wq, keys1, keys2, u, v):
    b, s, d = x1.shape
    qh, h2 = _normproj(x1, norm_w, sc, sh, wq.astype(BF16), tm=512, tn=512, emit_h=True)
    eidx, gw = _peer_topk(qh, keys1, keys2)
    eidx = jnp.transpose(eidx, (2, 0, 1)).reshape(b * s, PEER_SEL)
    gw = jnp.transpose(gw, (2, 0, 1)).reshape(b * s, PEER_SEL)
    sub = d // LANES
    uv = jnp.stack([u.reshape(-1, sub, LANES), v.reshape(-1, sub, LANES)], axis=1)
    out = _peer_gather(eidx, gw, h2.reshape(b * s, sub, LANES), uv)
    return out.reshape(b, s, d)


def _final_kernel(x_ref, p_ref, g_ref, w_ref, o_ref):
    x = x_ref[0] + g_ref[0] * p_ref[0]
    o_ref[0] = x * lax.rsqrt(jnp.mean(x * x, axis=-1, keepdims=True) + EPS) * w_ref[...]


def _final(x1, peer, g2, wf):
    b, s, d = x1.shape
    tm = min(512, s)
    blk = pl.BlockSpec((1, tm, d), lambda bi, i: (bi, i, 0))
    return pl.pallas_call(
        _final_kernel,
        grid=(b, s // tm),
        in_specs=[blk, blk, pl.BlockSpec((1, 1, d), lambda bi, i: (bi, 0, 0)),
                  pl.BlockSpec((1, d), lambda bi, i: (0, 0))],
        out_specs=blk,
        out_shape=jax.ShapeDtypeStruct((b, s, d), F32),
        compiler_params=_cparams(("parallel", "parallel")),
        name="final_norm",
    )(x1, peer, g2.reshape(b, 1, d), wf.reshape(1, d))


def _pad_rows(a, mult=8):
    pad = (-a.shape[0]) % mult
    return jnp.pad(a, ((0, pad), (0, 0)))


def _permute_w_in(w):
    o_aa, o_bq, o_bg, o_ma, o_end = 4096, 4112, 7696, 7744, 11840
    d = w.shape[0]
    z = lambda n: jnp.zeros((d, n), w.dtype)
    hg = 3 * NSA_HPG
    cols = [w[:, 0:o_aa], w[:, o_bq:o_bg], w[:, o_ma:o_end],
            w[:, o_aa:o_bq], z(LANES - 16),
            w[:, o_bg:o_bg + hg], z(LANES - hg),
            w[:, o_bg + hg:o_ma], z(LANES - hg),
            z(LANES)]
    return jnp.concatenate(cols, axis=1).astype(BF16)


def kernel(x, c, positions, ada_w, ada_b, norm1_w, norm2_w, w_in, gdn_conv_w, gdn_A_log, gdn_dt_bias, gdn_norm_w, cmp_pos_k, cmp_w1_k, cmp_w2_k, cmp_pos_v, cmp_w1_v, cmp_w2_v, w_branch_gdn, w_branch_nsa, w_out, peer_wq, peer_keys1, peer_keys2, peer_u, peer_v, final_norm_w):
    b, s, d = x.shape
    l = 0
    mod = _ada(_pad_rows(c), ada_w[l], ada_b[l])[:b]
    sh1, sc1, g1, sh2, sc2, g2 = jnp.split(mod, 6, axis=-1)
    P = _normproj(x, norm1_w[l], sc1, sh1, _permute_w_in(w_in[l]), tm=1024, tn=512, emit_h=False)
    o_a = _gdn(P, gdn_conv_w[l], gdn_A_log[l], gdn_dt_bias[l], gdn_norm_w[l])
    o_b = _nsa(P, positions, cmp_pos_k[l], cmp_w1_k[l], cmp_w2_k[l], cmp_pos_v[l], cmp_w1_v[l], cmp_w2_v[l])
    y = _merge(o_a, o_b, w_branch_gdn[l].astype(BF16), w_branch_nsa[l].astype(BF16), P)
    x1 = _outproj(y, w_out[l].astype(BF16), x, g1)
    peer = _peer(x1, norm2_w[l], sc2, sh2, peer_wq[l], peer_keys1[l], peer_keys2[l], peer_u[l], peer_v[l])
    return _final(x1, peer, g2, final_norm_w)
```

```python
import functools
import math

import numpy as np
import jax
import jax.numpy as jnp
from jax import lax
from jax.experimental import pallas as pl
from jax.experimental.pallas import tpu as pltpu

F32 = jnp.float32
BF16 = jnp.bfloat16
HI = lax.Precision.HIGHEST

LANES = 128
VMEM_LIMIT = 56 * 1024 * 1024

EPS = 1e-6
ROPE_THETA = 500000.0
HEAD_DIM = 128
ROPE_DIM = HEAD_DIM // 4
ROPE_HALF = ROPE_DIM // 2

GDN_HEADS = 8
GDN_CONV = 4
GDN_CHUNK = 64

NSA_HEADS = 16
NSA_GROUPS = 2
NSA_HPG = NSA_HEADS // NSA_GROUPS
CMP_LEN = 32
CMP_STRIDE = 16
CMP_HIDDEN = 256
SLC_LEN = 64
SLC_TOPK = 16
WIN = 512

PEER_HEADS = 8
PEER_NKEYS = 128
PEER_QDIM = 256
PEER_TOPK = 16

NEG = -1e30
BIG = 1e9

J_AQ, J_AK, J_AV, J_AZ = 0, 8, 16, 24
J_BQ = 32
N_NSA_BLOCKS = 28
J_MA, J_MB = 60, 76
J_SMALL = 92
J_GATE = 93
NJ = 96


def _cparams(sem):
    return pltpu.CompilerParams(dimension_semantics=sem, vmem_limit_bytes=VMEM_LIMIT)


def _bdot(a, b):
    return jnp.dot(a.astype(BF16), b.astype(BF16), preferred_element_type=F32)


def _bdot_nt(a, b):
    return lax.dot_general(a.astype(BF16), b.astype(BF16), (((1,), (1,)), ((), ())),
                           preferred_element_type=F32)


def _split_bf16(a):
    hi = a.astype(BF16)
    return hi, (a - hi.astype(F32)).astype(BF16)


def _dot3(a, b, exact_a=False):
    bh, bl = _split_bf16(b)
    dot = functools.partial(jnp.dot, preferred_element_type=F32)
    if exact_a:
        return dot(a, bh) + dot(a, bl)
    ah, al = _split_bf16(a)
    return dot(ah, bh) + dot(al, bh) + dot(ah, bl)


def _dot3_nt(a, b):
    ah, al = _split_bf16(a)
    bh, bl = _split_bf16(b)
    dot = functools.partial(lax.dot_general, dimension_numbers=(((1,), (1,)), ((), ())),
                            preferred_element_type=F32)
    return dot(ah, bh) + dot(al, bh) + dot(ah, bl)


def _fdot(a, b):
    return jnp.dot(a, b, precision=HI, preferred_element_type=F32)


def _fdot_nt(a, b):
    return lax.dot_general(a, b, (((1,), (1,)), ((), ())), precision=HI,
                           preferred_element_type=F32)


def _sigmoid(x):
    return 1.0 / (1.0 + jnp.exp(-x))


def _silu(x):
    return x * _sigmoid(x)


def _gelu(x):
    return 0.5 * x * (1.0 + jnp.tanh(math.sqrt(2.0 / math.pi) * (x + 0.044715 * (x * x * x))))


def _softplus(x):
    return jnp.maximum(x, 0.0) + jnp.log(1.0 + jnp.exp(-jnp.abs(x)))


def _ada_kernel(c_ref, w_ref, b_ref, o_ref):
    o_ref[...] = _fdot(_silu(c_ref[...]), w_ref[...]) + b_ref[...]


def _ada(c_pad, ada_w, ada_b):
    m, d = c_pad.shape
    n = ada_w.shape[1]
    tn = 1024
    return pl.pallas_call(
        _ada_kernel,
        grid=(n // tn,),
        in_specs=[pl.BlockSpec((m, d), lambda j: (0, 0)),
                  pl.BlockSpec((d, tn), lambda j: (0, j)),
                  pl.BlockSpec((1, tn), lambda j: (0, j))],
        out_specs=pl.BlockSpec((m, tn), lambda j: (0, j)),
        out_shape=jax.ShapeDtypeStruct((m, n), F32),
        compiler_params=_cparams(("parallel",)),
        name="ada_mod",
    )(c_pad, ada_w, ada_b.reshape(1, n))


def _normproj_kernel(x_ref, nw_ref, sc_ref, sh_ref, w_ref, *rest, nk, emit_h):
    if emit_h:
        o_ref, h_out_ref, h_scr = rest
    else:
        o_ref, h_scr = rest

    @pl.when(pl.program_id(2) == 0)
    def _():
        x = x_ref[0]
        ms = jnp.mean(x * x, axis=-1, keepdims=True)
        y = x * lax.rsqrt(ms + EPS) * nw_ref[...]
        h = y * (1.0 + sc_ref[0]) + sh_ref[0]
        h_scr[...] = h.astype(BF16)
        if emit_h:
            h_out_ref[0] = h

    acc = jnp.dot(h_scr[...], w_ref[...], preferred_element_type=F32)
    for k in range(nk):
        o_ref[0, k] = acc[:, k * LANES:(k + 1) * LANES]


def _normproj(x, nw, sc, sh, w_bf16, *, tm, tn, emit_h):
    b, s, d = x.shape
    n = w_bf16.shape[1]
    tm = min(tm, s)
    nk = tn // LANES
    out_shape = [jax.ShapeDtypeStruct((b, n // LANES, s, LANES), F32)]
    out_specs = [pl.BlockSpec((1, nk, tm, LANES), lambda bi, i, j: (bi, j, i, 0))]
    if emit_h:
        out_shape.append(jax.ShapeDtypeStruct((b, s, d), F32))
        out_specs.append(pl.BlockSpec((1, tm, d), lambda bi, i, j: (bi, i, 0)))
    res = pl.pallas_call(
        functools.partial(_normproj_kernel, nk=nk, emit_h=emit_h),
        grid=(b, s // tm, n // tn),
        in_specs=[pl.BlockSpec((1, tm, d), lambda bi, i, j: (bi, i, 0)),
                  pl.BlockSpec((1, d), lambda bi, i, j: (0, 0)),
                  pl.BlockSpec((1, 1, d), lambda bi, i, j: (bi, 0, 0)),
                  pl.BlockSpec((1, 1, d), lambda bi, i, j: (bi, 0, 0)),
                  pl.BlockSpec((d, tn), lambda bi, i, j: (0, j))],
        out_specs=out_specs,
        out_shape=out_shape,
        scratch_shapes=[pltpu.VMEM((tm, d), BF16)],
        compiler_params=_cparams(("parallel", "parallel", "arbitrary")),
        name="normproj_h" if emit_h else "normproj",
    )(x, nw.reshape(1, d), sc.reshape(b, 1, d), sh.reshape(b, 1, d), w_bf16)
    return res if emit_h else res[0]


GDN_CB = 512
GDN_GB = 256


def _gdn_local_kernel(q_ref, qh_ref, k_ref, kh_ref, v_ref, vh_ref, sm_ref,
                      cwq_ref, cwk_ref, cwv_ref, alog_ref, dtb_ref,
                      w_ref, u0_ref, qk_ref, qg_ref, kd_ref, ge_ref):
    h = pl.program_id(1)
    i = pl.program_id(2)
    C, G = GDN_CHUNK, GDN_GB
    first = (i == 0)

    def conv_act(main_ref, halo_ref, cw_ref, r0):
        if r0 == 0:
            prev = jnp.where(first, 0.0, halo_ref[0, 0])
        else:
            prev = main_ref[0, 0, r0 - 8:r0, :]
        ext = jnp.concatenate([prev, main_ref[0, 0, r0:r0 + G, :]], axis=0)
        w = cw_ref[0]
        y = w[0:1, :] * ext[5:5 + G, :]
        for j in range(1, GDN_CONV):
            y = y + w[j:j + 1, :] * ext[5 + j:5 + j + G, :]
        return _silu(y)

    rid = lax.broadcasted_iota(jnp.int32, (G, G), 0)
    cid = lax.broadcasted_iota(jnp.int32, (G, G), 1)
    same = (rid // C) == (cid // C)
    incl = jnp.logical_and(same, rid >= cid)
    strict = jnp.logical_and(same, rid > cid)
    eye = rid == cid
    is_last = cid == (rid // C) * C + (C - 1)
    joins = []
    bs = 1
    while bs < C:
        joins.append(jnp.logical_and(jnp.logical_and((rid // (2 * bs)) == (cid // (2 * bs)),
                                                     (rid & (2 * bs - 1)) >= bs),
                                     (cid & (2 * bs - 1)) < bs))
        bs *= 2
    tri = jnp.where(incl, 1.0, 0.0).astype(BF16)
    eye_f = jnp.where(eye, 1.0, 0.0).astype(F32)
    lane = lax.broadcasted_iota(jnp.int32, (G, LANES), 1)
    neg_a = -jnp.exp(alog_ref[...])
    dtb = dtb_ref[...]

    for grp in range(GDN_CB // G):
        r0 = grp * G
        q = conv_act(q_ref, qh_ref, cwq_ref, r0)
        k = conv_act(k_ref, kh_ref, cwk_ref, r0)
        v = conv_act(v_ref, vh_ref, cwv_ref, r0)
        q = q * lax.rsqrt(jnp.sum(q * q, axis=-1, keepdims=True) + EPS) * (HEAD_DIM ** -0.5)
        k = k * lax.rsqrt(jnp.sum(k * k, axis=-1, keepdims=True) + EPS)

        sm = sm_ref[0, 0, r0:r0 + G, :]
        g_all = neg_a * _softplus(sm + dtb)
        gc_all = _dot3(tri, g_all, exact_a=True)
        gc = jnp.sum(jnp.where(lane == h, gc_all, 0.0), axis=1, keepdims=True)
        beta = jnp.sum(jnp.where(lane == GDN_HEADS + h, _sigmoid(sm), 0.0), axis=1, keepdims=True)
        gc_row = jnp.sum(jnp.where(eye, gc, 0.0), axis=0, keepdims=True)
        gc_last = jnp.sum(jnp.where(is_last, gc_row, 0.0), axis=1, keepdims=True)
        decay = jnp.where(incl, jnp.exp(jnp.where(incl, gc - gc_row, 0.0)), 0.0)
        gamma = jnp.exp(gc)

        kk = _dot3_nt(k, k)
        lmat = jnp.where(strict, decay * kk, 0.0) * beta
        tinv = eye_f - jnp.where(joins[0], lmat, 0.0)
        for lvl in range(1, len(joins)):
            tinv = tinv - _dot3(_dot3(tinv, jnp.where(joins[lvl], lmat, 0.0)), tinv)
        rhs = jnp.concatenate([(beta * gamma) * k, beta * v], axis=1)
        wu = _dot3(tinv, rhs)
        qk = decay * _bdot_nt(q, k)

        rows = slice(r0, r0 + G)
        w_ref[0, 0, rows, :] = wu[:, :HEAD_DIM].astype(BF16)
        u0_ref[0, 0, rows, :] = wu[:, HEAD_DIM:]
        qg_ref[0, 0, rows, :] = (gamma * q).astype(BF16)
        kd_ref[0, 0, rows, :] = (jnp.exp(gc_last - gc) * k).astype(BF16)
        ge_all = jnp.broadcast_to(jnp.exp(gc_last), (G, LANES))
        for c in range(G // C):
            cr = slice(c * C, (c + 1) * C)
            qk_ref[0, 0, r0 + c * C:r0 + (c + 1) * C, :] = qk[cr, cr].astype(BF16)
            ge_ref[0, 0, pl.ds((i * (GDN_CB // G) + grp) * (G // C) + c, 1), :] = ge_all[c * C:c * C + 1, :]


def _gdn_local(P, cw, alog_pad, dtb_pad):
    b, _, s, _ = P.shape
    H, CB, C = GDN_HEADS, GDN_CB, GDN_CHUNK
    n = s // C

    def main(j0):
        return pl.BlockSpec((1, 1, CB, LANES), lambda bi, h, i: (bi, j0 + h, i, 0))

    def halo(j0):
        return pl.BlockSpec((1, 1, 8, LANES),
                            lambda bi, h, i: (bi, j0 + h, jnp.maximum(i * (CB // 8) - 1, 0), 0))

    def cws(j0):
        return pl.BlockSpec((1, GDN_CONV, LANES), lambda bi, h, i: (j0 + h, 0, 0))

    row = pl.BlockSpec((1, LANES), lambda bi, h, i: (0, 0))
    hs = lambda width: pl.BlockSpec((1, 1, CB, width), lambda bi, h, i: (bi, h, i, 0))
    return pl.pallas_call(
        _gdn_local_kernel,
        grid=(b, H, s // CB),
        in_specs=[main(J_AQ), halo(J_AQ), main(J_AK), halo(J_AK), main(J_AV), halo(J_AV),
                  pl.BlockSpec((1, 1, CB, LANES), lambda bi, h, i: (bi, J_SMALL, i, 0)),
                  cws(0), cws(8), cws(16), row, row],
        out_specs=[hs(LANES), hs(LANES), hs(C), hs(LANES), hs(LANES),
                   pl.BlockSpec((1, 1, n, LANES), lambda bi, h, i: (bi, h, 0, 0))],
        out_shape=[jax.ShapeDtypeStruct((b, H, s, LANES), BF16),
                   jax.ShapeDtypeStruct((b, H, s, LANES), F32),
                   jax.ShapeDtypeStruct((b, H, s, C), BF16),
                   jax.ShapeDtypeStruct((b, H, s, LANES), BF16),
                   jax.ShapeDtypeStruct((b, H, s, LANES), BF16),
                   jax.ShapeDtypeStruct((b, H, n, LANES), F32)],
        compiler_params=_cparams(("parallel", "parallel", "arbitrary")),
        name="gdn_local",
    )(P, P, P, P, P, P, P, cw, cw, cw, alog_pad, dtb_pad)


GDN_HB = 4


def _gdn_scan_kernel(w_ref, u0_ref, qk_ref, qg_ref, kd_ref, ge_ref, z_ref, nw_ref, o_ref, s_scr,
                     *, sb):
    C = GDN_CHUNK

    @pl.when(pl.program_id(2) == 0)
    def _():
        s_scr[...] = jnp.zeros_like(s_scr)

    nw = nw_ref[...]

    def body(n, carry):
        r = pl.multiple_of(n * C, C)
        for hh in range(GDN_HB):
            st = s_scr[hh]
            stb = st.astype(BF16)
            u = u0_ref[0, hh, pl.ds(r, C), :] - jnp.dot(w_ref[0, hh, pl.ds(r, C), :], stb,
                                                       preferred_element_type=F32)
            ub = u.astype(BF16)
            o = (jnp.dot(qg_ref[0, hh, pl.ds(r, C), :], stb, preferred_element_type=F32)
                 + jnp.dot(qk_ref[0, hh, pl.ds(r, C), :], ub, preferred_element_type=F32))
            ge = ge_ref[0, hh, pl.ds(n, 1), :]
            s_scr[hh] = ge * st + lax.dot_general(kd_ref[0, hh, pl.ds(r, C), :], ub,
                                                  (((0,), (0,)), ((), ())),
                                                  preferred_element_type=F32)
            on = o * lax.rsqrt(jnp.mean(o * o, axis=-1, keepdims=True) + EPS) * nw
            z = z_ref[0, hh, pl.ds(r, C), :]
            o_ref[0, pl.ds(r, C), hh * LANES:(hh + 1) * LANES] = (on * _silu(z)).astype(BF16)
        return carry

    lax.fori_loop(0, sb // C, body, 0)


def _gdn_scan(w, u0, qk, qg, kd, ge, P, norm_w):
    b, H, s, _ = w.shape
    C, HB = GDN_CHUNK, GDN_HB
    sb = min(1024, s)
    hs = lambda width: pl.BlockSpec((1, HB, sb, width), lambda bi, hb, i: (bi, hb, i, 0))
    return pl.pallas_call(
        functools.partial(_gdn_scan_kernel, sb=sb),
        grid=(b, H // HB, s // sb),
        in_specs=[hs(LANES), hs(LANES), hs(C), hs(LANES), hs(LANES),
                  pl.BlockSpec((1, HB, sb // C, LANES), lambda bi, hb, i: (bi, hb, i, 0)),
                  pl.BlockSpec((1, HB, sb, LANES), lambda bi, hb, i: (bi, J_AZ // HB + hb, i, 0)),
                  pl.BlockSpec((1, LANES), lambda bi, hb, i: (0, 0))],
        out_specs=pl.BlockSpec((1, sb, HB * LANES), lambda bi, hb, i: (bi, i, hb)),
        out_shape=jax.ShapeDtypeStruct((b, s, H * LANES), BF16),
        scratch_shapes=[pltpu.VMEM((HB, HEAD_DIM, HEAD_DIM), F32)],
        compiler_params=_cparams(("parallel", "parallel", "arbitrary")),
        name="gdn_scan",
    )(w, u0, qk, qg, kd, ge, P, norm_w.reshape(1, LANES))


def _pad_lanes_row(v):
    return jnp.pad(v.astype(F32), (0, LANES - v.shape[0])).reshape(1, LANES)


def _gdn(P, conv_w, a_log, dt_bias, norm_w):
    cw = jnp.transpose(conv_w.reshape(GDN_CONV, 3 * GDN_HEADS, LANES), (1, 0, 2))
    w, u0, qk, qg, kd, ge = _gdn_local(P, cw, _pad_lanes_row(a_log), _pad_lanes_row(dt_bias))
    return _gdn_scan(w, u0, qk, qg, kd, ge, P, norm_w)


NSA_TQ = 128
NSA_TK_SEL = 512
NSA_TK_WIN = 256
NSA_HC = 2
R_Q, R_KC, R_VC, R_KS, R_VS, R_KW, R_VW = 0, 16, 18, 20, 22, 24, 26
ROPE_NB = 4


def _rope_kernel(x_ref, pos_ref, invf_ref, o_ref, cos_scr, sin_scr):
    j = pl.program_id(2)
    lane = lax.broadcasted_iota(jnp.int32, cos_scr.shape, 1)

    @pl.when(j == 0)
    def _():
        ang = pos_ref[0] * invf_ref[...]
        sn = jnp.sin(ang)
        cos_scr[...] = jnp.where(lane < ROPE_DIM, jnp.cos(ang), 1.0)
        sin_scr[...] = jnp.where(lane < ROPE_HALF, -sn, jnp.where(lane < ROPE_DIM, sn, 0.0))

    is_q = j < NSA_HEADS // ROPE_NB
    scale = jnp.where(is_q, HEAD_DIM ** -0.5, 1.0)
    for k in range(ROPE_NB):
        x = x_ref[0, k]
        swapped = jnp.where(lane < ROPE_HALF, pltpu.roll(x, LANES - ROPE_HALF, axis=1),
                            pltpu.roll(x, ROPE_HALF, axis=1))
        rot = x * cos_scr[...] + swapped * sin_scr[...]
        out = rot * scale if k < 2 else jnp.where(is_q, rot, x) * scale
        o_ref[0, k] = out.astype(BF16)


def _rope(P, pos_f32, invf):
    b, _, s, _ = P.shape
    tr = min(512, s)
    nb = ROPE_NB
    return pl.pallas_call(
        _rope_kernel,
        grid=(b, s // tr, N_NSA_BLOCKS // nb),
        in_specs=[pl.BlockSpec((1, nb, tr, LANES), lambda bi, i, j: (bi, J_BQ // nb + j, i, 0)),
                  pl.BlockSpec((1, tr, 1), lambda bi, i, j: (bi, i, 0)),
                  pl.BlockSpec((1, LANES), lambda bi, i, j: (0, 0))],
        out_specs=pl.BlockSpec((1, nb, tr, LANES), lambda bi, i, j: (bi, j, i, 0)),
        out_shape=jax.ShapeDtypeStruct((b, N_NSA_BLOCKS, s, LANES), BF16),
        scratch_shapes=[pltpu.VMEM((tr, LANES), F32), pltpu.VMEM((tr, LANES), F32)],
        compiler_params=_cparams(("parallel", "parallel", "arbitrary")),
        name="nsa_rope",
    )(P, pos_f32, invf)


def _compress_kernel(r_ref, w1_ref, w2_ref, pos_ref, o_ref):
    r = r_ref[0, 0]
    nr = r.shape[0]
    half = CMP_STRIDE * HEAD_DIM
    a = jnp.dot(r, w1_ref[0, :half, :], preferred_element_type=F32)
    bm = jnp.dot(r, w1_ref[0, half:, :], preferred_element_type=F32)
    pos8 = jnp.broadcast_to(pos_ref[0], (8, CMP_LEN * HEAD_DIM)).astype(BF16)
    pb = jnp.dot(pos8, w1_ref[0], preferred_element_type=F32)[0:1, :]
    hid = a + pltpu.roll(bm, nr - 1, axis=0) + pb
    out = jnp.dot(_gelu(hid).astype(BF16), w2_ref[0], preferred_element_type=F32)
    row = lax.broadcasted_iota(jnp.int32, out.shape, 0)
    o_ref[0, 0] = jnp.where(row < nr - 1, out, 0.0).astype(BF16)


def _compress(rows, w1, w2, pos):
    b, _, nr, width = rows.shape
    return pl.pallas_call(
        _compress_kernel,
        grid=(b, 4),
        in_specs=[pl.BlockSpec((1, 1, nr, width), lambda bi, j: (bi, j, 0, 0)),
                  pl.BlockSpec((1, CMP_LEN * HEAD_DIM, CMP_HIDDEN), lambda bi, j: (j // 2, 0, 0)),
                  pl.BlockSpec((1, CMP_HIDDEN, HEAD_DIM), lambda bi, j: (j // 2, 0, 0)),
                  pl.BlockSpec((1, 1, CMP_LEN * HEAD_DIM), lambda bi, j: (j // 2, 0, 0))],
        out_specs=pl.BlockSpec((1, 1, nr, HEAD_DIM), lambda bi, j: (bi, j, 0, 0)),
        out_shape=jax.ShapeDtypeStruct((b, 4, nr, HEAD_DIM), BF16),
        compiler_params=_cparams(("parallel", "arbitrary")),
        name="nsa_compress",
    )(rows, w1, w2, pos)


def _nsa_kernel(q_ref, kc_ref, vc_ref, ks_ref, vs_ref, kw_ref, vw_ref, gate_ref, ovt_ref, e_ref,
                o_ref, m_scr, l_scr, acc_scr, os_scr, *, s_len):
    i = pl.program_id(2)
    TQ, HP = NSA_TQ, NSA_HPG
    R = HP * TQ
    nsb = s_len // SLC_LEN
    ncp = s_len // CMP_STRIDE
    n_sel = min(SLC_TOPK, nsb)
    s0 = i * TQ
    q2 = q_ref[0].reshape(R, HEAD_DIM)

    sc = _bdot_nt(q2, kc_ref[0, 0])
    tq_r = s0 + (lax.broadcasted_iota(jnp.int32, (R, ncp), 0) & (TQ - 1))
    ncol = lax.broadcasted_iota(jnp.int32, (R, ncp), 1)
    valid = jnp.logical_and(ncol * CMP_STRIDE + (CMP_LEN - 1) <= tq_r, ncol < ncp - 1)
    scm = jnp.where(valid, sc, NEG)
    e = jnp.where(valid, jnp.exp(scm - jnp.max(scm, axis=1, keepdims=True)), 0.0)
    p_c = e / jnp.maximum(jnp.sum(e, axis=1, keepdims=True), 1e-30)
    o_c = _bdot(p_c, vc_ref[0, 0])
    psum = jnp.sum(p_c.reshape(HP, TQ, ncp), axis=0)
    imp = _fdot_nt(ovt_ref[...], psum)
    blk = lax.broadcasted_iota(jnp.int32, (nsb, TQ), 0)
    tq_l = s0 + lax.broadcasted_iota(jnp.int32, (nsb, TQ), 1)
    cur = tq_l // SLC_LEN
    forced = jnp.logical_or(blk == 0, jnp.logical_or(blk == cur, blk == cur - 1))
    imp = jnp.where(forced, BIG, imp)
    imp = jnp.where(blk * SLC_LEN <= tq_l, imp, NEG)
    rank = jnp.zeros((nsb, TQ), jnp.int32)
    for j in range(nsb):
        rj = imp[j:j + 1, :]
        beats = jnp.logical_or(rj > imp, jnp.logical_and(rj == imp, blk > j))
        rank = rank + jnp.where(beats, 1, 0)
    sel_t = jnp.where(rank < n_sel, 1.0, 0.0).astype(F32)
    if nsb < LANES:
        sel_t = jnp.concatenate([sel_t, jnp.zeros((LANES - nsb, TQ), F32)], axis=0)
    sel = jnp.transpose(sel_t).astype(BF16)

    ones_blk = jnp.ones((TQ, HEAD_DIM), BF16)
    HC = NSA_HC

    def flash(k_ref, v_ref, tk, lo, hi, mask_fn):
        m_scr[...] = jnp.full(m_scr.shape, NEG, F32)
        l_scr[...] = jnp.zeros(l_scr.shape, F32)
        acc_scr[...] = jnp.zeros(acc_scr.shape, F32)
        qpos = s0 + lax.broadcasted_iota(jnp.int32, (TQ, tk), 0)
        kofs = lax.broadcasted_iota(jnp.int32, (TQ, tk), 1)
        ones_v = jnp.concatenate([ones_blk] * (tk // TQ), axis=0)

        def body(j, carry):
            r = pl.multiple_of(j * tk, tk)
            bias = jnp.where(mask_fn(r, r + kofs, qpos), 0.0, NEG)
            kb = k_ref[0, 0, pl.ds(r, tk), :]
            vb = jnp.concatenate([v_ref[0, 0, pl.ds(r, tk), :], ones_v], axis=1)
            for c in range(HP // HC):
                rows = slice(c * HC * TQ, (c + 1) * HC * TQ)
                s = _bdot_nt(q_ref[0, c * HC:(c + 1) * HC].reshape(HC * TQ, HEAD_DIM), kb)
                s = (s.reshape(HC, TQ, tk) + bias[None]).reshape(HC * TQ, tk)
                m_prev = m_scr[rows, :]
                m_new = jnp.maximum(m_prev, jnp.max(s, axis=1, keepdims=True))
                alpha = jnp.exp(m_prev - m_new)
                p = jnp.exp(s - jnp.concatenate([m_new] * (tk // LANES), axis=1)).astype(BF16)
                pv = jnp.dot(p, vb, preferred_element_type=F32)
                l_scr[rows, :] = alpha * l_scr[rows, :] + pv[:, HEAD_DIM:]
                acc_scr[rows, :] = alpha * acc_scr[rows, :] + pv[:, :HEAD_DIM]
                m_scr[rows, :] = m_new
            return carry

        lax.fori_loop(lo, hi, body, 0)

    tks = min(NSA_TK_SEL, s_len)

    def sel_mask(r, kpos, qpos):
        chosen = jnp.dot(sel, e_ref[:, pl.ds(r, tks)], preferred_element_type=F32)
        return jnp.logical_and(chosen > 0.5, kpos <= qpos)

    flash(ks_ref, vs_ref, tks, 0, (s0 + TQ - 1) // tks + 1, sel_mask)
    os_scr[...] = acc_scr[...] / l_scr[...]

    def win_mask(r, kpos, qpos):
        return jnp.logical_and(kpos <= qpos, kpos > qpos - WIN)

    tkw = min(NSA_TK_WIN, s_len)
    flash(kw_ref, vw_ref, tkw, jnp.maximum(s0 - WIN + 1, 0) // tkw, (s0 + TQ - 1) // tkw + 1, win_mask)

    gates = _sigmoid(gate_ref[0, 0])
    for hh in range(HP):
        rows = slice(hh * TQ, (hh + 1) * TQ)
        o_w = acc_scr[rows, :] / l_scr[rows, :]
        out = (gates[:, 3 * hh:3 * hh + 1] * o_c[rows, :]
               + gates[:, 3 * hh + 1:3 * hh + 2] * os_scr[rows, :]
               + gates[:, 3 * hh + 2:3 * hh + 3] * o_w)
        o_ref[0, :, hh * HEAD_DIM:(hh + 1) * HEAD_DIM] = out.astype(BF16)


def _nsa_attend(rp, cmp_kv, P, ovt, expand):
    b, _, s, _ = rp.shape
    TQ, HP, G = NSA_TQ, NSA_HPG, NSA_GROUPS
    ncp = cmp_kv.shape[2]
    R = HP * TQ
    full = lambda j0: pl.BlockSpec((1, 1, s, HEAD_DIM), lambda bi, g, i: (bi, j0 + g, 0, 0))
    cmp_spec = lambda j0: pl.BlockSpec((1, 1, ncp, HEAD_DIM), lambda bi, g, i: (bi, j0 + g, 0, 0))
    return pl.pallas_call(
        functools.partial(_nsa_kernel, s_len=s),
        grid=(b, G, s // TQ),
        in_specs=[pl.BlockSpec((1, HP, TQ, HEAD_DIM), lambda bi, g, i: (bi, g, i, 0)),
                  cmp_spec(0), cmp_spec(2),
                  full(R_KS), full(R_VS), full(R_KW), full(R_VW),
                  pl.BlockSpec((1, 1, TQ, LANES), lambda bi, g, i: (bi, J_GATE + g, i, 0)),
                  pl.BlockSpec(ovt.shape, lambda bi, g, i: (0, 0)),
                  pl.BlockSpec(expand.shape, lambda bi, g, i: (0, 0))],
        out_specs=pl.BlockSpec((1, TQ, HP * HEAD_DIM), lambda bi, g, i: (bi, i, g)),
        out_shape=jax.ShapeDtypeStruct((b, s, NSA_HEADS * HEAD_DIM), BF16),
        scratch_shapes=[pltpu.VMEM((R, LANES), F32), pltpu.VMEM((R, LANES), F32),
                        pltpu.VMEM((R, HEAD_DIM), F32), pltpu.VMEM((R, HEAD_DIM), F32)],
        compiler_params=_cparams(("parallel", "parallel", "arbitrary")),
        name="nsa_attend",
    )(rp, cmp_kv, cmp_kv, rp, rp, rp, rp, P, ovt, expand)


def _nsa(P, positions, cmp_pos_k, cmp_w1_k, cmp_w2_k, cmp_pos_v, cmp_w1_v, cmp_w2_v):
    b, _, s, _ = P.shape
    lanes = np.arange(LANES)
    invf = np.where(lanes < ROPE_DIM, ROPE_THETA ** (-(lanes % ROPE_HALF) / ROPE_HALF), 0.0)
    rp = _rope(P, positions.astype(F32).reshape(b, s, 1), jnp.asarray(invf, F32).reshape(1, LANES))

    ncp = s // CMP_STRIDE
    rows = rp[:, R_KC:R_KC + 4].reshape(b, 4, ncp, CMP_STRIDE * HEAD_DIM)
    w1 = jnp.stack([cmp_w1_k, cmp_w1_v]).astype(BF16)
    w2 = jnp.stack([cmp_w2_k, cmp_w2_v]).astype(BF16)
    pos = jnp.stack([cmp_pos_k, cmp_pos_v]).reshape(2, 1, CMP_LEN * HEAD_DIM)
    cmp_kv = _compress(rows, w1, w2, pos)

    nsb = s // SLC_LEN
    cmp_start = np.arange(ncp) * CMP_STRIDE
    slc_start = np.arange(nsb) * SLC_LEN
    ov = ((cmp_start[:, None] < slc_start[None, :] + SLC_LEN)
          & (cmp_start[:, None] + CMP_LEN > slc_start[None, :])
          & (np.arange(ncp)[:, None] < ncp - 1))
    ovt = jnp.asarray(ov.T.astype(np.float32))
    expand = (np.arange(LANES)[:, None] == (np.arange(s)[None, :] // SLC_LEN))
    expand = jnp.asarray(expand.astype(np.float32), BF16)
    return _nsa_attend(rp, cmp_kv, P, ovt, expand)


MERGE_TN = 512


def _merge_kernel(oa_ref, ob_ref, wg_ref, wn_ref, ma_ref, mb_ref, y_ref):
    ya = jnp.dot(oa_ref[0], wg_ref[...], preferred_element_type=F32)
    yb = jnp.dot(ob_ref[0], wn_ref[...], preferred_element_type=F32)
    for k in range(MERGE_TN // LANES):
        cols = slice(k * LANES, (k + 1) * LANES)
        y = _sigmoid(ma_ref[0, k]) * ya[:, cols] + _sigmoid(mb_ref[0, k]) * yb[:, cols]
        y_ref[0, :, cols] = y.astype(BF16)


def _merge(o_a, o_b, wg, wn, P):
    b, s, da = o_a.shape
    db = o_b.shape[2]
    d = wg.shape[1]
    tm, tn = min(512, s), MERGE_TN
    nk = tn // LANES
    return pl.pallas_call(
        _merge_kernel,
        grid=(b, s // tm, d // tn),
        in_specs=[pl.BlockSpec((1, tm, da), lambda bi, i, j: (bi, i, 0)),
                  pl.BlockSpec((1, tm, db), lambda bi, i, j: (bi, i, 0)),
                  pl.BlockSpec((da, tn), lambda bi, i, j: (0, j)),
                  pl.BlockSpec((db, tn), lambda bi, i, j: (0, j)),
                  pl.BlockSpec((1, nk, tm, LANES), lambda bi, i, j: (bi, J_MA // nk + j, i, 0)),
                  pl.BlockSpec((1, nk, tm, LANES), lambda bi, i, j: (bi, J_MB // nk + j, i, 0))],
        out_specs=pl.BlockSpec((1, tm, tn), lambda bi, i, j: (bi, i, j)),
        out_shape=jax.ShapeDtypeStruct((b, s, d), BF16),
        compiler_params=_cparams(("parallel", "parallel", "arbitrary")),
        name="merge",
    )(o_a, o_b, wg, wn, P, P)


def _outproj_kernel(y_ref, w_ref, x_ref, g_ref, o_ref):
    o_ref[0] = x_ref[0] + g_ref[0] * jnp.dot(y_ref[0], w_ref[...], preferred_element_type=F32)


def _outproj(y, w, x, g):
    b, s, d = x.shape
    tm, tn = min(512, s), 512
    return pl.pallas_call(
        _outproj_kernel,
        grid=(b, s // tm, d // tn),
        in_specs=[pl.BlockSpec((1, tm, d), lambda bi, i, j: (bi, i, 0)),
                  pl.BlockSpec((d, tn), lambda bi, i, j: (0, j)),
                  pl.BlockSpec((1, tm, tn), lambda bi, i, j: (bi, i, j)),
                  pl.BlockSpec((1, 1, tn), lambda bi, i, j: (bi, 0, j))],
        out_specs=pl.BlockSpec((1, tm, tn), lambda bi, i, j: (bi, i, j)),
        out_shape=jax.ShapeDtypeStruct((b, s, d), F32),
        compiler_params=_cparams(("parallel", "parallel", "arbitrary")),
        name="outproj",
    )(y, w, x, g.reshape(b, 1, d))


PEER_TB = 256
PEER_TG = 64
PEER_AHEAD = 4
PEER_SLOTS = 8
PEER_SEL = PEER_HEADS * PEER_TOPK
MIN_F32 = -3.0e38


def _topk_rows(vals, payload=None):
    nrow = vals.shape[0]
    rid = lax.broadcasted_iota(jnp.int32, vals.shape, 0)
    out_v, out_i = [], []
    for _ in range(PEER_TOPK):
        m = jnp.max(vals, axis=0, keepdims=True)
        idx = jnp.min(jnp.where(vals == m, rid, nrow), axis=0, keepdims=True)
        hit = rid == idx
        out_v.append(m)
        if payload is None:
            out_i.append(idx)
        else:
            out_i.append(jnp.sum(jnp.where(hit, payload, 0), axis=0, keepdims=True))
        vals = jnp.where(hit, MIN_F32, vals)
    return jnp.concatenate(out_v, axis=0), jnp.concatenate(out_i, axis=0)


def _peer_topk_kernel(qh_ref, k1_ref, k2_ref, eidx_ref, gw_ref):
    def head(h, carry):
        s1 = _fdot_nt(k1_ref[h], qh_ref[0, 2 * h])
        s2 = _fdot_nt(k2_ref[h], qh_ref[0, 2 * h + 1])
        v1, i1 = _topk_rows(s1)
        v2, i2 = _topk_rows(s2)
        keep = [PEER_TOPK // (a + 1) for a in range(PEER_TOPK)]
        npad = (-sum(keep)) % 8
        cand = jnp.concatenate([v1[a:a + 1, :] + v2[0:keep[a], :] for a in range(PEER_TOPK)]
                               + [jnp.full((npad, v1.shape[1]), MIN_F32, F32)], axis=0)
        cidx = jnp.concatenate([i1[a:a + 1, :] * PEER_NKEYS + i2[0:keep[a], :] for a in range(PEER_TOPK)]
                               + [jnp.zeros((npad, v1.shape[1]), jnp.int32)], axis=0)
        sc, eidx = _topk_rows(cand, cidx)
        ex = jnp.exp(sc - sc[0:1, :])
        eidx_ref[h] = eidx
        gw_ref[h] = ex / jnp.sum(ex, axis=0, keepdims=True)
        return carry

    lax.fori_loop(0, PEER_HEADS, head, 0)


def _peer_topk(qh, keys1, keys2):
    b, nb, s, _ = qh.shape
    tb = min(PEER_TB, s)
    nt = s // tb
    kspec = pl.BlockSpec(keys1.shape, lambda bi, i: (0, 0, 0))
    ospec = pl.BlockSpec((PEER_HEADS, PEER_TOPK, tb), lambda bi, i: (0, 0, bi * nt + i))
    return pl.pallas_call(
        _peer_topk_kernel,
        grid=(b, nt),
        in_specs=[pl.BlockSpec((1, nb, tb, LANES), lambda bi, i: (bi, 0, i, 0)), kspec, kspec],
        out_specs=[ospec, ospec],
        out_shape=[jax.ShapeDtypeStruct((PEER_HEADS, PEER_TOPK, b * s), jnp.int32),
                   jax.ShapeDtypeStruct((PEER_HEADS, PEER_TOPK, b * s), F32)],
        compiler_params=_cparams(("parallel", "parallel")),
        name="peer_topk",
    )(qh, keys1, keys2)


def _peer_gather_kernel(idx_ref, gw_ref, x_ref, uv_hbm, o_ref, buf, a_scr, c_scr, sem, *, tg):
    NS, D, NSLOT = PEER_SEL, PEER_AHEAD, PEER_SLOTS
    sub = x_ref.shape[1]
    rid = lax.broadcasted_iota(jnp.int32, (NS, NS), 0)
    cid = lax.broadcasted_iota(jnp.int32, (NS, NS), 1)
    eye = rid == cid
    ones = jnp.ones((LANES, LANES), BF16)

    def lane_sum_rep(m):
        hi, lo = _split_bf16(m)
        return (jnp.dot(hi, ones, preferred_element_type=F32)
                + jnp.dot(lo, ones, preferred_element_type=F32))

    def wait_rows(t):
        slot = t & (NSLOT - 1)
        pltpu.make_async_copy(uv_hbm.at[pl.ds(0, NS)], buf.at[slot], sem.at[slot]).wait()

    def phase(t, do_issue, do_dot, do_prev):
        if do_dot:
            wait_rows(t)
        if do_prev:
            act = lane_sum_rep(a_scr[(t - 1) & 1])
            gw_rep = lane_sum_rep(jnp.where(eye, gw_ref[pl.ds(t - 1, 1), :], 0.0))
            c_scr[...] = gw_rep * _gelu(act)
        if do_dot:
            x = x_ref[t]
            xl, xh = x[0:8, :], x[8:16, :]
            slot_d = t & (NSLOT - 1)
            par_d = t & 1
        if do_issue:
            slot_i = (t + D) & (NSLOT - 1)
            base_i = (t + D) * NS
        for j in range(NS):
            if do_issue:
                pltpu.make_async_copy(uv_hbm.at[idx_ref[base_i + j]], buf.at[slot_i, j],
                                      sem.at[slot_i]).start()
            if do_dot:
                uj = buf[slot_d, j, 0]
                a_scr[par_d, j:j + 1, :] = jnp.sum(uj[0:8, :] * xl + uj[8:16, :] * xh,
                                                   axis=0, keepdims=True)
        if do_prev:
            slot_s = (t - 1) & (NSLOT - 1)
            accs = [jnp.zeros((sub, LANES), F32) for _ in range(4)]
            for j in range(NS):
                accs[j % 4] = accs[j % 4] + c_scr[j:j + 1, :] * buf[slot_s, j, 1]
            o_ref[t - 1] = (accs[0] + accs[1]) + (accs[2] + accs[3])

    def ramp(t, carry):
        slot = t & (NSLOT - 1)

        def body(j, c):
            pltpu.make_async_copy(uv_hbm.at[idx_ref[t * NS + j]], buf.at[slot, j], sem.at[slot]).start()
            return c
        return lax.fori_loop(0, NS, body, carry)

    def looped(do_issue):
        def f(t, carry):
            phase(t, do_issue, True, True)
            return carry
        return f

    lax.fori_loop(0, D, ramp, 0)
    phase(0, True, True, False)
    lax.fori_loop(1, tg - D, looped(True), 0)
    lax.fori_loop(tg - D, tg, looped(False), 0)
    phase(tg, False, False, True)


def _peer_gather(eidx, gw, h2r, uv):
    t = eidx.shape[0]
    tg = min(PEER_TG, t)
    sub = h2r.shape[1]
    return pl.pallas_call(
        functools.partial(_peer_gather_kernel, tg=tg),
        grid=(t // tg,),
        in_specs=[pl.BlockSpec((tg * PEER_SEL,), lambda i: (i,), memory_space=pltpu.SMEM),
                  pl.BlockSpec((tg, PEER_SEL), lambda i: (i, 0)),
                  pl.BlockSpec((tg, sub, LANES), lambda i: (i, 0, 0)),
                  pl.BlockSpec(memory_space=pl.ANY)],
        out_specs=pl.BlockSpec((tg, sub, LANES), lambda i: (i, 0, 0)),
        out_shape=jax.ShapeDtypeStruct((t, sub, LANES), F32),
        scratch_shapes=[pltpu.VMEM((PEER_SLOTS, PEER_SEL, 2, sub, LANES), F32),
                        pltpu.VMEM((2, PEER_SEL, LANES), F32),
                        pltpu.VMEM((PEER_SEL, LANES), F32),
                        pltpu.SemaphoreType.DMA((PEER_SLOTS,))],
        compiler_params=_cparams(("arbitrary",)),
        name="peer_gather",
    )(eidx.reshape(-1), gw, h2r, uv)


def _peer(x1, norm_w, sc, sh, wq, keys1, keys2, u, v):
    b, s, d = x1.shape
    qh, h2 = _normproj(x1, norm_w, sc, sh, wq.astype(BF16), tm=512, tn=512, emit_h=True)
    eidx, gw = _peer_topk(qh, keys1, keys2)
    eidx = jnp.transpose(eidx, (2, 0, 1)).reshape(b * s, PEER_SEL)
    gw = jnp.transpose(gw, (2, 0, 1)).reshape(b * s, PEER_SEL)
    sub = d // LANES
    uv = jnp.stack([u.reshape(-1, sub, LANES), v.reshape(-1, sub, LANES)], axis=1)
    out = _peer_gather(eidx, gw, h2.reshape(b * s, sub, LANES), uv)
    return out.reshape(b, s, d)


def _final_kernel(x_ref, p_ref, g_ref, w_ref, o_ref):
    x = x_ref[0] + g_ref[0] * p_ref[0]
    o_ref[0] = x * lax.rsqrt(jnp.mean(x * x, axis=-1, keepdims=True) + EPS) * w_ref[...]


def _final(x1, peer, g2, wf):
    b, s, d = x1.shape
    tm = min(512, s)
    blk = pl.BlockSpec((1, tm, d), lambda bi, i: (bi, i, 0))
    return pl.pallas_call(
        _final_kernel,
        grid=(b, s // tm),
        in_specs=[blk, blk, pl.BlockSpec((1, 1, d), lambda bi, i: (bi, 0, 0)),
                  pl.BlockSpec((1, d), lambda bi, i: (0, 0))],
        out_specs=blk,
        out_shape=jax.ShapeDtypeStruct((b, s, d), F32),
        compiler_params=_cparams(("parallel", "parallel")),
        name="final_norm",
    )(x1, peer, g2.reshape(b, 1, d), wf.reshape(1, d))


def _pad_rows(a, mult=8):
    pad = (-a.shape[0]) % mult
    return jnp.pad(a, ((0, pad), (0, 0)))


def _permute_w_in(w):
    o_aa, o_bq, o_bg, o_ma, o_end = 4096, 4112, 7696, 7744, 11840
    d = w.shape[0]
    z = lambda n: jnp.zeros((d, n), w.dtype)
    hg = 3 * NSA_HPG
    cols = [w[:, 0:o_aa], w[:, o_bq:o_bg], w[:, o_ma:o_end],
            w[:, o_aa:o_bq], z(LANES - 16),
            w[:, o_bg:o_bg + hg], z(LANES - hg),
            w[:, o_bg + hg:o_ma], z(LANES - hg),
            z(LANES)]
    return jnp.concatenate(cols, axis=1).astype(BF16)


def kernel(x, c, positions, ada_w, ada_b, norm1_w, norm2_w, w_in, gdn_conv_w, gdn_A_log, gdn_dt_bias, gdn_norm_w, cmp_pos_k, cmp_w1_k, cmp_w2_k, cmp_pos_v, cmp_w1_v, cmp_w2_v, w_branch_gdn, w_branch_nsa, w_out, peer_wq, peer_keys1, peer_keys2, peer_u, peer_v, final_norm_w):
    b, s, d = x.shape
    l = 0
    mod = _ada(_pad_rows(c), ada_w[l], ada_b[l])[:b]
    sh1, sc1, g1, sh2, sc2, g2 = jnp.split(mod, 6, axis=-1)
    P = _normproj(x, norm1_w[l], sc1, sh1, _permute_w_in(w_in[l]), tm=1024, tn=512, emit_h=False)
    o_a = _gdn(P, gdn_conv_w[l], gdn_A_log[l], gdn_dt_bias[l], gdn_norm_w[l])
    o_b = _nsa(P, positions, cmp_pos_k[l], cmp_w1_k[l], cmp_w2_k[l], cmp_pos_v[l], cmp_w1_v[l], cmp_w2_v[l])
    y = _merge(o_a, o_b, w_branch_gdn[l].astype(BF16), w_branch_nsa[l].astype(BF16), P)
    x1 = _outproj(y, w_out[l].astype(BF16), x, g1)
    peer = _peer(x1, norm2_w[l], sc2, sh2, peer_wq[l], peer_keys1[l], peer_keys2[l], peer_u[l], peer_v[l])
    return _final(x1, peer, g2, final_norm_w)
```

```python
import functools
import math

import numpy as np
import jax
import jax.numpy as jnp
from jax import lax
from jax.experimental import pallas as pl
from jax.experimental.pallas import tpu as pltpu

F32 = jnp.float32
BF16 = jnp.bfloat16
HI = lax.Precision.HIGHEST

LANES = 128
VMEM_LIMIT = 56 * 1024 * 1024

EPS = 1e-6
ROPE_THETA = 500000.0
HEAD_DIM = 128
ROPE_DIM = HEAD_DIM // 4
ROPE_HALF = ROPE_DIM // 2

GDN_HEADS = 8
GDN_CONV = 4
GDN_CHUNK = 64

NSA_HEADS = 16
NSA_GROUPS = 2
NSA_HPG = NSA_HEADS // NSA_GROUPS
CMP_LEN = 32
CMP_STRIDE = 16
CMP_HIDDEN = 256
SLC_LEN = 64
SLC_TOPK = 16
WIN = 512

PEER_HEADS = 8
PEER_NKEYS = 128
PEER_QDIM = 256
PEER_TOPK = 16

NEG = -1e30
BIG = 1e9

J_AQ, J_AK, J_AV, J_AZ = 0, 8, 16, 24
J_BQ = 32
N_NSA_BLOCKS = 28
J_MA, J_MB = 60, 76
J_SMALL = 92
J_GATE = 93
NJ = 96


def _cparams(sem):
    return pltpu.CompilerParams(dimension_semantics=sem, vmem_limit_bytes=VMEM_LIMIT)


def _bdot(a, b):
    return jnp.dot(a.astype(BF16), b.astype(BF16), preferred_element_type=F32)


def _bdot_nt(a, b):
    return lax.dot_general(a.astype(BF16), b.astype(BF16), (((1,), (1,)), ((), ())),
                           preferred_element_type=F32)


def _split_bf16(a):
    hi = a.astype(BF16)
    return hi, (a - hi.astype(F32)).astype(BF16)


def _dot3(a, b, exact_a=False):
    bh, bl = _split_bf16(b)
    dot = functools.partial(jnp.dot, preferred_element_type=F32)
    if exact_a:
        return dot(a, bh) + dot(a, bl)
    ah, al = _split_bf16(a)
    return dot(ah, bh) + dot(al, bh) + dot(ah, bl)


def _dot3_nt(a, b):
    ah, al = _split_bf16(a)
    bh, bl = _split_bf16(b)
    dot = functools.partial(lax.dot_general, dimension_numbers=(((1,), (1,)), ((), ())),
                            preferred_element_type=F32)
    return dot(ah, bh) + dot(al, bh) + dot(ah, bl)


def _fdot(a, b):
    return jnp.dot(a, b, precision=HI, preferred_element_type=F32)


def _fdot_nt(a, b):
    return lax.dot_general(a, b, (((1,), (1,)), ((), ())), precision=HI,
                           preferred_element_type=F32)


def _sigmoid(x):
    return 1.0 / (1.0 + jnp.exp(-x))


def _silu(x):
    return x * _sigmoid(x)


def _gelu(x):
    return 0.5 * x * (1.0 + jnp.tanh(math.sqrt(2.0 / math.pi) * (x + 0.044715 * (x * x * x))))


def _softplus(x):
    return jnp.maximum(x, 0.0) + jnp.log(1.0 + jnp.exp(-jnp.abs(x)))


def _ada_kernel(c_ref, w_ref, b_ref, o_ref):
    o_ref[...] = _fdot(_silu(c_ref[...]), w_ref[...]) + b_ref[...]


def _ada(c_pad, ada_w, ada_b):
    m, d = c_pad.shape
    n = ada_w.shape[1]
    tn = 1024
    return pl.pallas_call(
        _ada_kernel,
        grid=(n // tn,),
        in_specs=[pl.BlockSpec((m, d), lambda j: (0, 0)),
                  pl.BlockSpec((d, tn), lambda j: (0, j)),
                  pl.BlockSpec((1, tn), lambda j: (0, j))],
        out_specs=pl.BlockSpec((m, tn), lambda j: (0, j)),
        out_shape=jax.ShapeDtypeStruct((m, n), F32),
        compiler_params=_cparams(("parallel",)),
        name="ada_mod",
    )(c_pad, ada_w, ada_b.reshape(1, n))


def _normproj_kernel(x_ref, nw_ref, sc_ref, sh_ref, w_ref, *rest, nk, emit_h):
    if emit_h:
        o_ref, h_out_ref, h_scr = rest
    else:
        o_ref, h_scr = rest

    @pl.when(pl.program_id(2) == 0)
    def _():
        x = x_ref[0]
        ms = jnp.mean(x * x, axis=-1, keepdims=True)
        y = x * lax.rsqrt(ms + EPS) * nw_ref[...]
        h = y * (1.0 + sc_ref[0]) + sh_ref[0]
        h_scr[...] = h.astype(BF16)
        if emit_h:
            h_out_ref[0] = h

    acc = jnp.dot(h_scr[...], w_ref[...], preferred_element_type=F32)
    for k in range(nk):
        o_ref[0, k] = acc[:, k * LANES:(k + 1) * LANES]


def _normproj(x, nw, sc, sh, w_bf16, *, tm, tn, emit_h):
    b, s, d = x.shape
    n = w_bf16.shape[1]
    tm = min(tm, s)
    nk = tn // LANES
    out_shape = [jax.ShapeDtypeStruct((b, n // LANES, s, LANES), F32)]
    out_specs = [pl.BlockSpec((1, nk, tm, LANES), lambda bi, i, j: (bi, j, i, 0))]
    if emit_h:
        out_shape.append(jax.ShapeDtypeStruct((b, s, d), F32))
        out_specs.append(pl.BlockSpec((1, tm, d), lambda bi, i, j: (bi, i, 0)))
    res = pl.pallas_call(
        functools.partial(_normproj_kernel, nk=nk, emit_h=emit_h),
        grid=(b, s // tm, n // tn),
        in_specs=[pl.BlockSpec((1, tm, d), lambda bi, i, j: (bi, i, 0)),
                  pl.BlockSpec((1, d), lambda bi, i, j: (0, 0)),
                  pl.BlockSpec((1, 1, d), lambda bi, i, j: (bi, 0, 0)),
                  pl.BlockSpec((1, 1, d), lambda bi, i, j: (bi, 0, 0)),
                  pl.BlockSpec((d, tn), lambda bi, i, j: (0, j))],
        out_specs=out_specs,
        out_shape=out_shape,
        scratch_shapes=[pltpu.VMEM((tm, d), BF16)],
        compiler_params=_cparams(("parallel", "parallel", "arbitrary")),
        name="normproj_h" if emit_h else "normproj",
    )(x, nw.reshape(1, d), sc.reshape(b, 1, d), sh.reshape(b, 1, d), w_bf16)
    return res if emit_h else res[0]


GDN_CB = 512
GDN_GB = 256


def _gdn_local_kernel(q_ref, qh_ref, k_ref, kh_ref, v_ref, vh_ref, sm_ref,
                      cwq_ref, cwk_ref, cwv_ref, alog_ref, dtb_ref,
                      w_ref, u0_ref, qk_ref, qg_ref, kd_ref, ge_ref):
    h = pl.program_id(1)
    i = pl.program_id(2)
    C, G = GDN_CHUNK, GDN_GB
    first = (i == 0)

    def conv_act(main_ref, halo_ref, cw_ref, r0):
        if r0 == 0:
            prev = jnp.where(first, 0.0, halo_ref[0, 0])
        else:
            prev = main_ref[0, 0, r0 - 8:r0, :]
        ext = jnp.concatenate([prev, main_ref[0, 0, r0:r0 + G, :]], axis=0)
        w = cw_ref[0]
        y = w[0:1, :] * ext[5:5 + G, :]
        for j in range(1, GDN_CONV):
            y = y + w[j:j + 1, :] * ext[5 + j:5 + j + G, :]
        return _silu(y)

    rid = lax.broadcasted_iota(jnp.int32, (G, G), 0)
    cid = lax.broadcasted_iota(jnp.int32, (G, G), 1)
    same = (rid // C) == (cid // C)
    incl = jnp.logical_and(same, rid >= cid)
    strict = jnp.logical_and(same, rid > cid)
    eye = rid == cid
    is_last = cid == (rid // C) * C + (C - 1)
    joins = []
    bs = 1
    while bs < C:
        joins.append(jnp.logical_and(jnp.logical_and((rid // (2 * bs)) == (cid // (2 * bs)),
                                                     (rid & (2 * bs - 1)) >= bs),
                                     (cid & (2 * bs - 1)) < bs))
        bs *= 2
    tri = jnp.where(incl, 1.0, 0.0).astype(BF16)
    eye_f = jnp.where(eye, 1.0, 0.0).astype(F32)
    lane = lax.broadcasted_iota(jnp.int32, (G, LANES), 1)
    neg_a = -jnp.exp(alog_ref[...])
    dtb = dtb_ref[...]

    for grp in range(GDN_CB // G):
        r0 = grp * G
        q = conv_act(q_ref, qh_ref, cwq_ref, r0)
        k = conv_act(k_ref, kh_ref, cwk_ref, r0)
        v = conv_act(v_ref, vh_ref, cwv_ref, r0)
        q = q * lax.rsqrt(jnp.sum(q * q, axis=-1, keepdims=True) + EPS) * (HEAD_DIM ** -0.5)
        k = k * lax.rsqrt(jnp.sum(k * k, axis=-1, keepdims=True) + EPS)

        sm = sm_ref[0, 0, r0:r0 + G, :]
        g_all = neg_a * _softplus(sm + dtb)
        gc_all = _dot3(tri, g_all, exact_a=True)
        gc = jnp.sum(jnp.where(lane == h, gc_all, 0.0), axis=1, keepdims=True)
        beta = jnp.sum(jnp.where(lane == GDN_HEADS + h, _sigmoid(sm), 0.0), axis=1, keepdims=True)
        gc_row = jnp.sum(jnp.where(eye, gc, 0.0), axis=0, keepdims=True)
        gc_last = jnp.sum(jnp.where(is_last, gc_row, 0.0), axis=1, keepdims=True)
        decay = jnp.where(incl, jnp.exp(jnp.where(incl, gc - gc_row, 0.0)), 0.0)
        gamma = jnp.exp(gc)

        kk = _dot3_nt(k, k)
        lmat = jnp.where(strict, decay * kk, 0.0) * beta
        tinv = eye_f - jnp.where(joins[0], lmat, 0.0)
        for lvl in range(1, len(joins)):
            tinv = tinv - _dot3(_dot3(tinv, jnp.where(joins[lvl], lmat, 0.0)), tinv)
        rhs = jnp.concatenate([(beta * gamma) * k, beta * v], axis=1)
        wu = _dot3(tinv, rhs)
        qk = decay * _bdot_nt(q, k)

        rows = slice(r0, r0 + G)
        w_ref[0, 0, rows, :] = wu[:, :HEAD_DIM].astype(BF16)
        u0_ref[0, 0, rows, :] = wu[:, HEAD_DIM:]
        qg_ref[0, 0, rows, :] = (gamma * q).astype(BF16)
        kd_ref[0, 0, rows, :] = (jnp.exp(gc_last - gc) * k).astype(BF16)
        ge_all = jnp.broadcast_to(jnp.exp(gc_last), (G, LANES))
        for c in range(G // C):
            cr = slice(c * C, (c + 1) * C)
            qk_ref[0, 0, r0 + c * C:r0 + (c + 1) * C, :] = qk[cr, cr].astype(BF16)
            ge_ref[0, 0, pl.ds((i * (GDN_CB // G) + grp) * (G // C) + c, 1), :] = ge_all[c * C:c * C + 1, :]


def _gdn_local(P, cw, alog_pad, dtb_pad):
    b, _, s, _ = P.shape
    H, CB, C = GDN_HEADS, GDN_CB, GDN_CHUNK
    n = s // C

    def main(j0):
        return pl.BlockSpec((1, 1, CB, LANES), lambda bi, h, i: (bi, j0 + h, i, 0))

    def halo(j0):
        return pl.BlockSpec((1, 1, 8, LANES),
                            lambda bi, h, i: (bi, j0 + h, jnp.maximum(i * (CB // 8) - 1, 0), 0))

    def cws(j0):
        return pl.BlockSpec((1, GDN_CONV, LANES), lambda bi, h, i: (j0 + h, 0, 0))

    row = pl.BlockSpec((1, LANES), lambda bi, h, i: (0, 0))
    hs = lambda width: pl.BlockSpec((1, 1, CB, width), lambda bi, h, i: (bi, h, i, 0))
    return pl.pallas_call(
        _gdn_local_kernel,
        grid=(b, H, s // CB),
        in_specs=[main(J_AQ), halo(J_AQ), main(J_AK), halo(J_AK), main(J_AV), halo(J_AV),
                  pl.BlockSpec((1, 1, CB, LANES), lambda bi, h, i: (bi, J_SMALL, i, 0)),
                  cws(0), cws(8), cws(16), row, row],
        out_specs=[hs(LANES), hs(LANES), hs(C), hs(LANES), hs(LANES),
                   pl.BlockSpec((1, 1, n, LANES), lambda bi, h, i: (bi, h, 0, 0))],
        out_shape=[jax.ShapeDtypeStruct((b, H, s, LANES), BF16),
                   jax.ShapeDtypeStruct((b, H, s, LANES), F32),
                   jax.ShapeDtypeStruct((b, H, s, C), BF16),
                   jax.ShapeDtypeStruct((b, H, s, LANES), BF16),
                   jax.ShapeDtypeStruct((b, H, s, LANES), BF16),
                   jax.ShapeDtypeStruct((b, H, n, LANES), F32)],
        compiler_params=_cparams(("parallel", "parallel", "arbitrary")),
        name="gdn_local",
    )(P, P, P, P, P, P, P, cw, cw, cw, alog_pad, dtb_pad)


GDN_HB = 4


def _gdn_scan_kernel(w_ref, u0_ref, qk_ref, qg_ref, kd_ref, ge_ref, z_ref, nw_ref, o_ref, s_scr,
                     *, sb):
    C = GDN_CHUNK

    @pl.when(pl.program_id(2) == 0)
    def _():
        s_scr[...] = jnp.zeros_like(s_scr)

    nw = nw_ref[...]

    def body(n, carry):
        r = pl.multiple_of(n * C, C)
        for hh in range(GDN_HB):
            st = s_scr[hh]
            stb = st.astype(BF16)
            u = u0_ref[0, hh, pl.ds(r, C), :] - jnp.dot(w_ref[0, hh, pl.ds(r, C), :], stb,
                                                       preferred_element_type=F32)
            ub = u.astype(BF16)
            o = (jnp.dot(qg_ref[0, hh, pl.ds(r, C), :], stb, preferred_element_type=F32)
                 + jnp.dot(qk_ref[0, hh, pl.ds(r, C), :], ub, preferred_element_type=F32))
            ge = ge_ref[0, hh, pl.ds(n, 1), :]
            s_scr[hh] = ge * st + lax.dot_general(kd_ref[0, hh, pl.ds(r, C), :], ub,
                                                  (((0,), (0,)), ((), ())),
                                                  preferred_element_type=F32)
            on = o * lax.rsqrt(jnp.mean(o * o, axis=-1, keepdims=True) + EPS) * nw
            z = z_ref[0, hh, pl.ds(r, C), :]
            o_ref[0, pl.ds(r, C), hh * LANES:(hh + 1) * LANES] = (on * _silu(z)).astype(BF16)
        return carry

    lax.fori_loop(0, sb // C, body, 0)


def _gdn_scan(w, u0, qk, qg, kd, ge, P, norm_w):
    b, H, s, _ = w.shape
    C, HB = GDN_CHUNK, GDN_HB
    sb = min(1024, s)
    hs = lambda width: pl.BlockSpec((1, HB, sb, width), lambda bi, hb, i: (bi, hb, i, 0))
    return pl.pallas_call(
        functools.partial(_gdn_scan_kernel, sb=sb),
        grid=(b, H // HB, s // sb),
        in_specs=[hs(LANES), hs(LANES), hs(C), hs(LANES), hs(LANES),
                  pl.BlockSpec((1, HB, sb // C, LANES), lambda bi, hb, i: (bi, hb, i, 0)),
                  pl.BlockSpec((1, HB, sb, LANES), lambda bi, hb, i: (bi, J_AZ // HB + hb, i, 0)),
                  pl.BlockSpec((1, LANES), lambda bi, hb, i: (0, 0))],
        out_specs=pl.BlockSpec((1, sb, HB * LANES), lambda bi, hb, i: (bi, i, hb)),
        out_shape=jax.ShapeDtypeStruct((b, s, H * LANES), BF16),
        scratch_shapes=[pltpu.VMEM((HB, HEAD_DIM, HEAD_DIM), F32)],
        compiler_params=_cparams(("parallel", "parallel", "arbitrary")),
        name="gdn_scan",
    )(w, u0, qk, qg, kd, ge, P, norm_w.reshape(1, LANES))


def _pad_lanes_row(v):
    return jnp.pad(v.astype(F32), (0, LANES - v.shape[0])).reshape(1, LANES)


def _gdn(P, conv_w, a_log, dt_bias, norm_w):
    cw = jnp.transpose(conv_w.reshape(GDN_CONV, 3 * GDN_HEADS, LANES), (1, 0, 2))
    w, u0, qk, qg, kd, ge = _gdn_local(P, cw, _pad_lanes_row(a_log), _pad_lanes_row(dt_bias))
    return _gdn_scan(w, u0, qk, qg, kd, ge, P, norm_w)


NSA_TQ = 128
NSA_TK_SEL = 512
NSA_TK_WIN = 256
NSA_HC = 2
R_Q, R_KC, R_VC, R_KS, R_VS, R_KW, R_VW = 0, 16, 18, 20, 22, 24, 26
ROPE_NB = 4


def _rope_kernel(x_ref, pos_ref, invf_ref, o_ref, cos_scr, sin_scr):
    j = pl.program_id(2)
    lane = lax.broadcasted_iota(jnp.int32, cos_scr.shape, 1)

    @pl.when(j == 0)
    def _():
        ang = pos_ref[0] * invf_ref[...]
        sn = jnp.sin(ang)
        cos_scr[...] = jnp.where(lane < ROPE_DIM, jnp.cos(ang), 1.0)
        sin_scr[...] = jnp.where(lane < ROPE_HALF, -sn, jnp.where(lane < ROPE_DIM, sn, 0.0))

    is_q = j < NSA_HEADS // ROPE_NB
    scale = jnp.where(is_q, HEAD_DIM ** -0.5, 1.0)
    for k in range(ROPE_NB):
        x = x_ref[0, k]
        swapped = jnp.where(lane < ROPE_HALF, pltpu.roll(x, LANES - ROPE_HALF, axis=1),
                            pltpu.roll(x, ROPE_HALF, axis=1))
        rot = x * cos_scr[...] + swapped * sin_scr[...]
        out = rot * scale if k < 2 else jnp.where(is_q, rot, x) * scale
        o_ref[0, k] = out.astype(BF16)


def _rope(P, pos_f32, invf):
    b, _, s, _ = P.shape
    tr = min(512, s)
    nb = ROPE_NB
    return pl.pallas_call(
        _rope_kernel,
        grid=(b, s // tr, N_NSA_BLOCKS // nb),
        in_specs=[pl.BlockSpec((1, nb, tr, LANES), lambda bi, i, j: (bi, J_BQ // nb + j, i, 0)),
                  pl.BlockSpec((1, tr, 1), lambda bi, i, j: (bi, i, 0)),
                  pl.BlockSpec((1, LANES), lambda bi, i, j: (0, 0))],
        out_specs=pl.BlockSpec((1, nb, tr, LANES), lambda bi, i, j: (bi, j, i, 0)),
        out_shape=jax.ShapeDtypeStruct((b, N_NSA_BLOCKS, s, LANES), BF16),
        scratch_shapes=[pltpu.VMEM((tr, LANES), F32), pltpu.VMEM((tr, LANES), F32)],
        compiler_params=_cparams(("parallel", "parallel", "arbitrary")),
        name="nsa_rope",
    )(P, pos_f32, invf)


def _compress_kernel(r_ref, w1_ref, w2_ref, pos_ref, o_ref):
    r = r_ref[0, 0]
    nr = r.shape[0]
    half = CMP_STRIDE * HEAD_DIM
    a = jnp.dot(r, w1_ref[0, :half, :], preferred_element_type=F32)
    bm = jnp.dot(r, w1_ref[0, half:, :], preferred_element_type=F32)
    pos8 = jnp.broadcast_to(pos_ref[0], (8, CMP_LEN * HEAD_DIM)).astype(BF16)
    pb = jnp.dot(pos8, w1_ref[0], preferred_element_type=F32)[0:1, :]
    hid = a + pltpu.roll(bm, nr - 1, axis=0) + pb
    out = jnp.dot(_gelu(hid).astype(BF16), w2_ref[0], preferred_element_type=F32)
    row = lax.broadcasted_iota(jnp.int32, out.shape, 0)
    o_ref[0, 0] = jnp.where(row < nr - 1, out, 0.0).astype(BF16)


def _compress(rows, w1, w2, pos):
    b, _, nr, width = rows.shape
    return pl.pallas_call(
        _compress_kernel,
        grid=(b, 4),
        in_specs=[pl.BlockSpec((1, 1, nr, width), lambda bi, j: (bi, j, 0, 0)),
                  pl.BlockSpec((1, CMP_LEN * HEAD_DIM, CMP_HIDDEN), lambda bi, j: (j // 2, 0, 0)),
                  pl.BlockSpec((1, CMP_HIDDEN, HEAD_DIM), lambda bi, j: (j // 2, 0, 0)),
                  pl.BlockSpec((1, 1, CMP_LEN * HEAD_DIM), lambda bi, j: (j // 2, 0, 0))],
        out_specs=pl.BlockSpec((1, 1, nr, HEAD_DIM), lambda bi, j: (bi, j, 0, 0)),
        out_shape=jax.ShapeDtypeStruct((b, 4, nr, HEAD_DIM), BF16),
        compiler_params=_cparams(("parallel", "arbitrary")),
        name="nsa_compress",
    )(rows, w1, w2, pos)


def _nsa_kernel(q_ref, kc_ref, vc_ref, ks_ref, vs_ref, kw_ref, vw_ref, gate_ref, ovt_ref, e_ref,
                o_ref, m_scr, l_scr, acc_scr, os_scr, *, s_len):
    i = pl.program_id(2)
    TQ, HP = NSA_TQ, NSA_HPG
    R = HP * TQ
    nsb = s_len // SLC_LEN
    ncp = s_len // CMP_STRIDE
    n_sel = min(SLC_TOPK, nsb)
    s0 = i * TQ
    q2 = q_ref[0].reshape(R, HEAD_DIM)

    sc = _bdot_nt(q2, kc_ref[0, 0])
    tq_r = s0 + (lax.broadcasted_iota(jnp.int32, (R, ncp), 0) & (TQ - 1))
    ncol = lax.broadcasted_iota(jnp.int32, (R, ncp), 1)
    valid = jnp.logical_and(ncol * CMP_STRIDE + (CMP_LEN - 1) <= tq_r, ncol < ncp - 1)
    scm = jnp.where(valid, sc, NEG)
    e = jnp.where(valid, jnp.exp(scm - jnp.max(scm, axis=1, keepdims=True)), 0.0)
    p_c = e / jnp.maximum(jnp.sum(e, axis=1, keepdims=True), 1e-30)
    o_c = _bdot(p_c, vc_ref[0, 0])
    psum = jnp.sum(p_c.reshape(HP, TQ, ncp), axis=0)
    imp = _fdot_nt(ovt_ref[...], psum)
    blk = lax.broadcasted_iota(jnp.int32, (nsb, TQ), 0)
    tq_l = s0 + lax.broadcasted_iota(jnp.int32, (nsb, TQ), 1)
    cur = tq_l // SLC_LEN
    forced = jnp.logical_or(blk == 0, jnp.logical_or(blk == cur, blk == cur - 1))
    imp = jnp.where(forced, BIG, imp)
    imp = jnp.where(blk * SLC_LEN <= tq_l, imp, NEG)
    rank = jnp.zeros((nsb, TQ), jnp.int32)
    for j in range(nsb):
        rj = imp[j:j + 1, :]
        beats = jnp.logical_or(rj > imp, jnp.logical_and(rj == imp, blk > j))
        rank = rank + jnp.where(beats, 1, 0)
    sel_t = jnp.where(rank < n_sel, 1.0, 0.0).astype(F32)
    if nsb < LANES:
        sel_t = jnp.concatenate([sel_t, jnp.zeros((LANES - nsb, TQ), F32)], axis=0)
    sel = jnp.transpose(sel_t).astype(BF16)

    ones_blk = jnp.ones((TQ, HEAD_DIM), BF16)
    HC = NSA_HC

    def flash(k_ref, v_ref, tk, lo, hi, mask_fn):
        m_scr[...] = jnp.full(m_scr.shape, NEG, F32)
        l_scr[...] = jnp.zeros(l_scr.shape, F32)
        acc_scr[...] = jnp.zeros(acc_scr.shape, F32)
        qpos = s0 + lax.broadcasted_iota(jnp.int32, (TQ, tk), 0)
        kofs = lax.broadcasted_iota(jnp.int32, (TQ, tk), 1)
        ones_v = jnp.concatenate([ones_blk] * (tk // TQ), axis=0)

        def body(j, carry):
            r = pl.multiple_of(j * tk, tk)
            bias = jnp.where(mask_fn(r, r + kofs, qpos), 0.0, NEG)
            kb = k_ref[0, 0, pl.ds(r, tk), :]
            vb = jnp.concatenate([v_ref[0, 0, pl.ds(r, tk), :], ones_v], axis=1)
            for c in range(HP // HC):
                rows = slice(c * HC * TQ, (c + 1) * HC * TQ)
                s = _bdot_nt(q_ref[0, c * HC:(c + 1) * HC].reshape(HC * TQ, HEAD_DIM), kb)
                s = (s.reshape(HC, TQ, tk) + bias[None]).reshape(HC * TQ, tk)
                m_prev = m_scr[rows, :]
                m_new = jnp.maximum(m_prev, jnp.max(s, axis=1, keepdims=True))
                alpha = jnp.exp(m_prev - m_new)
                p = jnp.exp(s - jnp.concatenate([m_new] * (tk // LANES), axis=1)).astype(BF16)
                pv = jnp.dot(p, vb, preferred_element_type=F32)
                l_scr[rows, :] = alpha * l_scr[rows, :] + pv[:, HEAD_DIM:]
                acc_scr[rows, :] = alpha * acc_scr[rows, :] + pv[:, :HEAD_DIM]
                m_scr[rows, :] = m_new
            return carry

        lax.fori_loop(lo, hi, body, 0)

    tks = min(NSA_TK_SEL, s_len)

    def sel_mask(r, kpos, qpos):
        chosen = jnp.dot(sel, e_ref[:, pl.ds(r, tks)], preferred_element_type=F32)
        return jnp.logical_and(chosen > 0.5, kpos <= qpos)

    flash(ks_ref, vs_ref, tks, 0, (s0 + TQ - 1) // tks + 1, sel_mask)
    os_scr[...] = acc_scr[...] / l_scr[...]

    def win_mask(r, kpos, qpos):
        return jnp.logical_and(kpos <= qpos, kpos > qpos - WIN)

    tkw = min(NSA_TK_WIN, s_len)
    flash(kw_ref, vw_ref, tkw, jnp.maximum(s0 - WIN + 1, 0) // tkw, (s0 + TQ - 1) // tkw + 1, win_mask)

    gates = _sigmoid(gate_ref[0, 0])
    for hh in range(HP):
        rows = slice(hh * TQ, (hh + 1) * TQ)
        o_w = acc_scr[rows, :] / l_scr[rows, :]
        out = (gates[:, 3 * hh:3 * hh + 1] * o_c[rows, :]
               + gates[:, 3 * hh + 1:3 * hh + 2] * os_scr[rows, :]
               + gates[:, 3 * hh + 2:3 * hh + 3] * o_w)
        o_ref[0, :, hh * HEAD_DIM:(hh + 1) * HEAD_DIM] = out.astype(BF16)


def _nsa_attend(rp, cmp_kv, P, ovt, expand):
    b, _, s, _ = rp.shape
    TQ, HP, G = NSA_TQ, NSA_HPG, NSA_GROUPS
    ncp = cmp_kv.shape[2]
    R = HP * TQ
    full = lambda j0: pl.BlockSpec((1, 1, s, HEAD_DIM), lambda bi, g, i: (bi, j0 + g, 0, 0))
    cmp_spec = lambda j0: pl.BlockSpec((1, 1, ncp, HEAD_DIM), lambda bi, g, i: (bi, j0 + g, 0, 0))
    return pl.pallas_call(
        functools.partial(_nsa_kernel, s_len=s),
        grid=(b, G, s // TQ),
        in_specs=[pl.BlockSpec((1, HP, TQ, HEAD_DIM), lambda bi, g, i: (bi, g, i, 0)),
                  cmp_spec(0), cmp_spec(2),
                  full(R_KS), full(R_VS), full(R_KW), full(R_VW),
                  pl.BlockSpec((1, 1, TQ, LANES), lambda bi, g, i: (bi, J_GATE + g, i, 0)),
                  pl.BlockSpec(ovt.shape, lambda bi, g, i: (0, 0)),
                  pl.BlockSpec(expand.shape, lambda bi, g, i: (0, 0))],
        out_specs=pl.BlockSpec((1, TQ, HP * HEAD_DIM), lambda bi, g, i: (bi, i, g)),
        out_shape=jax.ShapeDtypeStruct((b, s, NSA_HEADS * HEAD_DIM), BF16),
        scratch_shapes=[pltpu.VMEM((R, LANES), F32), pltpu.VMEM((R, LANES), F32),
                        pltpu.VMEM((R, HEAD_DIM), F32), pltpu.VMEM((R, HEAD_DIM), F32)],
        compiler_params=_cparams(("parallel", "parallel", "arbitrary")),
        name="nsa_attend",
    )(rp, cmp_kv, cmp_kv, rp, rp, rp, rp, P, ovt, expand)


def _nsa(P, positions, cmp_pos_k, cmp_w1_k, cmp_w2_k, cmp_pos_v, cmp_w1_v, cmp_w2_v):
    b, _, s, _ = P.shape
    lanes = np.arange(LANES)
    invf = np.where(lanes < ROPE_DIM, ROPE_THETA ** (-(lanes % ROPE_HALF) / ROPE_HALF), 0.0)
    rp = _rope(P, positions.astype(F32).reshape(b, s, 1), jnp.asarray(invf, F32).reshape(1, LANES))

    ncp = s // CMP_STRIDE
    rows = rp[:, R_KC:R_KC + 4].reshape(b, 4, ncp, CMP_STRIDE * HEAD_DIM)
    w1 = jnp.stack([cmp_w1_k, cmp_w1_v]).astype(BF16)
    w2 = jnp.stack([cmp_w2_k, cmp_w2_v]).astype(BF16)
    pos = jnp.stack([cmp_pos_k, cmp_pos_v]).reshape(2, 1, CMP_LEN * HEAD_DIM)
    cmp_kv = _compress(rows, w1, w2, pos)

    nsb = s // SLC_LEN
    cmp_start = np.arange(ncp) * CMP_STRIDE
    slc_start = np.arange(nsb) * SLC_LEN
    ov = ((cmp_start[:, None] < slc_start[None, :] + SLC_LEN)
          & (cmp_start[:, None] + CMP_LEN > slc_start[None, :])
          & (np.arange(ncp)[:, None] < ncp - 1))
    ovt = jnp.asarray(ov.T.astype(np.float32))
    expand = (np.arange(LANES)[:, None] == (np.arange(s)[None, :] // SLC_LEN))
    expand = jnp.asarray(expand.astype(np.float32), BF16)
    return _nsa_attend(rp, cmp_kv, P, ovt, expand)


MERGE_TN = 512


def _merge_kernel(oa_ref, ob_ref, wg_ref, wn_ref, ma_ref, mb_ref, y_ref):
    ya = jnp.dot(oa_ref[0], wg_ref[...], preferred_element_type=F32)
    yb = jnp.dot(ob_ref[0], wn_ref[...], preferred_element_type=F32)
    for k in range(MERGE_TN // LANES):
        cols = slice(k * LANES, (k + 1) * LANES)
        y = _sigmoid(ma_ref[0, k]) * ya[:, cols] + _sigmoid(mb_ref[0, k]) * yb[:, cols]
        y_ref[0, :, cols] = y.astype(BF16)


def _merge(o_a, o_b, wg, wn, P):
    b, s, da = o_a.shape
    db = o_b.shape[2]
    d = wg.shape[1]
    tm, tn = min(512, s), MERGE_TN
    nk = tn // LANES
    return pl.pallas_call(
        _merge_kernel,
        grid=(b, s // tm, d // tn),
        in_specs=[pl.BlockSpec((1, tm, da), lambda bi, i, j: (bi, i, 0)),
                  pl.BlockSpec((1, tm, db), lambda bi, i, j: (bi, i, 0)),
                  pl.BlockSpec((da, tn), lambda bi, i, j: (0, j)),
                  pl.BlockSpec((db, tn), lambda bi, i, j: (0, j)),
                  pl.BlockSpec((1, nk, tm, LANES), lambda bi, i, j: (bi, J_MA // nk + j, i, 0)),
                  pl.BlockSpec((1, nk, tm, LANES), lambda bi, i, j: (bi, J_MB // nk + j, i, 0))],
        out_specs=pl.BlockSpec((1, tm, tn), lambda bi, i, j: (bi, i, j)),
        out_shape=jax.ShapeDtypeStruct((b, s, d), BF16),
        compiler_params=_cparams(("parallel", "parallel", "arbitrary")),
        name="merge",
    )(o_a, o_b, wg, wn, P, P)


def _outproj_kernel(y_ref, w_ref, x_ref, g_ref, o_ref):
    o_ref[0] = x_ref[0] + g_ref[0] * jnp.dot(y_ref[0], w_ref[...], preferred_element_type=F32)


def _outproj(y, w, x, g):
    b, s, d = x.shape
    tm, tn = min(512, s), 512
    return pl.pallas_call(
        _outproj_kernel,
        grid=(b, s // tm, d // tn),
        in_specs=[pl.BlockSpec((1, tm, d), lambda bi, i, j: (bi, i, 0)),
                  pl.BlockSpec((d, tn), lambda bi, i, j: (0, j)),
                  pl.BlockSpec((1, tm, tn), lambda bi, i, j: (bi, i, j)),
                  pl.BlockSpec((1, 1, tn), lambda bi, i, j: (bi, 0, j))],
        out_specs=pl.BlockSpec((1, tm, tn), lambda bi, i, j: (bi, i, j)),
        out_shape=jax.ShapeDtypeStruct((b, s, d), F32),
        compiler_params=_cparams(("parallel", "parallel", "arbitrary")),
        name="outproj",
    )(y, w, x, g.reshape(b, 1, d))


PEER_TB = 256
PEER_TG = 64
PEER_AHEAD = 4
PEER_SLOTS = 8
PEER_SEL = PEER_HEADS * PEER_TOPK
MIN_F32 = -3.0e38


def _topk_rows(vals, payload=None):
    nrow = vals.shape[0]
    rid = lax.broadcasted_iota(jnp.int32, vals.shape, 0)
    out_v, out_i = [], []
    for _ in range(PEER_TOPK):
        m = jnp.max(vals, axis=0, keepdims=True)
        idx = jnp.min(jnp.where(vals == m, rid, nrow), axis=0, keepdims=True)
        hit = rid == idx
        out_v.append(m)
        if payload is None:
            out_i.append(idx)
        else:
            out_i.append(jnp.sum(jnp.where(hit, payload, 0), axis=0, keepdims=True))
        vals = jnp.where(hit, MIN_F32, vals)
    return jnp.concatenate(out_v, axis=0), jnp.concatenate(out_i, axis=0)


def _peer_topk_kernel(qh_ref, k1_ref, k2_ref, eidx_ref, gw_ref):
    def head(h, carry):
        s1 = _fdot_nt(k1_ref[h], qh_ref[0, 2 * h])
        s2 = _fdot_nt(k2_ref[h], qh_ref[0, 2 * h + 1])
        v1, i1 = _topk_rows(s1)
        v2, i2 = _topk_rows(s2)
        keep = [PEER_TOPK // (a + 1) for a in range(PEER_TOPK)]
        npad = (-sum(keep)) % 8
        cand = jnp.concatenate([v1[a:a + 1, :] + v2[0:keep[a], :] for a in range(PEER_TOPK)]
                               + [jnp.full((npad, v1.shape[1]), MIN_F32, F32)], axis=0)
        cidx = jnp.concatenate([i1[a:a + 1, :] * PEER_NKEYS + i2[0:keep[a], :] for a in range(PEER_TOPK)]
                               + [jnp.zeros((npad, v1.shape[1]), jnp.int32)], axis=0)
        sc, eidx = _topk_rows(cand, cidx)
        ex = jnp.exp(sc - sc[0:1, :])
        eidx_ref[h] = eidx
        gw_ref[h] = ex / jnp.sum(ex, axis=0, keepdims=True)
        return carry

    lax.fori_loop(0, PEER_HEADS, head, 0)


def _peer_topk(qh, keys1, keys2):
    b, nb, s, _ = qh.shape
    tb = min(PEER_TB, s)
    nt = s // tb
    kspec = pl.BlockSpec(keys1.shape, lambda bi, i: (0, 0, 0))
    ospec = pl.BlockSpec((PEER_HEADS, PEER_TOPK, tb), lambda bi, i: (0, 0, bi * nt + i))
    return pl.pallas_call(
        _peer_topk_kernel,
        grid=(b, nt),
        in_specs=[pl.BlockSpec((1, nb, tb, LANES), lambda bi, i: (bi, 0, i, 0)), kspec, kspec],
        out_specs=[ospec, ospec],
        out_shape=[jax.ShapeDtypeStruct((PEER_HEADS, PEER_TOPK, b * s), jnp.int32),
                   jax.ShapeDtypeStruct((PEER_HEADS, PEER_TOPK, b * s), F32)],
        compiler_params=_cparams(("parallel", "parallel")),
        name="peer_topk",
    )(qh, keys1, keys2)


def _peer_gather_kernel(idx_ref, idxn_ref, gw_ref, x_ref, uv_hbm, o_ref, buf, a_scr, c_scr, sem, *, tg):
    NS, D, NSLOT = PEER_SEL, PEER_AHEAD, PEER_SLOTS
    sub = x_ref.shape[1] // LANES
    rid = lax.broadcasted_iota(jnp.int32, (NS, NS), 0)
    cid = lax.broadcasted_iota(jnp.int32, (NS, NS), 1)
    eye = rid == cid
    ones = jnp.ones((LANES, LANES), BF16)

    def lane_sum_rep(m):
        hi, lo = _split_bf16(m)
        return (jnp.dot(hi, ones, preferred_element_type=F32)
                + jnp.dot(lo, ones, preferred_element_type=F32))

    def wait_rows(t):
        slot = t & (NSLOT - 1)
        pltpu.make_async_copy(uv_hbm.at[pl.ds(0, NS)], buf.at[slot], sem.at[slot]).wait()

    def phase(t, issue_from, do_dot, do_prev):
        do_issue = issue_from is not None
        if do_dot:
            wait_rows(t)
        if do_prev:
            act = lane_sum_rep(a_scr[(t - 1) & 1])
            gw_rep = lane_sum_rep(jnp.where(eye, gw_ref[pl.ds(t - 1, 1), :], 0.0))
            c_scr[...] = gw_rep * _gelu(act)
        if do_dot:
            xrow = x_ref[pl.ds(t, 1), :]
            xl = jnp.concatenate([xrow[:, k * LANES:(k + 1) * LANES] for k in range(8)], axis=0)
            xh = jnp.concatenate([xrow[:, k * LANES:(k + 1) * LANES] for k in range(8, sub)], axis=0)
            slot_d = t & (NSLOT - 1)
            par_d = t & 1
        if do_issue:
            slot_i = (t + D) & (NSLOT - 1)
            base_i = (t + D) * NS if issue_from is idx_ref else (t + D - tg) * NS
        for j in range(NS):
            if do_issue:
                pltpu.make_async_copy(uv_hbm.at[issue_from[base_i + j]], buf.at[slot_i, j],
                                      sem.at[slot_i]).start()
            if do_dot:
                uj = buf[slot_d, j, 0].astype(F32)
                a_scr[par_d, j:j + 1, :] = jnp.sum(uj[0:8, :] * xl + uj[8:16, :] * xh,
                                                   axis=0, keepdims=True)
        if do_prev:
            slot_s = (t - 1) & (NSLOT - 1)
            accs = [jnp.zeros((sub, LANES), F32) for _ in range(4)]
            for j in range(NS):
                accs[j % 4] = accs[j % 4] + c_scr[j:j + 1, :] * buf[slot_s, j, 1].astype(F32)
            o_ref[t - 1] = (accs[0] + accs[1]) + (accs[2] + accs[3])

    def ramp(t, carry):
        slot = t & (NSLOT - 1)

        def body(j, c):
            pltpu.make_async_copy(uv_hbm.at[idx_ref[t * NS + j]], buf.at[slot, j], sem.at[slot]).start()
            return c
        return lax.fori_loop(0, NS, body, carry)

    def looped(issue_from):
        def f(t, carry):
            phase(t, issue_from, True, True)
            return carry
        return f

    step = pl.program_id(0)
    last = pl.num_programs(0) - 1

    @pl.when(step == 0)
    def _():
        lax.fori_loop(0, D, ramp, 0)

    phase(0, idx_ref, True, False)
    lax.fori_loop(1, tg - D, looped(idx_ref), 0)

    @pl.when(step < last)
    def _():
        lax.fori_loop(tg - D, tg, looped(idxn_ref), 0)

    @pl.when(step == last)
    def _():
        lax.fori_loop(tg - D, tg, looped(None), 0)

    phase(tg, None, False, True)


def _peer_gather(eidx, gw, h2r, uv):
    t = eidx.shape[0]
    tg = min(PEER_TG, t)
    d = h2r.shape[1]
    sub = d // LANES
    return pl.pallas_call(
        functools.partial(_peer_gather_kernel, tg=tg),
        grid=(t // tg,),
        in_specs=[pl.BlockSpec((tg * PEER_SEL,), lambda i: (i,), memory_space=pltpu.SMEM),
                  pl.BlockSpec((tg * PEER_SEL,), lambda i: (jnp.minimum(i + 1, t // tg - 1),),
                               memory_space=pltpu.SMEM),
                  pl.BlockSpec((tg, PEER_SEL), lambda i: (i, 0)),
                  pl.BlockSpec((tg, d), lambda i: (i, 0)),
                  pl.BlockSpec(memory_space=pl.ANY)],
        out_specs=pl.BlockSpec((tg, sub, LANES), lambda i: (i, 0, 0)),
        out_shape=jax.ShapeDtypeStruct((t, sub, LANES), F32),
        scratch_shapes=[pltpu.VMEM((PEER_SLOTS, PEER_SEL, 2, sub, LANES), uv.dtype),
                        pltpu.VMEM((2, PEER_SEL, LANES), F32),
                        pltpu.VMEM((PEER_SEL, LANES), F32),
                        pltpu.SemaphoreType.DMA((PEER_SLOTS,))],
        compiler_params=_cparams(("arbitrary",)),
        name="peer_gather",
    )(eidx.reshape(-1), eidx.reshape(-1), gw, h2r, uv)


def _peer(x1, norm_w, sc, sh, wq, keys1, keys2, u, v):
    b, s, d = x1.shape
    qh, h2 = _normproj(x1, norm_w, sc, sh, wq.astype(BF16), tm=512, tn=512, emit_h=True)
    eidx, gw = _peer_topk(qh, keys1, keys2)
    eidx = jnp.transpose(eidx, (2, 0, 1)).reshape(b * s, PEER_SEL)
    gw = jnp.transpose(gw, (2, 0, 1)).reshape(b * s, PEER_SEL)
    sub = d // LANES
    uv = jnp.stack([u.reshape(-1, sub, LANES), v.reshape(-1, sub, LANES)], axis=1).astype(BF16)
    out = _peer_gather(eidx, gw, h2.reshape(b * s, d), uv)
    return out.reshape(b, s, d)


def _final_kernel(x_ref, p_ref, g_ref, w_ref, o_ref):
    x = x_ref[0] + g_ref[0] * p_ref[0]
    o_ref[0] = x * lax.rsqrt(jnp.mean(x * x, axis=-1, keepdims=True) + EPS) * w_ref[...]


def _final(x1, peer, g2, wf):
    b, s, d = x1.shape
    tm = min(512, s)
    blk = pl.BlockSpec((1, tm, d), lambda bi, i: (bi, i, 0))
    return pl.pallas_call(
        _final_kernel,
        grid=(b, s // tm),
        in_specs=[blk, blk, pl.BlockSpec((1, 1, d), lambda bi, i: (bi, 0, 0)),
                  pl.BlockSpec((1, d), lambda bi, i: (0, 0))],
        out_specs=blk,
        out_shape=jax.ShapeDtypeStruct((b, s, d), F32),
        compiler_params=_cparams(("parallel", "parallel")),
        name="final_norm",
    )(x1, peer, g2.reshape(b, 1, d), wf.reshape(1, d))


def _pad_rows(a, mult=8):
    pad = (-a.shape[0]) % mult
    return jnp.pad(a, ((0, pad), (0, 0)))


def _permute_w_in(w):
    o_aa, o_bq, o_bg, o_ma, o_end = 4096, 4112, 7696, 7744, 11840
    d = w.shape[0]
    z = lambda n: jnp.zeros((d, n), w.dtype)
    hg = 3 * NSA_HPG
    cols = [w[:, 0:o_aa], w[:, o_bq:o_bg], w[:, o_ma:o_end],
            w[:, o_aa:o_bq], z(LANES - 16),
            w[:, o_bg:o_bg + hg], z(LANES - hg),
            w[:, o_bg + hg:o_ma], z(LANES - hg),
            z(LANES)]
    return jnp.concatenate(cols, axis=1).astype(BF16)


def kernel(x, c, positions, ada_w, ada_b, norm1_w, norm2_w, w_in, gdn_conv_w, gdn_A_log, gdn_dt_bias, gdn_norm_w, cmp_pos_k, cmp_w1_k, cmp_w2_k, cmp_pos_v, cmp_w1_v, cmp_w2_v, w_branch_gdn, w_branch_nsa, w_out, peer_wq, peer_keys1, peer_keys2, peer_u, peer_v, final_norm_w):
    b, s, d = x.shape
    l = 0
    mod = _ada(_pad_rows(c), ada_w[l], ada_b[l])[:b]
    sh1, sc1, g1, sh2, sc2, g2 = jnp.split(mod, 6, axis=-1)
    P = _normproj(x, norm1_w[l], sc1, sh1, _permute_w_in(w_in[l]), tm=1024, tn=512, emit_h=False)
    o_a = _gdn(P, gdn_conv_w[l], gdn_A_log[l], gdn_dt_bias[l], gdn_norm_w[l])
    o_b = _nsa(P, positions, cmp_pos_k[l], cmp_w1_k[l], cmp_w2_k[l], cmp_pos_v[l], cmp_w1_v[l], cmp_w2_v[l])
    y = _merge(o_a, o_b, w_branch_gdn[l].astype(BF16), w_branch_nsa[l].astype(BF16), P)
    x1 = _outproj(y, w_out[l].astype(BF16), x, g1)
    peer = _peer(x1, norm2_w[l], sc2, sh2, peer_wq[l], peer_keys1[l], peer_keys2[l], peer_u[l], peer_v[l])
    return _final(x1, peer, g2, final_norm_w)
```

```python
import functools
import math

import numpy as np
import jax
import jax.numpy as jnp
from jax import lax
from jax.experimental import pallas as pl
from jax.experimental.pallas import tpu as pltpu

F32 = jnp.float32
BF16 = jnp.bfloat16
HI = lax.Precision.HIGHEST

LANES = 128
VMEM_LIMIT = 56 * 1024 * 1024

EPS = 1e-6
ROPE_THETA = 500000.0
HEAD_DIM = 128
ROPE_DIM = HEAD_DIM // 4
ROPE_HALF = ROPE_DIM // 2

GDN_HEADS = 8
GDN_CONV = 4
GDN_CHUNK = 64

NSA_HEADS = 16
NSA_GROUPS = 2
NSA_HPG = NSA_HEADS // NSA_GROUPS
CMP_LEN = 32
CMP_STRIDE = 16
CMP_HIDDEN = 256
SLC_LEN = 64
SLC_TOPK = 16
WIN = 512

PEER_HEADS = 8
PEER_NKEYS = 128
PEER_QDIM = 256
PEER_TOPK = 16

NEG = -1e30
BIG = 1e9

J_AQ, J_AK, J_AV, J_AZ = 0, 8, 16, 24
J_BQ = 32
N_NSA_BLOCKS = 28
J_MA, J_MB = 60, 76
J_SMALL = 92
J_GATE = 93
NJ = 96


def _cparams(sem):
    return pltpu.CompilerParams(dimension_semantics=sem, vmem_limit_bytes=VMEM_LIMIT)


def _bdot(a, b):
    return jnp.dot(a.astype(BF16), b.astype(BF16), preferred_element_type=F32)


def _bdot_nt(a, b):
    return lax.dot_general(a.astype(BF16), b.astype(BF16), (((1,), (1,)), ((), ())),
                           preferred_element_type=F32)


def _split_bf16(a):
    hi = a.astype(BF16)
    return hi, (a - hi.astype(F32)).astype(BF16)


def _dot3(a, b, exact_a=False):
    bh, bl = _split_bf16(b)
    dot = functools.partial(jnp.dot, preferred_element_type=F32)
    if exact_a:
        return dot(a, bh) + dot(a, bl)
    ah, al = _split_bf16(a)
    return dot(ah, bh) + dot(al, bh) + dot(ah, bl)


def _dot3_nt(a, b):
    ah, al = _split_bf16(a)
    bh, bl = _split_bf16(b)
    dot = functools.partial(lax.dot_general, dimension_numbers=(((1,), (1,)), ((), ())),
                            preferred_element_type=F32)
    return dot(ah, bh) + dot(al, bh) + dot(ah, bl)


def _fdot(a, b):
    return jnp.dot(a, b, precision=HI, preferred_element_type=F32)


def _fdot_nt(a, b):
    return lax.dot_general(a, b, (((1,), (1,)), ((), ())), precision=HI,
                           preferred_element_type=F32)


def _sigmoid(x):
    return 1.0 / (1.0 + jnp.exp(-x))


def _silu(x):
    return x * _sigmoid(x)


def _gelu(x):
    return 0.5 * x * (1.0 + jnp.tanh(math.sqrt(2.0 / math.pi) * (x + 0.044715 * (x * x * x))))


def _softplus(x):
    return jnp.maximum(x, 0.0) + jnp.log(1.0 + jnp.exp(-jnp.abs(x)))


def _ada_kernel(c_ref, w_ref, b_ref, o_ref):
    o_ref[...] = _fdot(_silu(c_ref[...]), w_ref[...]) + b_ref[...]


def _ada(c_pad, ada_w, ada_b):
    m, d = c_pad.shape
    n = ada_w.shape[1]
    tn = 1024
    return pl.pallas_call(
        _ada_kernel,
        grid=(n // tn,),
        in_specs=[pl.BlockSpec((m, d), lambda j: (0, 0)),
                  pl.BlockSpec((d, tn), lambda j: (0, j)),
                  pl.BlockSpec((1, tn), lambda j: (0, j))],
        out_specs=pl.BlockSpec((m, tn), lambda j: (0, j)),
        out_shape=jax.ShapeDtypeStruct((m, n), F32),
        compiler_params=_cparams(("parallel",)),
        name="ada_mod",
    )(c_pad, ada_w, ada_b.reshape(1, n))


def _normproj_kernel(x_ref, nw_ref, sc_ref, sh_ref, w_ref, *rest, nk, emit_h):
    if emit_h:
        o_ref, h_out_ref, h_scr = rest
    else:
        o_ref, h_scr = rest

    @pl.when(pl.program_id(2) == 0)
    def _():
        x = x_ref[0]
        ms = jnp.mean(x * x, axis=-1, keepdims=True)
        y = x * lax.rsqrt(ms + EPS) * nw_ref[...]
        h = y * (1.0 + sc_ref[0]) + sh_ref[0]
        h_scr[...] = h.astype(BF16)
        if emit_h:
            h_out_ref[0] = h

    acc = jnp.dot(h_scr[...], w_ref[...], preferred_element_type=F32)
    for k in range(nk):
        o_ref[0, k] = acc[:, k * LANES:(k + 1) * LANES]


def _normproj(x, nw, sc, sh, w_bf16, *, tm, tn, emit_h):
    b, s, d = x.shape
    n = w_bf16.shape[1]
    tm = min(tm, s)
    nk = tn // LANES
    out_shape = [jax.ShapeDtypeStruct((b, n // LANES, s, LANES), F32)]
    out_specs = [pl.BlockSpec((1, nk, tm, LANES), lambda bi, i, j: (bi, j, i, 0))]
    if emit_h:
        out_shape.append(jax.ShapeDtypeStruct((b, s, d), F32))
        out_specs.append(pl.BlockSpec((1, tm, d), lambda bi, i, j: (bi, i, 0)))
    res = pl.pallas_call(
        functools.partial(_normproj_kernel, nk=nk, emit_h=emit_h),
        grid=(b, s // tm, n // tn),
        in_specs=[pl.BlockSpec((1, tm, d), lambda bi, i, j: (bi, i, 0)),
                  pl.BlockSpec((1, d), lambda bi, i, j: (0, 0)),
                  pl.BlockSpec((1, 1, d), lambda bi, i, j: (bi, 0, 0)),
                  pl.BlockSpec((1, 1, d), lambda bi, i, j: (bi, 0, 0)),
                  pl.BlockSpec((d, tn), lambda bi, i, j: (0, j))],
        out_specs=out_specs,
        out_shape=out_shape,
        scratch_shapes=[pltpu.VMEM((tm, d), BF16)],
        compiler_params=_cparams(("parallel", "parallel", "arbitrary")),
        name="normproj_h" if emit_h else "normproj",
    )(x, nw.reshape(1, d), sc.reshape(b, 1, d), sh.reshape(b, 1, d), w_bf16)
    return res if emit_h else res[0]


GDN_CB = 512
GDN_GB = 256


def _gdn_local_kernel(q_ref, qh_ref, k_ref, kh_ref, v_ref, vh_ref, sm_ref,
                      cwq_ref, cwk_ref, cwv_ref, alog_ref, dtb_ref,
                      w_ref, u0_ref, qk_ref, qg_ref, kd_ref, ge_ref):
    h = pl.program_id(1)
    i = pl.program_id(2)
    C, G = GDN_CHUNK, GDN_GB
    first = (i == 0)

    def conv_act(main_ref, halo_ref, cw_ref, r0):
        if r0 == 0:
            prev = jnp.where(first, 0.0, halo_ref[0, 0])
        else:
            prev = main_ref[0, 0, r0 - 8:r0, :]
        ext = jnp.concatenate([prev, main_ref[0, 0, r0:r0 + G, :]], axis=0)
        w = cw_ref[0]
        y = w[0:1, :] * ext[5:5 + G, :]
        for j in range(1, GDN_CONV):
            y = y + w[j:j + 1, :] * ext[5 + j:5 + j + G, :]
        return _silu(y)

    rid = lax.broadcasted_iota(jnp.int32, (G, G), 0)
    cid = lax.broadcasted_iota(jnp.int32, (G, G), 1)
    same = (rid // C) == (cid // C)
    incl = jnp.logical_and(same, rid >= cid)
    strict = jnp.logical_and(same, rid > cid)
    eye = rid == cid
    is_last = cid == (rid // C) * C + (C - 1)
    joins = []
    bs = 1
    while bs < C:
        joins.append(jnp.logical_and(jnp.logical_and((rid // (2 * bs)) == (cid // (2 * bs)),
                                                     (rid & (2 * bs - 1)) >= bs),
                                     (cid & (2 * bs - 1)) < bs))
        bs *= 2
    tri = jnp.where(incl, 1.0, 0.0).astype(BF16)
    eye_f = jnp.where(eye, 1.0, 0.0).astype(F32)
    lane = lax.broadcasted_iota(jnp.int32, (G, LANES), 1)
    neg_a = -jnp.exp(alog_ref[...])
    dtb = dtb_ref[...]

    for grp in range(GDN_CB // G):
        r0 = grp * G
        q = conv_act(q_ref, qh_ref, cwq_ref, r0)
        k = conv_act(k_ref, kh_ref, cwk_ref, r0)
        v = conv_act(v_ref, vh_ref, cwv_ref, r0)
        q = q * lax.rsqrt(jnp.sum(q * q, axis=-1, keepdims=True) + EPS) * (HEAD_DIM ** -0.5)
        k = k * lax.rsqrt(jnp.sum(k * k, axis=-1, keepdims=True) + EPS)

        sm = sm_ref[0, 0, r0:r0 + G, :]
        g_all = neg_a * _softplus(sm + dtb)
        gc_all = _dot3(tri, g_all, exact_a=True)
        gc = jnp.sum(jnp.where(lane == h, gc_all, 0.0), axis=1, keepdims=True)
        beta = jnp.sum(jnp.where(lane == GDN_HEADS + h, _sigmoid(sm), 0.0), axis=1, keepdims=True)
        gc_row = jnp.sum(jnp.where(eye, gc, 0.0), axis=0, keepdims=True)
        gc_last = jnp.sum(jnp.where(is_last, gc_row, 0.0), axis=1, keepdims=True)
        decay = jnp.where(incl, jnp.exp(jnp.where(incl, gc - gc_row, 0.0)), 0.0)
        gamma = jnp.exp(gc)

        kk = _dot3_nt(k, k)
        lmat = jnp.where(strict, decay * kk, 0.0) * beta
        tinv = eye_f - jnp.where(joins[0], lmat, 0.0)
        for lvl in range(1, len(joins)):
            tinv = tinv - _dot3(_dot3(tinv, jnp.where(joins[lvl], lmat, 0.0)), tinv)
        rhs = jnp.concatenate([(beta * gamma) * k, beta * v], axis=1)
        wu = _dot3(tinv, rhs)
        qk = decay * _bdot_nt(q, k)

        rows = slice(r0, r0 + G)
        w_ref[0, 0, rows, :] = wu[:, :HEAD_DIM].astype(BF16)
        u0_ref[0, 0, rows, :] = wu[:, HEAD_DIM:]
        qg_ref[0, 0, rows, :] = (gamma * q).astype(BF16)
        kd_ref[0, 0, rows, :] = (jnp.exp(gc_last - gc) * k).astype(BF16)
        ge_all = jnp.broadcast_to(jnp.exp(gc_last), (G, LANES))
        for c in range(G // C):
            cr = slice(c * C, (c + 1) * C)
            qk_ref[0, 0, r0 + c * C:r0 + (c + 1) * C, :] = qk[cr, cr].astype(BF16)
            ge_ref[0, 0, pl.ds((i * (GDN_CB // G) + grp) * (G // C) + c, 1), :] = ge_all[c * C:c * C + 1, :]


def _gdn_local(P, cw, alog_pad, dtb_pad):
    b, _, s, _ = P.shape
    H, CB, C = GDN_HEADS, GDN_CB, GDN_CHUNK
    n = s // C

    def main(j0):
        return pl.BlockSpec((1, 1, CB, LANES), lambda bi, h, i: (bi, j0 + h, i, 0))

    def halo(j0):
        return pl.BlockSpec((1, 1, 8, LANES),
                            lambda bi, h, i: (bi, j0 + h, jnp.maximum(i * (CB // 8) - 1, 0), 0))

    def cws(j0):
        return pl.BlockSpec((1, GDN_CONV, LANES), lambda bi, h, i: (j0 + h, 0, 0))

    row = pl.BlockSpec((1, LANES), lambda bi, h, i: (0, 0))
    hs = lambda width: pl.BlockSpec((1, 1, CB, width), lambda bi, h, i: (bi, h, i, 0))
    return pl.pallas_call(
        _gdn_local_kernel,
        grid=(b, H, s // CB),
        in_specs=[main(J_AQ), halo(J_AQ), main(J_AK), halo(J_AK), main(J_AV), halo(J_AV),
                  pl.BlockSpec((1, 1, CB, LANES), lambda bi, h, i: (bi, J_SMALL, i, 0)),
                  cws(0), cws(8), cws(16), row, row],
        out_specs=[hs(LANES), hs(LANES), hs(C), hs(LANES), hs(LANES),
                   pl.BlockSpec((1, 1, n, LANES), lambda bi, h, i: (bi, h, 0, 0))],
        out_shape=[jax.ShapeDtypeStruct((b, H, s, LANES), BF16),
                   jax.ShapeDtypeStruct((b, H, s, LANES), F32),
                   jax.ShapeDtypeStruct((b, H, s, C), BF16),
                   jax.ShapeDtypeStruct((b, H, s, LANES), BF16),
                   jax.ShapeDtypeStruct((b, H, s, LANES), BF16),
                   jax.ShapeDtypeStruct((b, H, n, LANES), F32)],
        compiler_params=_cparams(("parallel", "parallel", "arbitrary")),
        name="gdn_local",
    )(P, P, P, P, P, P, P, cw, cw, cw, alog_pad, dtb_pad)


GDN_HB = 4


def _gdn_scan_kernel(w_ref, u0_ref, qk_ref, qg_ref, kd_ref, ge_ref, z_ref, nw_ref, o_ref, s_scr,
                     *, sb):
    C = GDN_CHUNK

    @pl.when(pl.program_id(2) == 0)
    def _():
        s_scr[...] = jnp.zeros_like(s_scr)

    nw = nw_ref[...]

    def body(n, carry):
        r = pl.multiple_of(n * C, C)
        for hh in range(GDN_HB):
            st = s_scr[hh]
            stb = st.astype(BF16)
            u = u0_ref[0, hh, pl.ds(r, C), :] - jnp.dot(w_ref[0, hh, pl.ds(r, C), :], stb,
                                                       preferred_element_type=F32)
            ub = u.astype(BF16)
            o = (jnp.dot(qg_ref[0, hh, pl.ds(r, C), :], stb, preferred_element_type=F32)
                 + jnp.dot(qk_ref[0, hh, pl.ds(r, C), :], ub, preferred_element_type=F32))
            ge = ge_ref[0, hh, pl.ds(n, 1), :]
            s_scr[hh] = ge * st + lax.dot_general(kd_ref[0, hh, pl.ds(r, C), :], ub,
                                                  (((0,), (0,)), ((), ())),
                                                  preferred_element_type=F32)
            on = o * lax.rsqrt(jnp.mean(o * o, axis=-1, keepdims=True) + EPS) * nw
            z = z_ref[0, hh, pl.ds(r, C), :]
            o_ref[0, pl.ds(r, C), hh * LANES:(hh + 1) * LANES] = (on * _silu(z)).astype(BF16)
        return carry

    lax.fori_loop(0, sb // C, body, 0)


def _gdn_scan(w, u0, qk, qg, kd, ge, P, norm_w):
    b, H, s, _ = w.shape
    C, HB = GDN_CHUNK, GDN_HB
    sb = min(1024, s)
    hs = lambda width: pl.BlockSpec((1, HB, sb, width), lambda bi, hb, i: (bi, hb, i, 0))
    return pl.pallas_call(
        functools.partial(_gdn_scan_kernel, sb=sb),
        grid=(b, H // HB, s // sb),
        in_specs=[hs(LANES), hs(LANES), hs(C), hs(LANES), hs(LANES),
                  pl.BlockSpec((1, HB, sb // C, LANES), lambda bi, hb, i: (bi, hb, i, 0)),
                  pl.BlockSpec((1, HB, sb, LANES), lambda bi, hb, i: (bi, J_AZ // HB + hb, i, 0)),
                  pl.BlockSpec((1, LANES), lambda bi, hb, i: (0, 0))],
        out_specs=pl.BlockSpec((1, sb, HB * LANES), lambda bi, hb, i: (bi, i, hb)),
        out_shape=jax.ShapeDtypeStruct((b, s, H * LANES), BF16),
        scratch_shapes=[pltpu.VMEM((HB, HEAD_DIM, HEAD_DIM), F32)],
        compiler_params=_cparams(("parallel", "parallel", "arbitrary")),
        name="gdn_scan",
    )(w, u0, qk, qg, kd, ge, P, norm_w.reshape(1, LANES))


def _pad_lanes_row(v):
    return jnp.pad(v.astype(F32), (0, LANES - v.shape[0])).reshape(1, LANES)


def _gdn(P, conv_w, a_log, dt_bias, norm_w):
    cw = jnp.transpose(conv_w.reshape(GDN_CONV, 3 * GDN_HEADS, LANES), (1, 0, 2))
    w, u0, qk, qg, kd, ge = _gdn_local(P, cw, _pad_lanes_row(a_log), _pad_lanes_row(dt_bias))
    return _gdn_scan(w, u0, qk, qg, kd, ge, P, norm_w)


NSA_TQ = 128
NSA_TK_SEL = 512
NSA_TK_WIN = 256
NSA_HC = 2
R_Q, R_KC, R_VC, R_KS, R_VS, R_KW, R_VW = 0, 16, 18, 20, 22, 24, 26
ROPE_NB = 4


def _rope_kernel(x_ref, pos_ref, invf_ref, o_ref, cos_scr, sin_scr):
    j = pl.program_id(2)
    lane = lax.broadcasted_iota(jnp.int32, cos_scr.shape, 1)

    @pl.when(j == 0)
    def _():
        ang = pos_ref[0] * invf_ref[...]
        sn = jnp.sin(ang)
        cos_scr[...] = jnp.where(lane < ROPE_DIM, jnp.cos(ang), 1.0)
        sin_scr[...] = jnp.where(lane < ROPE_HALF, -sn, jnp.where(lane < ROPE_DIM, sn, 0.0))

    is_q = j < NSA_HEADS // ROPE_NB
    scale = jnp.where(is_q, HEAD_DIM ** -0.5, 1.0)
    for k in range(ROPE_NB):
        x = x_ref[0, k]
        swapped = jnp.where(lane < ROPE_HALF, pltpu.roll(x, LANES - ROPE_HALF, axis=1),
                            pltpu.roll(x, ROPE_HALF, axis=1))
        rot = x * cos_scr[...] + swapped * sin_scr[...]
        out = rot * scale if k < 2 else jnp.where(is_q, rot, x) * scale
        o_ref[0, k] = out.astype(BF16)


def _rope(P, pos_f32, invf):
    b, _, s, _ = P.shape
    tr = min(512, s)
    nb = ROPE_NB
    return pl.pallas_call(
        _rope_kernel,
        grid=(b, s // tr, N_NSA_BLOCKS // nb),
        in_specs=[pl.BlockSpec((1, nb, tr, LANES), lambda bi, i, j: (bi, J_BQ // nb + j, i, 0)),
                  pl.BlockSpec((1, tr, 1), lambda bi, i, j: (bi, i, 0)),
                  pl.BlockSpec((1, LANES), lambda bi, i, j: (0, 0))],
        out_specs=pl.BlockSpec((1, nb, tr, LANES), lambda bi, i, j: (bi, j, i, 0)),
        out_shape=jax.ShapeDtypeStruct((b, N_NSA_BLOCKS, s, LANES), BF16),
        scratch_shapes=[pltpu.VMEM((tr, LANES), F32), pltpu.VMEM((tr, LANES), F32)],
        compiler_params=_cparams(("parallel", "parallel", "arbitrary")),
        name="nsa_rope",
    )(P, pos_f32, invf)


def _compress_kernel(r_ref, w1_ref, w2_ref, pos_ref, o_ref):
    r = r_ref[0, 0]
    nr = r.shape[0]
    half = CMP_STRIDE * HEAD_DIM
    a = jnp.dot(r, w1_ref[0, :half, :], preferred_element_type=F32)
    bm = jnp.dot(r, w1_ref[0, half:, :], preferred_element_type=F32)
    pos8 = jnp.broadcast_to(pos_ref[0], (8, CMP_LEN * HEAD_DIM)).astype(BF16)
    pb = jnp.dot(pos8, w1_ref[0], preferred_element_type=F32)[0:1, :]
    hid = a + pltpu.roll(bm, nr - 1, axis=0) + pb
    out = jnp.dot(_gelu(hid).astype(BF16), w2_ref[0], preferred_element_type=F32)
    row = lax.broadcasted_iota(jnp.int32, out.shape, 0)
    o_ref[0, 0] = jnp.where(row < nr - 1, out, 0.0).astype(BF16)


def _compress(rows, w1, w2, pos):
    b, _, nr, width = rows.shape
    return pl.pallas_call(
        _compress_kernel,
        grid=(b, 4),
        in_specs=[pl.BlockSpec((1, 1, nr, width), lambda bi, j: (bi, j, 0, 0)),
                  pl.BlockSpec((1, CMP_LEN * HEAD_DIM, CMP_HIDDEN), lambda bi, j: (j // 2, 0, 0)),
                  pl.BlockSpec((1, CMP_HIDDEN, HEAD_DIM), lambda bi, j: (j // 2, 0, 0)),
                  pl.BlockSpec((1, 1, CMP_LEN * HEAD_DIM), lambda bi, j: (j // 2, 0, 0))],
        out_specs=pl.BlockSpec((1, 1, nr, HEAD_DIM), lambda bi, j: (bi, j, 0, 0)),
        out_shape=jax.ShapeDtypeStruct((b, 4, nr, HEAD_DIM), BF16),
        compiler_params=_cparams(("parallel", "arbitrary")),
        name="nsa_compress",
    )(rows, w1, w2, pos)


def _nsa_kernel(q_ref, kc_ref, vc_ref, ks_ref, vs_ref, kw_ref, vw_ref, gate_ref, ovt_ref, e_ref,
                o_ref, m_scr, l_scr, acc_scr, os_scr, *, s_len):
    i = pl.program_id(2)
    TQ, HP = NSA_TQ, NSA_HPG
    R = HP * TQ
    nsb = s_len // SLC_LEN
    ncp = s_len // CMP_STRIDE
    n_sel = min(SLC_TOPK, nsb)
    s0 = i * TQ
    q2 = q_ref[0].reshape(R, HEAD_DIM)

    sc = _bdot_nt(q2, kc_ref[0, 0])
    tq_r = s0 + (lax.broadcasted_iota(jnp.int32, (R, ncp), 0) & (TQ - 1))
    ncol = lax.broadcasted_iota(jnp.int32, (R, ncp), 1)
    valid = jnp.logical_and(ncol * CMP_STRIDE + (CMP_LEN - 1) <= tq_r, ncol < ncp - 1)
    scm = jnp.where(valid, sc, NEG)
    e = jnp.where(valid, jnp.exp(scm - jnp.max(scm, axis=1, keepdims=True)), 0.0)
    p_c = e / jnp.maximum(jnp.sum(e, axis=1, keepdims=True), 1e-30)
    o_c = _bdot(p_c, vc_ref[0, 0])
    psum = jnp.sum(p_c.reshape(HP, TQ, ncp), axis=0)
    imp = _fdot_nt(ovt_ref[...], psum)
    blk = lax.broadcasted_iota(jnp.int32, (nsb, TQ), 0)
    tq_l = s0 + lax.broadcasted_iota(jnp.int32, (nsb, TQ), 1)
    cur = tq_l // SLC_LEN
    forced = jnp.logical_or(blk == 0, jnp.logical_or(blk == cur, blk == cur - 1))
    imp = jnp.where(forced, BIG, imp)
    imp = jnp.where(blk * SLC_LEN <= tq_l, imp, NEG)
    rank = jnp.zeros((nsb, TQ), jnp.int32)
    for j in range(nsb):
        rj = imp[j:j + 1, :]
        beats = jnp.logical_or(rj > imp, jnp.logical_and(rj == imp, blk > j))
        rank = rank + jnp.where(beats, 1, 0)
    sel_t = jnp.where(rank < n_sel, 1.0, 0.0).astype(F32)
    if nsb < LANES:
        sel_t = jnp.concatenate([sel_t, jnp.zeros((LANES - nsb, TQ), F32)], axis=0)
    sel = jnp.transpose(sel_t).astype(BF16)

    ones_blk = jnp.ones((TQ, HEAD_DIM), BF16)
    HC = NSA_HC

    def flash(k_ref, v_ref, tk, lo, hi, mask_fn):
        m_scr[...] = jnp.full(m_scr.shape, NEG, F32)
        l_scr[...] = jnp.zeros(l_scr.shape, F32)
        acc_scr[...] = jnp.zeros(acc_scr.shape, F32)
        qpos = s0 + lax.broadcasted_iota(jnp.int32, (TQ, tk), 0)
        kofs = lax.broadcasted_iota(jnp.int32, (TQ, tk), 1)
        ones_v = jnp.concatenate([ones_blk] * (tk // TQ), axis=0)

        def body(j, carry):
            r = pl.multiple_of(j * tk, tk)
            bias = jnp.where(mask_fn(r, r + kofs, qpos), 0.0, NEG)
            kb = k_ref[0, 0, pl.ds(r, tk), :]
            vb = jnp.concatenate([v_ref[0, 0, pl.ds(r, tk), :], ones_v], axis=1)
            for c in range(HP // HC):
                rows = slice(c * HC * TQ, (c + 1) * HC * TQ)
                s = _bdot_nt(q_ref[0, c * HC:(c + 1) * HC].reshape(HC * TQ, HEAD_DIM), kb)
                s = (s.reshape(HC, TQ, tk) + bias[None]).reshape(HC * TQ, tk)
                m_prev = m_scr[rows, :]
                m_new = jnp.maximum(m_prev, jnp.max(s, axis=1, keepdims=True))
                alpha = jnp.exp(m_prev - m_new)
                p = jnp.exp(s - jnp.concatenate([m_new] * (tk // LANES), axis=1)).astype(BF16)
                pv = jnp.dot(p, vb, preferred_element_type=F32)
                l_scr[rows, :] = alpha * l_scr[rows, :] + pv[:, HEAD_DIM:]
                acc_scr[rows, :] = alpha * acc_scr[rows, :] + pv[:, :HEAD_DIM]
                m_scr[rows, :] = m_new
            return carry

        lax.fori_loop(lo, hi, body, 0)

    tks = min(NSA_TK_SEL, s_len)

    def sel_mask(r, kpos, qpos):
        chosen = jnp.dot(sel, e_ref[:, pl.ds(r, tks)], preferred_element_type=F32)
        return jnp.logical_and(chosen > 0.5, kpos <= qpos)

    flash(ks_ref, vs_ref, tks, 0, (s0 + TQ - 1) // tks + 1, sel_mask)
    os_scr[...] = acc_scr[...] / l_scr[...]

    def win_mask(r, kpos, qpos):
        return jnp.logical_and(kpos <= qpos, kpos > qpos - WIN)

    tkw = min(NSA_TK_WIN, s_len)
    flash(kw_ref, vw_ref, tkw, jnp.maximum(s0 - WIN + 1, 0) // tkw, (s0 + TQ - 1) // tkw + 1, win_mask)

    gates = _sigmoid(gate_ref[0, 0])
    for hh in range(HP):
        rows = slice(hh * TQ, (hh + 1) * TQ)
        o_w = acc_scr[rows, :] / l_scr[rows, :]
        out = (gates[:, 3 * hh:3 * hh + 1] * o_c[rows, :]
               + gates[:, 3 * hh + 1:3 * hh + 2] * os_scr[rows, :]
               + gates[:, 3 * hh + 2:3 * hh + 3] * o_w)
        o_ref[0, :, hh * HEAD_DIM:(hh + 1) * HEAD_DIM] = out.astype(BF16)


def _nsa_attend(rp, cmp_kv, P, ovt, expand):
    b, _, s, _ = rp.shape
    TQ, HP, G = NSA_TQ, NSA_HPG, NSA_GROUPS
    ncp = cmp_kv.shape[2]
    R = HP * TQ
    full = lambda j0: pl.BlockSpec((1, 1, s, HEAD_DIM), lambda bi, g, i: (bi, j0 + g, 0, 0))
    cmp_spec = lambda j0: pl.BlockSpec((1, 1, ncp, HEAD_DIM), lambda bi, g, i: (bi, j0 + g, 0, 0))
    return pl.pallas_call(
        functools.partial(_nsa_kernel, s_len=s),
        grid=(b, G, s // TQ),
        in_specs=[pl.BlockSpec((1, HP, TQ, HEAD_DIM), lambda bi, g, i: (bi, g, i, 0)),
                  cmp_spec(0), cmp_spec(2),
                  full(R_KS), full(R_VS), full(R_KW), full(R_VW),
                  pl.BlockSpec((1, 1, TQ, LANES), lambda bi, g, i: (bi, J_GATE + g, i, 0)),
                  pl.BlockSpec(ovt.shape, lambda bi, g, i: (0, 0)),
                  pl.BlockSpec(expand.shape, lambda bi, g, i: (0, 0))],
        out_specs=pl.BlockSpec((1, TQ, HP * HEAD_DIM), lambda bi, g, i: (bi, i, g)),
        out_shape=jax.ShapeDtypeStruct((b, s, NSA_HEADS * HEAD_DIM), BF16),
        scratch_shapes=[pltpu.VMEM((R, LANES), F32), pltpu.VMEM((R, LANES), F32),
                        pltpu.VMEM((R, HEAD_DIM), F32), pltpu.VMEM((R, HEAD_DIM), F32)],
        compiler_params=_cparams(("parallel", "parallel", "arbitrary")),
        name="nsa_attend",
    )(rp, cmp_kv, cmp_kv, rp, rp, rp, rp, P, ovt, expand)


def _nsa(P, positions, cmp_pos_k, cmp_w1_k, cmp_w2_k, cmp_pos_v, cmp_w1_v, cmp_w2_v):
    b, _, s, _ = P.shape
    lanes = np.arange(LANES)
    invf = np.where(lanes < ROPE_DIM, ROPE_THETA ** (-(lanes % ROPE_HALF) / ROPE_HALF), 0.0)
    rp = _rope(P, positions.astype(F32).reshape(b, s, 1), jnp.asarray(invf, F32).reshape(1, LANES))

    ncp = s // CMP_STRIDE
    rows = rp[:, R_KC:R_KC + 4].reshape(b, 4, ncp, CMP_STRIDE * HEAD_DIM)
    w1 = jnp.stack([cmp_w1_k, cmp_w1_v]).astype(BF16)
    w2 = jnp.stack([cmp_w2_k, cmp_w2_v]).astype(BF16)
    pos = jnp.stack([cmp_pos_k, cmp_pos_v]).reshape(2, 1, CMP_LEN * HEAD_DIM)
    cmp_kv = _compress(rows, w1, w2, pos)

    nsb = s // SLC_LEN
    cmp_start = np.arange(ncp) * CMP_STRIDE
    slc_start = np.arange(nsb) * SLC_LEN
    ov = ((cmp_start[:, None] < slc_start[None, :] + SLC_LEN)
          & (cmp_start[:, None] + CMP_LEN > slc_start[None, :])
          & (np.arange(ncp)[:, None] < ncp - 1))
    ovt = jnp.asarray(ov.T.astype(np.float32))
    expand = (np.arange(LANES)[:, None] == (np.arange(s)[None, :] // SLC_LEN))
    expand = jnp.asarray(expand.astype(np.float32), BF16)
    return _nsa_attend(rp, cmp_kv, P, ovt, expand)


MERGE_TN = 512


def _merge_kernel(oa_ref, ob_ref, wg_ref, wn_ref, ma_ref, mb_ref, y_ref):
    ya = jnp.dot(oa_ref[0], wg_ref[...], preferred_element_type=F32)
    yb = jnp.dot(ob_ref[0], wn_ref[...], preferred_element_type=F32)
    for k in range(MERGE_TN // LANES):
        cols = slice(k * LANES, (k + 1) * LANES)
        y = _sigmoid(ma_ref[0, k]) * ya[:, cols] + _sigmoid(mb_ref[0, k]) * yb[:, cols]
        y_ref[0, :, cols] = y.astype(BF16)


def _merge(o_a, o_b, wg, wn, P):
    b, s, da = o_a.shape
    db = o_b.shape[2]
    d = wg.shape[1]
    tm, tn = min(512, s), MERGE_TN
    nk = tn // LANES
    return pl.pallas_call(
        _merge_kernel,
        grid=(b, s // tm, d // tn),
        in_specs=[pl.BlockSpec((1, tm, da), lambda bi, i, j: (bi, i, 0)),
                  pl.BlockSpec((1, tm, db), lambda bi, i, j: (bi, i, 0)),
                  pl.BlockSpec((da, tn), lambda bi, i, j: (0, j)),
                  pl.BlockSpec((db, tn), lambda bi, i, j: (0, j)),
                  pl.BlockSpec((1, nk, tm, LANES), lambda bi, i, j: (bi, J_MA // nk + j, i, 0)),
                  pl.BlockSpec((1, nk, tm, LANES), lambda bi, i, j: (bi, J_MB // nk + j, i, 0))],
        out_specs=pl.BlockSpec((1, tm, tn), lambda bi, i, j: (bi, i, j)),
        out_shape=jax.ShapeDtypeStruct((b, s, d), BF16),
        compiler_params=_cparams(("parallel", "parallel", "arbitrary")),
        name="merge",
    )(o_a, o_b, wg, wn, P, P)


def _outproj_kernel(y_ref, w_ref, x_ref, g_ref, o_ref):
    o_ref[0] = x_ref[0] + g_ref[0] * jnp.dot(y_ref[0], w_ref[...], preferred_element_type=F32)


def _outproj(y, w, x, g):
    b, s, d = x.shape
    tm, tn = min(512, s), 512
    return pl.pallas_call(
        _outproj_kernel,
        grid=(b, s // tm, d // tn),
        in_specs=[pl.BlockSpec((1, tm, d), lambda bi, i, j: (bi, i, 0)),
                  pl.BlockSpec((d, tn), lambda bi, i, j: (0, j)),
                  pl.BlockSpec((1, tm, tn), lambda bi, i, j: (bi, i, j)),
                  pl.BlockSpec((1, 1, tn), lambda bi, i, j: (bi, 0, j))],
        out_specs=pl.BlockSpec((1, tm, tn), lambda bi, i, j: (bi, i, j)),
        out_shape=jax.ShapeDtypeStruct((b, s, d), F32),
        compiler_params=_cparams(("parallel", "parallel", "arbitrary")),
        name="outproj",
    )(y, w, x, g.reshape(b, 1, d))


PEER_TB = 256
PEER_TG = 64
PEER_AHEAD = 4
PEER_SLOTS = 8
PEER_SEL = PEER_HEADS * PEER_TOPK
MIN_F32 = -3.0e38


def _topk_rows(vals, payload=None):
    nrow = vals.shape[0]
    rid = lax.broadcasted_iota(jnp.int32, vals.shape, 0)
    out_v, out_i = [], []
    for _ in range(PEER_TOPK):
        m = jnp.max(vals, axis=0, keepdims=True)
        idx = jnp.min(jnp.where(vals == m, rid, nrow), axis=0, keepdims=True)
        hit = rid == idx
        out_v.append(m)
        if payload is None:
            out_i.append(idx)
        else:
            out_i.append(jnp.sum(jnp.where(hit, payload, 0), axis=0, keepdims=True))
        vals = jnp.where(hit, MIN_F32, vals)
    return jnp.concatenate(out_v, axis=0), jnp.concatenate(out_i, axis=0)


def _peer_topk_kernel(qh_ref, k1_ref, k2_ref, eidx_ref, gw_ref):
    def head(h, carry):
        s1 = _fdot_nt(k1_ref[h], qh_ref[0, 2 * h])
        s2 = _fdot_nt(k2_ref[h], qh_ref[0, 2 * h + 1])
        v1, i1 = _topk_rows(s1)
        v2, i2 = _topk_rows(s2)
        keep = [PEER_TOPK // (a + 1) for a in range(PEER_TOPK)]
        npad = (-sum(keep)) % 8
        cand = jnp.concatenate([v1[a:a + 1, :] + v2[0:keep[a], :] for a in range(PEER_TOPK)]
                               + [jnp.full((npad, v1.shape[1]), MIN_F32, F32)], axis=0)
        cidx = jnp.concatenate([i1[a:a + 1, :] * PEER_NKEYS + i2[0:keep[a], :] for a in range(PEER_TOPK)]
                               + [jnp.zeros((npad, v1.shape[1]), jnp.int32)], axis=0)
        sc, eidx = _topk_rows(cand, cidx)
        ex = jnp.exp(sc - sc[0:1, :])
        eidx_ref[h] = eidx
        gw_ref[h] = ex / jnp.sum(ex, axis=0, keepdims=True)
        return carry

    lax.fori_loop(0, PEER_HEADS, head, 0)


def _peer_topk(qh, keys1, keys2):
    b, nb, s, _ = qh.shape
    tb = min(PEER_TB, s)
    nt = s // tb
    kspec = pl.BlockSpec(keys1.shape, lambda bi, i: (0, 0, 0))
    ospec = pl.BlockSpec((PEER_HEADS, PEER_TOPK, tb), lambda bi, i: (0, 0, bi * nt + i))
    return pl.pallas_call(
        _peer_topk_kernel,
        grid=(b, nt),
        in_specs=[pl.BlockSpec((1, nb, tb, LANES), lambda bi, i: (bi, 0, i, 0)), kspec, kspec],
        out_specs=[ospec, ospec],
        out_shape=[jax.ShapeDtypeStruct((PEER_HEADS, PEER_TOPK, b * s), jnp.int32),
                   jax.ShapeDtypeStruct((PEER_HEADS, PEER_TOPK, b * s), F32)],
        compiler_params=_cparams(("parallel", "parallel")),
        name="peer_topk",
    )(qh, keys1, keys2)


def _peer_gather_kernel(idx_ref, idxn_ref, gw_ref, x_ref, uv_hbm, o_ref, buf, a_scr, sem, *, tg):
    NS, D, NSLOT = PEER_SEL, PEER_AHEAD, PEER_SLOTS
    RPE = uv_hbm.shape[1]
    SUBW = RPE // 2
    half = SUBW * LANES

    def wait_rows(t):
        slot = t & (NSLOT - 1)
        pltpu.make_async_copy(uv_hbm.at[pl.ds(0, NS)], buf.at[slot], sem.at[slot]).wait()

    rid = lax.broadcasted_iota(jnp.int32, (NS, NS), 0)
    cid = lax.broadcasted_iota(jnp.int32, (NS, NS), 1)
    eye = rid == cid
    ones = jnp.ones((LANES, LANES), BF16)

    def lane_sum_rep(m):
        hi, lo = _split_bf16(m)
        return (jnp.dot(hi, ones, preferred_element_type=F32)
                + jnp.dot(lo, ones, preferred_element_type=F32))

    def halves(slot, tab, sg):
        w = buf[slot, tab * SUBW + sg]
        return pltpu.bitcast(w << 16, F32), pltpu.bitcast(w & jnp.uint32(0xFFFF0000), F32)

    def phase(t, issue_from, do_dot, do_prev):
        if do_dot:
            wait_rows(t)
            xrow = x_ref[pl.ds(t, 1), :]
            acc = jnp.zeros((NS, LANES), F32)
        if do_prev:
            act = lane_sum_rep(a_scr[(t - 1) & 1])
            gw_rep = lane_sum_rep(jnp.where(eye, gw_ref[pl.ds(t - 1, 1), :], 0.0))
            coef = gw_rep * _gelu(act)
            out_lo, out_hi = [], []
        if issue_from is not None:
            slot_i = (t + D) & (NSLOT - 1)
            base_i = (t + D) * NS if issue_from is idx_ref else (t + D - tg) * NS
        for sg in range(SUBW):
            if issue_from is not None:
                for j in range(sg * NS // SUBW, (sg + 1) * NS // SUBW):
                    pltpu.make_async_copy(uv_hbm.at[issue_from[base_i + j]],
                                          buf.at[slot_i, :, j, :], sem.at[slot_i]).start()
            if do_dot:
                lo, hi = halves(t & (NSLOT - 1), 0, sg)
                acc = acc + lo * xrow[:, sg * LANES:(sg + 1) * LANES]
                acc = acc + hi * xrow[:, half + sg * LANES:half + (sg + 1) * LANES]
            if do_prev:
                lo, hi = halves((t - 1) & (NSLOT - 1), 1, sg)
                out_lo.append(jnp.sum(lo * coef, axis=0, keepdims=True))
                out_hi.append(jnp.sum(hi * coef, axis=0, keepdims=True))
        if do_dot:
            a_scr[t & 1] = acc
        if do_prev:
            o_ref[t - 1] = jnp.concatenate(out_lo + out_hi, axis=0)

    def ramp(t, carry):
        slot = t & (NSLOT - 1)

        def body(j, c):
            pltpu.make_async_copy(uv_hbm.at[idx_ref[t * NS + j]], buf.at[slot, :, j, :], sem.at[slot]).start()
            return c
        return lax.fori_loop(0, NS, body, carry)

    def looped(issue_from):
        def f(t, carry):
            phase(t, issue_from, True, True)
            return carry
        return f

    step = pl.program_id(0)
    last = pl.num_programs(0) - 1

    @pl.when(step == 0)
    def _():
        lax.fori_loop(0, D, ramp, 0)

    phase(0, idx_ref, True, False)
    lax.fori_loop(1, tg - D, looped(idx_ref), 0)

    @pl.when(step < last)
    def _():
        lax.fori_loop(tg - D, tg, looped(idxn_ref), 0)

    @pl.when(step == last)
    def _():
        lax.fori_loop(tg - D, tg, looped(None), 0)

    phase(tg, None, False, True)


def _peer_gather(eidx, gw, h2r, uvc):
    t = eidx.shape[0]
    tg = min(PEER_TG, t)
    d = h2r.shape[1]
    rpe = uvc.shape[1]
    return pl.pallas_call(
        functools.partial(_peer_gather_kernel, tg=tg),
        grid=(t // tg,),
        in_specs=[pl.BlockSpec((tg * PEER_SEL,), lambda i: (i,), memory_space=pltpu.SMEM),
                  pl.BlockSpec((tg * PEER_SEL,), lambda i: (jnp.minimum(i + 1, t // tg - 1),),
                               memory_space=pltpu.SMEM),
                  pl.BlockSpec((tg, PEER_SEL), lambda i: (i, 0)),
                  pl.BlockSpec((tg, d), lambda i: (i, 0)),
                  pl.BlockSpec(memory_space=pl.ANY)],
        out_specs=pl.BlockSpec((tg, d // LANES, LANES), lambda i: (i, 0, 0)),
        out_shape=jax.ShapeDtypeStruct((t, d // LANES, LANES), F32),
        scratch_shapes=[pltpu.VMEM((PEER_SLOTS, rpe, PEER_SEL, LANES), jnp.uint32),
                        pltpu.VMEM((2, PEER_SEL, LANES), F32),
                        pltpu.SemaphoreType.DMA((PEER_SLOTS,))],
        compiler_params=_cparams(("arbitrary",)),
        name="peer_gather",
    )(eidx.reshape(-1), eidx.reshape(-1), gw, h2r, uvc)


def _pack_halves(tab):
    e, d = tab.shape
    bits = lax.bitcast_convert_type(tab.astype(BF16), jnp.uint16).astype(jnp.uint32)
    bits = bits.reshape(e, 2, d // (2 * LANES), LANES)
    return bits[:, 0] | (bits[:, 1] << 16)


def _peer(x1, norm_w, sc, sh, wq, keys1, keys2, u, v):
    b, s, d = x1.shape
    qh, h2 = _normproj(x1, norm_w, sc, sh, wq.astype(BF16), tm=512, tn=512, emit_h=True)
    eidx, gw = _peer_topk(qh, keys1, keys2)
    eidx = jnp.transpose(eidx, (2, 0, 1)).reshape(b * s, PEER_SEL)
    gw = jnp.transpose(gw, (2, 0, 1)).reshape(b * s, PEER_SEL)
    uvc = jnp.concatenate([_pack_halves(u), _pack_halves(v)], axis=1)
    out = _peer_gather(eidx, gw, h2.reshape(b * s, d), uvc)
    return out.reshape(b, s, d)


def _final_kernel(x_ref, p_ref, g_ref, w_ref, o_ref):
    x = x_ref[0] + g_ref[0] * p_ref[0]
    o_ref[0] = x * lax.rsqrt(jnp.mean(x * x, axis=-1, keepdims=True) + EPS) * w_ref[...]


def _final(x1, peer, g2, wf):
    b, s, d = x1.shape
    tm = min(512, s)
    blk = pl.BlockSpec((1, tm, d), lambda bi, i: (bi, i, 0))
    return pl.pallas_call(
        _final_kernel,
        grid=(b, s // tm),
        in_specs=[blk, blk, pl.BlockSpec((1, 1, d), lambda bi, i: (bi, 0, 0)),
                  pl.BlockSpec((1, d), lambda bi, i: (0, 0))],
        out_specs=blk,
        out_shape=jax.ShapeDtypeStruct((b, s, d), F32),
        compiler_params=_cparams(("parallel", "parallel")),
        name="final_norm",
    )(x1, peer, g2.reshape(b, 1, d), wf.reshape(1, d))


def _pad_rows(a, mult=8):
    pad = (-a.shape[0]) % mult
    return jnp.pad(a, ((0, pad), (0, 0)))


def _permute_w_in(w):
    o_aa, o_bq, o_bg, o_ma, o_end = 4096, 4112, 7696, 7744, 11840
    d = w.shape[0]
    z = lambda n: jnp.zeros((d, n), w.dtype)
    hg = 3 * NSA_HPG
    cols = [w[:, 0:o_aa], w[:, o_bq:o_bg], w[:, o_ma:o_end],
            w[:, o_aa:o_bq], z(LANES - 16),
            w[:, o_bg:o_bg + hg], z(LANES - hg),
            w[:, o_bg + hg:o_ma], z(LANES - hg),
            z(LANES)]
    return jnp.concatenate(cols, axis=1).astype(BF16)


def kernel(x, c, positions, ada_w, ada_b, norm1_w, norm2_w, w_in, gdn_conv_w, gdn_A_log, gdn_dt_bias, gdn_norm_w, cmp_pos_k, cmp_w1_k, cmp_w2_k, cmp_pos_v, cmp_w1_v, cmp_w2_v, w_branch_gdn, w_branch_nsa, w_out, peer_wq, peer_keys1, peer_keys2, peer_u, peer_v, final_norm_w):
    b, s, d = x.shape
    l = 0
    mod = _ada(_pad_rows(c), ada_w[l], ada_b[l])[:b]
    sh1, sc1, g1, sh2, sc2, g2 = jnp.split(mod, 6, axis=-1)
    P = _normproj(x, norm1_w[l], sc1, sh1, _permute_w_in(w_in[l]), tm=1024, tn=512, emit_h=False)
    o_a = _gdn(P, gdn_conv_w[l], gdn_A_log[l], gdn_dt_bias[l], gdn_norm_w[l])
    o_b = _nsa(P, positions, cmp_pos_k[l], cmp_w1_k[l], cmp_w2_k[l], cmp_pos_v[l], cmp_w1_v[l], cmp_w2_v[l])
    y = _merge(o_a, o_b, w_branch_gdn[l].astype(BF16), w_branch_nsa[l].astype(BF16), P)
    x1 = _outproj(y, w_out[l].astype(BF16), x, g1)
    peer = _peer(x1, norm2_w[l], sc2, sh2, peer_wq[l], peer_keys1[l], peer_keys2[l], peer_u[l], peer_v[l])
    return _final(x1, peer, g2, final_norm_w)
```

```python
import functools
import math

import numpy as np
import jax
import jax.numpy as jnp
from jax import lax
from jax.experimental import pallas as pl
from jax.experimental.pallas import tpu as pltpu

F32 = jnp.float32
BF16 = jnp.bfloat16
HI = lax.Precision.HIGHEST

LANES = 128
VMEM_LIMIT = 56 * 1024 * 1024

EPS = 1e-6
ROPE_THETA = 500000.0
HEAD_DIM = 128
ROPE_DIM = HEAD_DIM // 4
ROPE_HALF = ROPE_DIM // 2

GDN_HEADS = 8
GDN_CONV = 4
GDN_CHUNK = 64

NSA_HEADS = 16
NSA_GROUPS = 2
NSA_HPG = NSA_HEADS // NSA_GROUPS
CMP_LEN = 32
CMP_STRIDE = 16
CMP_HIDDEN = 256
SLC_LEN = 64
SLC_TOPK = 16
WIN = 512

PEER_HEADS = 8
PEER_NKEYS = 128
PEER_QDIM = 256
PEER_TOPK = 16

NEG = -1e30
BIG = 1e9

J_AQ, J_AK, J_AV, J_AZ = 0, 8, 16, 24
J_BQ = 32
N_NSA_BLOCKS = 28
J_MA, J_MB = 60, 76
J_SMALL = 92
J_GATE = 93
NJ = 96


def _cparams(sem):
    return pltpu.CompilerParams(dimension_semantics=sem, vmem_limit_bytes=VMEM_LIMIT)


def _bdot(a, b):
    return jnp.dot(a.astype(BF16), b.astype(BF16), preferred_element_type=F32)


def _bdot_nt(a, b):
    return lax.dot_general(a.astype(BF16), b.astype(BF16), (((1,), (1,)), ((), ())),
                           preferred_element_type=F32)


def _split_bf16(a):
    hi = a.astype(BF16)
    return hi, (a - hi.astype(F32)).astype(BF16)


def _dot3(a, b, exact_a=False):
    bh, bl = _split_bf16(b)
    dot = functools.partial(jnp.dot, preferred_element_type=F32)
    if exact_a:
        return dot(a, bh) + dot(a, bl)
    ah, al = _split_bf16(a)
    return dot(ah, bh) + dot(al, bh) + dot(ah, bl)


def _dot3_nt(a, b):
    ah, al = _split_bf16(a)
    bh, bl = _split_bf16(b)
    dot = functools.partial(lax.dot_general, dimension_numbers=(((1,), (1,)), ((), ())),
                            preferred_element_type=F32)
    return dot(ah, bh) + dot(al, bh) + dot(ah, bl)


def _fdot(a, b):
    return jnp.dot(a, b, precision=HI, preferred_element_type=F32)


def _fdot_nt(a, b):
    return lax.dot_general(a, b, (((1,), (1,)), ((), ())), precision=HI,
                           preferred_element_type=F32)


def _sigmoid(x):
    return 1.0 / (1.0 + jnp.exp(-x))


def _silu(x):
    return x * _sigmoid(x)


def _gelu(x):
    return 0.5 * x * (1.0 + jnp.tanh(math.sqrt(2.0 / math.pi) * (x + 0.044715 * (x * x * x))))


def _softplus(x):
    return jnp.maximum(x, 0.0) + jnp.log(1.0 + jnp.exp(-jnp.abs(x)))


def _ada_kernel(c_ref, w_ref, b_ref, o_ref):
    o_ref[...] = _fdot(_silu(c_ref[...]), w_ref[...]) + b_ref[...]


def _ada(c_pad, ada_w, ada_b):
    m, d = c_pad.shape
    n = ada_w.shape[1]
    tn = 1024
    return pl.pallas_call(
        _ada_kernel,
        grid=(n // tn,),
        in_specs=[pl.BlockSpec((m, d), lambda j: (0, 0)),
                  pl.BlockSpec((d, tn), lambda j: (0, j)),
                  pl.BlockSpec((1, tn), lambda j: (0, j))],
        out_specs=pl.BlockSpec((m, tn), lambda j: (0, j)),
        out_shape=jax.ShapeDtypeStruct((m, n), F32),
        compiler_params=_cparams(("parallel",)),
        name="ada_mod",
    )(c_pad, ada_w, ada_b.reshape(1, n))


def _normproj_kernel(x_ref, nw_ref, sc_ref, sh_ref, w_ref, *rest, nk, emit_h):
    if emit_h:
        o_ref, h_out_ref, h_scr = rest
    else:
        o_ref, h_scr = rest

    @pl.when(pl.program_id(2) == 0)
    def _():
        x = x_ref[0]
        ms = jnp.mean(x * x, axis=-1, keepdims=True)
        y = x * lax.rsqrt(ms + EPS) * nw_ref[...]
        h = y * (1.0 + sc_ref[0]) + sh_ref[0]
        h_scr[...] = h.astype(BF16)
        if emit_h:
            h_out_ref[0] = h

    acc = jnp.dot(h_scr[...], w_ref[...], preferred_element_type=F32)
    for k in range(nk):
        o_ref[0, k] = acc[:, k * LANES:(k + 1) * LANES]


def _normproj(x, nw, sc, sh, w_bf16, *, tm, tn, emit_h):
    b, s, d = x.shape
    n = w_bf16.shape[1]
    tm = min(tm, s)
    nk = tn // LANES
    out_shape = [jax.ShapeDtypeStruct((b, n // LANES, s, LANES), F32)]
    out_specs = [pl.BlockSpec((1, nk, tm, LANES), lambda bi, i, j: (bi, j, i, 0))]
    if emit_h:
        out_shape.append(jax.ShapeDtypeStruct((b, s, d), F32))
        out_specs.append(pl.BlockSpec((1, tm, d), lambda bi, i, j: (bi, i, 0)))
    res = pl.pallas_call(
        functools.partial(_normproj_kernel, nk=nk, emit_h=emit_h),
        grid=(b, s // tm, n // tn),
        in_specs=[pl.BlockSpec((1, tm, d), lambda bi, i, j: (bi, i, 0)),
                  pl.BlockSpec((1, d), lambda bi, i, j: (0, 0)),
                  pl.BlockSpec((1, 1, d), lambda bi, i, j: (bi, 0, 0)),
                  pl.BlockSpec((1, 1, d), lambda bi, i, j: (bi, 0, 0)),
                  pl.BlockSpec((d, tn), lambda bi, i, j: (0, j))],
        out_specs=out_specs,
        out_shape=out_shape,
        scratch_shapes=[pltpu.VMEM((tm, d), BF16)],
        compiler_params=_cparams(("parallel", "parallel", "arbitrary")),
        name="normproj_h" if emit_h else "normproj",
    )(x, nw.reshape(1, d), sc.reshape(b, 1, d), sh.reshape(b, 1, d), w_bf16)
    return res if emit_h else res[0]


GDN_CB = 512
GDN_GB = 256


def _gdn_local_kernel(q_ref, qh_ref, k_ref, kh_ref, v_ref, vh_ref, sm_ref,
                      cwq_ref, cwk_ref, cwv_ref, alog_ref, dtb_ref,
                      w_ref, u0_ref, qk_ref, qg_ref, kd_ref, ge_ref):
    h = pl.program_id(1)
    i = pl.program_id(2)
    C, G = GDN_CHUNK, GDN_GB
    first = (i == 0)

    def conv_act(main_ref, halo_ref, cw_ref, r0):
        if r0 == 0:
            prev = jnp.where(first, 0.0, halo_ref[0, 0])
        else:
            prev = main_ref[0, 0, r0 - 8:r0, :]
        ext = jnp.concatenate([prev, main_ref[0, 0, r0:r0 + G, :]], axis=0)
        w = cw_ref[0]
        y = w[0:1, :] * ext[5:5 + G, :]
        for j in range(1, GDN_CONV):
            y = y + w[j:j + 1, :] * ext[5 + j:5 + j + G, :]
        return _silu(y)

    rid = lax.broadcasted_iota(jnp.int32, (G, G), 0)
    cid = lax.broadcasted_iota(jnp.int32, (G, G), 1)
    same = (rid // C) == (cid // C)
    incl = jnp.logical_and(same, rid >= cid)
    strict = jnp.logical_and(same, rid > cid)
    eye = rid == cid
    is_last = cid == (rid // C) * C + (C - 1)
    joins = []
    bs = 1
    while bs < C:
        joins.append(jnp.logical_and(jnp.logical_and((rid // (2 * bs)) == (cid // (2 * bs)),
                                                     (rid & (2 * bs - 1)) >= bs),
                                     (cid & (2 * bs - 1)) < bs))
        bs *= 2
    tri = jnp.where(incl, 1.0, 0.0).astype(BF16)
    eye_f = jnp.where(eye, 1.0, 0.0).astype(F32)
    lane = lax.broadcasted_iota(jnp.int32, (G, LANES), 1)
    neg_a = -jnp.exp(alog_ref[...])
    dtb = dtb_ref[...]

    for grp in range(GDN_CB // G):
        r0 = grp * G
        q = conv_act(q_ref, qh_ref, cwq_ref, r0)
        k = conv_act(k_ref, kh_ref, cwk_ref, r0)
        v = conv_act(v_ref, vh_ref, cwv_ref, r0)
        q = q * lax.rsqrt(jnp.sum(q * q, axis=-1, keepdims=True) + EPS) * (HEAD_DIM ** -0.5)
        k = k * lax.rsqrt(jnp.sum(k * k, axis=-1, keepdims=True) + EPS)

        sm = sm_ref[0, 0, r0:r0 + G, :]
        g_all = neg_a * _softplus(sm + dtb)
        gc_all = _dot3(tri, g_all, exact_a=True)
        gc = jnp.sum(jnp.where(lane == h, gc_all, 0.0), axis=1, keepdims=True)
        beta = jnp.sum(jnp.where(lane == GDN_HEADS + h, _sigmoid(sm), 0.0), axis=1, keepdims=True)
        gc_row = jnp.sum(jnp.where(eye, gc, 0.0), axis=0, keepdims=True)
        gc_last = jnp.sum(jnp.where(is_last, gc_row, 0.0), axis=1, keepdims=True)
        decay = jnp.where(incl, jnp.exp(jnp.where(incl, gc - gc_row, 0.0)), 0.0)
        gamma = jnp.exp(gc)

        kk = _dot3_nt(k, k)
        lmat = jnp.where(strict, decay * kk, 0.0) * beta
        tinv = eye_f - jnp.where(joins[0], lmat, 0.0)
        for lvl in range(1, len(joins)):
            tinv = tinv - _dot3(_dot3(tinv, jnp.where(joins[lvl], lmat, 0.0)), tinv)
        rhs = jnp.concatenate([(beta * gamma) * k, beta * v], axis=1)
        wu = _dot3(tinv, rhs)
        qk = decay * _bdot_nt(q, k)

        rows = slice(r0, r0 + G)
        w_ref[0, 0, rows, :] = wu[:, :HEAD_DIM].astype(BF16)
        u0_ref[0, 0, rows, :] = wu[:, HEAD_DIM:]
        qg_ref[0, 0, rows, :] = (gamma * q).astype(BF16)
        kd_ref[0, 0, rows, :] = (jnp.exp(gc_last - gc) * k).astype(BF16)
        ge_all = jnp.broadcast_to(jnp.exp(gc_last), (G, LANES))
        for c in range(G // C):
            cr = slice(c * C, (c + 1) * C)
            qk_ref[0, 0, r0 + c * C:r0 + (c + 1) * C, :] = qk[cr, cr].astype(BF16)
            ge_ref[0, 0, pl.ds((i * (GDN_CB // G) + grp) * (G // C) + c, 1), :] = ge_all[c * C:c * C + 1, :]


def _gdn_local(P, cw, alog_pad, dtb_pad):
    b, _, s, _ = P.shape
    H, CB, C = GDN_HEADS, GDN_CB, GDN_CHUNK
    n = s // C

    def main(j0):
        return pl.BlockSpec((1, 1, CB, LANES), lambda bi, h, i: (bi, j0 + h, i, 0))

    def halo(j0):
        return pl.BlockSpec((1, 1, 8, LANES),
                            lambda bi, h, i: (bi, j0 + h, jnp.maximum(i * (CB // 8) - 1, 0), 0))

    def cws(j0):
        return pl.BlockSpec((1, GDN_CONV, LANES), lambda bi, h, i: (j0 + h, 0, 0))

    row = pl.BlockSpec((1, LANES), lambda bi, h, i: (0, 0))
    hs = lambda width: pl.BlockSpec((1, 1, CB, width), lambda bi, h, i: (bi, h, i, 0))
    return pl.pallas_call(
        _gdn_local_kernel,
        grid=(b, H, s // CB),
        in_specs=[main(J_AQ), halo(J_AQ), main(J_AK), halo(J_AK), main(J_AV), halo(J_AV),
                  pl.BlockSpec((1, 1, CB, LANES), lambda bi, h, i: (bi, J_SMALL, i, 0)),
                  cws(0), cws(8), cws(16), row, row],
        out_specs=[hs(LANES), hs(LANES), hs(C), hs(LANES), hs(LANES),
                   pl.BlockSpec((1, 1, n, LANES), lambda bi, h, i: (bi, h, 0, 0))],
        out_shape=[jax.ShapeDtypeStruct((b, H, s, LANES), BF16),
                   jax.ShapeDtypeStruct((b, H, s, LANES), F32),
                   jax.ShapeDtypeStruct((b, H, s, C), BF16),
                   jax.ShapeDtypeStruct((b, H, s, LANES), BF16),
                   jax.ShapeDtypeStruct((b, H, s, LANES), BF16),
                   jax.ShapeDtypeStruct((b, H, n, LANES), F32)],
        compiler_params=_cparams(("parallel", "parallel", "arbitrary")),
        name="gdn_local",
    )(P, P, P, P, P, P, P, cw, cw, cw, alog_pad, dtb_pad)


GDN_HB = 4


def _gdn_scan_kernel(w_ref, u0_ref, qk_ref, qg_ref, kd_ref, ge_ref, z_ref, nw_ref, o_ref, s_scr,
                     *, sb):
    C = GDN_CHUNK

    @pl.when(pl.program_id(2) == 0)
    def _():
        s_scr[...] = jnp.zeros_like(s_scr)

    nw = nw_ref[...]

    def body(n, carry):
        r = pl.multiple_of(n * C, C)
        for hh in range(GDN_HB):
            st = s_scr[hh]
            stb = st.astype(BF16)
            u = u0_ref[0, hh, pl.ds(r, C), :] - jnp.dot(w_ref[0, hh, pl.ds(r, C), :], stb,
                                                       preferred_element_type=F32)
            ub = u.astype(BF16)
            o = (jnp.dot(qg_ref[0, hh, pl.ds(r, C), :], stb, preferred_element_type=F32)
                 + jnp.dot(qk_ref[0, hh, pl.ds(r, C), :], ub, preferred_element_type=F32))
            ge = ge_ref[0, hh, pl.ds(n, 1), :]
            s_scr[hh] = ge * st + lax.dot_general(kd_ref[0, hh, pl.ds(r, C), :], ub,
                                                  (((0,), (0,)), ((), ())),
                                                  preferred_element_type=F32)
            on = o * lax.rsqrt(jnp.mean(o * o, axis=-1, keepdims=True) + EPS) * nw
            z = z_ref[0, hh, pl.ds(r, C), :]
            o_ref[0, pl.ds(r, C), hh * LANES:(hh + 1) * LANES] = (on * _silu(z)).astype(BF16)
        return carry

    lax.fori_loop(0, sb // C, body, 0)


def _gdn_scan(w, u0, qk, qg, kd, ge, P, norm_w):
    b, H, s, _ = w.shape
    C, HB = GDN_CHUNK, GDN_HB
    sb = min(1024, s)
    hs = lambda width: pl.BlockSpec((1, HB, sb, width), lambda bi, hb, i: (bi, hb, i, 0))
    return pl.pallas_call(
        functools.partial(_gdn_scan_kernel, sb=sb),
        grid=(b, H // HB, s // sb),
        in_specs=[hs(LANES), hs(LANES), hs(C), hs(LANES), hs(LANES),
                  pl.BlockSpec((1, HB, sb // C, LANES), lambda bi, hb, i: (bi, hb, i, 0)),
                  pl.BlockSpec((1, HB, sb, LANES), lambda bi, hb, i: (bi, J_AZ // HB + hb, i, 0)),
                  pl.BlockSpec((1, LANES), lambda bi, hb, i: (0, 0))],
        out_specs=pl.BlockSpec((1, sb, HB * LANES), lambda bi, hb, i: (bi, i, hb)),
        out_shape=jax.ShapeDtypeStruct((b, s, H * LANES), BF16),
        scratch_shapes=[pltpu.VMEM((HB, HEAD_DIM, HEAD_DIM), F32)],
        compiler_params=_cparams(("parallel", "parallel", "arbitrary")),
        name="gdn_scan",
    )(w, u0, qk, qg, kd, ge, P, norm_w.reshape(1, LANES))


def _pad_lanes_row(v):
    return jnp.pad(v.astype(F32), (0, LANES - v.shape[0])).reshape(1, LANES)


def _gdn(P, conv_w, a_log, dt_bias, norm_w):
    cw = jnp.transpose(conv_w.reshape(GDN_CONV, 3 * GDN_HEADS, LANES), (1, 0, 2))
    w, u0, qk, qg, kd, ge = _gdn_local(P, cw, _pad_lanes_row(a_log), _pad_lanes_row(dt_bias))
    return _gdn_scan(w, u0, qk, qg, kd, ge, P, norm_w)


NSA_TQ = 128
NSA_TK_SEL = 512
NSA_TK_WIN = 256
NSA_HC = 2
R_Q, R_KC, R_VC, R_KS, R_VS, R_KW, R_VW = 0, 16, 18, 20, 22, 24, 26
ROPE_NB = 4


def _rope_kernel(x_ref, pos_ref, invf_ref, o_ref, cos_scr, sin_scr):
    j = pl.program_id(2)
    lane = lax.broadcasted_iota(jnp.int32, cos_scr.shape, 1)

    @pl.when(j == 0)
    def _():
        ang = pos_ref[0] * invf_ref[...]
        sn = jnp.sin(ang)
        cos_scr[...] = jnp.where(lane < ROPE_DIM, jnp.cos(ang), 1.0)
        sin_scr[...] = jnp.where(lane < ROPE_HALF, -sn, jnp.where(lane < ROPE_DIM, sn, 0.0))

    is_q = j < NSA_HEADS // ROPE_NB
    scale = jnp.where(is_q, HEAD_DIM ** -0.5, 1.0)
    for k in range(ROPE_NB):
        x = x_ref[0, k]
        swapped = jnp.where(lane < ROPE_HALF, pltpu.roll(x, LANES - ROPE_HALF, axis=1),
                            pltpu.roll(x, ROPE_HALF, axis=1))
        rot = x * cos_scr[...] + swapped * sin_scr[...]
        out = rot * scale if k < 2 else jnp.where(is_q, rot, x) * scale
        o_ref[0, k] = out.astype(BF16)


def _rope(P, pos_f32, invf):
    b, _, s, _ = P.shape
    tr = min(512, s)
    nb = ROPE_NB
    return pl.pallas_call(
        _rope_kernel,
        grid=(b, s // tr, N_NSA_BLOCKS // nb),
        in_specs=[pl.BlockSpec((1, nb, tr, LANES), lambda bi, i, j: (bi, J_BQ // nb + j, i, 0)),
                  pl.BlockSpec((1, tr, 1), lambda bi, i, j: (bi, i, 0)),
                  pl.BlockSpec((1, LANES), lambda bi, i, j: (0, 0))],
        out_specs=pl.BlockSpec((1, nb, tr, LANES), lambda bi, i, j: (bi, j, i, 0)),
        out_shape=jax.ShapeDtypeStruct((b, N_NSA_BLOCKS, s, LANES), BF16),
        scratch_shapes=[pltpu.VMEM((tr, LANES), F32), pltpu.VMEM((tr, LANES), F32)],
        compiler_params=_cparams(("parallel", "parallel", "arbitrary")),
        name="nsa_rope",
    )(P, pos_f32, invf)


def _compress_kernel(r_ref, w1_ref, w2_ref, pos_ref, o_ref):
    r = r_ref[0, 0]
    nr = r.shape[0]
    half = CMP_STRIDE * HEAD_DIM
    a = jnp.dot(r, w1_ref[0, :half, :], preferred_element_type=F32)
    bm = jnp.dot(r, w1_ref[0, half:, :], preferred_element_type=F32)
    pos8 = jnp.broadcast_to(pos_ref[0], (8, CMP_LEN * HEAD_DIM)).astype(BF16)
    pb = jnp.dot(pos8, w1_ref[0], preferred_element_type=F32)[0:1, :]
    hid = a + pltpu.roll(bm, nr - 1, axis=0) + pb
    out = jnp.dot(_gelu(hid).astype(BF16), w2_ref[0], preferred_element_type=F32)
    row = lax.broadcasted_iota(jnp.int32, out.shape, 0)
    o_ref[0, 0] = jnp.where(row < nr - 1, out, 0.0).astype(BF16)


def _compress(rows, w1, w2, pos):
    b, _, nr, width = rows.shape
    return pl.pallas_call(
        _compress_kernel,
        grid=(b, 4),
        in_specs=[pl.BlockSpec((1, 1, nr, width), lambda bi, j: (bi, j, 0, 0)),
                  pl.BlockSpec((1, CMP_LEN * HEAD_DIM, CMP_HIDDEN), lambda bi, j: (j // 2, 0, 0)),
                  pl.BlockSpec((1, CMP_HIDDEN, HEAD_DIM), lambda bi, j: (j // 2, 0, 0)),
                  pl.BlockSpec((1, 1, CMP_LEN * HEAD_DIM), lambda bi, j: (j // 2, 0, 0))],
        out_specs=pl.BlockSpec((1, 1, nr, HEAD_DIM), lambda bi, j: (bi, j, 0, 0)),
        out_shape=jax.ShapeDtypeStruct((b, 4, nr, HEAD_DIM), BF16),
        compiler_params=_cparams(("parallel", "arbitrary")),
        name="nsa_compress",
    )(rows, w1, w2, pos)


def _nsa_kernel(q_ref, kc_ref, vc_ref, ks_ref, vs_ref, kw_ref, vw_ref, gate_ref, ovt_ref, e_ref,
                o_ref, m_scr, l_scr, acc_scr, os_scr, *, s_len):
    i = pl.program_id(2)
    TQ, HP = NSA_TQ, NSA_HPG
    R = HP * TQ
    nsb = s_len // SLC_LEN
    ncp = s_len // CMP_STRIDE
    n_sel = min(SLC_TOPK, nsb)
    s0 = i * TQ
    q2 = q_ref[0].reshape(R, HEAD_DIM)

    sc = _bdot_nt(q2, kc_ref[0, 0])
    tq_r = s0 + (lax.broadcasted_iota(jnp.int32, (R, ncp), 0) & (TQ - 1))
    ncol = lax.broadcasted_iota(jnp.int32, (R, ncp), 1)
    valid = jnp.logical_and(ncol * CMP_STRIDE + (CMP_LEN - 1) <= tq_r, ncol < ncp - 1)
    scm = jnp.where(valid, sc, NEG)
    e = jnp.where(valid, jnp.exp(scm - jnp.max(scm, axis=1, keepdims=True)), 0.0)
    p_c = e / jnp.maximum(jnp.sum(e, axis=1, keepdims=True), 1e-30)
    o_c = _bdot(p_c, vc_ref[0, 0])
    psum = jnp.sum(p_c.reshape(HP, TQ, ncp), axis=0)
    imp = _fdot_nt(ovt_ref[...], psum)
    blk = lax.broadcasted_iota(jnp.int32, (nsb, TQ), 0)
    tq_l = s0 + lax.broadcasted_iota(jnp.int32, (nsb, TQ), 1)
    cur = tq_l // SLC_LEN
    forced = jnp.logical_or(blk == 0, jnp.logical_or(blk == cur, blk == cur - 1))
    imp = jnp.where(forced, BIG, imp)
    imp = jnp.where(blk * SLC_LEN <= tq_l, imp, NEG)
    rank = jnp.zeros((nsb, TQ), jnp.int32)
    for j in range(nsb):
        rj = imp[j:j + 1, :]
        beats = jnp.logical_or(rj > imp, jnp.logical_and(rj == imp, blk > j))
        rank = rank + jnp.where(beats, 1, 0)
    sel_t = jnp.where(rank < n_sel, 1.0, 0.0).astype(F32)
    if nsb < LANES:
        sel_t = jnp.concatenate([sel_t, jnp.zeros((LANES - nsb, TQ), F32)], axis=0)
    sel = jnp.transpose(sel_t).astype(BF16)

    ones_blk = jnp.ones((TQ, HEAD_DIM), BF16)
    HC = NSA_HC

    def flash(k_ref, v_ref, tk, lo, hi, mask_fn):
        m_scr[...] = jnp.full(m_scr.shape, NEG, F32)
        l_scr[...] = jnp.zeros(l_scr.shape, F32)
        acc_scr[...] = jnp.zeros(acc_scr.shape, F32)
        qpos = s0 + lax.broadcasted_iota(jnp.int32, (TQ, tk), 0)
        kofs = lax.broadcasted_iota(jnp.int32, (TQ, tk), 1)
        ones_v = jnp.concatenate([ones_blk] * (tk // TQ), axis=0)

        def body(j, carry):
            r = pl.multiple_of(j * tk, tk)
            bias = jnp.where(mask_fn(r, r + kofs, qpos), 0.0, NEG)
            kb = k_ref[0, 0, pl.ds(r, tk), :]
            vb = jnp.concatenate([v_ref[0, 0, pl.ds(r, tk), :], ones_v], axis=1)
            for c in range(HP // HC):
                rows = slice(c * HC * TQ, (c + 1) * HC * TQ)
                s = _bdot_nt(q_ref[0, c * HC:(c + 1) * HC].reshape(HC * TQ, HEAD_DIM), kb)
                s = (s.reshape(HC, TQ, tk) + bias[None]).reshape(HC * TQ, tk)
                m_prev = m_scr[rows, :]
                m_new = jnp.maximum(m_prev, jnp.max(s, axis=1, keepdims=True))
                alpha = jnp.exp(m_prev - m_new)
                p = jnp.exp(s - jnp.concatenate([m_new] * (tk // LANES), axis=1)).astype(BF16)
                pv = jnp.dot(p, vb, preferred_element_type=F32)
                l_scr[rows, :] = alpha * l_scr[rows, :] + pv[:, HEAD_DIM:]
                acc_scr[rows, :] = alpha * acc_scr[rows, :] + pv[:, :HEAD_DIM]
                m_scr[rows, :] = m_new
            return carry

        lax.fori_loop(lo, hi, body, 0)

    tks = min(NSA_TK_SEL, s_len)

    def sel_mask(r, kpos, qpos):
        chosen = jnp.dot(sel, e_ref[:, pl.ds(r, tks)], preferred_element_type=F32)
        return jnp.logical_and(chosen > 0.5, kpos <= qpos)

    flash(ks_ref, vs_ref, tks, 0, (s0 + TQ - 1) // tks + 1, sel_mask)
    os_scr[...] = acc_scr[...] / l_scr[...]

    def win_mask(r, kpos, qpos):
        return jnp.logical_and(kpos <= qpos, kpos > qpos - WIN)

    tkw = min(NSA_TK_WIN, s_len)
    flash(kw_ref, vw_ref, tkw, jnp.maximum(s0 - WIN + 1, 0) // tkw, (s0 + TQ - 1) // tkw + 1, win_mask)

    gates = _sigmoid(gate_ref[0, 0])
    for hh in range(HP):
        rows = slice(hh * TQ, (hh + 1) * TQ)
        o_w = acc_scr[rows, :] / l_scr[rows, :]
        out = (gates[:, 3 * hh:3 * hh + 1] * o_c[rows, :]
               + gates[:, 3 * hh + 1:3 * hh + 2] * os_scr[rows, :]
               + gates[:, 3 * hh + 2:3 * hh + 3] * o_w)
        o_ref[0, :, hh * HEAD_DIM:(hh + 1) * HEAD_DIM] = out.astype(BF16)


def _nsa_attend(rp, cmp_kv, P, ovt, expand):
    b, _, s, _ = rp.shape
    TQ, HP, G = NSA_TQ, NSA_HPG, NSA_GROUPS
    ncp = cmp_kv.shape[2]
    R = HP * TQ
    full = lambda j0: pl.BlockSpec((1, 1, s, HEAD_DIM), lambda bi, g, i: (bi, j0 + g, 0, 0))
    cmp_spec = lambda j0: pl.BlockSpec((1, 1, ncp, HEAD_DIM), lambda bi, g, i: (bi, j0 + g, 0, 0))
    return pl.pallas_call(
        functools.partial(_nsa_kernel, s_len=s),
        grid=(b, G, s // TQ),
        in_specs=[pl.BlockSpec((1, HP, TQ, HEAD_DIM), lambda bi, g, i: (bi, g, i, 0)),
                  cmp_spec(0), cmp_spec(2),
                  full(R_KS), full(R_VS), full(R_KW), full(R_VW),
                  pl.BlockSpec((1, 1, TQ, LANES), lambda bi, g, i: (bi, J_GATE + g, i, 0)),
                  pl.BlockSpec(ovt.shape, lambda bi, g, i: (0, 0)),
                  pl.BlockSpec(expand.shape, lambda bi, g, i: (0, 0))],
        out_specs=pl.BlockSpec((1, TQ, HP * HEAD_DIM), lambda bi, g, i: (bi, i, g)),
        out_shape=jax.ShapeDtypeStruct((b, s, NSA_HEADS * HEAD_DIM), BF16),
        scratch_shapes=[pltpu.VMEM((R, LANES), F32), pltpu.VMEM((R, LANES), F32),
                        pltpu.VMEM((R, HEAD_DIM), F32), pltpu.VMEM((R, HEAD_DIM), F32)],
        compiler_params=_cparams(("parallel", "parallel", "arbitrary")),
        name="nsa_attend",
    )(rp, cmp_kv, cmp_kv, rp, rp, rp, rp, P, ovt, expand)


def _nsa(P, positions, cmp_pos_k, cmp_w1_k, cmp_w2_k, cmp_pos_v, cmp_w1_v, cmp_w2_v):
    b, _, s, _ = P.shape
    lanes = np.arange(LANES)
    invf = np.where(lanes < ROPE_DIM, ROPE_THETA ** (-(lanes % ROPE_HALF) / ROPE_HALF), 0.0)
    rp = _rope(P, positions.astype(F32).reshape(b, s, 1), jnp.asarray(invf, F32).reshape(1, LANES))

    ncp = s // CMP_STRIDE
    rows = rp[:, R_KC:R_KC + 4].reshape(b, 4, ncp, CMP_STRIDE * HEAD_DIM)
    w1 = jnp.stack([cmp_w1_k, cmp_w1_v]).astype(BF16)
    w2 = jnp.stack([cmp_w2_k, cmp_w2_v]).astype(BF16)
    pos = jnp.stack([cmp_pos_k, cmp_pos_v]).reshape(2, 1, CMP_LEN * HEAD_DIM)
    cmp_kv = _compress(rows, w1, w2, pos)

    nsb = s // SLC_LEN
    cmp_start = np.arange(ncp) * CMP_STRIDE
    slc_start = np.arange(nsb) * SLC_LEN
    ov = ((cmp_start[:, None] < slc_start[None, :] + SLC_LEN)
          & (cmp_start[:, None] + CMP_LEN > slc_start[None, :])
          & (np.arange(ncp)[:, None] < ncp - 1))
    ovt = jnp.asarray(ov.T.astype(np.float32))
    expand = (np.arange(LANES)[:, None] == (np.arange(s)[None, :] // SLC_LEN))
    expand = jnp.asarray(expand.astype(np.float32), BF16)
    return _nsa_attend(rp, cmp_kv, P, ovt, expand)


MERGE_TN = 512


def _merge_kernel(oa_ref, ob_ref, wg_ref, wn_ref, ma_ref, mb_ref, y_ref):
    ya = jnp.dot(oa_ref[0], wg_ref[...], preferred_element_type=F32)
    yb = jnp.dot(ob_ref[0], wn_ref[...], preferred_element_type=F32)
    for k in range(MERGE_TN // LANES):
        cols = slice(k * LANES, (k + 1) * LANES)
        y = _sigmoid(ma_ref[0, k]) * ya[:, cols] + _sigmoid(mb_ref[0, k]) * yb[:, cols]
        y_ref[0, :, cols] = y.astype(BF16)


def _merge(o_a, o_b, wg, wn, P):
    b, s, da = o_a.shape
    db = o_b.shape[2]
    d = wg.shape[1]
    tm, tn = min(512, s), MERGE_TN
    nk = tn // LANES
    return pl.pallas_call(
        _merge_kernel,
        grid=(b, s // tm, d // tn),
        in_specs=[pl.BlockSpec((1, tm, da), lambda bi, i, j: (bi, i, 0)),
                  pl.BlockSpec((1, tm, db), lambda bi, i, j: (bi, i, 0)),
                  pl.BlockSpec((da, tn), lambda bi, i, j: (0, j)),
                  pl.BlockSpec((db, tn), lambda bi, i, j: (0, j)),
                  pl.BlockSpec((1, nk, tm, LANES), lambda bi, i, j: (bi, J_MA // nk + j, i, 0)),
                  pl.BlockSpec((1, nk, tm, LANES), lambda bi, i, j: (bi, J_MB // nk + j, i, 0))],
        out_specs=pl.BlockSpec((1, tm, tn), lambda bi, i, j: (bi, i, j)),
        out_shape=jax.ShapeDtypeStruct((b, s, d), BF16),
        compiler_params=_cparams(("parallel", "parallel", "arbitrary")),
        name="merge",
    )(o_a, o_b, wg, wn, P, P)


def _outproj_kernel(y_ref, w_ref, x_ref, g_ref, o_ref):
    o_ref[0] = x_ref[0] + g_ref[0] * jnp.dot(y_ref[0], w_ref[...], preferred_element_type=F32)


def _outproj(y, w, x, g):
    b, s, d = x.shape
    tm, tn = min(512, s), 512
    return pl.pallas_call(
        _outproj_kernel,
        grid=(b, s // tm, d // tn),
        in_specs=[pl.BlockSpec((1, tm, d), lambda bi, i, j: (bi, i, 0)),
                  pl.BlockSpec((d, tn), lambda bi, i, j: (0, j)),
                  pl.BlockSpec((1, tm, tn), lambda bi, i, j: (bi, i, j)),
                  pl.BlockSpec((1, 1, tn), lambda bi, i, j: (bi, 0, j))],
        out_specs=pl.BlockSpec((1, tm, tn), lambda bi, i, j: (bi, i, j)),
        out_shape=jax.ShapeDtypeStruct((b, s, d), F32),
        compiler_params=_cparams(("parallel", "parallel", "arbitrary")),
        name="outproj",
    )(y, w, x, g.reshape(b, 1, d))


PEER_TB = 256
PEER_TG = 64
PEER_AHEAD = 12
PEER_SLOTS = 16
PEER_SEL = PEER_HEADS * PEER_TOPK
MIN_F32 = -3.0e38


def _topk_rows(vals, payload=None):
    nrow = vals.shape[0]
    rid = lax.broadcasted_iota(jnp.int32, vals.shape, 0)
    out_v, out_i = [], []
    for _ in range(PEER_TOPK):
        m = jnp.max(vals, axis=0, keepdims=True)
        idx = jnp.min(jnp.where(vals == m, rid, nrow), axis=0, keepdims=True)
        hit = rid == idx
        out_v.append(m)
        if payload is None:
            out_i.append(idx)
        else:
            out_i.append(jnp.sum(jnp.where(hit, payload, 0), axis=0, keepdims=True))
        vals = jnp.where(hit, MIN_F32, vals)
    return jnp.concatenate(out_v, axis=0), jnp.concatenate(out_i, axis=0)


def _peer_topk_kernel(qh_ref, k1_ref, k2_ref, eidx_ref, gw_ref):
    def head(h, carry):
        s1 = _fdot_nt(k1_ref[h], qh_ref[0, 2 * h])
        s2 = _fdot_nt(k2_ref[h], qh_ref[0, 2 * h + 1])
        v1, i1 = _topk_rows(s1)
        v2, i2 = _topk_rows(s2)
        keep = [PEER_TOPK // (a + 1) for a in range(PEER_TOPK)]
        npad = (-sum(keep)) % 8
        cand = jnp.concatenate([v1[a:a + 1, :] + v2[0:keep[a], :] for a in range(PEER_TOPK)]
                               + [jnp.full((npad, v1.shape[1]), MIN_F32, F32)], axis=0)
        cidx = jnp.concatenate([i1[a:a + 1, :] * PEER_NKEYS + i2[0:keep[a], :] for a in range(PEER_TOPK)]
                               + [jnp.zeros((npad, v1.shape[1]), jnp.int32)], axis=0)
        sc, eidx = _topk_rows(cand, cidx)
        ex = jnp.exp(sc - sc[0:1, :])
        eidx_ref[h] = eidx
        gw_ref[h] = ex / jnp.sum(ex, axis=0, keepdims=True)
        return carry

    lax.fori_loop(0, PEER_HEADS, head, 0)


def _peer_topk(qh, keys1, keys2):
    b, nb, s, _ = qh.shape
    tb = min(PEER_TB, s)
    nt = s // tb
    kspec = pl.BlockSpec(keys1.shape, lambda bi, i: (0, 0, 0))
    ospec = pl.BlockSpec((PEER_HEADS, PEER_TOPK, tb), lambda bi, i: (0, 0, bi * nt + i))
    return pl.pallas_call(
        _peer_topk_kernel,
        grid=(b, nt),
        in_specs=[pl.BlockSpec((1, nb, tb, LANES), lambda bi, i: (bi, 0, i, 0)), kspec, kspec],
        out_specs=[ospec, ospec],
        out_shape=[jax.ShapeDtypeStruct((PEER_HEADS, PEER_TOPK, b * s), jnp.int32),
                   jax.ShapeDtypeStruct((PEER_HEADS, PEER_TOPK, b * s), F32)],
        compiler_params=_cparams(("parallel", "parallel")),
        name="peer_topk",
    )(qh, keys1, keys2)


def _peer_gather_kernel(idx_ref, idxn_ref, gw_ref, x_ref, uv_hbm, o_ref, buf, a_scr, sem, *, tg):
    NS, D, NSLOT = PEER_SEL, PEER_AHEAD, PEER_SLOTS
    RPE = uv_hbm.shape[1]
    SUBW = RPE // 2
    half = SUBW * LANES

    def wait_rows(t):
        slot = t & (NSLOT - 1)
        pltpu.make_async_copy(uv_hbm.at[pl.ds(0, NS)], buf.at[slot], sem.at[slot]).wait()

    rid = lax.broadcasted_iota(jnp.int32, (NS, NS), 0)
    cid = lax.broadcasted_iota(jnp.int32, (NS, NS), 1)
    eye = rid == cid
    ones = jnp.ones((LANES, LANES), BF16)

    def lane_sum_rep(m):
        hi, lo = _split_bf16(m)
        return (jnp.dot(hi, ones, preferred_element_type=F32)
                + jnp.dot(lo, ones, preferred_element_type=F32))

    def halves(slot, tab, sg):
        w = buf[slot, tab * SUBW + sg]
        return pltpu.bitcast(w << 16, F32), pltpu.bitcast(w & jnp.uint32(0xFFFF0000), F32)

    def phase(t, issue_from, do_dot, do_prev):
        if do_dot:
            wait_rows(t)
            xrow = x_ref[pl.ds(t, 1), :]
            acc = jnp.zeros((NS, LANES), F32)
        if do_prev:
            act = lane_sum_rep(a_scr[(t - 1) & 1])
            gw_rep = lane_sum_rep(jnp.where(eye, gw_ref[pl.ds(t - 1, 1), :], 0.0))
            coef = gw_rep * _gelu(act)
            out_lo, out_hi = [], []
        if issue_from is not None:
            slot_i = (t + D) & (NSLOT - 1)
            base_i = (t + D) * NS if issue_from is idx_ref else (t + D - tg) * NS
        for sg in range(SUBW):
            if issue_from is not None:
                for j in range(sg * NS // SUBW, (sg + 1) * NS // SUBW):
                    pltpu.make_async_copy(uv_hbm.at[issue_from[base_i + j]],
                                          buf.at[slot_i, :, j, :], sem.at[slot_i]).start(priority=j % 2)
            if do_dot:
                lo, hi = halves(t & (NSLOT - 1), 0, sg)
                acc = acc + lo * xrow[:, sg * LANES:(sg + 1) * LANES]
                acc = acc + hi * xrow[:, half + sg * LANES:half + (sg + 1) * LANES]
            if do_prev:
                lo, hi = halves((t - 1) & (NSLOT - 1), 1, sg)
                out_lo.append(jnp.sum(lo * coef, axis=0, keepdims=True))
                out_hi.append(jnp.sum(hi * coef, axis=0, keepdims=True))
        if do_dot:
            a_scr[t & 1] = acc
        if do_prev:
            o_ref[t - 1] = jnp.concatenate(out_lo + out_hi, axis=0)

    def ramp(t, carry):
        slot = t & (NSLOT - 1)

        def body(j, c):
            pltpu.make_async_copy(uv_hbm.at[idx_ref[t * NS + j]], buf.at[slot, :, j, :], sem.at[slot]).start()
            return c
        return lax.fori_loop(0, NS, body, carry)

    def looped(issue_from):
        def f(t, carry):
            phase(t, issue_from, True, True)
            return carry
        return f

    step = pl.program_id(0)
    last = pl.num_programs(0) - 1

    @pl.when(step == 0)
    def _():
        lax.fori_loop(0, D, ramp, 0)

    phase(0, idx_ref, True, False)
    lax.fori_loop(1, tg - D, looped(idx_ref), 0)

    @pl.when(step < last)
    def _():
        lax.fori_loop(tg - D, tg, looped(idxn_ref), 0)

    @pl.when(step == last)
    def _():
        lax.fori_loop(tg - D, tg, looped(None), 0)

    phase(tg, None, False, True)


def _peer_gather(eidx, gw, h2r, uvc):
    t = eidx.shape[0]
    tg = min(PEER_TG, t)
    d = h2r.shape[1]
    rpe = uvc.shape[1]
    return pl.pallas_call(
        functools.partial(_peer_gather_kernel, tg=tg),
        grid=(t // tg,),
        in_specs=[pl.BlockSpec((tg * PEER_SEL,), lambda i: (i,), memory_space=pltpu.SMEM),
                  pl.BlockSpec((tg * PEER_SEL,), lambda i: (jnp.minimum(i + 1, t // tg - 1),),
                               memory_space=pltpu.SMEM),
                  pl.BlockSpec((tg, PEER_SEL), lambda i: (i, 0)),
                  pl.BlockSpec((tg, d), lambda i: (i, 0)),
                  pl.BlockSpec(memory_space=pl.ANY)],
        out_specs=pl.BlockSpec((tg, d // LANES, LANES), lambda i: (i, 0, 0)),
        out_shape=jax.ShapeDtypeStruct((t, d // LANES, LANES), F32),
        scratch_shapes=[pltpu.VMEM((PEER_SLOTS, rpe, PEER_SEL, LANES), jnp.uint32),
                        pltpu.VMEM((2, PEER_SEL, LANES), F32),
                        pltpu.SemaphoreType.DMA((PEER_SLOTS,))],
        compiler_params=_cparams(("arbitrary",)),
        name="peer_gather",
    )(eidx.reshape(-1), eidx.reshape(-1), gw, h2r, uvc)


def _pack_halves(tab):
    e, d = tab.shape
    bits = lax.bitcast_convert_type(tab.astype(BF16), jnp.uint16).astype(jnp.uint32)
    bits = bits.reshape(e, 2, d // (2 * LANES), LANES)
    return bits[:, 0] | (bits[:, 1] << 16)


def _peer(x1, norm_w, sc, sh, wq, keys1, keys2, u, v):
    b, s, d = x1.shape
    qh, h2 = _normproj(x1, norm_w, sc, sh, wq.astype(BF16), tm=512, tn=512, emit_h=True)
    eidx, gw = _peer_topk(qh, keys1, keys2)
    eidx = jnp.transpose(eidx, (2, 0, 1)).reshape(b * s, PEER_SEL)
    gw = jnp.transpose(gw, (2, 0, 1)).reshape(b * s, PEER_SEL)
    uvc = jnp.concatenate([_pack_halves(u), _pack_halves(v)], axis=1)
    out = _peer_gather(eidx, gw, h2.reshape(b * s, d), uvc)
    return out.reshape(b, s, d)


def _final_kernel(x_ref, p_ref, g_ref, w_ref, o_ref):
    x = x_ref[0] + g_ref[0] * p_ref[0]
    o_ref[0] = x * lax.rsqrt(jnp.mean(x * x, axis=-1, keepdims=True) + EPS) * w_ref[...]


def _final(x1, peer, g2, wf):
    b, s, d = x1.shape
    tm = min(512, s)
    blk = pl.BlockSpec((1, tm, d), lambda bi, i: (bi, i, 0))
    return pl.pallas_call(
        _final_kernel,
        grid=(b, s // tm),
        in_specs=[blk, blk, pl.BlockSpec((1, 1, d), lambda bi, i: (bi, 0, 0)),
                  pl.BlockSpec((1, d), lambda bi, i: (0, 0))],
        out_specs=blk,
        out_shape=jax.ShapeDtypeStruct((b, s, d), F32),
        compiler_params=_cparams(("parallel", "parallel")),
        name="final_norm",
    )(x1, peer, g2.reshape(b, 1, d), wf.reshape(1, d))


def _pad_rows(a, mult=8):
    pad = (-a.shape[0]) % mult
    return jnp.pad(a, ((0, pad), (0, 0)))


def _permute_w_in(w):
    o_aa, o_bq, o_bg, o_ma, o_end = 4096, 4112, 7696, 7744, 11840
    d = w.shape[0]
    z = lambda n: jnp.zeros((d, n), w.dtype)
    hg = 3 * NSA_HPG
    cols = [w[:, 0:o_aa], w[:, o_bq:o_bg], w[:, o_ma:o_end],
            w[:, o_aa:o_bq], z(LANES - 16),
            w[:, o_bg:o_bg + hg], z(LANES - hg),
            w[:, o_bg + hg:o_ma], z(LANES - hg),
            z(LANES)]
    return jnp.concatenate(cols, axis=1).astype(BF16)


def kernel(x, c, positions, ada_w, ada_b, norm1_w, norm2_w, w_in, gdn_conv_w, gdn_A_log, gdn_dt_bias, gdn_norm_w, cmp_pos_k, cmp_w1_k, cmp_w2_k, cmp_pos_v, cmp_w1_v, cmp_w2_v, w_branch_gdn, w_branch_nsa, w_out, peer_wq, peer_keys1, peer_keys2, peer_u, peer_v, final_norm_w):
    b, s, d = x.shape
    l = 0
    mod = _ada(_pad_rows(c), ada_w[l], ada_b[l])[:b]
    sh1, sc1, g1, sh2, sc2, g2 = jnp.split(mod, 6, axis=-1)
    P = _normproj(x, norm1_w[l], sc1, sh1, _permute_w_in(w_in[l]), tm=1024, tn=512, emit_h=False)
    o_a = _gdn(P, gdn_conv_w[l], gdn_A_log[l], gdn_dt_bias[l], gdn_norm_w[l])
    o_b = _nsa(P, positions, cmp_pos_k[l], cmp_w1_k[l], cmp_w2_k[l], cmp_pos_v[l], cmp_w1_v[l], cmp_w2_v[l])
    y = _merge(o_a, o_b, w_branch_gdn[l].astype(BF16), w_branch_nsa[l].astype(BF16), P)
    x1 = _outproj(y, w_out[l].astype(BF16), x, g1)
    peer = _peer(x1, norm2_w[l], sc2, sh2, peer_wq[l], peer_keys1[l], peer_keys2[l], peer_u[l], peer_v[l])
    return _final(x1, peer, g2, final_norm_w)
```

```python
import functools
import math

import numpy as np
import jax
import jax.numpy as jnp
from jax import lax
from jax.experimental import pallas as pl
from jax.experimental.pallas import tpu as pltpu

F32 = jnp.float32
BF16 = jnp.bfloat16
HI = lax.Precision.HIGHEST

LANES = 128
VMEM_LIMIT = 56 * 1024 * 1024

EPS = 1e-6
ROPE_THETA = 500000.0
HEAD_DIM = 128
ROPE_DIM = HEAD_DIM // 4
ROPE_HALF = ROPE_DIM // 2

GDN_HEADS = 8
GDN_CONV = 4
GDN_CHUNK = 64

NSA_HEADS = 16
NSA_GROUPS = 2
NSA_HPG = NSA_HEADS // NSA_GROUPS
CMP_LEN = 32
CMP_STRIDE = 16
CMP_HIDDEN = 256
SLC_LEN = 64
SLC_TOPK = 16
WIN = 512

PEER_HEADS = 8
PEER_NKEYS = 128
PEER_QDIM = 256
PEER_TOPK = 16

NEG = -1e30
BIG = 1e9

J_AQ, J_AK, J_AV, J_AZ = 0, 8, 16, 24
J_BQ = 32
N_NSA_BLOCKS = 28
J_MA, J_MB = 60, 76
J_SMALL = 92
J_GATE = 93
NJ = 96


def _cparams(sem):
    return pltpu.CompilerParams(dimension_semantics=sem, vmem_limit_bytes=VMEM_LIMIT)


def _bdot(a, b):
    return jnp.dot(a.astype(BF16), b.astype(BF16), preferred_element_type=F32)


def _bdot_nt(a, b):
    return lax.dot_general(a.astype(BF16), b.astype(BF16), (((1,), (1,)), ((), ())),
                           preferred_element_type=F32)


def _split_bf16(a):
    hi = a.astype(BF16)
    return hi, (a - hi.astype(F32)).astype(BF16)


def _dot3(a, b, exact_a=False):
    bh, bl = _split_bf16(b)
    dot = functools.partial(jnp.dot, preferred_element_type=F32)
    if exact_a:
        return dot(a, bh) + dot(a, bl)
    ah, al = _split_bf16(a)
    return dot(ah, bh) + dot(al, bh) + dot(ah, bl)


def _dot3_nt(a, b):
    ah, al = _split_bf16(a)
    bh, bl = _split_bf16(b)
    dot = functools.partial(lax.dot_general, dimension_numbers=(((1,), (1,)), ((), ())),
                            preferred_element_type=F32)
    return dot(ah, bh) + dot(al, bh) + dot(ah, bl)


def _fdot(a, b):
    return jnp.dot(a, b, precision=HI, preferred_element_type=F32)


def _fdot_nt(a, b):
    return lax.dot_general(a, b, (((1,), (1,)), ((), ())), precision=HI,
                           preferred_element_type=F32)


def _sigmoid(x):
    return 1.0 / (1.0 + jnp.exp(-x))


def _silu(x):
    return x * _sigmoid(x)


def _gelu(x):
    return 0.5 * x * (1.0 + jnp.tanh(math.sqrt(2.0 / math.pi) * (x + 0.044715 * (x * x * x))))


def _softplus(x):
    return jnp.maximum(x, 0.0) + jnp.log(1.0 + jnp.exp(-jnp.abs(x)))


def _ada_kernel(c_ref, w_ref, b_ref, o_ref):
    o_ref[...] = _fdot(_silu(c_ref[...]), w_ref[...]) + b_ref[...]


def _ada(c_pad, ada_w, ada_b):
    m, d = c_pad.shape
    n = ada_w.shape[1]
    tn = 1024
    return pl.pallas_call(
        _ada_kernel,
        grid=(n // tn,),
        in_specs=[pl.BlockSpec((m, d), lambda j: (0, 0)),
                  pl.BlockSpec((d, tn), lambda j: (0, j)),
                  pl.BlockSpec((1, tn), lambda j: (0, j))],
        out_specs=pl.BlockSpec((m, tn), lambda j: (0, j)),
        out_shape=jax.ShapeDtypeStruct((m, n), F32),
        compiler_params=_cparams(("parallel",)),
        name="ada_mod",
    )(c_pad, ada_w, ada_b.reshape(1, n))


def _normproj_kernel(x_ref, nw_ref, sc_ref, sh_ref, w_ref, *rest, nk, emit_h):
    if emit_h:
        o_ref, h_out_ref, h_scr = rest
    else:
        o_ref, h_scr = rest

    @pl.when(pl.program_id(2) == 0)
    def _():
        x = x_ref[0]
        ms = jnp.mean(x * x, axis=-1, keepdims=True)
        y = x * lax.rsqrt(ms + EPS) * nw_ref[...]
        h = y * (1.0 + sc_ref[0]) + sh_ref[0]
        h_scr[...] = h.astype(BF16)
        if emit_h:
            h_out_ref[0] = h

    acc = jnp.dot(h_scr[...], w_ref[...], preferred_element_type=F32)
    for k in range(nk):
        o_ref[0, k] = acc[:, k * LANES:(k + 1) * LANES]


def _normproj(x, nw, sc, sh, w_bf16, *, tm, tn, emit_h):
    b, s, d = x.shape
    n = w_bf16.shape[1]
    tm = min(tm, s)
    nk = tn // LANES
    out_shape = [jax.ShapeDtypeStruct((b, n // LANES, s, LANES), F32)]
    out_specs = [pl.BlockSpec((1, nk, tm, LANES), lambda bi, i, j: (bi, j, i, 0))]
    if emit_h:
        out_shape.append(jax.ShapeDtypeStruct((b, s, d), F32))
        out_specs.append(pl.BlockSpec((1, tm, d), lambda bi, i, j: (bi, i, 0)))
    res = pl.pallas_call(
        functools.partial(_normproj_kernel, nk=nk, emit_h=emit_h),
        grid=(b, s // tm, n // tn),
        in_specs=[pl.BlockSpec((1, tm, d), lambda bi, i, j: (bi, i, 0)),
                  pl.BlockSpec((1, d), lambda bi, i, j: (0, 0)),
                  pl.BlockSpec((1, 1, d), lambda bi, i, j: (bi, 0, 0)),
                  pl.BlockSpec((1, 1, d), lambda bi, i, j: (bi, 0, 0)),
                  pl.BlockSpec((d, tn), lambda bi, i, j: (0, j))],
        out_specs=out_specs,
        out_shape=out_shape,
        scratch_shapes=[pltpu.VMEM((tm, d), BF16)],
        compiler_params=_cparams(("parallel", "parallel", "arbitrary")),
        name="normproj_h" if emit_h else "normproj",
    )(x, nw.reshape(1, d), sc.reshape(b, 1, d), sh.reshape(b, 1, d), w_bf16)
    return res if emit_h else res[0]


GDN_CB = 1024
GDN_GB = 256


def _gdn_local_kernel(q_ref, qh_ref, k_ref, kh_ref, v_ref, vh_ref, sm_ref,
                      cwq_ref, cwk_ref, cwv_ref, alog_ref, dtb_ref,
                      w_ref, u0_ref, qk_ref, qg_ref, kd_ref, ge_ref, *, cb):
    h = pl.program_id(1)
    i = pl.program_id(2)
    C, G = GDN_CHUNK, GDN_GB
    first = (i == 0)

    def conv_act(main_ref, halo_ref, cw_ref, r0):
        if r0 == 0:
            prev = jnp.where(first, 0.0, halo_ref[0, 0])
        else:
            prev = main_ref[0, 0, r0 - 8:r0, :]
        ext = jnp.concatenate([prev, main_ref[0, 0, r0:r0 + G, :]], axis=0)
        w = cw_ref[0]
        y = w[0:1, :] * ext[5:5 + G, :]
        for j in range(1, GDN_CONV):
            y = y + w[j:j + 1, :] * ext[5 + j:5 + j + G, :]
        return _silu(y)

    rid = lax.broadcasted_iota(jnp.int32, (G, G), 0)
    cid = lax.broadcasted_iota(jnp.int32, (G, G), 1)
    same = (rid // C) == (cid // C)
    incl = jnp.logical_and(same, rid >= cid)
    strict = jnp.logical_and(same, rid > cid)
    eye = rid == cid
    is_last = cid == (rid // C) * C + (C - 1)
    joins = []
    bs = 1
    while bs < C:
        joins.append(jnp.logical_and(jnp.logical_and((rid // (2 * bs)) == (cid // (2 * bs)),
                                                     (rid & (2 * bs - 1)) >= bs),
                                     (cid & (2 * bs - 1)) < bs))
        bs *= 2
    tri = jnp.where(incl, 1.0, 0.0).astype(BF16)
    eye_f = jnp.where(eye, 1.0, 0.0).astype(F32)
    lane = lax.broadcasted_iota(jnp.int32, (G, LANES), 1)
    neg_a = -jnp.exp(alog_ref[...])
    dtb = dtb_ref[...]

    for grp in range(cb // G):
        r0 = grp * G
        q = conv_act(q_ref, qh_ref, cwq_ref, r0)
        k = conv_act(k_ref, kh_ref, cwk_ref, r0)
        v = conv_act(v_ref, vh_ref, cwv_ref, r0)
        q = q * lax.rsqrt(jnp.sum(q * q, axis=-1, keepdims=True) + EPS) * (HEAD_DIM ** -0.5)
        k = k * lax.rsqrt(jnp.sum(k * k, axis=-1, keepdims=True) + EPS)

        sm = sm_ref[0, 0, r0:r0 + G, :]
        g_all = neg_a * _softplus(sm + dtb)
        gc_all = _dot3(tri, g_all, exact_a=True)
        gc = jnp.sum(jnp.where(lane == h, gc_all, 0.0), axis=1, keepdims=True)
        beta = jnp.sum(jnp.where(lane == GDN_HEADS + h, _sigmoid(sm), 0.0), axis=1, keepdims=True)
        gc_row = jnp.sum(jnp.where(eye, gc, 0.0), axis=0, keepdims=True)
        gc_last = jnp.sum(jnp.where(is_last, gc_row, 0.0), axis=1, keepdims=True)
        decay = jnp.where(incl, jnp.exp(jnp.where(incl, gc - gc_row, 0.0)), 0.0)
        gamma = jnp.exp(gc)

        kk = _dot3_nt(k, k)
        lmat = jnp.where(strict, decay * kk, 0.0) * beta
        tinv = eye_f - jnp.where(joins[0], lmat, 0.0)
        for lvl in range(1, len(joins)):
            tinv = tinv - _dot3(_dot3(tinv, jnp.where(joins[lvl], lmat, 0.0)), tinv)
        rhs = jnp.concatenate([(beta * gamma) * k, beta * v], axis=1)
        wu = _dot3(tinv, rhs)
        qk = decay * _bdot_nt(q, k)

        rows = slice(r0, r0 + G)
        w_ref[0, 0, rows, :] = wu[:, :HEAD_DIM].astype(BF16)
        u0_ref[0, 0, rows, :] = wu[:, HEAD_DIM:]
        qg_ref[0, 0, rows, :] = (gamma * q).astype(BF16)
        kd_ref[0, 0, rows, :] = (jnp.exp(gc_last - gc) * k).astype(BF16)
        ge_all = jnp.broadcast_to(jnp.exp(gc_last), (G, LANES))
        for c in range(G // C):
            cr = slice(c * C, (c + 1) * C)
            qk_ref[0, 0, r0 + c * C:r0 + (c + 1) * C, :] = qk[cr, cr].astype(BF16)
            ge_ref[0, 0, pl.ds((i * (cb // G) + grp) * (G // C) + c, 1), :] = ge_all[c * C:c * C + 1, :]


def _gdn_local(P, cw, alog_pad, dtb_pad):
    b, _, s, _ = P.shape
    H, CB, C = GDN_HEADS, min(GDN_CB, s), GDN_CHUNK
    n = s // C

    def main(j0):
        return pl.BlockSpec((1, 1, CB, LANES), lambda bi, h, i: (bi, j0 + h, i, 0))

    def halo(j0):
        return pl.BlockSpec((1, 1, 8, LANES),
                            lambda bi, h, i: (bi, j0 + h, jnp.maximum(i * (CB // 8) - 1, 0), 0))

    def cws(j0):
        return pl.BlockSpec((1, GDN_CONV, LANES), lambda bi, h, i: (j0 + h, 0, 0))

    row = pl.BlockSpec((1, LANES), lambda bi, h, i: (0, 0))
    hs = lambda width: pl.BlockSpec((1, 1, CB, width), lambda bi, h, i: (bi, h, i, 0))
    return pl.pallas_call(
        functools.partial(_gdn_local_kernel, cb=CB),
        grid=(b, H, s // CB),
        in_specs=[main(J_AQ), halo(J_AQ), main(J_AK), halo(J_AK), main(J_AV), halo(J_AV),
                  pl.BlockSpec((1, 1, CB, LANES), lambda bi, h, i: (bi, J_SMALL, i, 0)),
                  cws(0), cws(8), cws(16), row, row],
        out_specs=[hs(LANES), hs(LANES), hs(C), hs(LANES), hs(LANES),
                   pl.BlockSpec((1, 1, n, LANES), lambda bi, h, i: (bi, h, 0, 0))],
        out_shape=[jax.ShapeDtypeStruct((b, H, s, LANES), BF16),
                   jax.ShapeDtypeStruct((b, H, s, LANES), F32),
                   jax.ShapeDtypeStruct((b, H, s, C), BF16),
                   jax.ShapeDtypeStruct((b, H, s, LANES), BF16),
                   jax.ShapeDtypeStruct((b, H, s, LANES), BF16),
                   jax.ShapeDtypeStruct((b, H, n, LANES), F32)],
        compiler_params=_cparams(("parallel", "parallel", "arbitrary")),
        name="gdn_local",
    )(P, P, P, P, P, P, P, cw, cw, cw, alog_pad, dtb_pad)


GDN_HB = 4


def _gdn_scan_kernel(w_ref, u0_ref, qk_ref, qg_ref, kd_ref, ge_ref, z_ref, nw_ref, o_ref, s_scr,
                     *, sb):
    C = GDN_CHUNK

    @pl.when(pl.program_id(2) == 0)
    def _():
        s_scr[...] = jnp.zeros_like(s_scr)

    nw = nw_ref[...]

    def body(n, carry):
        r = pl.multiple_of(n * C, C)
        for hh in range(GDN_HB):
            st = s_scr[hh]
            stb = st.astype(BF16)
            u = u0_ref[0, hh, pl.ds(r, C), :] - jnp.dot(w_ref[0, hh, pl.ds(r, C), :], stb,
                                                       preferred_element_type=F32)
            ub = u.astype(BF16)
            o = (jnp.dot(qg_ref[0, hh, pl.ds(r, C), :], stb, preferred_element_type=F32)
                 + jnp.dot(qk_ref[0, hh, pl.ds(r, C), :], ub, preferred_element_type=F32))
            ge = ge_ref[0, hh, pl.ds(n, 1), :]
            s_scr[hh] = ge * st + lax.dot_general(kd_ref[0, hh, pl.ds(r, C), :], ub,
                                                  (((0,), (0,)), ((), ())),
                                                  preferred_element_type=F32)
            on = o * lax.rsqrt(jnp.mean(o * o, axis=-1, keepdims=True) + EPS) * nw
            z = z_ref[0, hh, pl.ds(r, C), :]
            o_ref[0, pl.ds(r, C), hh * LANES:(hh + 1) * LANES] = (on * _silu(z)).astype(BF16)
        return carry

    lax.fori_loop(0, sb // C, body, 0)


def _gdn_scan(w, u0, qk, qg, kd, ge, P, norm_w):
    b, H, s, _ = w.shape
    C, HB = GDN_CHUNK, GDN_HB
    sb = min(1024, s)
    hs = lambda width: pl.BlockSpec((1, HB, sb, width), lambda bi, hb, i: (bi, hb, i, 0))
    return pl.pallas_call(
        functools.partial(_gdn_scan_kernel, sb=sb),
        grid=(b, H // HB, s // sb),
        in_specs=[hs(LANES), hs(LANES), hs(C), hs(LANES), hs(LANES),
                  pl.BlockSpec((1, HB, sb // C, LANES), lambda bi, hb, i: (bi, hb, i, 0)),
                  pl.BlockSpec((1, HB, sb, LANES), lambda bi, hb, i: (bi, J_AZ // HB + hb, i, 0)),
                  pl.BlockSpec((1, LANES), lambda bi, hb, i: (0, 0))],
        out_specs=pl.BlockSpec((1, sb, HB * LANES), lambda bi, hb, i: (bi, i, hb)),
        out_shape=jax.ShapeDtypeStruct((b, s, H * LANES), BF16),
        scratch_shapes=[pltpu.VMEM((HB, HEAD_DIM, HEAD_DIM), F32)],
        compiler_params=_cparams(("parallel", "parallel", "arbitrary")),
        name="gdn_scan",
    )(w, u0, qk, qg, kd, ge, P, norm_w.reshape(1, LANES))


def _pad_lanes_row(v):
    return jnp.pad(v.astype(F32), (0, LANES - v.shape[0])).reshape(1, LANES)


def _gdn(P, conv_w, a_log, dt_bias, norm_w):
    cw = jnp.transpose(conv_w.reshape(GDN_CONV, 3 * GDN_HEADS, LANES), (1, 0, 2))
    w, u0, qk, qg, kd, ge = _gdn_local(P, cw, _pad_lanes_row(a_log), _pad_lanes_row(dt_bias))
    return _gdn_scan(w, u0, qk, qg, kd, ge, P, norm_w)


NSA_TQ = 128
NSA_TK_SEL = 512
NSA_TK_WIN = 256
NSA_HC = 2
R_Q, R_KC, R_VC, R_KS, R_VS, R_KW, R_VW = 0, 16, 18, 20, 22, 24, 26
ROPE_NB = 4


def _rope_kernel(x_ref, pos_ref, invf_ref, o_ref, cos_scr, sin_scr):
    j = pl.program_id(2)
    lane = lax.broadcasted_iota(jnp.int32, cos_scr.shape, 1)

    @pl.when(j == 0)
    def _():
        ang = pos_ref[0] * invf_ref[...]
        sn = jnp.sin(ang)
        cos_scr[...] = jnp.where(lane < ROPE_DIM, jnp.cos(ang), 1.0)
        sin_scr[...] = jnp.where(lane < ROPE_HALF, -sn, jnp.where(lane < ROPE_DIM, sn, 0.0))

    is_q = j < NSA_HEADS // ROPE_NB
    scale = jnp.where(is_q, HEAD_DIM ** -0.5, 1.0)
    for k in range(ROPE_NB):
        x = x_ref[0, k]
        swapped = jnp.where(lane < ROPE_HALF, pltpu.roll(x, LANES - ROPE_HALF, axis=1),
                            pltpu.roll(x, ROPE_HALF, axis=1))
        rot = x * cos_scr[...] + swapped * sin_scr[...]
        out = rot * scale if k < 2 else jnp.where(is_q, rot, x) * scale
        o_ref[0, k] = out.astype(BF16)


def _rope(P, pos_f32, invf):
    b, _, s, _ = P.shape
    tr = min(512, s)
    nb = ROPE_NB
    return pl.pallas_call(
        _rope_kernel,
        grid=(b, s // tr, N_NSA_BLOCKS // nb),
        in_specs=[pl.BlockSpec((1, nb, tr, LANES), lambda bi, i, j: (bi, J_BQ // nb + j, i, 0)),
                  pl.BlockSpec((1, tr, 1), lambda bi, i, j: (bi, i, 0)),
                  pl.BlockSpec((1, LANES), lambda bi, i, j: (0, 0))],
        out_specs=pl.BlockSpec((1, nb, tr, LANES), lambda bi, i, j: (bi, j, i, 0)),
        out_shape=jax.ShapeDtypeStruct((b, N_NSA_BLOCKS, s, LANES), BF16),
        scratch_shapes=[pltpu.VMEM((tr, LANES), F32), pltpu.VMEM((tr, LANES), F32)],
        compiler_params=_cparams(("parallel", "parallel", "arbitrary")),
        name="nsa_rope",
    )(P, pos_f32, invf)


def _compress_kernel(r_ref, w1_ref, w2_ref, pos_ref, o_ref):
    r = r_ref[0, 0]
    nr = r.shape[0]
    half = CMP_STRIDE * HEAD_DIM
    a = jnp.dot(r, w1_ref[0, :half, :], preferred_element_type=F32)
    bm = jnp.dot(r, w1_ref[0, half:, :], preferred_element_type=F32)
    pos8 = jnp.broadcast_to(pos_ref[0], (8, CMP_LEN * HEAD_DIM)).astype(BF16)
    pb = jnp.dot(pos8, w1_ref[0], preferred_element_type=F32)[0:1, :]
    hid = a + pltpu.roll(bm, nr - 1, axis=0) + pb
    out = jnp.dot(_gelu(hid).astype(BF16), w2_ref[0], preferred_element_type=F32)
    row = lax.broadcasted_iota(jnp.int32, out.shape, 0)
    o_ref[0, 0] = jnp.where(row < nr - 1, out, 0.0).astype(BF16)


def _compress(rows, w1, w2, pos):
    b, _, nr, width = rows.shape
    return pl.pallas_call(
        _compress_kernel,
        grid=(b, 4),
        in_specs=[pl.BlockSpec((1, 1, nr, width), lambda bi, j: (bi, j, 0, 0)),
                  pl.BlockSpec((1, CMP_LEN * HEAD_DIM, CMP_HIDDEN), lambda bi, j: (j // 2, 0, 0)),
                  pl.BlockSpec((1, CMP_HIDDEN, HEAD_DIM), lambda bi, j: (j // 2, 0, 0)),
                  pl.BlockSpec((1, 1, CMP_LEN * HEAD_DIM), lambda bi, j: (j // 2, 0, 0))],
        out_specs=pl.BlockSpec((1, 1, nr, HEAD_DIM), lambda bi, j: (bi, j, 0, 0)),
        out_shape=jax.ShapeDtypeStruct((b, 4, nr, HEAD_DIM), BF16),
        compiler_params=_cparams(("parallel", "arbitrary")),
        name="nsa_compress",
    )(rows, w1, w2, pos)


def _nsa_kernel(q_ref, kc_ref, vc_ref, ks_ref, vs_ref, kw_ref, vw_ref, gate_ref, ovt_ref, e_ref,
                o_ref, m_scr, l_scr, acc_scr, os_scr, *, s_len):
    i = pl.program_id(2)
    TQ, HP = NSA_TQ, NSA_HPG
    R = HP * TQ
    nsb = s_len // SLC_LEN
    ncp = s_len // CMP_STRIDE
    n_sel = min(SLC_TOPK, nsb)
    s0 = i * TQ
    q2 = q_ref[0].reshape(R, HEAD_DIM)

    sc = _bdot_nt(q2, kc_ref[0, 0])
    tq_r = s0 + (lax.broadcasted_iota(jnp.int32, (R, ncp), 0) & (TQ - 1))
    ncol = lax.broadcasted_iota(jnp.int32, (R, ncp), 1)
    valid = jnp.logical_and(ncol * CMP_STRIDE + (CMP_LEN - 1) <= tq_r, ncol < ncp - 1)
    scm = jnp.where(valid, sc, NEG)
    e = jnp.where(valid, jnp.exp(scm - jnp.max(scm, axis=1, keepdims=True)), 0.0)
    p_c = e / jnp.maximum(jnp.sum(e, axis=1, keepdims=True), 1e-30)
    o_c = _bdot(p_c, vc_ref[0, 0])
    psum = jnp.sum(p_c.reshape(HP, TQ, ncp), axis=0)
    imp = _fdot_nt(ovt_ref[...], psum)
    blk = lax.broadcasted_iota(jnp.int32, (nsb, TQ), 0)
    tq_l = s0 + lax.broadcasted_iota(jnp.int32, (nsb, TQ), 1)
    cur = tq_l // SLC_LEN
    forced = jnp.logical_or(blk == 0, jnp.logical_or(blk == cur, blk == cur - 1))
    imp = jnp.where(forced, BIG, imp)
    imp = jnp.where(blk * SLC_LEN <= tq_l, imp, NEG)
    rank = jnp.zeros((nsb, TQ), jnp.int32)
    for j in range(nsb):
        rj = imp[j:j + 1, :]
        beats = jnp.logical_or(rj > imp, jnp.logical_and(rj == imp, blk > j))
        rank = rank + jnp.where(beats, 1, 0)
    sel_t = jnp.where(rank < n_sel, 1.0, 0.0).astype(F32)
    if nsb < LANES:
        sel_t = jnp.concatenate([sel_t, jnp.zeros((LANES - nsb, TQ), F32)], axis=0)
    sel = jnp.transpose(sel_t).astype(BF16)

    ones_blk = jnp.ones((TQ, HEAD_DIM), BF16)
    HC = NSA_HC

    def flash(k_ref, v_ref, tk, lo, hi, mask_fn):
        m_scr[...] = jnp.full(m_scr.shape, NEG, F32)
        l_scr[...] = jnp.zeros(l_scr.shape, F32)
        acc_scr[...] = jnp.zeros(acc_scr.shape, F32)
        qpos = s0 + lax.broadcasted_iota(jnp.int32, (TQ, tk), 0)
        kofs = lax.broadcasted_iota(jnp.int32, (TQ, tk), 1)
        ones_v = jnp.concatenate([ones_blk] * (tk // TQ), axis=0)

        def body(j, carry):
            r = pl.multiple_of(j * tk, tk)
            bias = jnp.where(mask_fn(r, r + kofs, qpos), 0.0, NEG)
            kb = k_ref[0, 0, pl.ds(r, tk), :]
            vb = jnp.concatenate([v_ref[0, 0, pl.ds(r, tk), :], ones_v], axis=1)
            for c in range(HP // HC):
                rows = slice(c * HC * TQ, (c + 1) * HC * TQ)
                s = _bdot_nt(q_ref[0, c * HC:(c + 1) * HC].reshape(HC * TQ, HEAD_DIM), kb)
                s = (s.reshape(HC, TQ, tk) + bias[None]).reshape(HC * TQ, tk)
                m_prev = m_scr[rows, :]
                m_new = jnp.maximum(m_prev, jnp.max(s, axis=1, keepdims=True))
                alpha = jnp.exp(m_prev - m_new)
                p = jnp.exp(s - jnp.concatenate([m_new] * (tk // LANES), axis=1)).astype(BF16)
                pv = jnp.dot(p, vb, preferred_element_type=F32)
                l_scr[rows, :] = alpha * l_scr[rows, :] + pv[:, HEAD_DIM:]
                acc_scr[rows, :] = alpha * acc_scr[rows, :] + pv[:, :HEAD_DIM]
                m_scr[rows, :] = m_new
            return carry

        lax.fori_loop(lo, hi, body, 0)

    tks = min(NSA_TK_SEL, s_len)

    def sel_mask(r, kpos, qpos):
        chosen = jnp.dot(sel, e_ref[:, pl.ds(r, tks)], preferred_element_type=F32)
        return jnp.logical_and(chosen > 0.5, kpos <= qpos)

    flash(ks_ref, vs_ref, tks, 0, (s0 + TQ - 1) // tks + 1, sel_mask)
    os_scr[...] = acc_scr[...] / l_scr[...]

    def win_mask(r, kpos, qpos):
        return jnp.logical_and(kpos <= qpos, kpos > qpos - WIN)

    tkw = min(NSA_TK_WIN, s_len)
    flash(kw_ref, vw_ref, tkw, jnp.maximum(s0 - WIN + 1, 0) // tkw, (s0 + TQ - 1) // tkw + 1, win_mask)

    gates = _sigmoid(gate_ref[0, 0])
    for hh in range(HP):
        rows = slice(hh * TQ, (hh + 1) * TQ)
        o_w = acc_scr[rows, :] / l_scr[rows, :]
        out = (gates[:, 3 * hh:3 * hh + 1] * o_c[rows, :]
               + gates[:, 3 * hh + 1:3 * hh + 2] * os_scr[rows, :]
               + gates[:, 3 * hh + 2:3 * hh + 3] * o_w)
        o_ref[0, :, hh * HEAD_DIM:(hh + 1) * HEAD_DIM] = out.astype(BF16)


def _nsa_attend(rp, cmp_kv, P, ovt, expand):
    b, _, s, _ = rp.shape
    TQ, HP, G = NSA_TQ, NSA_HPG, NSA_GROUPS
    ncp = cmp_kv.shape[2]
    R = HP * TQ
    full = lambda j0: pl.BlockSpec((1, 1, s, HEAD_DIM), lambda bi, g, i: (bi, j0 + g, 0, 0))
    cmp_spec = lambda j0: pl.BlockSpec((1, 1, ncp, HEAD_DIM), lambda bi, g, i: (bi, j0 + g, 0, 0))
    return pl.pallas_call(
        functools.partial(_nsa_kernel, s_len=s),
        grid=(b, G, s // TQ),
        in_specs=[pl.BlockSpec((1, HP, TQ, HEAD_DIM), lambda bi, g, i: (bi, g, i, 0)),
                  cmp_spec(0), cmp_spec(2),
                  full(R_KS), full(R_VS), full(R_KW), full(R_VW),
                  pl.BlockSpec((1, 1, TQ, LANES), lambda bi, g, i: (bi, J_GATE + g, i, 0)),
                  pl.BlockSpec(ovt.shape, lambda bi, g, i: (0, 0)),
                  pl.BlockSpec(expand.shape, lambda bi, g, i: (0, 0))],
        out_specs=pl.BlockSpec((1, TQ, HP * HEAD_DIM), lambda bi, g, i: (bi, i, g)),
        out_shape=jax.ShapeDtypeStruct((b, s, NSA_HEADS * HEAD_DIM), BF16),
        scratch_shapes=[pltpu.VMEM((R, LANES), F32), pltpu.VMEM((R, LANES), F32),
                        pltpu.VMEM((R, HEAD_DIM), F32), pltpu.VMEM((R, HEAD_DIM), F32)],
        compiler_params=_cparams(("parallel", "parallel", "arbitrary")),
        name="nsa_attend",
    )(rp, cmp_kv, cmp_kv, rp, rp, rp, rp, P, ovt, expand)


def _nsa(P, positions, cmp_pos_k, cmp_w1_k, cmp_w2_k, cmp_pos_v, cmp_w1_v, cmp_w2_v):
    b, _, s, _ = P.shape
    lanes = np.arange(LANES)
    invf = np.where(lanes < ROPE_DIM, ROPE_THETA ** (-(lanes % ROPE_HALF) / ROPE_HALF), 0.0)
    rp = _rope(P, positions.astype(F32).reshape(b, s, 1), jnp.asarray(invf, F32).reshape(1, LANES))

    ncp = s // CMP_STRIDE
    rows = rp[:, R_KC:R_KC + 4].reshape(b, 4, ncp, CMP_STRIDE * HEAD_DIM)
    w1 = jnp.stack([cmp_w1_k, cmp_w1_v]).astype(BF16)
    w2 = jnp.stack([cmp_w2_k, cmp_w2_v]).astype(BF16)
    pos = jnp.stack([cmp_pos_k, cmp_pos_v]).reshape(2, 1, CMP_LEN * HEAD_DIM)
    cmp_kv = _compress(rows, w1, w2, pos)

    nsb = s // SLC_LEN
    cmp_start = np.arange(ncp) * CMP_STRIDE
    slc_start = np.arange(nsb) * SLC_LEN
    ov = ((cmp_start[:, None] < slc_start[None, :] + SLC_LEN)
          & (cmp_start[:, None] + CMP_LEN > slc_start[None, :])
          & (np.arange(ncp)[:, None] < ncp - 1))
    ovt = jnp.asarray(ov.T.astype(np.float32))
    expand = (np.arange(LANES)[:, None] == (np.arange(s)[None, :] // SLC_LEN))
    expand = jnp.asarray(expand.astype(np.float32), BF16)
    return _nsa_attend(rp, cmp_kv, P, ovt, expand)


MERGE_TN = 512


def _merge_kernel(oa_ref, ob_ref, wg_ref, wn_ref, ma_ref, mb_ref, y_ref):
    ya = jnp.dot(oa_ref[0], wg_ref[...], preferred_element_type=F32)
    yb = jnp.dot(ob_ref[0], wn_ref[...], preferred_element_type=F32)
    for k in range(MERGE_TN // LANES):
        cols = slice(k * LANES, (k + 1) * LANES)
        y = _sigmoid(ma_ref[0, k]) * ya[:, cols] + _sigmoid(mb_ref[0, k]) * yb[:, cols]
        y_ref[0, :, cols] = y.astype(BF16)


def _merge(o_a, o_b, wg, wn, P):
    b, s, da = o_a.shape
    db = o_b.shape[2]
    d = wg.shape[1]
    tm, tn = min(512, s), MERGE_TN
    nk = tn // LANES
    return pl.pallas_call(
        _merge_kernel,
        grid=(b, s // tm, d // tn),
        in_specs=[pl.BlockSpec((1, tm, da), lambda bi, i, j: (bi, i, 0)),
                  pl.BlockSpec((1, tm, db), lambda bi, i, j: (bi, i, 0)),
                  pl.BlockSpec((da, tn), lambda bi, i, j: (0, j)),
                  pl.BlockSpec((db, tn), lambda bi, i, j: (0, j)),
                  pl.BlockSpec((1, nk, tm, LANES), lambda bi, i, j: (bi, J_MA // nk + j, i, 0)),
                  pl.BlockSpec((1, nk, tm, LANES), lambda bi, i, j: (bi, J_MB // nk + j, i, 0))],
        out_specs=pl.BlockSpec((1, tm, tn), lambda bi, i, j: (bi, i, j)),
        out_shape=jax.ShapeDtypeStruct((b, s, d), BF16),
        compiler_params=_cparams(("parallel", "parallel", "arbitrary")),
        name="merge",
    )(o_a, o_b, wg, wn, P, P)


def _outproj_kernel(y_ref, w_ref, x_ref, g_ref, o_ref):
    o_ref[0] = x_ref[0] + g_ref[0] * jnp.dot(y_ref[0], w_ref[...], preferred_element_type=F32)


def _outproj(y, w, x, g):
    b, s, d = x.shape
    tm, tn = min(512, s), 512
    return pl.pallas_call(
        _outproj_kernel,
        grid=(b, s // tm, d // tn),
        in_specs=[pl.BlockSpec((1, tm, d), lambda bi, i, j: (bi, i, 0)),
                  pl.BlockSpec((d, tn), lambda bi, i, j: (0, j)),
                  pl.BlockSpec((1, tm, tn), lambda bi, i, j: (bi, i, j)),
                  pl.BlockSpec((1, 1, tn), lambda bi, i, j: (bi, 0, j))],
        out_specs=pl.BlockSpec((1, tm, tn), lambda bi, i, j: (bi, i, j)),
        out_shape=jax.ShapeDtypeStruct((b, s, d), F32),
        compiler_params=_cparams(("parallel", "parallel", "arbitrary")),
        name="outproj",
    )(y, w, x, g.reshape(b, 1, d))


PEER_TB = 512
PEER_TG = 64
PEER_AHEAD = 12
PEER_SLOTS = 16
PEER_SEL = PEER_HEADS * PEER_TOPK
MIN_F32 = -3.0e38


def _topk_rows(vals, payload=None):
    nrow = vals.shape[0]
    rid = lax.broadcasted_iota(jnp.int32, vals.shape, 0)
    out_v, out_i = [], []
    for _ in range(PEER_TOPK):
        m = jnp.max(vals, axis=0, keepdims=True)
        idx = jnp.min(jnp.where(vals == m, rid, nrow), axis=0, keepdims=True)
        hit = rid == idx
        out_v.append(m)
        if payload is None:
            out_i.append(idx)
        else:
            out_i.append(jnp.sum(jnp.where(hit, payload, 0), axis=0, keepdims=True))
        vals = jnp.where(hit, MIN_F32, vals)
    return jnp.concatenate(out_v, axis=0), jnp.concatenate(out_i, axis=0)


def _peer_topk_kernel(qh_ref, k1_ref, k2_ref, eidx_ref, gw_ref):
    def head(h, carry):
        s1 = _fdot_nt(k1_ref[h], qh_ref[0, 2 * h])
        s2 = _fdot_nt(k2_ref[h], qh_ref[0, 2 * h + 1])
        v1, i1 = _topk_rows(s1)
        v2, i2 = _topk_rows(s2)
        keep = [PEER_TOPK // (a + 1) for a in range(PEER_TOPK)]
        npad = (-sum(keep)) % 8
        cand = jnp.concatenate([v1[a:a + 1, :] + v2[0:keep[a], :] for a in range(PEER_TOPK)]
                               + [jnp.full((npad, v1.shape[1]), MIN_F32, F32)], axis=0)
        cidx = jnp.concatenate([i1[a:a + 1, :] * PEER_NKEYS + i2[0:keep[a], :] for a in range(PEER_TOPK)]
                               + [jnp.zeros((npad, v1.shape[1]), jnp.int32)], axis=0)
        sc, eidx = _topk_rows(cand, cidx)
        ex = jnp.exp(sc - sc[0:1, :])
        eidx_ref[h] = eidx
        gw_ref[h] = ex / jnp.sum(ex, axis=0, keepdims=True)
        return carry

    lax.fori_loop(0, PEER_HEADS, head, 0)


def _peer_topk(qh, keys1, keys2):
    b, nb, s, _ = qh.shape
    tb = min(PEER_TB, s)
    nt = s // tb
    kspec = pl.BlockSpec(keys1.shape, lambda bi, i: (0, 0, 0))
    ospec = pl.BlockSpec((PEER_HEADS, PEER_TOPK, tb), lambda bi, i: (0, 0, bi * nt + i))
    return pl.pallas_call(
        _peer_topk_kernel,
        grid=(b, nt),
        in_specs=[pl.BlockSpec((1, nb, tb, LANES), lambda bi, i: (bi, 0, i, 0)), kspec, kspec],
        out_specs=[ospec, ospec],
        out_shape=[jax.ShapeDtypeStruct((PEER_HEADS, PEER_TOPK, b * s), jnp.int32),
                   jax.ShapeDtypeStruct((PEER_HEADS, PEER_TOPK, b * s), F32)],
        compiler_params=_cparams(("parallel", "parallel")),
        name="peer_topk",
    )(qh, keys1, keys2)


def _peer_gather_kernel(idx_ref, idxn_ref, gw_ref, x_ref, uv_hbm, o_ref, buf, a_scr, sem, *, tg):
    NS, D, NSLOT = PEER_SEL, PEER_AHEAD, PEER_SLOTS
    RPE = uv_hbm.shape[1]
    SUBW = RPE // 2
    half = SUBW * LANES

    def wait_rows(t):
        slot = t & (NSLOT - 1)
        pltpu.make_async_copy(uv_hbm.at[pl.ds(0, NS)], buf.at[slot], sem.at[slot]).wait()

    rid = lax.broadcasted_iota(jnp.int32, (NS, NS), 0)
    cid = lax.broadcasted_iota(jnp.int32, (NS, NS), 1)
    eye = rid == cid
    ones = jnp.ones((LANES, LANES), BF16)

    def lane_sum_rep(m):
        hi, lo = _split_bf16(m)
        return (jnp.dot(hi, ones, preferred_element_type=F32)
                + jnp.dot(lo, ones, preferred_element_type=F32))

    def halves(slot, tab, sg):
        w = buf[slot, tab * SUBW + sg]
        return pltpu.bitcast(w << 16, F32), pltpu.bitcast(w & jnp.uint32(0xFFFF0000), F32)

    def phase(t, issue_from, do_dot, do_prev):
        if do_dot:
            wait_rows(t)
            xrow = x_ref[pl.ds(t, 1), :]
            acc = jnp.zeros((NS, LANES), F32)
        if do_prev:
            act = lane_sum_rep(a_scr[(t - 1) & 1])
            gw_rep = lane_sum_rep(jnp.where(eye, gw_ref[pl.ds(t - 1, 1), :], 0.0))
            coef = gw_rep * _gelu(act)
            out_lo, out_hi = [], []
        if issue_from is not None:
            slot_i = (t + D) & (NSLOT - 1)
            base_i = (t + D) * NS if issue_from is idx_ref else (t + D - tg) * NS
        for sg in range(SUBW):
            if issue_from is not None:
                for j in range(sg * NS // SUBW, (sg + 1) * NS // SUBW):
                    pltpu.make_async_copy(uv_hbm.at[issue_from[base_i + j]],
                                          buf.at[slot_i, :, j, :], sem.at[slot_i]).start(priority=j % 2)
            if do_dot:
                lo, hi = halves(t & (NSLOT - 1), 0, sg)
                acc = acc + lo * xrow[:, sg * LANES:(sg + 1) * LANES]
                acc = acc + hi * xrow[:, half + sg * LANES:half + (sg + 1) * LANES]
            if do_prev:
                lo, hi = halves((t - 1) & (NSLOT - 1), 1, sg)
                out_lo.append(jnp.sum(lo * coef, axis=0, keepdims=True))
                out_hi.append(jnp.sum(hi * coef, axis=0, keepdims=True))
        if do_dot:
            a_scr[t & 1] = acc
        if do_prev:
            o_ref[t - 1] = jnp.concatenate(out_lo + out_hi, axis=0)

    def ramp(t, carry):
        slot = t & (NSLOT - 1)

        def body(j, c):
            pltpu.make_async_copy(uv_hbm.at[idx_ref[t * NS + j]], buf.at[slot, :, j, :], sem.at[slot]).start()
            return c
        return lax.fori_loop(0, NS, body, carry)

    def looped(issue_from):
        def f(t, carry):
            phase(t, issue_from, True, True)
            return carry
        return f

    step = pl.program_id(0)
    last = pl.num_programs(0) - 1

    @pl.when(step == 0)
    def _():
        lax.fori_loop(0, D, ramp, 0)

    phase(0, idx_ref, True, False)
    lax.fori_loop(1, tg - D, looped(idx_ref), 0)

    @pl.when(step < last)
    def _():
        lax.fori_loop(tg - D, tg, looped(idxn_ref), 0)

    @pl.when(step == last)
    def _():
        lax.fori_loop(tg - D, tg, looped(None), 0)

    phase(tg, None, False, True)


def _peer_gather(eidx, gw, h2r, uvc):
    t = eidx.shape[0]
    tg = min(PEER_TG, t)
    d = h2r.shape[1]
    rpe = uvc.shape[1]
    return pl.pallas_call(
        functools.partial(_peer_gather_kernel, tg=tg),
        grid=(t // tg,),
        in_specs=[pl.BlockSpec((tg * PEER_SEL,), lambda i: (i,), memory_space=pltpu.SMEM),
                  pl.BlockSpec((tg * PEER_SEL,), lambda i: (jnp.minimum(i + 1, t // tg - 1),),
                               memory_space=pltpu.SMEM),
                  pl.BlockSpec((tg, PEER_SEL), lambda i: (i, 0)),
                  pl.BlockSpec((tg, d), lambda i: (i, 0)),
                  pl.BlockSpec(memory_space=pl.ANY)],
        out_specs=pl.BlockSpec((tg, d // LANES, LANES), lambda i: (i, 0, 0)),
        out_shape=jax.ShapeDtypeStruct((t, d // LANES, LANES), F32),
        scratch_shapes=[pltpu.VMEM((PEER_SLOTS, rpe, PEER_SEL, LANES), jnp.uint32),
                        pltpu.VMEM((2, PEER_SEL, LANES), F32),
                        pltpu.SemaphoreType.DMA((PEER_SLOTS,))],
        compiler_params=_cparams(("arbitrary",)),
        name="peer_gather",
    )(eidx.reshape(-1), eidx.reshape(-1), gw, h2r, uvc)


PACK_EB = 256


def _pack_kernel(u_ref, v_ref, o_ref):
    half = u_ref.shape[1] // 2
    for tab, ref in enumerate((u_ref, v_ref)):
        for sg in range(half // LANES):
            lo = ref[:, sg * LANES:(sg + 1) * LANES].astype(BF16).astype(F32)
            hi = ref[:, half + sg * LANES:half + (sg + 1) * LANES].astype(BF16).astype(F32)
            word = (pltpu.bitcast(lo, jnp.uint32) >> 16) | (pltpu.bitcast(hi, jnp.uint32) & jnp.uint32(0xFFFF0000))
            o_ref[:, tab * (half // LANES) + sg, :] = word


def _pack_tables(u, v):
    e, d = u.shape
    eb = min(PACK_EB, e)
    return pl.pallas_call(
        _pack_kernel,
        grid=(e // eb,),
        in_specs=[pl.BlockSpec((eb, d), lambda i: (i, 0)), pl.BlockSpec((eb, d), lambda i: (i, 0))],
        out_specs=pl.BlockSpec((eb, d // LANES, LANES), lambda i: (i, 0, 0)),
        out_shape=jax.ShapeDtypeStruct((e, d // LANES, LANES), jnp.uint32),
        compiler_params=_cparams(("parallel",)),
        name="peer_pack",
    )(u, v)


def _peer(x1, norm_w, sc, sh, wq, keys1, keys2, u, v):
    b, s, d = x1.shape
    qh, h2 = _normproj(x1, norm_w, sc, sh, wq.astype(BF16), tm=512, tn=512, emit_h=True)
    eidx, gw = _peer_topk(qh, keys1, keys2)
    eidx = jnp.transpose(eidx, (2, 0, 1)).reshape(b * s, PEER_SEL)
    gw = jnp.transpose(gw, (2, 0, 1)).reshape(b * s, PEER_SEL)
    uvc = _pack_tables(u, v)
    out = _peer_gather(eidx, gw, h2.reshape(b * s, d), uvc)
    return out.reshape(b, s, d)


def _final_kernel(x_ref, p_ref, g_ref, w_ref, o_ref):
    x = x_ref[0] + g_ref[0] * p_ref[0]
    o_ref[0] = x * lax.rsqrt(jnp.mean(x * x, axis=-1, keepdims=True) + EPS) * w_ref[...]


def _final(x1, peer, g2, wf):
    b, s, d = x1.shape
    tm = min(512, s)
    blk = pl.BlockSpec((1, tm, d), lambda bi, i: (bi, i, 0))
    return pl.pallas_call(
        _final_kernel,
        grid=(b, s // tm),
        in_specs=[blk, blk, pl.BlockSpec((1, 1, d), lambda bi, i: (bi, 0, 0)),
                  pl.BlockSpec((1, d), lambda bi, i: (0, 0))],
        out_specs=blk,
        out_shape=jax.ShapeDtypeStruct((b, s, d), F32),
        compiler_params=_cparams(("parallel", "parallel")),
        name="final_norm",
    )(x1, peer, g2.reshape(b, 1, d), wf.reshape(1, d))


def _pad_rows(a, mult=8):
    pad = (-a.shape[0]) % mult
    return jnp.pad(a, ((0, pad), (0, 0)))


def _permute_w_in(w):
    o_aa, o_bq, o_bg, o_ma, o_end = 4096, 4112, 7696, 7744, 11840
    d = w.shape[0]
    z = lambda n: jnp.zeros((d, n), w.dtype)
    hg = 3 * NSA_HPG
    cols = [w[:, 0:o_aa], w[:, o_bq:o_bg], w[:, o_ma:o_end],
            w[:, o_aa:o_bq], z(LANES - 16),
            w[:, o_bg:o_bg + hg], z(LANES - hg),
            w[:, o_bg + hg:o_ma], z(LANES - hg),
            z(LANES)]
    return jnp.concatenate(cols, axis=1).astype(BF16)


def kernel(x, c, positions, ada_w, ada_b, norm1_w, norm2_w, w_in, gdn_conv_w, gdn_A_log, gdn_dt_bias, gdn_norm_w, cmp_pos_k, cmp_w1_k, cmp_w2_k, cmp_pos_v, cmp_w1_v, cmp_w2_v, w_branch_gdn, w_branch_nsa, w_out, peer_wq, peer_keys1, peer_keys2, peer_u, peer_v, final_norm_w):
    b, s, d = x.shape
    l = 0
    mod = _ada(_pad_rows(c), ada_w[l], ada_b[l])[:b]
    sh1, sc1, g1, sh2, sc2, g2 = jnp.split(mod, 6, axis=-1)
    P = _normproj(x, norm1_w[l], sc1, sh1, _permute_w_in(w_in[l]), tm=1024, tn=512, emit_h=False)
    o_a = _gdn(P, gdn_conv_w[l], gdn_A_log[l], gdn_dt_bias[l], gdn_norm_w[l])
    o_b = _nsa(P, positions, cmp_pos_k[l], cmp_w1_k[l], cmp_w2_k[l], cmp_pos_v[l], cmp_w1_v[l], cmp_w2_v[l])
    y = _merge(o_a, o_b, w_branch_gdn[l].astype(BF16), w_branch_nsa[l].astype(BF16), P)
    x1 = _outproj(y, w_out[l].astype(BF16), x, g1)
    peer = _peer(x1, norm2_w[l], sc2, sh2, peer_wq[l], peer_keys1[l], peer_keys2[l], peer_u[l], peer_v[l])
    return _final(x1, peer, g2, final_norm_w)
```

```python
import functools
import math

import numpy as np
import jax
import jax.numpy as jnp
from jax import lax
from jax.experimental import pallas as pl
from jax.experimental.pallas import tpu as pltpu

F32 = jnp.float32
BF16 = jnp.bfloat16
HI = lax.Precision.HIGHEST

LANES = 128
VMEM_LIMIT = 56 * 1024 * 1024

EPS = 1e-6
ROPE_THETA = 500000.0
HEAD_DIM = 128
ROPE_DIM = HEAD_DIM // 4
ROPE_HALF = ROPE_DIM // 2

GDN_HEADS = 8
GDN_CONV = 4
GDN_CHUNK = 64

NSA_HEADS = 16
NSA_GROUPS = 2
NSA_HPG = NSA_HEADS // NSA_GROUPS
CMP_LEN = 32
CMP_STRIDE = 16
CMP_HIDDEN = 256
SLC_LEN = 64
SLC_TOPK = 16
WIN = 512

PEER_HEADS = 8
PEER_NKEYS = 128
PEER_QDIM = 256
PEER_TOPK = 16

NEG = -1e30
BIG = 1e9

J_AQ, J_AK, J_AV, J_AZ = 0, 8, 16, 24
J_BQ = 32
N_NSA_BLOCKS = 28
J_MA, J_MB = 60, 76
J_SMALL = 92
J_GATE = 93
NJ = 96


def _cparams(sem):
    return pltpu.CompilerParams(dimension_semantics=sem, vmem_limit_bytes=VMEM_LIMIT)


def _bdot(a, b):
    return jnp.dot(a.astype(BF16), b.astype(BF16), preferred_element_type=F32)


def _bdot_nt(a, b):
    return lax.dot_general(a.astype(BF16), b.astype(BF16), (((1,), (1,)), ((), ())),
                           preferred_element_type=F32)


def _split_bf16(a):
    hi = a.astype(BF16)
    return hi, (a - hi.astype(F32)).astype(BF16)


def _dot3(a, b, exact_a=False):
    bh, bl = _split_bf16(b)
    dot = functools.partial(jnp.dot, preferred_element_type=F32)
    if exact_a:
        return dot(a, bh) + dot(a, bl)
    ah, al = _split_bf16(a)
    return dot(ah, bh) + dot(al, bh) + dot(ah, bl)


def _dot3_nt(a, b):
    ah, al = _split_bf16(a)
    bh, bl = _split_bf16(b)
    dot = functools.partial(lax.dot_general, dimension_numbers=(((1,), (1,)), ((), ())),
                            preferred_element_type=F32)
    return dot(ah, bh) + dot(al, bh) + dot(ah, bl)


def _fdot(a, b):
    return jnp.dot(a, b, precision=HI, preferred_element_type=F32)


def _fdot_nt(a, b):
    return lax.dot_general(a, b, (((1,), (1,)), ((), ())), precision=HI,
                           preferred_element_type=F32)


def _sigmoid(x):
    return 1.0 / (1.0 + jnp.exp(-x))


def _silu(x):
    return x * _sigmoid(x)


def _gelu(x):
    return 0.5 * x * (1.0 + jnp.tanh(math.sqrt(2.0 / math.pi) * (x + 0.044715 * (x * x * x))))


def _softplus(x):
    return jnp.maximum(x, 0.0) + jnp.log(1.0 + jnp.exp(-jnp.abs(x)))


def _ada_kernel(c_ref, w_ref, b_ref, o_ref):
    o_ref[...] = _fdot(_silu(c_ref[...]), w_ref[...]) + b_ref[...]


def _ada(c_pad, ada_w, ada_b):
    m, d = c_pad.shape
    n = ada_w.shape[1]
    tn = 1024
    return pl.pallas_call(
        _ada_kernel,
        grid=(n // tn,),
        in_specs=[pl.BlockSpec((m, d), lambda j: (0, 0)),
                  pl.BlockSpec((d, tn), lambda j: (0, j)),
                  pl.BlockSpec((1, tn), lambda j: (0, j))],
        out_specs=pl.BlockSpec((m, tn), lambda j: (0, j)),
        out_shape=jax.ShapeDtypeStruct((m, n), F32),
        compiler_params=_cparams(("parallel",)),
        name="ada_mod",
    )(c_pad, ada_w, ada_b.reshape(1, n))


def _normproj_kernel(x_ref, nw_ref, sc_ref, sh_ref, w_ref, *rest, nk, emit_h):
    if emit_h:
        o_ref, h_out_ref, h_scr = rest
    else:
        o_ref, h_scr = rest

    @pl.when(pl.program_id(2) == 0)
    def _():
        x = x_ref[0]
        ms = jnp.mean(x * x, axis=-1, keepdims=True)
        y = x * lax.rsqrt(ms + EPS) * nw_ref[...]
        h = y * (1.0 + sc_ref[0]) + sh_ref[0]
        h_scr[...] = h.astype(BF16)
        if emit_h:
            h_out_ref[0] = h

    acc = jnp.dot(h_scr[...], w_ref[...], preferred_element_type=F32)
    for k in range(nk):
        o_ref[0, k] = acc[:, k * LANES:(k + 1) * LANES]


def _normproj(x, nw, sc, sh, w_bf16, *, tm, tn, emit_h):
    b, s, d = x.shape
    n = w_bf16.shape[1]
    tm = min(tm, s)
    nk = tn // LANES
    out_shape = [jax.ShapeDtypeStruct((b, n // LANES, s, LANES), F32)]
    out_specs = [pl.BlockSpec((1, nk, tm, LANES), lambda bi, i, j: (bi, j, i, 0))]
    if emit_h:
        out_shape.append(jax.ShapeDtypeStruct((b, s, d), F32))
        out_specs.append(pl.BlockSpec((1, tm, d), lambda bi, i, j: (bi, i, 0)))
    res = pl.pallas_call(
        functools.partial(_normproj_kernel, nk=nk, emit_h=emit_h),
        grid=(b, s // tm, n // tn),
        in_specs=[pl.BlockSpec((1, tm, d), lambda bi, i, j: (bi, i, 0)),
                  pl.BlockSpec((1, d), lambda bi, i, j: (0, 0)),
                  pl.BlockSpec((1, 1, d), lambda bi, i, j: (bi, 0, 0)),
                  pl.BlockSpec((1, 1, d), lambda bi, i, j: (bi, 0, 0)),
                  pl.BlockSpec((d, tn), lambda bi, i, j: (0, j))],
        out_specs=out_specs,
        out_shape=out_shape,
        scratch_shapes=[pltpu.VMEM((tm, d), BF16)],
        compiler_params=_cparams(("parallel", "parallel", "arbitrary")),
        name="normproj_h" if emit_h else "normproj",
    )(x, nw.reshape(1, d), sc.reshape(b, 1, d), sh.reshape(b, 1, d), w_bf16)
    return res if emit_h else res[0]


GDN_CB = 1024
GDN_GB = 256


def _gdn_local_kernel(q_ref, qh_ref, k_ref, kh_ref, v_ref, vh_ref, sm_ref,
                      cwq_ref, cwk_ref, cwv_ref, alog_ref, dtb_ref,
                      w_ref, u0_ref, qk_ref, qg_ref, kd_ref, ge_ref, *, cb):
    h = pl.program_id(1)
    i = pl.program_id(2)
    C, G = GDN_CHUNK, GDN_GB
    first = (i == 0)

    def conv_act(main_ref, halo_ref, cw_ref, r0):
        if r0 == 0:
            prev = jnp.where(first, 0.0, halo_ref[0, 0])
        else:
            prev = main_ref[0, 0, r0 - 8:r0, :]
        ext = jnp.concatenate([prev, main_ref[0, 0, r0:r0 + G, :]], axis=0)
        w = cw_ref[0]
        y = w[0:1, :] * ext[5:5 + G, :]
        for j in range(1, GDN_CONV):
            y = y + w[j:j + 1, :] * ext[5 + j:5 + j + G, :]
        return _silu(y)

    rid = lax.broadcasted_iota(jnp.int32, (G, G), 0)
    cid = lax.broadcasted_iota(jnp.int32, (G, G), 1)
    same = (rid // C) == (cid // C)
    incl = jnp.logical_and(same, rid >= cid)
    strict = jnp.logical_and(same, rid > cid)
    eye = rid == cid
    is_last = cid == (rid // C) * C + (C - 1)
    joins = []
    bs = 1
    while bs < C:
        joins.append(jnp.logical_and(jnp.logical_and((rid // (2 * bs)) == (cid // (2 * bs)),
                                                     (rid & (2 * bs - 1)) >= bs),
                                     (cid & (2 * bs - 1)) < bs))
        bs *= 2
    tri = jnp.where(incl, 1.0, 0.0).astype(BF16)
    eye_f = jnp.where(eye, 1.0, 0.0).astype(F32)
    lane = lax.broadcasted_iota(jnp.int32, (G, LANES), 1)
    neg_a = -jnp.exp(alog_ref[...])
    dtb = dtb_ref[...]

    for grp in range(cb // G):
        r0 = grp * G
        q = conv_act(q_ref, qh_ref, cwq_ref, r0)
        k = conv_act(k_ref, kh_ref, cwk_ref, r0)
        v = conv_act(v_ref, vh_ref, cwv_ref, r0)
        q = q * lax.rsqrt(jnp.sum(q * q, axis=-1, keepdims=True) + EPS) * (HEAD_DIM ** -0.5)
        k = k * lax.rsqrt(jnp.sum(k * k, axis=-1, keepdims=True) + EPS)

        sm = sm_ref[0, 0, r0:r0 + G, :]
        g_all = neg_a * _softplus(sm + dtb)
        gc_all = _dot3(tri, g_all, exact_a=True)
        gc = jnp.sum(jnp.where(lane == h, gc_all, 0.0), axis=1, keepdims=True)
        beta = jnp.sum(jnp.where(lane == GDN_HEADS + h, _sigmoid(sm), 0.0), axis=1, keepdims=True)
        gc_row = jnp.sum(jnp.where(eye, gc, 0.0), axis=0, keepdims=True)
        gc_last = jnp.sum(jnp.where(is_last, gc_row, 0.0), axis=1, keepdims=True)
        decay = jnp.where(incl, jnp.exp(jnp.where(incl, gc - gc_row, 0.0)), 0.0)
        gamma = jnp.exp(gc)

        kk = _dot3_nt(k, k)
        lmat = jnp.where(strict, decay * kk, 0.0) * beta
        tinv = eye_f - jnp.where(joins[0], lmat, 0.0)
        for lvl in range(1, len(joins)):
            tinv = tinv - _bdot(_bdot(tinv, jnp.where(joins[lvl], lmat, 0.0)), tinv)
        resid = (eye_f - tinv) - _dot3(lmat, tinv)
        tinv = tinv + _bdot(tinv, resid)
        rhs = jnp.concatenate([(beta * gamma) * k, beta * v], axis=1)
        wu = _dot3(tinv, rhs)
        qk = decay * _bdot_nt(q, k)

        rows = slice(r0, r0 + G)
        w_ref[0, 0, rows, :] = wu[:, :HEAD_DIM].astype(BF16)
        u0_ref[0, 0, rows, :] = wu[:, HEAD_DIM:]
        qg_ref[0, 0, rows, :] = (gamma * q).astype(BF16)
        kd_ref[0, 0, rows, :] = (jnp.exp(gc_last - gc) * k).astype(BF16)
        ge_all = jnp.broadcast_to(jnp.exp(gc_last), (G, LANES))
        for c in range(G // C):
            cr = slice(c * C, (c + 1) * C)
            qk_ref[0, 0, r0 + c * C:r0 + (c + 1) * C, :] = qk[cr, cr].astype(BF16)
            ge_ref[0, 0, pl.ds((i * (cb // G) + grp) * (G // C) + c, 1), :] = ge_all[c * C:c * C + 1, :]


def _gdn_local(P, cw, alog_pad, dtb_pad):
    b, _, s, _ = P.shape
    H, CB, C = GDN_HEADS, min(GDN_CB, s), GDN_CHUNK
    n = s // C

    def main(j0):
        return pl.BlockSpec((1, 1, CB, LANES), lambda bi, h, i: (bi, j0 + h, i, 0))

    def halo(j0):
        return pl.BlockSpec((1, 1, 8, LANES),
                            lambda bi, h, i: (bi, j0 + h, jnp.maximum(i * (CB // 8) - 1, 0), 0))

    def cws(j0):
        return pl.BlockSpec((1, GDN_CONV, LANES), lambda bi, h, i: (j0 + h, 0, 0))

    row = pl.BlockSpec((1, LANES), lambda bi, h, i: (0, 0))
    hs = lambda width: pl.BlockSpec((1, 1, CB, width), lambda bi, h, i: (bi, h, i, 0))
    return pl.pallas_call(
        functools.partial(_gdn_local_kernel, cb=CB),
        grid=(b, H, s // CB),
        in_specs=[main(J_AQ), halo(J_AQ), main(J_AK), halo(J_AK), main(J_AV), halo(J_AV),
                  pl.BlockSpec((1, 1, CB, LANES), lambda bi, h, i: (bi, J_SMALL, i, 0)),
                  cws(0), cws(8), cws(16), row, row],
        out_specs=[hs(LANES), hs(LANES), hs(C), hs(LANES), hs(LANES),
                   pl.BlockSpec((1, 1, n, LANES), lambda bi, h, i: (bi, h, 0, 0))],
        out_shape=[jax.ShapeDtypeStruct((b, H, s, LANES), BF16),
                   jax.ShapeDtypeStruct((b, H, s, LANES), F32),
                   jax.ShapeDtypeStruct((b, H, s, C), BF16),
                   jax.ShapeDtypeStruct((b, H, s, LANES), BF16),
                   jax.ShapeDtypeStruct((b, H, s, LANES), BF16),
                   jax.ShapeDtypeStruct((b, H, n, LANES), F32)],
        compiler_params=_cparams(("parallel", "parallel", "arbitrary")),
        name="gdn_local",
    )(P, P, P, P, P, P, P, cw, cw, cw, alog_pad, dtb_pad)


GDN_HB = 4


def _gdn_scan_kernel(w_ref, u0_ref, qk_ref, qg_ref, kd_ref, ge_ref, z_ref, nw_ref, o_ref, s_scr,
                     *, sb):
    C = GDN_CHUNK

    @pl.when(pl.program_id(2) == 0)
    def _():
        s_scr[...] = jnp.zeros_like(s_scr)

    nw = nw_ref[...]

    def body(n, carry):
        r = pl.multiple_of(n * C, C)
        for hh in range(GDN_HB):
            st = s_scr[hh]
            stb = st.astype(BF16)
            u = u0_ref[0, hh, pl.ds(r, C), :] - jnp.dot(w_ref[0, hh, pl.ds(r, C), :], stb,
                                                       preferred_element_type=F32)
            ub = u.astype(BF16)
            o = (jnp.dot(qg_ref[0, hh, pl.ds(r, C), :], stb, preferred_element_type=F32)
                 + jnp.dot(qk_ref[0, hh, pl.ds(r, C), :], ub, preferred_element_type=F32))
            ge = ge_ref[0, hh, pl.ds(n, 1), :]
            s_scr[hh] = ge * st + lax.dot_general(kd_ref[0, hh, pl.ds(r, C), :], ub,
                                                  (((0,), (0,)), ((), ())),
                                                  preferred_element_type=F32)
            on = o * lax.rsqrt(jnp.mean(o * o, axis=-1, keepdims=True) + EPS) * nw
            z = z_ref[0, hh, pl.ds(r, C), :]
            o_ref[0, pl.ds(r, C), hh * LANES:(hh + 1) * LANES] = (on * _silu(z)).astype(BF16)
        return carry

    lax.fori_loop(0, sb // C, body, 0)


def _gdn_scan(w, u0, qk, qg, kd, ge, P, norm_w):
    b, H, s, _ = w.shape
    C, HB = GDN_CHUNK, GDN_HB
    sb = min(1024, s)
    hs = lambda width: pl.BlockSpec((1, HB, sb, width), lambda bi, hb, i: (bi, hb, i, 0))
    return pl.pallas_call(
        functools.partial(_gdn_scan_kernel, sb=sb),
        grid=(b, H // HB, s // sb),
        in_specs=[hs(LANES), hs(LANES), hs(C), hs(LANES), hs(LANES),
                  pl.BlockSpec((1, HB, sb // C, LANES), lambda bi, hb, i: (bi, hb, i, 0)),
                  pl.BlockSpec((1, HB, sb, LANES), lambda bi, hb, i: (bi, J_AZ // HB + hb, i, 0)),
                  pl.BlockSpec((1, LANES), lambda bi, hb, i: (0, 0))],
        out_specs=pl.BlockSpec((1, sb, HB * LANES), lambda bi, hb, i: (bi, i, hb)),
        out_shape=jax.ShapeDtypeStruct((b, s, H * LANES), BF16),
        scratch_shapes=[pltpu.VMEM((HB, HEAD_DIM, HEAD_DIM), F32)],
        compiler_params=_cparams(("parallel", "parallel", "arbitrary")),
        name="gdn_scan",
    )(w, u0, qk, qg, kd, ge, P, norm_w.reshape(1, LANES))


def _pad_lanes_row(v):
    return jnp.pad(v.astype(F32), (0, LANES - v.shape[0])).reshape(1, LANES)


def _gdn(P, conv_w, a_log, dt_bias, norm_w):
    cw = jnp.transpose(conv_w.reshape(GDN_CONV, 3 * GDN_HEADS, LANES), (1, 0, 2))
    w, u0, qk, qg, kd, ge = _gdn_local(P, cw, _pad_lanes_row(a_log), _pad_lanes_row(dt_bias))
    return _gdn_scan(w, u0, qk, qg, kd, ge, P, norm_w)


NSA_TQ = 128
NSA_TK_SEL = 512
NSA_TK_WIN = 256
NSA_HC = 2
R_Q, R_KC, R_VC, R_KS, R_VS, R_KW, R_VW = 0, 16, 18, 20, 22, 24, 26
ROPE_NB = 4


def _rope_kernel(x_ref, pos_ref, invf_ref, o_ref, cos_scr, sin_scr):
    j = pl.program_id(2)
    lane = lax.broadcasted_iota(jnp.int32, cos_scr.shape, 1)

    @pl.when(j == 0)
    def _():
        ang = pos_ref[0] * invf_ref[...]
        sn = jnp.sin(ang)
        cos_scr[...] = jnp.where(lane < ROPE_DIM, jnp.cos(ang), 1.0)
        sin_scr[...] = jnp.where(lane < ROPE_HALF, -sn, jnp.where(lane < ROPE_DIM, sn, 0.0))

    is_q = j < NSA_HEADS // ROPE_NB
    scale = jnp.where(is_q, HEAD_DIM ** -0.5, 1.0)
    for k in range(ROPE_NB):
        x = x_ref[0, k]
        swapped = jnp.where(lane < ROPE_HALF, pltpu.roll(x, LANES - ROPE_HALF, axis=1),
                            pltpu.roll(x, ROPE_HALF, axis=1))
        rot = x * cos_scr[...] + swapped * sin_scr[...]
        out = rot * scale if k < 2 else jnp.where(is_q, rot, x) * scale
        o_ref[0, k] = out.astype(BF16)


def _rope(P, pos_f32, invf):
    b, _, s, _ = P.shape
    tr = min(512, s)
    nb = ROPE_NB
    return pl.pallas_call(
        _rope_kernel,
        grid=(b, s // tr, N_NSA_BLOCKS // nb),
        in_specs=[pl.BlockSpec((1, nb, tr, LANES), lambda bi, i, j: (bi, J_BQ // nb + j, i, 0)),
                  pl.BlockSpec((1, tr, 1), lambda bi, i, j: (bi, i, 0)),
                  pl.BlockSpec((1, LANES), lambda bi, i, j: (0, 0))],
        out_specs=pl.BlockSpec((1, nb, tr, LANES), lambda bi, i, j: (bi, j, i, 0)),
        out_shape=jax.ShapeDtypeStruct((b, N_NSA_BLOCKS, s, LANES), BF16),
        scratch_shapes=[pltpu.VMEM((tr, LANES), F32), pltpu.VMEM((tr, LANES), F32)],
        compiler_params=_cparams(("parallel", "parallel", "arbitrary")),
        name="nsa_rope",
    )(P, pos_f32, invf)


def _compress_kernel(r_ref, w1_ref, w2_ref, pos_ref, o_ref):
    r = r_ref[0, 0]
    nr = r.shape[0]
    half = CMP_STRIDE * HEAD_DIM
    a = jnp.dot(r, w1_ref[0, :half, :], preferred_element_type=F32)
    bm = jnp.dot(r, w1_ref[0, half:, :], preferred_element_type=F32)
    pos8 = jnp.broadcast_to(pos_ref[0], (8, CMP_LEN * HEAD_DIM)).astype(BF16)
    pb = jnp.dot(pos8, w1_ref[0], preferred_element_type=F32)[0:1, :]
    hid = a + pltpu.roll(bm, nr - 1, axis=0) + pb
    out = jnp.dot(_gelu(hid).astype(BF16), w2_ref[0], preferred_element_type=F32)
    row = lax.broadcasted_iota(jnp.int32, out.shape, 0)
    o_ref[0, 0] = jnp.where(row < nr - 1, out, 0.0).astype(BF16)


def _compress(rows, w1, w2, pos):
    b, _, nr, width = rows.shape
    return pl.pallas_call(
        _compress_kernel,
        grid=(b, 4),
        in_specs=[pl.BlockSpec((1, 1, nr, width), lambda bi, j: (bi, j, 0, 0)),
                  pl.BlockSpec((1, CMP_LEN * HEAD_DIM, CMP_HIDDEN), lambda bi, j: (j // 2, 0, 0)),
                  pl.BlockSpec((1, CMP_HIDDEN, HEAD_DIM), lambda bi, j: (j // 2, 0, 0)),
                  pl.BlockSpec((1, 1, CMP_LEN * HEAD_DIM), lambda bi, j: (j // 2, 0, 0))],
        out_specs=pl.BlockSpec((1, 1, nr, HEAD_DIM), lambda bi, j: (bi, j, 0, 0)),
        out_shape=jax.ShapeDtypeStruct((b, 4, nr, HEAD_DIM), BF16),
        compiler_params=_cparams(("parallel", "arbitrary")),
        name="nsa_compress",
    )(rows, w1, w2, pos)


def _nsa_kernel(q_ref, kc_ref, vc_ref, ks_ref, vs_ref, kw_ref, vw_ref, gate_ref, ovt_ref, e_ref,
                o_ref, m_scr, l_scr, acc_scr, os_scr, *, s_len):
    i = pl.program_id(2)
    TQ, HP = NSA_TQ, NSA_HPG
    R = HP * TQ
    nsb = s_len // SLC_LEN
    ncp = s_len // CMP_STRIDE
    n_sel = min(SLC_TOPK, nsb)
    s0 = i * TQ
    q2 = q_ref[0].reshape(R, HEAD_DIM)

    sc = _bdot_nt(q2, kc_ref[0, 0])
    tq_r = s0 + (lax.broadcasted_iota(jnp.int32, (R, ncp), 0) & (TQ - 1))
    ncol = lax.broadcasted_iota(jnp.int32, (R, ncp), 1)
    valid = jnp.logical_and(ncol * CMP_STRIDE + (CMP_LEN - 1) <= tq_r, ncol < ncp - 1)
    scm = jnp.where(valid, sc, NEG)
    e = jnp.where(valid, jnp.exp(scm - jnp.max(scm, axis=1, keepdims=True)), 0.0)
    p_c = e / jnp.maximum(jnp.sum(e, axis=1, keepdims=True), 1e-30)
    o_c = _bdot(p_c, vc_ref[0, 0])
    psum = jnp.sum(p_c.reshape(HP, TQ, ncp), axis=0)
    imp = _fdot_nt(ovt_ref[...], psum)
    blk = lax.broadcasted_iota(jnp.int32, (nsb, TQ), 0)
    tq_l = s0 + lax.broadcasted_iota(jnp.int32, (nsb, TQ), 1)
    cur = tq_l // SLC_LEN
    forced = jnp.logical_or(blk == 0, jnp.logical_or(blk == cur, blk == cur - 1))
    imp = jnp.where(forced, BIG, imp)
    imp = jnp.where(blk * SLC_LEN <= tq_l, imp, NEG)
    rank = jnp.zeros((nsb, TQ), jnp.int32)
    for j in range(nsb):
        rj = imp[j:j + 1, :]
        beats = jnp.logical_or(rj > imp, jnp.logical_and(rj == imp, blk > j))
        rank = rank + jnp.where(beats, 1, 0)
    sel_t = jnp.where(rank < n_sel, 1.0, 0.0).astype(F32)
    if nsb < LANES:
        sel_t = jnp.concatenate([sel_t, jnp.zeros((LANES - nsb, TQ), F32)], axis=0)
    sel = jnp.transpose(sel_t).astype(BF16)

    ones_blk = jnp.ones((TQ, HEAD_DIM), BF16)
    HC = NSA_HC

    def flash(k_ref, v_ref, tk, lo, hi, mask_fn):
        m_scr[...] = jnp.full(m_scr.shape, NEG, F32)
        l_scr[...] = jnp.zeros(l_scr.shape, F32)
        acc_scr[...] = jnp.zeros(acc_scr.shape, F32)
        qpos = s0 + lax.broadcasted_iota(jnp.int32, (TQ, tk), 0)
        kofs = lax.broadcasted_iota(jnp.int32, (TQ, tk), 1)
        ones_v = jnp.concatenate([ones_blk] * (tk // TQ), axis=0)

        def body(j, carry):
            r = pl.multiple_of(j * tk, tk)
            bias = jnp.where(mask_fn(r, r + kofs, qpos), 0.0, NEG)
            kb = k_ref[0, 0, pl.ds(r, tk), :]
            vb = jnp.concatenate([v_ref[0, 0, pl.ds(r, tk), :], ones_v], axis=1)
            for c in range(HP // HC):
                rows = slice(c * HC * TQ, (c + 1) * HC * TQ)
                s = _bdot_nt(q_ref[0, c * HC:(c + 1) * HC].reshape(HC * TQ, HEAD_DIM), kb)
                s = (s.reshape(HC, TQ, tk) + bias[None]).reshape(HC * TQ, tk)
                m_prev = m_scr[rows, :]
                m_new = jnp.maximum(m_prev, jnp.max(s, axis=1, keepdims=True))
                alpha = jnp.exp(m_prev - m_new)
                p = jnp.exp(s - jnp.concatenate([m_new] * (tk // LANES), axis=1)).astype(BF16)
                pv = jnp.dot(p, vb, preferred_element_type=F32)
                l_scr[rows, :] = alpha * l_scr[rows, :] + pv[:, HEAD_DIM:]
                acc_scr[rows, :] = alpha * acc_scr[rows, :] + pv[:, :HEAD_DIM]
                m_scr[rows, :] = m_new
            return carry

        lax.fori_loop(lo, hi, body, 0)

    tks = min(NSA_TK_SEL, s_len)

    def sel_mask(r, kpos, qpos):
        chosen = jnp.dot(sel, e_ref[:, pl.ds(r, tks)], preferred_element_type=F32)
        return jnp.logical_and(chosen > 0.5, kpos <= qpos)

    flash(ks_ref, vs_ref, tks, 0, (s0 + TQ - 1) // tks + 1, sel_mask)
    os_scr[...] = acc_scr[...] / l_scr[...]

    def win_mask(r, kpos, qpos):
        return jnp.logical_and(kpos <= qpos, kpos > qpos - WIN)

    tkw = min(NSA_TK_WIN, s_len)
    flash(kw_ref, vw_ref, tkw, jnp.maximum(s0 - WIN + 1, 0) // tkw, (s0 + TQ - 1) // tkw + 1, win_mask)

    gates = _sigmoid(gate_ref[0, 0])
    for hh in range(HP):
        rows = slice(hh * TQ, (hh + 1) * TQ)
        o_w = acc_scr[rows, :] / l_scr[rows, :]
        out = (gates[:, 3 * hh:3 * hh + 1] * o_c[rows, :]
               + gates[:, 3 * hh + 1:3 * hh + 2] * os_scr[rows, :]
               + gates[:, 3 * hh + 2:3 * hh + 3] * o_w)
        o_ref[0, :, hh * HEAD_DIM:(hh + 1) * HEAD_DIM] = out.astype(BF16)


def _nsa_attend(rp, cmp_kv, P, ovt, expand):
    b, _, s, _ = rp.shape
    TQ, HP, G = NSA_TQ, NSA_HPG, NSA_GROUPS
    ncp = cmp_kv.shape[2]
    R = HP * TQ
    full = lambda j0: pl.BlockSpec((1, 1, s, HEAD_DIM), lambda bi, g, i: (bi, j0 + g, 0, 0))
    cmp_spec = lambda j0: pl.BlockSpec((1, 1, ncp, HEAD_DIM), lambda bi, g, i: (bi, j0 + g, 0, 0))
    return pl.pallas_call(
        functools.partial(_nsa_kernel, s_len=s),
        grid=(b, G, s // TQ),
        in_specs=[pl.BlockSpec((1, HP, TQ, HEAD_DIM), lambda bi, g, i: (bi, g, i, 0)),
                  cmp_spec(0), cmp_spec(2),
                  full(R_KS), full(R_VS), full(R_KW), full(R_VW),
                  pl.BlockSpec((1, 1, TQ, LANES), lambda bi, g, i: (bi, J_GATE + g, i, 0)),
                  pl.BlockSpec(ovt.shape, lambda bi, g, i: (0, 0)),
                  pl.BlockSpec(expand.shape, lambda bi, g, i: (0, 0))],
        out_specs=pl.BlockSpec((1, TQ, HP * HEAD_DIM), lambda bi, g, i: (bi, i, g)),
        out_shape=jax.ShapeDtypeStruct((b, s, NSA_HEADS * HEAD_DIM), BF16),
        scratch_shapes=[pltpu.VMEM((R, LANES), F32), pltpu.VMEM((R, LANES), F32),
                        pltpu.VMEM((R, HEAD_DIM), F32), pltpu.VMEM((R, HEAD_DIM), F32)],
        compiler_params=_cparams(("parallel", "parallel", "arbitrary")),
        name="nsa_attend",
    )(rp, cmp_kv, cmp_kv, rp, rp, rp, rp, P, ovt, expand)


def _nsa(P, positions, cmp_pos_k, cmp_w1_k, cmp_w2_k, cmp_pos_v, cmp_w1_v, cmp_w2_v):
    b, _, s, _ = P.shape
    lanes = np.arange(LANES)
    invf = np.where(lanes < ROPE_DIM, ROPE_THETA ** (-(lanes % ROPE_HALF) / ROPE_HALF), 0.0)
    rp = _rope(P, positions.astype(F32).reshape(b, s, 1), jnp.asarray(invf, F32).reshape(1, LANES))

    ncp = s // CMP_STRIDE
    rows = rp[:, R_KC:R_KC + 4].reshape(b, 4, ncp, CMP_STRIDE * HEAD_DIM)
    w1 = jnp.stack([cmp_w1_k, cmp_w1_v]).astype(BF16)
    w2 = jnp.stack([cmp_w2_k, cmp_w2_v]).astype(BF16)
    pos = jnp.stack([cmp_pos_k, cmp_pos_v]).reshape(2, 1, CMP_LEN * HEAD_DIM)
    cmp_kv = _compress(rows, w1, w2, pos)

    nsb = s // SLC_LEN
    cmp_start = np.arange(ncp) * CMP_STRIDE
    slc_start = np.arange(nsb) * SLC_LEN
    ov = ((cmp_start[:, None] < slc_start[None, :] + SLC_LEN)
          & (cmp_start[:, None] + CMP_LEN > slc_start[None, :])
          & (np.arange(ncp)[:, None] < ncp - 1))
    ovt = jnp.asarray(ov.T.astype(np.float32))
    expand = (np.arange(LANES)[:, None] == (np.arange(s)[None, :] // SLC_LEN))
    expand = jnp.asarray(expand.astype(np.float32), BF16)
    return _nsa_attend(rp, cmp_kv, P, ovt, expand)


MERGE_TN = 512


def _merge_kernel(oa_ref, ob_ref, wg_ref, wn_ref, ma_ref, mb_ref, y_ref):
    ya = jnp.dot(oa_ref[0], wg_ref[...], preferred_element_type=F32)
    yb = jnp.dot(ob_ref[0], wn_ref[...], preferred_element_type=F32)
    for k in range(MERGE_TN // LANES):
        cols = slice(k * LANES, (k + 1) * LANES)
        y = _sigmoid(ma_ref[0, k]) * ya[:, cols] + _sigmoid(mb_ref[0, k]) * yb[:, cols]
        y_ref[0, :, cols] = y.astype(BF16)


def _merge(o_a, o_b, wg, wn, P):
    b, s, da = o_a.shape
    db = o_b.shape[2]
    d = wg.shape[1]
    tm, tn = min(512, s), MERGE_TN
    nk = tn // LANES
    return pl.pallas_call(
        _merge_kernel,
        grid=(b, s // tm, d // tn),
        in_specs=[pl.BlockSpec((1, tm, da), lambda bi, i, j: (bi, i, 0)),
                  pl.BlockSpec((1, tm, db), lambda bi, i, j: (bi, i, 0)),
                  pl.BlockSpec((da, tn), lambda bi, i, j: (0, j)),
                  pl.BlockSpec((db, tn), lambda bi, i, j: (0, j)),
                  pl.BlockSpec((1, nk, tm, LANES), lambda bi, i, j: (bi, J_MA // nk + j, i, 0)),
                  pl.BlockSpec((1, nk, tm, LANES), lambda bi, i, j: (bi, J_MB // nk + j, i, 0))],
        out_specs=pl.BlockSpec((1, tm, tn), lambda bi, i, j: (bi, i, j)),
        out_shape=jax.ShapeDtypeStruct((b, s, d), BF16),
        compiler_params=_cparams(("parallel", "parallel", "arbitrary")),
        name="merge",
    )(o_a, o_b, wg, wn, P, P)


def _outproj_kernel(y_ref, w_ref, x_ref, g_ref, o_ref):
    o_ref[0] = x_ref[0] + g_ref[0] * jnp.dot(y_ref[0], w_ref[...], preferred_element_type=F32)


def _outproj(y, w, x, g):
    b, s, d = x.shape
    tm, tn = min(512, s), 512
    return pl.pallas_call(
        _outproj_kernel,
        grid=(b, s // tm, d // tn),
        in_specs=[pl.BlockSpec((1, tm, d), lambda bi, i, j: (bi, i, 0)),
                  pl.BlockSpec((d, tn), lambda bi, i, j: (0, j)),
                  pl.BlockSpec((1, tm, tn), lambda bi, i, j: (bi, i, j)),
                  pl.BlockSpec((1, 1, tn), lambda bi, i, j: (bi, 0, j))],
        out_specs=pl.BlockSpec((1, tm, tn), lambda bi, i, j: (bi, i, j)),
        out_shape=jax.ShapeDtypeStruct((b, s, d), F32),
        compiler_params=_cparams(("parallel", "parallel", "arbitrary")),
        name="outproj",
    )(y, w, x, g.reshape(b, 1, d))


PEER_TB = 512
PEER_TG = 64
PEER_AHEAD = 28
PEER_SLOTS = 32
PEER_SEL = PEER_HEADS * PEER_TOPK
MIN_F32 = -3.0e38


def _topk_rows(vals, payload=None):
    nrow = vals.shape[0]
    rid = lax.broadcasted_iota(jnp.int32, vals.shape, 0)
    out_v, out_i = [], []
    for _ in range(PEER_TOPK):
        m = jnp.max(vals, axis=0, keepdims=True)
        idx = jnp.min(jnp.where(vals == m, rid, nrow), axis=0, keepdims=True)
        hit = rid == idx
        out_v.append(m)
        if payload is None:
            out_i.append(idx)
        else:
            out_i.append(jnp.sum(jnp.where(hit, payload, 0), axis=0, keepdims=True))
        vals = jnp.where(hit, MIN_F32, vals)
    return jnp.concatenate(out_v, axis=0), jnp.concatenate(out_i, axis=0)


def _peer_topk_kernel(qh_ref, k1_ref, k2_ref, eidx_ref, gw_ref):
    def head(h, carry):
        s1 = _fdot_nt(k1_ref[h], qh_ref[0, 2 * h])
        s2 = _fdot_nt(k2_ref[h], qh_ref[0, 2 * h + 1])
        v1, i1 = _topk_rows(s1)
        v2, i2 = _topk_rows(s2)
        keep = [PEER_TOPK // (a + 1) for a in range(PEER_TOPK)]
        npad = (-sum(keep)) % 8
        cand = jnp.concatenate([v1[a:a + 1, :] + v2[0:keep[a], :] for a in range(PEER_TOPK)]
                               + [jnp.full((npad, v1.shape[1]), MIN_F32, F32)], axis=0)
        cidx = jnp.concatenate([i1[a:a + 1, :] * PEER_NKEYS + i2[0:keep[a], :] for a in range(PEER_TOPK)]
                               + [jnp.zeros((npad, v1.shape[1]), jnp.int32)], axis=0)
        sc, eidx = _topk_rows(cand, cidx)
        ex = jnp.exp(sc - sc[0:1, :])
        eidx_ref[h] = eidx
        gw_ref[h] = ex / jnp.sum(ex, axis=0, keepdims=True)
        return carry

    lax.fori_loop(0, PEER_HEADS, head, 0)


def _peer_topk(qh, keys1, keys2):
    b, nb, s, _ = qh.shape
    tb = min(PEER_TB, s)
    nt = s // tb
    kspec = pl.BlockSpec(keys1.shape, lambda bi, i: (0, 0, 0))
    ospec = pl.BlockSpec((PEER_HEADS, PEER_TOPK, tb), lambda bi, i: (0, 0, bi * nt + i))
    return pl.pallas_call(
        _peer_topk_kernel,
        grid=(b, nt),
        in_specs=[pl.BlockSpec((1, nb, tb, LANES), lambda bi, i: (bi, 0, i, 0)), kspec, kspec],
        out_specs=[ospec, ospec],
        out_shape=[jax.ShapeDtypeStruct((PEER_HEADS, PEER_TOPK, b * s), jnp.int32),
                   jax.ShapeDtypeStruct((PEER_HEADS, PEER_TOPK, b * s), F32)],
        compiler_params=_cparams(("parallel", "parallel")),
        name="peer_topk",
    )(qh, keys1, keys2)


def _peer_gather_kernel(idx_ref, idxn_ref, gw_ref, x_ref, uv_hbm, o_ref, buf, a_scr, sem, *, tg):
    NS, D, NSLOT = PEER_SEL, PEER_AHEAD, PEER_SLOTS
    RPE = uv_hbm.shape[1]
    SUBW = RPE // 2
    half = SUBW * LANES

    def wait_rows(t):
        slot = t & (NSLOT - 1)
        pltpu.make_async_copy(uv_hbm.at[pl.ds(0, NS)], buf.at[slot], sem.at[slot]).wait()

    rid = lax.broadcasted_iota(jnp.int32, (NS, NS), 0)
    cid = lax.broadcasted_iota(jnp.int32, (NS, NS), 1)
    eye = rid == cid
    ones = jnp.ones((LANES, LANES), BF16)

    def lane_sum_rep(m):
        hi, lo = _split_bf16(m)
        return (jnp.dot(hi, ones, preferred_element_type=F32)
                + jnp.dot(lo, ones, preferred_element_type=F32))

    def halves(slot, tab, sg):
        w = buf[slot, tab * SUBW + sg]
        return pltpu.bitcast(w << 16, F32), pltpu.bitcast(w & jnp.uint32(0xFFFF0000), F32)

    def phase(t, issue_from, do_dot, do_prev):
        if do_dot:
            wait_rows(t)
            xrow = x_ref[pl.ds(t, 1), :]
            acc = jnp.zeros((NS, LANES), F32)
        if do_prev:
            act = lane_sum_rep(a_scr[(t - 1) & 1])
            gw_rep = lane_sum_rep(jnp.where(eye, gw_ref[pl.ds(t - 1, 1), :], 0.0))
            coef = gw_rep * _gelu(act)
            out_lo, out_hi = [], []
        if issue_from is not None:
            slot_i = (t + D) & (NSLOT - 1)
            base_i = (t + D) * NS if issue_from is idx_ref else (t + D - tg) * NS
        for sg in range(SUBW):
            if issue_from is not None:
                for j in range(sg * NS // SUBW, (sg + 1) * NS // SUBW):
                    pltpu.make_async_copy(uv_hbm.at[issue_from[base_i + j]],
                                          buf.at[slot_i, :, j, :], sem.at[slot_i]).start(priority=j % 2)
            if do_dot:
                lo, hi = halves(t & (NSLOT - 1), 0, sg)
                acc = acc + lo * xrow[:, sg * LANES:(sg + 1) * LANES]
                acc = acc + hi * xrow[:, half + sg * LANES:half + (sg + 1) * LANES]
            if do_prev:
                lo, hi = halves((t - 1) & (NSLOT - 1), 1, sg)
                out_lo.append(jnp.sum(lo * coef, axis=0, keepdims=True))
                out_hi.append(jnp.sum(hi * coef, axis=0, keepdims=True))
        if do_dot:
            a_scr[t & 1] = acc
        if do_prev:
            o_ref[pl.ds(t - 1, 1), :] = jnp.concatenate(out_lo + out_hi, axis=1)

    def ramp(t, carry):
        slot = t & (NSLOT - 1)

        def body(j, c):
            pltpu.make_async_copy(uv_hbm.at[idx_ref[t * NS + j]], buf.at[slot, :, j, :], sem.at[slot]).start()
            return c
        return lax.fori_loop(0, NS, body, carry)

    def looped(issue_from):
        def f(t, carry):
            phase(t, issue_from, True, True)
            return carry
        return f

    step = pl.program_id(0)
    last = pl.num_programs(0) - 1

    @pl.when(step == 0)
    def _():
        lax.fori_loop(0, D, ramp, 0)

    phase(0, idx_ref, True, False)
    lax.fori_loop(1, tg - D, looped(idx_ref), 0)

    @pl.when(step < last)
    def _():
        lax.fori_loop(tg - D, tg, looped(idxn_ref), 0)

    @pl.when(step == last)
    def _():
        lax.fori_loop(tg - D, tg, looped(None), 0)

    phase(tg, None, False, True)


def _peer_gather(eidx, gw, h2r, uvc):
    t = eidx.shape[0]
    tg = min(PEER_TG, t)
    d = h2r.shape[1]
    rpe = uvc.shape[1]
    return pl.pallas_call(
        functools.partial(_peer_gather_kernel, tg=tg),
        grid=(t // tg,),
        in_specs=[pl.BlockSpec((tg * PEER_SEL,), lambda i: (i,), memory_space=pltpu.SMEM),
                  pl.BlockSpec((tg * PEER_SEL,), lambda i: (jnp.minimum(i + 1, t // tg - 1),),
                               memory_space=pltpu.SMEM),
                  pl.BlockSpec((tg, PEER_SEL), lambda i: (i, 0)),
                  pl.BlockSpec((tg, d), lambda i: (i, 0)),
                  pl.BlockSpec(memory_space=pl.ANY)],
        out_specs=pl.BlockSpec((tg, d), lambda i: (i, 0)),
        out_shape=jax.ShapeDtypeStruct((t, d), F32),
        scratch_shapes=[pltpu.VMEM((PEER_SLOTS, rpe, PEER_SEL, LANES), jnp.uint32),
                        pltpu.VMEM((2, PEER_SEL, LANES), F32),
                        pltpu.SemaphoreType.DMA((PEER_SLOTS,))],
        compiler_params=_cparams(("arbitrary",)),
        name="peer_gather",
    )(eidx.reshape(-1), eidx.reshape(-1), gw, h2r, uvc)


PACK_EB = 256


def _pack_kernel(u_ref, v_ref, o_ref):
    half = u_ref.shape[1] // 2
    for tab, ref in enumerate((u_ref, v_ref)):
        for sg in range(half // LANES):
            lo = ref[:, sg * LANES:(sg + 1) * LANES].astype(BF16).astype(F32)
            hi = ref[:, half + sg * LANES:half + (sg + 1) * LANES].astype(BF16).astype(F32)
            word = (pltpu.bitcast(lo, jnp.uint32) >> 16) | (pltpu.bitcast(hi, jnp.uint32) & jnp.uint32(0xFFFF0000))
            o_ref[:, tab * (half // LANES) + sg, :] = word


def _pack_tables(u, v):
    e, d = u.shape
    eb = min(PACK_EB, e)
    return pl.pallas_call(
        _pack_kernel,
        grid=(e // eb,),
        in_specs=[pl.BlockSpec((eb, d), lambda i: (i, 0)), pl.BlockSpec((eb, d), lambda i: (i, 0))],
        out_specs=pl.BlockSpec((eb, d // LANES, LANES), lambda i: (i, 0, 0)),
        out_shape=jax.ShapeDtypeStruct((e, d // LANES, LANES), jnp.uint32),
        compiler_params=_cparams(("parallel",)),
        name="peer_pack",
    )(u, v)


def _peer(x1, norm_w, sc, sh, wq, keys1, keys2, u, v):
    b, s, d = x1.shape
    qh, h2 = _normproj(x1, norm_w, sc, sh, wq.astype(BF16), tm=512, tn=512, emit_h=True)
    eidx, gw = _peer_topk(qh, keys1, keys2)
    eidx = jnp.transpose(eidx, (2, 0, 1)).reshape(b * s, PEER_SEL)
    gw = jnp.transpose(gw, (2, 0, 1)).reshape(b * s, PEER_SEL)
    uvc = _pack_tables(u, v)
    out = _peer_gather(eidx, gw, h2.reshape(b * s, d), uvc)
    return out.reshape(b, s, d)


def _final_kernel(x_ref, p_ref, g_ref, w_ref, o_ref):
    x = x_ref[0] + g_ref[0] * p_ref[0]
    o_ref[0] = x * lax.rsqrt(jnp.mean(x * x, axis=-1, keepdims=True) + EPS) * w_ref[...]


def _final(x1, peer, g2, wf):
    b, s, d = x1.shape
    tm = min(512, s)
    blk = pl.BlockSpec((1, tm, d), lambda bi, i: (bi, i, 0))
    return pl.pallas_call(
        _final_kernel,
        grid=(b, s // tm),
        in_specs=[blk, blk, pl.BlockSpec((1, 1, d), lambda bi, i: (bi, 0, 0)),
                  pl.BlockSpec((1, d), lambda bi, i: (0, 0))],
        out_specs=blk,
        out_shape=jax.ShapeDtypeStruct((b, s, d), F32),
        compiler_params=_cparams(("parallel", "parallel")),
        name="final_norm",
    )(x1, peer, g2.reshape(b, 1, d), wf.reshape(1, d))


def _pad_rows(a, mult=8):
    pad = (-a.shape[0]) % mult
    return jnp.pad(a, ((0, pad), (0, 0)))


def _permute_w_in(w):
    o_aa, o_bq, o_bg, o_ma, o_end = 4096, 4112, 7696, 7744, 11840
    d = w.shape[0]
    z = lambda n: jnp.zeros((d, n), w.dtype)
    hg = 3 * NSA_HPG
    cols = [w[:, 0:o_aa], w[:, o_bq:o_bg], w[:, o_ma:o_end],
            w[:, o_aa:o_bq], z(LANES - 16),
            w[:, o_bg:o_bg + hg], z(LANES - hg),
            w[:, o_bg + hg:o_ma], z(LANES - hg),
            z(LANES)]
    return jnp.concatenate(cols, axis=1).astype(BF16)


def kernel(x, c, positions, ada_w, ada_b, norm1_w, norm2_w, w_in, gdn_conv_w, gdn_A_log, gdn_dt_bias, gdn_norm_w, cmp_pos_k, cmp_w1_k, cmp_w2_k, cmp_pos_v, cmp_w1_v, cmp_w2_v, w_branch_gdn, w_branch_nsa, w_out, peer_wq, peer_keys1, peer_keys2, peer_u, peer_v, final_norm_w):
    b, s, d = x.shape
    l = 0
    mod = _ada(_pad_rows(c), ada_w[l], ada_b[l])[:b]
    sh1, sc1, g1, sh2, sc2, g2 = jnp.split(mod, 6, axis=-1)
    P = _normproj(x, norm1_w[l], sc1, sh1, _permute_w_in(w_in[l]), tm=1024, tn=1024, emit_h=False)
    o_a = _gdn(P, gdn_conv_w[l], gdn_A_log[l], gdn_dt_bias[l], gdn_norm_w[l])
    o_b = _nsa(P, positions, cmp_pos_k[l], cmp_w1_k[l], cmp_w2_k[l], cmp_pos_v[l], cmp_w1_v[l], cmp_w2_v[l])
    y = _merge(o_a, o_b, w_branch_gdn[l].astype(BF16), w_branch_nsa[l].astype(BF16), P)
    x1 = _outproj(y, w_out[l].astype(BF16), x, g1)
    peer = _peer(x1, norm2_w[l], sc2, sh2, peer_wq[l], peer_keys1[l], peer_keys2[l], peer_u[l], peer_v[l])
    return _final(x1, peer, g2, final_norm_w)
```

```python
import functools
import math

import numpy as np
import jax
import jax.numpy as jnp
from jax import lax
from jax.experimental import pallas as pl
from jax.experimental.pallas import tpu as pltpu

F32 = jnp.float32
BF16 = jnp.bfloat16
HI = lax.Precision.HIGHEST

LANES = 128
VMEM_LIMIT = 56 * 1024 * 1024

EPS = 1e-6
ROPE_THETA = 500000.0
HEAD_DIM = 128
ROPE_DIM = HEAD_DIM // 4
ROPE_HALF = ROPE_DIM // 2

GDN_HEADS = 8
GDN_CONV = 4
GDN_CHUNK = 64

NSA_HEADS = 16
NSA_GROUPS = 2
NSA_HPG = NSA_HEADS // NSA_GROUPS
CMP_LEN = 32
CMP_STRIDE = 16
CMP_HIDDEN = 256
SLC_LEN = 64
SLC_TOPK = 16
WIN = 512

PEER_HEADS = 8
PEER_NKEYS = 128
PEER_QDIM = 256
PEER_TOPK = 16

NEG = -1e30
BIG = 1e9

J_AQ, J_AK, J_AV, J_AZ = 0, 8, 16, 24
J_BQ = 32
N_NSA_BLOCKS = 28
J_MA, J_MB = 60, 76
J_SMALL = 92
J_GATE = 93
NJ = 96


def _cparams(sem):
    return pltpu.CompilerParams(dimension_semantics=sem, vmem_limit_bytes=VMEM_LIMIT)


def _bdot(a, b):
    return jnp.dot(a.astype(BF16), b.astype(BF16), preferred_element_type=F32)


def _bdot_nt(a, b):
    return lax.dot_general(a.astype(BF16), b.astype(BF16), (((1,), (1,)), ((), ())),
                           preferred_element_type=F32)


def _split_bf16(a):
    hi = a.astype(BF16)
    return hi, (a - hi.astype(F32)).astype(BF16)


def _dot3(a, b, exact_a=False):
    bh, bl = _split_bf16(b)
    dot = functools.partial(jnp.dot, preferred_element_type=F32)
    if exact_a:
        return dot(a, bh) + dot(a, bl)
    ah, al = _split_bf16(a)
    return dot(ah, bh) + dot(al, bh) + dot(ah, bl)


def _dot3_nt(a, b):
    ah, al = _split_bf16(a)
    bh, bl = _split_bf16(b)
    dot = functools.partial(lax.dot_general, dimension_numbers=(((1,), (1,)), ((), ())),
                            preferred_element_type=F32)
    return dot(ah, bh) + dot(al, bh) + dot(ah, bl)


def _fdot(a, b):
    return jnp.dot(a, b, precision=HI, preferred_element_type=F32)


def _fdot_nt(a, b):
    return lax.dot_general(a, b, (((1,), (1,)), ((), ())), precision=HI,
                           preferred_element_type=F32)


def _sigmoid(x):
    return 1.0 / (1.0 + jnp.exp(-x))


def _silu(x):
    return x * _sigmoid(x)


def _gelu(x):
    return 0.5 * x * (1.0 + jnp.tanh(math.sqrt(2.0 / math.pi) * (x + 0.044715 * (x * x * x))))


def _softplus(x):
    return jnp.maximum(x, 0.0) + jnp.log(1.0 + jnp.exp(-jnp.abs(x)))


def _ada_kernel(c_ref, w_ref, b_ref, o_ref):
    o_ref[...] = _fdot(_silu(c_ref[...]), w_ref[...]) + b_ref[...]


def _ada(c_pad, ada_w, ada_b):
    m, d = c_pad.shape
    n = ada_w.shape[1]
    tn = 1024
    return pl.pallas_call(
        _ada_kernel,
        grid=(n // tn,),
        in_specs=[pl.BlockSpec((m, d), lambda j: (0, 0)),
                  pl.BlockSpec((d, tn), lambda j: (0, j)),
                  pl.BlockSpec((1, tn), lambda j: (0, j))],
        out_specs=pl.BlockSpec((m, tn), lambda j: (0, j)),
        out_shape=jax.ShapeDtypeStruct((m, n), F32),
        compiler_params=_cparams(("parallel",)),
        name="ada_mod",
    )(c_pad, ada_w, ada_b.reshape(1, n))


def _normproj_kernel(x_ref, nw_ref, sc_ref, sh_ref, w_ref, *rest, nk, emit_h):
    if emit_h:
        o_ref, h_out_ref, h_scr = rest
    else:
        o_ref, h_scr = rest

    @pl.when(pl.program_id(2) == 0)
    def _():
        x = x_ref[0]
        ms = jnp.mean(x * x, axis=-1, keepdims=True)
        y = x * lax.rsqrt(ms + EPS) * nw_ref[...]
        h = y * (1.0 + sc_ref[0]) + sh_ref[0]
        h_scr[...] = h.astype(BF16)
        if emit_h:
            h_out_ref[0] = h

    acc = jnp.dot(h_scr[...], w_ref[...], preferred_element_type=F32)
    for k in range(nk):
        o_ref[0, k] = acc[:, k * LANES:(k + 1) * LANES]


def _normproj(x, nw, sc, sh, w_bf16, *, tm, tn, emit_h):
    b, s, d = x.shape
    n = w_bf16.shape[1]
    tm = min(tm, s)
    nk = tn // LANES
    out_shape = [jax.ShapeDtypeStruct((b, n // LANES, s, LANES), F32)]
    out_specs = [pl.BlockSpec((1, nk, tm, LANES), lambda bi, i, j: (bi, j, i, 0))]
    if emit_h:
        out_shape.append(jax.ShapeDtypeStruct((b, s, d), F32))
        out_specs.append(pl.BlockSpec((1, tm, d), lambda bi, i, j: (bi, i, 0)))
    res = pl.pallas_call(
        functools.partial(_normproj_kernel, nk=nk, emit_h=emit_h),
        grid=(b, s // tm, n // tn),
        in_specs=[pl.BlockSpec((1, tm, d), lambda bi, i, j: (bi, i, 0)),
                  pl.BlockSpec((1, d), lambda bi, i, j: (0, 0)),
                  pl.BlockSpec((1, 1, d), lambda bi, i, j: (bi, 0, 0)),
                  pl.BlockSpec((1, 1, d), lambda bi, i, j: (bi, 0, 0)),
                  pl.BlockSpec((d, tn), lambda bi, i, j: (0, j))],
        out_specs=out_specs,
        out_shape=out_shape,
        scratch_shapes=[pltpu.VMEM((tm, d), BF16)],
        compiler_params=_cparams(("parallel", "parallel", "arbitrary")),
        name="normproj_h" if emit_h else "normproj",
    )(x, nw.reshape(1, d), sc.reshape(b, 1, d), sh.reshape(b, 1, d), w_bf16)
    return res if emit_h else res[0]


GDN_CB = 1024
GDN_GB = 256


def _gdn_local_kernel(q_ref, qh_ref, k_ref, kh_ref, v_ref, vh_ref, sm_ref,
                      cwq_ref, cwk_ref, cwv_ref, alog_ref, dtb_ref,
                      w_ref, u0_ref, qk_ref, qg_ref, kd_ref, ge_ref, *, cb):
    h = pl.program_id(1)
    i = pl.program_id(2)
    C, G = GDN_CHUNK, GDN_GB
    first = (i == 0)

    def conv_act(main_ref, halo_ref, cw_ref, r0):
        if r0 == 0:
            prev = jnp.where(first, 0.0, halo_ref[0, 0])
        else:
            prev = main_ref[0, 0, r0 - 8:r0, :]
        ext = jnp.concatenate([prev, main_ref[0, 0, r0:r0 + G, :]], axis=0)
        w = cw_ref[0]
        y = w[0:1, :] * ext[5:5 + G, :]
        for j in range(1, GDN_CONV):
            y = y + w[j:j + 1, :] * ext[5 + j:5 + j + G, :]
        return _silu(y)

    rid = lax.broadcasted_iota(jnp.int32, (G, G), 0)
    cid = lax.broadcasted_iota(jnp.int32, (G, G), 1)
    same = (rid // C) == (cid // C)
    incl = jnp.logical_and(same, rid >= cid)
    strict = jnp.logical_and(same, rid > cid)
    eye = rid == cid
    is_last = cid == (rid // C) * C + (C - 1)
    joins = []
    bs = 1
    while bs < C:
        joins.append(jnp.logical_and(jnp.logical_and((rid // (2 * bs)) == (cid // (2 * bs)),
                                                     (rid & (2 * bs - 1)) >= bs),
                                     (cid & (2 * bs - 1)) < bs))
        bs *= 2
    tri = jnp.where(incl, 1.0, 0.0).astype(BF16)
    eye_f = jnp.where(eye, 1.0, 0.0).astype(F32)
    lane = lax.broadcasted_iota(jnp.int32, (G, LANES), 1)
    neg_a = -jnp.exp(alog_ref[...])
    dtb = dtb_ref[...]

    for grp in range(cb // G):
        r0 = grp * G
        q = conv_act(q_ref, qh_ref, cwq_ref, r0)
        k = conv_act(k_ref, kh_ref, cwk_ref, r0)
        v = conv_act(v_ref, vh_ref, cwv_ref, r0)
        q = q * lax.rsqrt(jnp.sum(q * q, axis=-1, keepdims=True) + EPS) * (HEAD_DIM ** -0.5)
        k = k * lax.rsqrt(jnp.sum(k * k, axis=-1, keepdims=True) + EPS)

        sm = sm_ref[0, 0, r0:r0 + G, :]
        g_all = neg_a * _softplus(sm + dtb)
        gc_all = _dot3(tri, g_all, exact_a=True)
        gc = jnp.sum(jnp.where(lane == h, gc_all, 0.0), axis=1, keepdims=True)
        beta = jnp.sum(jnp.where(lane == GDN_HEADS + h, _sigmoid(sm), 0.0), axis=1, keepdims=True)
        gc_row = jnp.sum(jnp.where(eye, gc, 0.0), axis=0, keepdims=True)
        gc_last = jnp.sum(jnp.where(is_last, gc_row, 0.0), axis=1, keepdims=True)
        decay = jnp.where(incl, jnp.exp(jnp.where(incl, gc - gc_row, 0.0)), 0.0)
        gamma = jnp.exp(gc)

        kk = _dot3_nt(k, k)
        lmat = jnp.where(strict, decay * kk, 0.0) * beta
        tinv = eye_f - jnp.where(joins[0], lmat, 0.0)
        for lvl in range(1, len(joins)):
            tinv = tinv - _bdot(_bdot(tinv, jnp.where(joins[lvl], lmat, 0.0)), tinv)
        resid = (eye_f - tinv) - _dot3(lmat, tinv)
        tinv = tinv + _bdot(tinv, resid)
        rhs = jnp.concatenate([(beta * gamma) * k, beta * v], axis=1)
        wu = _dot3(tinv, rhs)
        qk = decay * _bdot_nt(q, k)

        rows = slice(r0, r0 + G)
        w_ref[0, 0, rows, :] = wu[:, :HEAD_DIM].astype(BF16)
        u0_ref[0, 0, rows, :] = wu[:, HEAD_DIM:]
        qg_ref[0, 0, rows, :] = (gamma * q).astype(BF16)
        kd_ref[0, 0, rows, :] = (jnp.exp(gc_last - gc) * k).astype(BF16)
        ge_all = jnp.broadcast_to(jnp.exp(gc_last), (G, LANES))
        for c in range(G // C):
            cr = slice(c * C, (c + 1) * C)
            qk_ref[0, 0, r0 + c * C:r0 + (c + 1) * C, :] = qk[cr, cr].astype(BF16)
            ge_ref[0, 0, pl.ds((i * (cb // G) + grp) * (G // C) + c, 1), :] = ge_all[c * C:c * C + 1, :]


def _gdn_local(P, cw, alog_pad, dtb_pad):
    b, _, s, _ = P.shape
    H, CB, C = GDN_HEADS, min(GDN_CB, s), GDN_CHUNK
    n = s // C

    def main(j0):
        return pl.BlockSpec((1, 1, CB, LANES), lambda bi, h, i: (bi, j0 + h, i, 0))

    def halo(j0):
        return pl.BlockSpec((1, 1, 8, LANES),
                            lambda bi, h, i: (bi, j0 + h, jnp.maximum(i * (CB // 8) - 1, 0), 0))

    def cws(j0):
        return pl.BlockSpec((1, GDN_CONV, LANES), lambda bi, h, i: (j0 + h, 0, 0))

    row = pl.BlockSpec((1, LANES), lambda bi, h, i: (0, 0))
    hs = lambda width: pl.BlockSpec((1, 1, CB, width), lambda bi, h, i: (bi, h, i, 0))
    return pl.pallas_call(
        functools.partial(_gdn_local_kernel, cb=CB),
        grid=(b, H, s // CB),
        in_specs=[main(J_AQ), halo(J_AQ), main(J_AK), halo(J_AK), main(J_AV), halo(J_AV),
                  pl.BlockSpec((1, 1, CB, LANES), lambda bi, h, i: (bi, J_SMALL, i, 0)),
                  cws(0), cws(8), cws(16), row, row],
        out_specs=[hs(LANES), hs(LANES), hs(C), hs(LANES), hs(LANES),
                   pl.BlockSpec((1, 1, n, LANES), lambda bi, h, i: (bi, h, 0, 0))],
        out_shape=[jax.ShapeDtypeStruct((b, H, s, LANES), BF16),
                   jax.ShapeDtypeStruct((b, H, s, LANES), F32),
                   jax.ShapeDtypeStruct((b, H, s, C), BF16),
                   jax.ShapeDtypeStruct((b, H, s, LANES), BF16),
                   jax.ShapeDtypeStruct((b, H, s, LANES), BF16),
                   jax.ShapeDtypeStruct((b, H, n, LANES), F32)],
        compiler_params=_cparams(("parallel", "parallel", "arbitrary")),
        name="gdn_local",
    )(P, P, P, P, P, P, P, cw, cw, cw, alog_pad, dtb_pad)


GDN_HB = 4


def _gdn_scan_kernel(w_ref, u0_ref, qk_ref, qg_ref, kd_ref, ge_ref, z_ref, nw_ref, o_ref, s_scr,
                     *, sb):
    C = GDN_CHUNK

    @pl.when(pl.program_id(2) == 0)
    def _():
        s_scr[...] = jnp.zeros_like(s_scr)

    nw = nw_ref[...]

    def body(n, carry):
        r = pl.multiple_of(n * C, C)
        for hh in range(GDN_HB):
            st = s_scr[hh]
            stb = st.astype(BF16)
            u = u0_ref[0, hh, pl.ds(r, C), :] - jnp.dot(w_ref[0, hh, pl.ds(r, C), :], stb,
                                                       preferred_element_type=F32)
            ub = u.astype(BF16)
            o = (jnp.dot(qg_ref[0, hh, pl.ds(r, C), :], stb, preferred_element_type=F32)
                 + jnp.dot(qk_ref[0, hh, pl.ds(r, C), :], ub, preferred_element_type=F32))
            ge = ge_ref[0, hh, pl.ds(n, 1), :]
            s_scr[hh] = ge * st + lax.dot_general(kd_ref[0, hh, pl.ds(r, C), :], ub,
                                                  (((0,), (0,)), ((), ())),
                                                  preferred_element_type=F32)
            on = o * lax.rsqrt(jnp.mean(o * o, axis=-1, keepdims=True) + EPS) * nw
            z = z_ref[0, hh, pl.ds(r, C), :]
            o_ref[0, pl.ds(r, C), hh * LANES:(hh + 1) * LANES] = (on * _silu(z)).astype(BF16)
        return carry

    lax.fori_loop(0, sb // C, body, 0)


def _gdn_scan(w, u0, qk, qg, kd, ge, P, norm_w):
    b, H, s, _ = w.shape
    C, HB = GDN_CHUNK, GDN_HB
    sb = min(1024, s)
    hs = lambda width: pl.BlockSpec((1, HB, sb, width), lambda bi, hb, i: (bi, hb, i, 0))
    return pl.pallas_call(
        functools.partial(_gdn_scan_kernel, sb=sb),
        grid=(b, H // HB, s // sb),
        in_specs=[hs(LANES), hs(LANES), hs(C), hs(LANES), hs(LANES),
                  pl.BlockSpec((1, HB, sb // C, LANES), lambda bi, hb, i: (bi, hb, i, 0)),
                  pl.BlockSpec((1, HB, sb, LANES), lambda bi, hb, i: (bi, J_AZ // HB + hb, i, 0)),
                  pl.BlockSpec((1, LANES), lambda bi, hb, i: (0, 0))],
        out_specs=pl.BlockSpec((1, sb, HB * LANES), lambda bi, hb, i: (bi, i, hb)),
        out_shape=jax.ShapeDtypeStruct((b, s, H * LANES), BF16),
        scratch_shapes=[pltpu.VMEM((HB, HEAD_DIM, HEAD_DIM), F32)],
        compiler_params=_cparams(("parallel", "parallel", "arbitrary")),
        name="gdn_scan",
    )(w, u0, qk, qg, kd, ge, P, norm_w.reshape(1, LANES))


def _pad_lanes_row(v):
    return jnp.pad(v.astype(F32), (0, LANES - v.shape[0])).reshape(1, LANES)


def _gdn(P, conv_w, a_log, dt_bias, norm_w):
    cw = jnp.transpose(conv_w.reshape(GDN_CONV, 3 * GDN_HEADS, LANES), (1, 0, 2))
    w, u0, qk, qg, kd, ge = _gdn_local(P, cw, _pad_lanes_row(a_log), _pad_lanes_row(dt_bias))
    return _gdn_scan(w, u0, qk, qg, kd, ge, P, norm_w)


NSA_TQ = 128
NSA_TK_SEL = 512
NSA_TK_WIN = 256
NSA_HC = 1
R_Q, R_KC, R_VC, R_KS, R_VS, R_KW, R_VW = 0, 16, 18, 20, 22, 24, 26
ROPE_NB = 4


def _rope_kernel(x_ref, pos_ref, invf_ref, o_ref, cos_scr, sin_scr):
    j = pl.program_id(2)
    lane = lax.broadcasted_iota(jnp.int32, cos_scr.shape, 1)

    @pl.when(j == 0)
    def _():
        ang = pos_ref[0] * invf_ref[...]
        sn = jnp.sin(ang)
        cos_scr[...] = jnp.where(lane < ROPE_DIM, jnp.cos(ang), 1.0)
        sin_scr[...] = jnp.where(lane < ROPE_HALF, -sn, jnp.where(lane < ROPE_DIM, sn, 0.0))

    is_q = j < NSA_HEADS // ROPE_NB
    scale = jnp.where(is_q, HEAD_DIM ** -0.5, 1.0)
    for k in range(ROPE_NB):
        x = x_ref[0, k]
        swapped = jnp.where(lane < ROPE_HALF, pltpu.roll(x, LANES - ROPE_HALF, axis=1),
                            pltpu.roll(x, ROPE_HALF, axis=1))
        rot = x * cos_scr[...] + swapped * sin_scr[...]
        out = rot * scale if k < 2 else jnp.where(is_q, rot, x) * scale
        o_ref[0, k] = out.astype(BF16)


def _rope(P, pos_f32, invf):
    b, _, s, _ = P.shape
    tr = min(1024, s)
    nb = ROPE_NB
    return pl.pallas_call(
        _rope_kernel,
        grid=(b, s // tr, N_NSA_BLOCKS // nb),
        in_specs=[pl.BlockSpec((1, nb, tr, LANES), lambda bi, i, j: (bi, J_BQ // nb + j, i, 0)),
                  pl.BlockSpec((1, tr, 1), lambda bi, i, j: (bi, i, 0)),
                  pl.BlockSpec((1, LANES), lambda bi, i, j: (0, 0))],
        out_specs=pl.BlockSpec((1, nb, tr, LANES), lambda bi, i, j: (bi, j, i, 0)),
        out_shape=jax.ShapeDtypeStruct((b, N_NSA_BLOCKS, s, LANES), BF16),
        scratch_shapes=[pltpu.VMEM((tr, LANES), F32), pltpu.VMEM((tr, LANES), F32)],
        compiler_params=_cparams(("parallel", "parallel", "arbitrary")),
        name="nsa_rope",
    )(P, pos_f32, invf)


def _compress_kernel(r_ref, w1_ref, w2_ref, pos_ref, o_ref):
    r = r_ref[0, 0]
    nr = r.shape[0]
    half = CMP_STRIDE * HEAD_DIM
    a = jnp.dot(r, w1_ref[0, :half, :], preferred_element_type=F32)
    bm = jnp.dot(r, w1_ref[0, half:, :], preferred_element_type=F32)
    pos8 = jnp.broadcast_to(pos_ref[0], (8, CMP_LEN * HEAD_DIM)).astype(BF16)
    pb = jnp.dot(pos8, w1_ref[0], preferred_element_type=F32)[0:1, :]
    hid = a + pltpu.roll(bm, nr - 1, axis=0) + pb
    out = jnp.dot(_gelu(hid).astype(BF16), w2_ref[0], preferred_element_type=F32)
    row = lax.broadcasted_iota(jnp.int32, out.shape, 0)
    o_ref[0, 0] = jnp.where(row < nr - 1, out, 0.0).astype(BF16)


def _compress(rows, w1, w2, pos):
    b, _, nr, width = rows.shape
    return pl.pallas_call(
        _compress_kernel,
        grid=(b, 4),
        in_specs=[pl.BlockSpec((1, 1, nr, width), lambda bi, j: (bi, j, 0, 0)),
                  pl.BlockSpec((1, CMP_LEN * HEAD_DIM, CMP_HIDDEN), lambda bi, j: (j // 2, 0, 0)),
                  pl.BlockSpec((1, CMP_HIDDEN, HEAD_DIM), lambda bi, j: (j // 2, 0, 0)),
                  pl.BlockSpec((1, 1, CMP_LEN * HEAD_DIM), lambda bi, j: (j // 2, 0, 0))],
        out_specs=pl.BlockSpec((1, 1, nr, HEAD_DIM), lambda bi, j: (bi, j, 0, 0)),
        out_shape=jax.ShapeDtypeStruct((b, 4, nr, HEAD_DIM), BF16),
        compiler_params=_cparams(("parallel", "arbitrary")),
        name="nsa_compress",
    )(rows, w1, w2, pos)


def _nsa_kernel(q_ref, kc_ref, vc_ref, ks_ref, vs_ref, kw_ref, vw_ref, gate_ref, ovt_ref, e_ref,
                o_ref, m_scr, l_scr, acc_scr, os_scr, *, s_len):
    i = pl.program_id(2)
    TQ, HP = NSA_TQ, NSA_HPG
    R = HP * TQ
    nsb = s_len // SLC_LEN
    ncp = s_len // CMP_STRIDE
    n_sel = min(SLC_TOPK, nsb)
    s0 = i * TQ
    q2 = q_ref[0].reshape(R, HEAD_DIM)

    sc = _bdot_nt(q2, kc_ref[0, 0])
    tq_r = s0 + (lax.broadcasted_iota(jnp.int32, (R, ncp), 0) & (TQ - 1))
    ncol = lax.broadcasted_iota(jnp.int32, (R, ncp), 1)
    valid = jnp.logical_and(ncol * CMP_STRIDE + (CMP_LEN - 1) <= tq_r, ncol < ncp - 1)
    scm = jnp.where(valid, sc, NEG)
    e = jnp.where(valid, jnp.exp(scm - jnp.max(scm, axis=1, keepdims=True)), 0.0)
    p_c = e / jnp.maximum(jnp.sum(e, axis=1, keepdims=True), 1e-30)
    o_c = _bdot(p_c, vc_ref[0, 0])
    psum = jnp.sum(p_c.reshape(HP, TQ, ncp), axis=0)
    imp = _fdot_nt(ovt_ref[...], psum)
    blk = lax.broadcasted_iota(jnp.int32, (nsb, TQ), 0)
    tq_l = s0 + lax.broadcasted_iota(jnp.int32, (nsb, TQ), 1)
    cur = tq_l // SLC_LEN
    forced = jnp.logical_or(blk == 0, jnp.logical_or(blk == cur, blk == cur - 1))
    imp = jnp.where(forced, BIG, imp)
    imp = jnp.where(blk * SLC_LEN <= tq_l, imp, NEG)
    rank = jnp.zeros((nsb, TQ), jnp.int32)
    for j in range(nsb):
        rj = imp[j:j + 1, :]
        beats = jnp.logical_or(rj > imp, jnp.logical_and(rj == imp, blk > j))
        rank = rank + jnp.where(beats, 1, 0)
    sel_t = jnp.where(rank < n_sel, 1.0, 0.0).astype(F32)
    if nsb < LANES:
        sel_t = jnp.concatenate([sel_t, jnp.zeros((LANES - nsb, TQ), F32)], axis=0)
    sel = jnp.transpose(sel_t).astype(BF16)

    ones_blk = jnp.ones((TQ, HEAD_DIM), BF16)
    HC = NSA_HC

    def flash(k_ref, v_ref, tk, lo, hi, mask_fn):
        m_scr[...] = jnp.full(m_scr.shape, NEG, F32)
        l_scr[...] = jnp.zeros(l_scr.shape, F32)
        acc_scr[...] = jnp.zeros(acc_scr.shape, F32)
        qpos = s0 + lax.broadcasted_iota(jnp.int32, (TQ, tk), 0)
        kofs = lax.broadcasted_iota(jnp.int32, (TQ, tk), 1)
        ones_v = jnp.concatenate([ones_blk] * (tk // TQ), axis=0)

        def body(j, carry):
            r = pl.multiple_of(j * tk, tk)
            bias = jnp.where(mask_fn(r, r + kofs, qpos), 0.0, NEG)
            kb = k_ref[0, 0, pl.ds(r, tk), :]
            vb = jnp.concatenate([v_ref[0, 0, pl.ds(r, tk), :], ones_v], axis=1)
            for c in range(HP // HC):
                rows = slice(c * HC * TQ, (c + 1) * HC * TQ)
                s = _bdot_nt(q_ref[0, c * HC:(c + 1) * HC].reshape(HC * TQ, HEAD_DIM), kb)
                s = (s.reshape(HC, TQ, tk) + bias[None]).reshape(HC * TQ, tk)
                m_prev = m_scr[rows, :]
                m_new = jnp.maximum(m_prev, jnp.max(s, axis=1, keepdims=True))
                alpha = jnp.exp(m_prev - m_new)
                p = jnp.exp(s - jnp.concatenate([m_new] * (tk // LANES), axis=1)).astype(BF16)
                pv = jnp.dot(p, vb, preferred_element_type=F32)
                l_scr[rows, :] = alpha * l_scr[rows, :] + pv[:, HEAD_DIM:]
                acc_scr[rows, :] = alpha * acc_scr[rows, :] + pv[:, :HEAD_DIM]
                m_scr[rows, :] = m_new
            return carry

        lax.fori_loop(lo, hi, body, 0)

    tks = min(NSA_TK_SEL, s_len)

    def sel_mask(r, kpos, qpos):
        chosen = jnp.dot(sel, e_ref[:, pl.ds(r, tks)], preferred_element_type=F32)
        return jnp.logical_and(chosen > 0.5, kpos <= qpos)

    flash(ks_ref, vs_ref, tks, 0, (s0 + TQ - 1) // tks + 1, sel_mask)
    os_scr[...] = acc_scr[...] / l_scr[...]

    def win_mask(r, kpos, qpos):
        return jnp.logical_and(kpos <= qpos, kpos > qpos - WIN)

    tkw = min(NSA_TK_WIN, s_len)
    flash(kw_ref, vw_ref, tkw, jnp.maximum(s0 - WIN + 1, 0) // tkw, (s0 + TQ - 1) // tkw + 1, win_mask)

    gates = _sigmoid(gate_ref[0, 0])
    for hh in range(HP):
        rows = slice(hh * TQ, (hh + 1) * TQ)
        o_w = acc_scr[rows, :] / l_scr[rows, :]
        out = (gates[:, 3 * hh:3 * hh + 1] * o_c[rows, :]
               + gates[:, 3 * hh + 1:3 * hh + 2] * os_scr[rows, :]
               + gates[:, 3 * hh + 2:3 * hh + 3] * o_w)
        o_ref[0, :, hh * HEAD_DIM:(hh + 1) * HEAD_DIM] = out.astype(BF16)


def _nsa_attend(rp, cmp_kv, P, ovt, expand):
    b, _, s, _ = rp.shape
    TQ, HP, G = NSA_TQ, NSA_HPG, NSA_GROUPS
    ncp = cmp_kv.shape[2]
    R = HP * TQ
    full = lambda j0: pl.BlockSpec((1, 1, s, HEAD_DIM), lambda bi, g, i: (bi, j0 + g, 0, 0))
    cmp_spec = lambda j0: pl.BlockSpec((1, 1, ncp, HEAD_DIM), lambda bi, g, i: (bi, j0 + g, 0, 0))
    return pl.pallas_call(
        functools.partial(_nsa_kernel, s_len=s),
        grid=(b, G, s // TQ),
        in_specs=[pl.BlockSpec((1, HP, TQ, HEAD_DIM), lambda bi, g, i: (bi, g, i, 0)),
                  cmp_spec(0), cmp_spec(2),
                  full(R_KS), full(R_VS), full(R_KW), full(R_VW),
                  pl.BlockSpec((1, 1, TQ, LANES), lambda bi, g, i: (bi, J_GATE + g, i, 0)),
                  pl.BlockSpec(ovt.shape, lambda bi, g, i: (0, 0)),
                  pl.BlockSpec(expand.shape, lambda bi, g, i: (0, 0))],
        out_specs=pl.BlockSpec((1, TQ, HP * HEAD_DIM), lambda bi, g, i: (bi, i, g)),
        out_shape=jax.ShapeDtypeStruct((b, s, NSA_HEADS * HEAD_DIM), BF16),
        scratch_shapes=[pltpu.VMEM((R, LANES), F32), pltpu.VMEM((R, LANES), F32),
                        pltpu.VMEM((R, HEAD_DIM), F32), pltpu.VMEM((R, HEAD_DIM), F32)],
        compiler_params=_cparams(("parallel", "parallel", "arbitrary")),
        name="nsa_attend",
    )(rp, cmp_kv, cmp_kv, rp, rp, rp, rp, P, ovt, expand)


def _nsa(P, positions, cmp_pos_k, cmp_w1_k, cmp_w2_k, cmp_pos_v, cmp_w1_v, cmp_w2_v):
    b, _, s, _ = P.shape
    lanes = np.arange(LANES)
    invf = np.where(lanes < ROPE_DIM, ROPE_THETA ** (-(lanes % ROPE_HALF) / ROPE_HALF), 0.0)
    rp = _rope(P, positions.astype(F32).reshape(b, s, 1), jnp.asarray(invf, F32).reshape(1, LANES))

    ncp = s // CMP_STRIDE
    rows = rp[:, R_KC:R_KC + 4].reshape(b, 4, ncp, CMP_STRIDE * HEAD_DIM)
    w1 = jnp.stack([cmp_w1_k, cmp_w1_v]).astype(BF16)
    w2 = jnp.stack([cmp_w2_k, cmp_w2_v]).astype(BF16)
    pos = jnp.stack([cmp_pos_k, cmp_pos_v]).reshape(2, 1, CMP_LEN * HEAD_DIM)
    cmp_kv = _compress(rows, w1, w2, pos)

    nsb = s // SLC_LEN
    cmp_start = np.arange(ncp) * CMP_STRIDE
    slc_start = np.arange(nsb) * SLC_LEN
    ov = ((cmp_start[:, None] < slc_start[None, :] + SLC_LEN)
          & (cmp_start[:, None] + CMP_LEN > slc_start[None, :])
          & (np.arange(ncp)[:, None] < ncp - 1))
    ovt = jnp.asarray(ov.T.astype(np.float32))
    expand = (np.arange(LANES)[:, None] == (np.arange(s)[None, :] // SLC_LEN))
    expand = jnp.asarray(expand.astype(np.float32), BF16)
    return _nsa_attend(rp, cmp_kv, P, ovt, expand)


MERGE_TN = 512


def _merge_kernel(oa_ref, ob_ref, wg_ref, wn_ref, ma_ref, mb_ref, y_ref):
    ya = jnp.dot(oa_ref[0], wg_ref[...], preferred_element_type=F32)
    yb = jnp.dot(ob_ref[0], wn_ref[...], preferred_element_type=F32)
    for k in range(MERGE_TN // LANES):
        cols = slice(k * LANES, (k + 1) * LANES)
        y = _sigmoid(ma_ref[0, k]) * ya[:, cols] + _sigmoid(mb_ref[0, k]) * yb[:, cols]
        y_ref[0, :, cols] = y.astype(BF16)


def _merge(o_a, o_b, wg, wn, P):
    b, s, da = o_a.shape
    db = o_b.shape[2]
    d = wg.shape[1]
    tm, tn = min(1024, s), MERGE_TN
    nk = tn // LANES
    return pl.pallas_call(
        _merge_kernel,
        grid=(b, s // tm, d // tn),
        in_specs=[pl.BlockSpec((1, tm, da), lambda bi, i, j: (bi, i, 0)),
                  pl.BlockSpec((1, tm, db), lambda bi, i, j: (bi, i, 0)),
                  pl.BlockSpec((da, tn), lambda bi, i, j: (0, j)),
                  pl.BlockSpec((db, tn), lambda bi, i, j: (0, j)),
                  pl.BlockSpec((1, nk, tm, LANES), lambda bi, i, j: (bi, J_MA // nk + j, i, 0)),
                  pl.BlockSpec((1, nk, tm, LANES), lambda bi, i, j: (bi, J_MB // nk + j, i, 0))],
        out_specs=pl.BlockSpec((1, tm, tn), lambda bi, i, j: (bi, i, j)),
        out_shape=jax.ShapeDtypeStruct((b, s, d), BF16),
        compiler_params=_cparams(("parallel", "parallel", "arbitrary")),
        name="merge",
    )(o_a, o_b, wg, wn, P, P)


def _outproj_kernel(y_ref, w_ref, x_ref, g_ref, o_ref):
    o_ref[0] = x_ref[0] + g_ref[0] * jnp.dot(y_ref[0], w_ref[...], preferred_element_type=F32)


def _outproj(y, w, x, g):
    b, s, d = x.shape
    tm, tn = min(1024, s), 512
    return pl.pallas_call(
        _outproj_kernel,
        grid=(b, s // tm, d // tn),
        in_specs=[pl.BlockSpec((1, tm, d), lambda bi, i, j: (bi, i, 0)),
                  pl.BlockSpec((d, tn), lambda bi, i, j: (0, j)),
                  pl.BlockSpec((1, tm, tn), lambda bi, i, j: (bi, i, j)),
                  pl.BlockSpec((1, 1, tn), lambda bi, i, j: (bi, 0, j))],
        out_specs=pl.BlockSpec((1, tm, tn), lambda bi, i, j: (bi, i, j)),
        out_shape=jax.ShapeDtypeStruct((b, s, d), F32),
        compiler_params=_cparams(("parallel", "parallel", "arbitrary")),
        name="outproj",
    )(y, w, x, g.reshape(b, 1, d))


PEER_TB = 512
PEER_TG = 64
PEER_AHEAD = 28
PEER_SLOTS = 32
PEER_SEL = PEER_HEADS * PEER_TOPK
MIN_F32 = -3.0e38


def _topk_rows(vals, payload=None):
    nrow = vals.shape[0]
    rid = lax.broadcasted_iota(jnp.int32, vals.shape, 0)
    out_v, out_i = [], []
    for _ in range(PEER_TOPK):
        m = jnp.max(vals, axis=0, keepdims=True)
        idx = jnp.min(jnp.where(vals == m, rid, nrow), axis=0, keepdims=True)
        hit = rid == idx
        out_v.append(m)
        if payload is None:
            out_i.append(idx)
        else:
            out_i.append(jnp.sum(jnp.where(hit, payload, 0), axis=0, keepdims=True))
        vals = jnp.where(hit, MIN_F32, vals)
    return jnp.concatenate(out_v, axis=0), jnp.concatenate(out_i, axis=0)


def _peer_topk_kernel(qh_ref, k1_ref, k2_ref, eidx_ref, gw_ref):
    def head(h, carry):
        s1 = _fdot_nt(k1_ref[h], qh_ref[0, 2 * h])
        s2 = _fdot_nt(k2_ref[h], qh_ref[0, 2 * h + 1])
        v1, i1 = _topk_rows(s1)
        v2, i2 = _topk_rows(s2)
        keep = [PEER_TOPK // (a + 1) for a in range(PEER_TOPK)]
        npad = (-sum(keep)) % 8
        cand = jnp.concatenate([v1[a:a + 1, :] + v2[0:keep[a], :] for a in range(PEER_TOPK)]
                               + [jnp.full((npad, v1.shape[1]), MIN_F32, F32)], axis=0)
        cidx = jnp.concatenate([i1[a:a + 1, :] * PEER_NKEYS + i2[0:keep[a], :] for a in range(PEER_TOPK)]
                               + [jnp.zeros((npad, v1.shape[1]), jnp.int32)], axis=0)
        sc, eidx = _topk_rows(cand, cidx)
        ex = jnp.exp(sc - sc[0:1, :])
        eidx_ref[h] = eidx
        gw_ref[h] = ex / jnp.sum(ex, axis=0, keepdims=True)
        return carry

    lax.fori_loop(0, PEER_HEADS, head, 0)


def _peer_topk(qh, keys1, keys2):
    b, nb, s, _ = qh.shape
    tb = min(PEER_TB, s)
    nt = s // tb
    kspec = pl.BlockSpec(keys1.shape, lambda bi, i: (0, 0, 0))
    ospec = pl.BlockSpec((PEER_HEADS, PEER_TOPK, tb), lambda bi, i: (0, 0, bi * nt + i))
    return pl.pallas_call(
        _peer_topk_kernel,
        grid=(b, nt),
        in_specs=[pl.BlockSpec((1, nb, tb, LANES), lambda bi, i: (bi, 0, i, 0)), kspec, kspec],
        out_specs=[ospec, ospec],
        out_shape=[jax.ShapeDtypeStruct((PEER_HEADS, PEER_TOPK, b * s), jnp.int32),
                   jax.ShapeDtypeStruct((PEER_HEADS, PEER_TOPK, b * s), F32)],
        compiler_params=_cparams(("parallel", "parallel")),
        name="peer_topk",
    )(qh, keys1, keys2)


def _peer_gather_kernel(idx_ref, idxn_ref, gw_ref, x_ref, uv_hbm, o_ref, buf, a_scr, sem, *, tg):
    NS, D, NSLOT = PEER_SEL, PEER_AHEAD, PEER_SLOTS
    RPE = uv_hbm.shape[1]
    SUBW = RPE // 2
    half = SUBW * LANES

    def wait_rows(t):
        slot = t & (NSLOT - 1)
        pltpu.make_async_copy(uv_hbm.at[pl.ds(0, NS)], buf.at[slot], sem.at[slot]).wait()

    rid = lax.broadcasted_iota(jnp.int32, (NS, NS), 0)
    cid = lax.broadcasted_iota(jnp.int32, (NS, NS), 1)
    eye = rid == cid
    ones = jnp.ones((LANES, LANES), BF16)

    def lane_sum_rep(m):
        hi, lo = _split_bf16(m)
        return (jnp.dot(hi, ones, preferred_element_type=F32)
                + jnp.dot(lo, ones, preferred_element_type=F32))

    def halves(slot, tab, sg):
        w = buf[slot, tab * SUBW + sg]
        return pltpu.bitcast(w << 16, F32), pltpu.bitcast(w & jnp.uint32(0xFFFF0000), F32)

    def phase(t, issue_from, do_dot, do_prev):
        if do_dot:
            wait_rows(t)
            xrow = x_ref[pl.ds(t, 1), :]
            acc = jnp.zeros((NS, LANES), F32)
        if do_prev:
            act = lane_sum_rep(a_scr[(t - 1) & 1])
            gw_rep = lane_sum_rep(jnp.where(eye, gw_ref[pl.ds(t - 1, 1), :], 0.0))
            coef = gw_rep * _gelu(act)
            out_lo, out_hi = [], []
        if issue_from is not None:
            slot_i = (t + D) & (NSLOT - 1)
            base_i = (t + D) * NS if issue_from is idx_ref else (t + D - tg) * NS
        for sg in range(SUBW):
            if issue_from is not None:
                for j in range(sg * NS // SUBW, (sg + 1) * NS // SUBW):
                    pltpu.make_async_copy(uv_hbm.at[issue_from[base_i + j]],
                                          buf.at[slot_i, :, j, :], sem.at[slot_i]).start(priority=j % 2)
            if do_dot:
                lo, hi = halves(t & (NSLOT - 1), 0, sg)
                acc = acc + lo * xrow[:, sg * LANES:(sg + 1) * LANES]
                acc = acc + hi * xrow[:, half + sg * LANES:half + (sg + 1) * LANES]
            if do_prev:
                lo, hi = halves((t - 1) & (NSLOT - 1), 1, sg)
                out_lo.append(jnp.sum(lo * coef, axis=0, keepdims=True))
                out_hi.append(jnp.sum(hi * coef, axis=0, keepdims=True))
        if do_dot:
            a_scr[t & 1] = acc
        if do_prev:
            o_ref[pl.ds(t - 1, 1), :] = jnp.concatenate(out_lo + out_hi, axis=1)

    def ramp(t, carry):
        slot = t & (NSLOT - 1)

        def body(j, c):
            pltpu.make_async_copy(uv_hbm.at[idx_ref[t * NS + j]], buf.at[slot, :, j, :], sem.at[slot]).start()
            return c
        return lax.fori_loop(0, NS, body, carry)

    def looped(issue_from):
        def f(t, carry):
            phase(t, issue_from, True, True)
            return carry
        return f

    step = pl.program_id(0)
    last = pl.num_programs(0) - 1

    @pl.when(step == 0)
    def _():
        lax.fori_loop(0, D, ramp, 0)

    phase(0, idx_ref, True, False)
    lax.fori_loop(1, tg - D, looped(idx_ref), 0)

    @pl.when(step < last)
    def _():
        lax.fori_loop(tg - D, tg, looped(idxn_ref), 0)

    @pl.when(step == last)
    def _():
        lax.fori_loop(tg - D, tg, looped(None), 0)

    phase(tg, None, False, True)


def _peer_gather(eidx, gw, h2r, uvc):
    t = eidx.shape[0]
    tg = min(PEER_TG, t)
    d = h2r.shape[1]
    rpe = uvc.shape[1]
    return pl.pallas_call(
        functools.partial(_peer_gather_kernel, tg=tg),
        grid=(t // tg,),
        in_specs=[pl.BlockSpec((tg * PEER_SEL,), lambda i: (i,), memory_space=pltpu.SMEM),
                  pl.BlockSpec((tg * PEER_SEL,), lambda i: (jnp.minimum(i + 1, t // tg - 1),),
                               memory_space=pltpu.SMEM),
                  pl.BlockSpec((tg, PEER_SEL), lambda i: (i, 0)),
                  pl.BlockSpec((tg, d), lambda i: (i, 0)),
                  pl.BlockSpec(memory_space=pl.ANY)],
        out_specs=pl.BlockSpec((tg, d), lambda i: (i, 0)),
        out_shape=jax.ShapeDtypeStruct((t, d), F32),
        scratch_shapes=[pltpu.VMEM((PEER_SLOTS, rpe, PEER_SEL, LANES), jnp.uint32),
                        pltpu.VMEM((2, PEER_SEL, LANES), F32),
                        pltpu.SemaphoreType.DMA((PEER_SLOTS,))],
        compiler_params=_cparams(("arbitrary",)),
        name="peer_gather",
    )(eidx.reshape(-1), eidx.reshape(-1), gw, h2r, uvc)


PACK_EB = 256


def _pack_kernel(u_ref, v_ref, o_ref):
    half = u_ref.shape[1] // 2
    for tab, ref in enumerate((u_ref, v_ref)):
        for sg in range(half // LANES):
            lo = ref[:, sg * LANES:(sg + 1) * LANES].astype(BF16).astype(F32)
            hi = ref[:, half + sg * LANES:half + (sg + 1) * LANES].astype(BF16).astype(F32)
            word = (pltpu.bitcast(lo, jnp.uint32) >> 16) | (pltpu.bitcast(hi, jnp.uint32) & jnp.uint32(0xFFFF0000))
            o_ref[:, tab * (half // LANES) + sg, :] = word


def _pack_tables(u, v):
    e, d = u.shape
    eb = min(PACK_EB, e)
    return pl.pallas_call(
        _pack_kernel,
        grid=(e // eb,),
        in_specs=[pl.BlockSpec((eb, d), lambda i: (i, 0)), pl.BlockSpec((eb, d), lambda i: (i, 0))],
        out_specs=pl.BlockSpec((eb, d // LANES, LANES), lambda i: (i, 0, 0)),
        out_shape=jax.ShapeDtypeStruct((e, d // LANES, LANES), jnp.uint32),
        compiler_params=_cparams(("parallel",)),
        name="peer_pack",
    )(u, v)


def _peer(x1, norm_w, sc, sh, wq, keys1, keys2, u, v):
    b, s, d = x1.shape
    qh, h2 = _normproj(x1, norm_w, sc, sh, wq.astype(BF16), tm=1024, tn=512, emit_h=True)
    eidx, gw = _peer_topk(qh, keys1, keys2)
    eidx = jnp.transpose(eidx, (2, 0, 1)).reshape(b * s, PEER_SEL)
    gw = jnp.transpose(gw, (2, 0, 1)).reshape(b * s, PEER_SEL)
    uvc = _pack_tables(u, v)
    out = _peer_gather(eidx, gw, h2.reshape(b * s, d), uvc)
    return out.reshape(b, s, d)


def _final_kernel(x_ref, p_ref, g_ref, w_ref, o_ref):
    x = x_ref[0] + g_ref[0] * p_ref[0]
    o_ref[0] = x * lax.rsqrt(jnp.mean(x * x, axis=-1, keepdims=True) + EPS) * w_ref[...]


def _final(x1, peer, g2, wf):
    b, s, d = x1.shape
    tm = min(512, s)
    blk = pl.BlockSpec((1, tm, d), lambda bi, i: (bi, i, 0))
    return pl.pallas_call(
        _final_kernel,
        grid=(b, s // tm),
        in_specs=[blk, blk, pl.BlockSpec((1, 1, d), lambda bi, i: (bi, 0, 0)),
                  pl.BlockSpec((1, d), lambda bi, i: (0, 0))],
        out_specs=blk,
        out_shape=jax.ShapeDtypeStruct((b, s, d), F32),
        compiler_params=_cparams(("parallel", "parallel")),
        name="final_norm",
    )(x1, peer, g2.reshape(b, 1, d), wf.reshape(1, d))


def _pad_rows(a, mult=8):
    pad = (-a.shape[0]) % mult
    return jnp.pad(a, ((0, pad), (0, 0)))


def _permute_w_in(w):
    o_aa, o_bq, o_bg, o_ma, o_end = 4096, 4112, 7696, 7744, 11840
    d = w.shape[0]
    z = lambda n: jnp.zeros((d, n), w.dtype)
    hg = 3 * NSA_HPG
    cols = [w[:, 0:o_aa], w[:, o_bq:o_bg], w[:, o_ma:o_end],
            w[:, o_aa:o_bq], z(LANES - 16),
            w[:, o_bg:o_bg + hg], z(LANES - hg),
            w[:, o_bg + hg:o_ma], z(LANES - hg),
            z(LANES)]
    return jnp.concatenate(cols, axis=1).astype(BF16)


def kernel(x, c, positions, ada_w, ada_b, norm1_w, norm2_w, w_in, gdn_conv_w, gdn_A_log, gdn_dt_bias, gdn_norm_w, cmp_pos_k, cmp_w1_k, cmp_w2_k, cmp_pos_v, cmp_w1_v, cmp_w2_v, w_branch_gdn, w_branch_nsa, w_out, peer_wq, peer_keys1, peer_keys2, peer_u, peer_v, final_norm_w):
    b, s, d = x.shape
    l = 0
    mod = _ada(_pad_rows(c), ada_w[l], ada_b[l])[:b]
    sh1, sc1, g1, sh2, sc2, g2 = jnp.split(mod, 6, axis=-1)
    P = _normproj(x, norm1_w[l], sc1, sh1, _permute_w_in(w_in[l]), tm=1024, tn=1024, emit_h=False)
    o_a = _gdn(P, gdn_conv_w[l], gdn_A_log[l], gdn_dt_bias[l], gdn_norm_w[l])
    o_b = _nsa(P, positions, cmp_pos_k[l], cmp_w1_k[l], cmp_w2_k[l], cmp_pos_v[l], cmp_w1_v[l], cmp_w2_v[l])
    y = _merge(o_a, o_b, w_branch_gdn[l].astype(BF16), w_branch_nsa[l].astype(BF16), P)
    x1 = _outproj(y, w_out[l].astype(BF16), x, g1)
    peer = _peer(x1, norm2_w[l], sc2, sh2, peer_wq[l], peer_keys1[l], peer_keys2[l], peer_u[l], peer_v[l])
    return _final(x1, peer, g2, final_norm_w)
```

```python
import functools
import math

import numpy as np
import jax
import jax.numpy as jnp
from jax import lax
from jax.experimental import pallas as pl
from jax.experimental.pallas import tpu as pltpu

F32 = jnp.float32
BF16 = jnp.bfloat16
HI = lax.Precision.HIGHEST

LANES = 128
VMEM_LIMIT = 56 * 1024 * 1024

EPS = 1e-6
ROPE_THETA = 500000.0
HEAD_DIM = 128
ROPE_DIM = HEAD_DIM // 4
ROPE_HALF = ROPE_DIM // 2

GDN_HEADS = 8
GDN_CONV = 4
GDN_CHUNK = 64

NSA_HEADS = 16
NSA_GROUPS = 2
NSA_HPG = NSA_HEADS // NSA_GROUPS
CMP_LEN = 32
CMP_STRIDE = 16
CMP_HIDDEN = 256
SLC_LEN = 64
SLC_TOPK = 16
WIN = 512

PEER_HEADS = 8
PEER_NKEYS = 128
PEER_QDIM = 256
PEER_TOPK = 16

NEG = -1e30
BIG = 1e9

J_AQ, J_AK, J_AV, J_AZ = 0, 8, 16, 24
J_BQ = 32
N_NSA_BLOCKS = 28
J_MA, J_MB = 60, 76
J_SMALL = 92
J_GATE = 93
NJ = 96


def _cparams(sem):
    return pltpu.CompilerParams(dimension_semantics=sem, vmem_limit_bytes=VMEM_LIMIT)


def _bdot(a, b):
    return jnp.dot(a.astype(BF16), b.astype(BF16), preferred_element_type=F32)


def _bdot_nt(a, b):
    return lax.dot_general(a.astype(BF16), b.astype(BF16), (((1,), (1,)), ((), ())),
                           preferred_element_type=F32)


def _split_bf16(a):
    hi = a.astype(BF16)
    return hi, (a - hi.astype(F32)).astype(BF16)


def _dot3(a, b, exact_a=False):
    bh, bl = _split_bf16(b)
    dot = functools.partial(jnp.dot, preferred_element_type=F32)
    if exact_a:
        return dot(a, bh) + dot(a, bl)
    ah, al = _split_bf16(a)
    return dot(ah, bh) + dot(al, bh) + dot(ah, bl)


def _dot3_nt(a, b):
    ah, al = _split_bf16(a)
    bh, bl = _split_bf16(b)
    dot = functools.partial(lax.dot_general, dimension_numbers=(((1,), (1,)), ((), ())),
                            preferred_element_type=F32)
    return dot(ah, bh) + dot(al, bh) + dot(ah, bl)


def _fdot(a, b):
    return jnp.dot(a, b, precision=HI, preferred_element_type=F32)


def _fdot_nt(a, b):
    return lax.dot_general(a, b, (((1,), (1,)), ((), ())), precision=HI,
                           preferred_element_type=F32)


def _sigmoid(x):
    return 1.0 / (1.0 + jnp.exp(-x))


def _silu(x):
    return x * _sigmoid(x)


def _gelu(x):
    return 0.5 * x * (1.0 + jnp.tanh(math.sqrt(2.0 / math.pi) * (x + 0.044715 * (x * x * x))))


def _softplus(x):
    return jnp.maximum(x, 0.0) + jnp.log(1.0 + jnp.exp(-jnp.abs(x)))


def _ada_kernel(c_ref, w_ref, b_ref, o_ref):
    o_ref[...] = _fdot(_silu(c_ref[...]), w_ref[...]) + b_ref[...]


def _ada(c_pad, ada_w, ada_b):
    m, d = c_pad.shape
    n = ada_w.shape[1]
    tn = 1024
    return pl.pallas_call(
        _ada_kernel,
        grid=(n // tn,),
        in_specs=[pl.BlockSpec((m, d), lambda j: (0, 0)),
                  pl.BlockSpec((d, tn), lambda j: (0, j)),
                  pl.BlockSpec((1, tn), lambda j: (0, j))],
        out_specs=pl.BlockSpec((m, tn), lambda j: (0, j)),
        out_shape=jax.ShapeDtypeStruct((m, n), F32),
        compiler_params=_cparams(("parallel",)),
        name="ada_mod",
    )(c_pad, ada_w, ada_b.reshape(1, n))


def _normproj_kernel(x_ref, nw_ref, sc_ref, sh_ref, w_ref, *rest, nk, emit_h):
    if emit_h:
        o_ref, h_out_ref, h_scr = rest
    else:
        o_ref, h_scr = rest

    @pl.when(pl.program_id(2) == 0)
    def _():
        x = x_ref[0]
        ms = jnp.mean(x * x, axis=-1, keepdims=True)
        y = x * lax.rsqrt(ms + EPS) * nw_ref[...]
        h = y * (1.0 + sc_ref[0]) + sh_ref[0]
        h_scr[...] = h.astype(BF16)
        if emit_h:
            h_out_ref[0] = h

    acc = jnp.dot(h_scr[...], w_ref[...], preferred_element_type=F32)
    for k in range(nk):
        o_ref[0, k] = acc[:, k * LANES:(k + 1) * LANES]


def _normproj(x, nw, sc, sh, w_bf16, *, tm, tn, emit_h):
    b, s, d = x.shape
    n = w_bf16.shape[1]
    tm = min(tm, s)
    nk = tn // LANES
    out_shape = [jax.ShapeDtypeStruct((b, n // LANES, s, LANES), F32)]
    out_specs = [pl.BlockSpec((1, nk, tm, LANES), lambda bi, i, j: (bi, j, i, 0))]
    if emit_h:
        out_shape.append(jax.ShapeDtypeStruct((b, s, d), F32))
        out_specs.append(pl.BlockSpec((1, tm, d), lambda bi, i, j: (bi, i, 0)))
    res = pl.pallas_call(
        functools.partial(_normproj_kernel, nk=nk, emit_h=emit_h),
        grid=(b, s // tm, n // tn),
        in_specs=[pl.BlockSpec((1, tm, d), lambda bi, i, j: (bi, i, 0)),
                  pl.BlockSpec((1, d), lambda bi, i, j: (0, 0)),
                  pl.BlockSpec((1, 1, d), lambda bi, i, j: (bi, 0, 0)),
                  pl.BlockSpec((1, 1, d), lambda bi, i, j: (bi, 0, 0)),
                  pl.BlockSpec((d, tn), lambda bi, i, j: (0, j))],
        out_specs=out_specs,
        out_shape=out_shape,
        scratch_shapes=[pltpu.VMEM((tm, d), BF16)],
        compiler_params=_cparams(("parallel", "parallel", "arbitrary")),
        name="normproj_h" if emit_h else "normproj",
    )(x, nw.reshape(1, d), sc.reshape(b, 1, d), sh.reshape(b, 1, d), w_bf16)
    return res if emit_h else res[0]


GDN_CB = 1024
GDN_GB = 256


def _gdn_local_kernel(q_ref, qh_ref, k_ref, kh_ref, v_ref, vh_ref, sm_ref,
                      cwq_ref, cwk_ref, cwv_ref, alog_ref, dtb_ref,
                      w_ref, u0_ref, qk_ref, qg_ref, kd_ref, ge_ref, *, cb):
    h = pl.program_id(1)
    i = pl.program_id(2)
    C, G = GDN_CHUNK, GDN_GB
    first = (i == 0)

    def conv_act(main_ref, halo_ref, cw_ref, r0):
        if r0 == 0:
            prev = jnp.where(first, 0.0, halo_ref[0, 0])
        else:
            prev = main_ref[0, 0, r0 - 8:r0, :]
        ext = jnp.concatenate([prev, main_ref[0, 0, r0:r0 + G, :]], axis=0)
        w = cw_ref[0]
        y = w[0:1, :] * ext[5:5 + G, :]
        for j in range(1, GDN_CONV):
            y = y + w[j:j + 1, :] * ext[5 + j:5 + j + G, :]
        return _silu(y)

    rid = lax.broadcasted_iota(jnp.int32, (G, G), 0)
    cid = lax.broadcasted_iota(jnp.int32, (G, G), 1)
    same = (rid // C) == (cid // C)
    incl = jnp.logical_and(same, rid >= cid)
    strict = jnp.logical_and(same, rid > cid)
    eye = rid == cid
    is_last = cid == (rid // C) * C + (C - 1)
    joins = []
    bs = 1
    while bs < C:
        joins.append(jnp.logical_and(jnp.logical_and((rid // (2 * bs)) == (cid // (2 * bs)),
                                                     (rid & (2 * bs - 1)) >= bs),
                                     (cid & (2 * bs - 1)) < bs))
        bs *= 2
    tri = jnp.where(incl, 1.0, 0.0).astype(BF16)
    eye_f = jnp.where(eye, 1.0, 0.0).astype(F32)
    lane = lax.broadcasted_iota(jnp.int32, (G, LANES), 1)
    neg_a = -jnp.exp(alog_ref[...])
    dtb = dtb_ref[...]

    for grp in range(cb // G):
        r0 = grp * G
        q = conv_act(q_ref, qh_ref, cwq_ref, r0)
        k = conv_act(k_ref, kh_ref, cwk_ref, r0)
        v = conv_act(v_ref, vh_ref, cwv_ref, r0)
        q = q * lax.rsqrt(jnp.sum(q * q, axis=-1, keepdims=True) + EPS) * (HEAD_DIM ** -0.5)
        k = k * lax.rsqrt(jnp.sum(k * k, axis=-1, keepdims=True) + EPS)

        sm = sm_ref[0, 0, r0:r0 + G, :]
        g_all = neg_a * _softplus(sm + dtb)
        gc_all = _dot3(tri, g_all, exact_a=True)
        gc = jnp.sum(jnp.where(lane == h, gc_all, 0.0), axis=1, keepdims=True)
        beta = jnp.sum(jnp.where(lane == GDN_HEADS + h, _sigmoid(sm), 0.0), axis=1, keepdims=True)
        gc_row = jnp.sum(jnp.where(eye, gc, 0.0), axis=0, keepdims=True)
        gc_last = jnp.sum(jnp.where(is_last, gc_row, 0.0), axis=1, keepdims=True)
        decay = jnp.where(incl, jnp.exp(jnp.where(incl, gc - gc_row, 0.0)), 0.0)
        gamma = jnp.exp(gc)

        kk = _dot3_nt(k, k)
        lmat = jnp.where(strict, decay * kk, 0.0) * beta
        tinv = eye_f - jnp.where(joins[0], lmat, 0.0)
        for lvl in range(1, len(joins)):
            tinv = tinv - _bdot(_bdot(tinv, jnp.where(joins[lvl], lmat, 0.0)), tinv)
        resid = (eye_f - tinv) - _dot3(lmat, tinv)
        tinv = tinv + _bdot(tinv, resid)
        rhs = jnp.concatenate([(beta * gamma) * k, beta * v], axis=1)
        wu = _dot3(tinv, rhs)
        qk = decay * _bdot_nt(q, k)

        rows = slice(r0, r0 + G)
        w_ref[0, 0, rows, :] = wu[:, :HEAD_DIM].astype(BF16)
        u0_ref[0, 0, rows, :] = wu[:, HEAD_DIM:]
        qg_ref[0, 0, rows, :] = (gamma * q).astype(BF16)
        kd_ref[0, 0, rows, :] = (jnp.exp(gc_last - gc) * k).astype(BF16)
        ge_all = jnp.broadcast_to(jnp.exp(gc_last), (G, LANES))
        for c in range(G // C):
            cr = slice(c * C, (c + 1) * C)
            qk_ref[0, 0, r0 + c * C:r0 + (c + 1) * C, :] = qk[cr, cr].astype(BF16)
            ge_ref[0, 0, pl.ds((i * (cb // G) + grp) * (G // C) + c, 1), :] = ge_all[c * C:c * C + 1, :]


def _gdn_local(P, cw, alog_pad, dtb_pad):
    b, _, s, _ = P.shape
    H, CB, C = GDN_HEADS, min(GDN_CB, s), GDN_CHUNK
    n = s // C

    def main(j0):
        return pl.BlockSpec((1, 1, CB, LANES), lambda bi, h, i: (bi, j0 + h, i, 0))

    def halo(j0):
        return pl.BlockSpec((1, 1, 8, LANES),
                            lambda bi, h, i: (bi, j0 + h, jnp.maximum(i * (CB // 8) - 1, 0), 0))

    def cws(j0):
        return pl.BlockSpec((1, GDN_CONV, LANES), lambda bi, h, i: (j0 + h, 0, 0))

    row = pl.BlockSpec((1, LANES), lambda bi, h, i: (0, 0))
    hs = lambda width: pl.BlockSpec((1, 1, CB, width), lambda bi, h, i: (bi, h, i, 0))
    return pl.pallas_call(
        functools.partial(_gdn_local_kernel, cb=CB),
        grid=(b, H, s // CB),
        in_specs=[main(J_AQ), halo(J_AQ), main(J_AK), halo(J_AK), main(J_AV), halo(J_AV),
                  pl.BlockSpec((1, 1, CB, LANES), lambda bi, h, i: (bi, J_SMALL, i, 0)),
                  cws(0), cws(8), cws(16), row, row],
        out_specs=[hs(LANES), hs(LANES), hs(C), hs(LANES), hs(LANES),
                   pl.BlockSpec((1, 1, n, LANES), lambda bi, h, i: (bi, h, 0, 0))],
        out_shape=[jax.ShapeDtypeStruct((b, H, s, LANES), BF16),
                   jax.ShapeDtypeStruct((b, H, s, LANES), F32),
                   jax.ShapeDtypeStruct((b, H, s, C), BF16),
                   jax.ShapeDtypeStruct((b, H, s, LANES), BF16),
                   jax.ShapeDtypeStruct((b, H, s, LANES), BF16),
                   jax.ShapeDtypeStruct((b, H, n, LANES), F32)],
        compiler_params=_cparams(("parallel", "parallel", "arbitrary")),
        name="gdn_local",
    )(P, P, P, P, P, P, P, cw, cw, cw, alog_pad, dtb_pad)


GDN_HB = 4


def _gdn_scan_kernel(w_ref, u0_ref, qk_ref, qg_ref, kd_ref, ge_ref, z_ref, nw_ref, o_ref, s_scr,
                     *, sb):
    C = GDN_CHUNK

    @pl.when(pl.program_id(2) == 0)
    def _():
        s_scr[...] = jnp.zeros_like(s_scr)

    nw = nw_ref[...]

    def body(n, carry):
        r = pl.multiple_of(n * C, C)
        for hh in range(GDN_HB):
            st = s_scr[hh]
            stb = st.astype(BF16)
            u = u0_ref[0, hh, pl.ds(r, C), :] - jnp.dot(w_ref[0, hh, pl.ds(r, C), :], stb,
                                                       preferred_element_type=F32)
            ub = u.astype(BF16)
            o = (jnp.dot(qg_ref[0, hh, pl.ds(r, C), :], stb, preferred_element_type=F32)
                 + jnp.dot(qk_ref[0, hh, pl.ds(r, C), :], ub, preferred_element_type=F32))
            ge = ge_ref[0, hh, pl.ds(n, 1), :]
            s_scr[hh] = ge * st + lax.dot_general(kd_ref[0, hh, pl.ds(r, C), :], ub,
                                                  (((0,), (0,)), ((), ())),
                                                  preferred_element_type=F32)
            on = o * lax.rsqrt(jnp.mean(o * o, axis=-1, keepdims=True) + EPS) * nw
            z = z_ref[0, hh, pl.ds(r, C), :]
            o_ref[0, pl.ds(r, C), hh * LANES:(hh + 1) * LANES] = (on * _silu(z)).astype(BF16)
        return carry

    lax.fori_loop(0, sb // C, body, 0)


def _gdn_scan(w, u0, qk, qg, kd, ge, P, norm_w):
    b, H, s, _ = w.shape
    C, HB = GDN_CHUNK, GDN_HB
    sb = min(1024, s)
    hs = lambda width: pl.BlockSpec((1, HB, sb, width), lambda bi, hb, i: (bi, hb, i, 0))
    return pl.pallas_call(
        functools.partial(_gdn_scan_kernel, sb=sb),
        grid=(b, H // HB, s // sb),
        in_specs=[hs(LANES), hs(LANES), hs(C), hs(LANES), hs(LANES),
                  pl.BlockSpec((1, HB, sb // C, LANES), lambda bi, hb, i: (bi, hb, i, 0)),
                  pl.BlockSpec((1, HB, sb, LANES), lambda bi, hb, i: (bi, J_AZ // HB + hb, i, 0)),
                  pl.BlockSpec((1, LANES), lambda bi, hb, i: (0, 0))],
        out_specs=pl.BlockSpec((1, sb, HB * LANES), lambda bi, hb, i: (bi, i, hb)),
        out_shape=jax.ShapeDtypeStruct((b, s, H * LANES), BF16),
        scratch_shapes=[pltpu.VMEM((HB, HEAD_DIM, HEAD_DIM), F32)],
        compiler_params=_cparams(("parallel", "parallel", "arbitrary")),
        name="gdn_scan",
    )(w, u0, qk, qg, kd, ge, P, norm_w.reshape(1, LANES))


def _pad_lanes_row(v):
    return jnp.pad(v.astype(F32), (0, LANES - v.shape[0])).reshape(1, LANES)


def _gdn(P, conv_w, a_log, dt_bias, norm_w):
    cw = jnp.transpose(conv_w.reshape(GDN_CONV, 3 * GDN_HEADS, LANES), (1, 0, 2))
    w, u0, qk, qg, kd, ge = _gdn_local(P, cw, _pad_lanes_row(a_log), _pad_lanes_row(dt_bias))
    return _gdn_scan(w, u0, qk, qg, kd, ge, P, norm_w)


NSA_TQ = 128
NSA_TK_SEL = 512
NSA_TK_WIN = 256
NSA_HC = 1
R_Q, R_KC, R_VC, R_KS, R_VS, R_KW, R_VW = 0, 16, 18, 20, 22, 24, 26
ROPE_NB = 4


def _rope_kernel(x_ref, pos_ref, invf_ref, o_ref, cos_scr, sin_scr):
    j = pl.program_id(2)
    lane = lax.broadcasted_iota(jnp.int32, cos_scr.shape, 1)

    @pl.when(j == 0)
    def _():
        ang = pos_ref[0] * invf_ref[...]
        sn = jnp.sin(ang)
        cos_scr[...] = jnp.where(lane < ROPE_DIM, jnp.cos(ang), 1.0)
        sin_scr[...] = jnp.where(lane < ROPE_HALF, -sn, jnp.where(lane < ROPE_DIM, sn, 0.0))

    is_q = j < NSA_HEADS // ROPE_NB
    scale = jnp.where(is_q, HEAD_DIM ** -0.5, 1.0)
    for k in range(ROPE_NB):
        x = x_ref[0, k]
        swapped = jnp.where(lane < ROPE_HALF, pltpu.roll(x, LANES - ROPE_HALF, axis=1),
                            pltpu.roll(x, ROPE_HALF, axis=1))
        rot = x * cos_scr[...] + swapped * sin_scr[...]
        out = rot * scale if k < 2 else jnp.where(is_q, rot, x) * scale
        o_ref[0, k] = out.astype(BF16)


def _rope(P, pos_f32, invf):
    b, _, s, _ = P.shape
    tr = min(1024, s)
    nb = ROPE_NB
    return pl.pallas_call(
        _rope_kernel,
        grid=(b, s // tr, N_NSA_BLOCKS // nb),
        in_specs=[pl.BlockSpec((1, nb, tr, LANES), lambda bi, i, j: (bi, J_BQ // nb + j, i, 0)),
                  pl.BlockSpec((1, tr, 1), lambda bi, i, j: (bi, i, 0)),
                  pl.BlockSpec((1, LANES), lambda bi, i, j: (0, 0))],
        out_specs=pl.BlockSpec((1, nb, tr, LANES), lambda bi, i, j: (bi, j, i, 0)),
        out_shape=jax.ShapeDtypeStruct((b, N_NSA_BLOCKS, s, LANES), BF16),
        scratch_shapes=[pltpu.VMEM((tr, LANES), F32), pltpu.VMEM((tr, LANES), F32)],
        compiler_params=_cparams(("parallel", "parallel", "arbitrary")),
        name="nsa_rope",
    )(P, pos_f32, invf)


def _compress_kernel(r_ref, w1_ref, w2_ref, pos_ref, o_ref):
    r = r_ref[0, 0]
    nr = r.shape[0]
    half = CMP_STRIDE * HEAD_DIM
    a = jnp.dot(r, w1_ref[0, :half, :], preferred_element_type=F32)
    bm = jnp.dot(r, w1_ref[0, half:, :], preferred_element_type=F32)
    pos8 = jnp.broadcast_to(pos_ref[0], (8, CMP_LEN * HEAD_DIM)).astype(BF16)
    pb = jnp.dot(pos8, w1_ref[0], preferred_element_type=F32)[0:1, :]
    hid = a + pltpu.roll(bm, nr - 1, axis=0) + pb
    out = jnp.dot(_gelu(hid).astype(BF16), w2_ref[0], preferred_element_type=F32)
    row = lax.broadcasted_iota(jnp.int32, out.shape, 0)
    o_ref[0, 0] = jnp.where(row < nr - 1, out, 0.0).astype(BF16)


def _compress(rows, w1, w2, pos):
    b, _, nr, width = rows.shape
    return pl.pallas_call(
        _compress_kernel,
        grid=(b, 4),
        in_specs=[pl.BlockSpec((1, 1, nr, width), lambda bi, j: (bi, j, 0, 0)),
                  pl.BlockSpec((1, CMP_LEN * HEAD_DIM, CMP_HIDDEN), lambda bi, j: (j // 2, 0, 0)),
                  pl.BlockSpec((1, CMP_HIDDEN, HEAD_DIM), lambda bi, j: (j // 2, 0, 0)),
                  pl.BlockSpec((1, 1, CMP_LEN * HEAD_DIM), lambda bi, j: (j // 2, 0, 0))],
        out_specs=pl.BlockSpec((1, 1, nr, HEAD_DIM), lambda bi, j: (bi, j, 0, 0)),
        out_shape=jax.ShapeDtypeStruct((b, 4, nr, HEAD_DIM), BF16),
        compiler_params=_cparams(("parallel", "arbitrary")),
        name="nsa_compress",
    )(rows, w1, w2, pos)


def _nsa_kernel(q_ref, kc_ref, vc_ref, ks_ref, vs_ref, kw_ref, vw_ref, gate_ref, ovt_ref, e_ref,
                o_ref, m_scr, l_scr, acc_scr, os_scr, *, s_len):
    i = pl.program_id(2)
    TQ, HP = NSA_TQ, NSA_HPG
    R = HP * TQ
    nsb = s_len // SLC_LEN
    ncp = s_len // CMP_STRIDE
    n_sel = min(SLC_TOPK, nsb)
    s0 = i * TQ
    q2 = q_ref[0].reshape(R, HEAD_DIM)

    sc = _bdot_nt(q2, kc_ref[0, 0])
    tq_r = s0 + (lax.broadcasted_iota(jnp.int32, (R, ncp), 0) & (TQ - 1))
    ncol = lax.broadcasted_iota(jnp.int32, (R, ncp), 1)
    valid = jnp.logical_and(ncol * CMP_STRIDE + (CMP_LEN - 1) <= tq_r, ncol < ncp - 1)
    scm = jnp.where(valid, sc, NEG)
    e = jnp.where(valid, jnp.exp(scm - jnp.max(scm, axis=1, keepdims=True)), 0.0)
    p_c = e / jnp.maximum(jnp.sum(e, axis=1, keepdims=True), 1e-30)
    o_c = _bdot(p_c, vc_ref[0, 0])
    psum = jnp.sum(p_c.reshape(HP, TQ, ncp), axis=0)
    imp = _fdot_nt(ovt_ref[...], psum)
    blk = lax.broadcasted_iota(jnp.int32, (nsb, TQ), 0)
    tq_l = s0 + lax.broadcasted_iota(jnp.int32, (nsb, TQ), 1)
    cur = tq_l // SLC_LEN
    forced = jnp.logical_or(blk == 0, jnp.logical_or(blk == cur, blk == cur - 1))
    imp = jnp.where(forced, BIG, imp)
    imp = jnp.where(blk * SLC_LEN <= tq_l, imp, NEG)
    rank = jnp.zeros((nsb, TQ), jnp.int32)
    for j in range(nsb):
        rj = imp[j:j + 1, :]
        beats = jnp.logical_or(rj > imp, jnp.logical_and(rj == imp, blk > j))
        rank = rank + jnp.where(beats, 1, 0)
    sel_t = jnp.where(rank < n_sel, 1.0, 0.0).astype(F32)
    if nsb < LANES:
        sel_t = jnp.concatenate([sel_t, jnp.zeros((LANES - nsb, TQ), F32)], axis=0)
    sel = jnp.transpose(sel_t).astype(BF16)

    ones_blk = jnp.ones((TQ, HEAD_DIM), BF16)
    HC = NSA_HC

    def flash(k_ref, v_ref, tk, lo, hi, mask_fn):
        m_scr[...] = jnp.full(m_scr.shape, NEG, F32)
        l_scr[...] = jnp.zeros(l_scr.shape, F32)
        acc_scr[...] = jnp.zeros(acc_scr.shape, F32)
        qpos = s0 + lax.broadcasted_iota(jnp.int32, (TQ, tk), 0)
        kofs = lax.broadcasted_iota(jnp.int32, (TQ, tk), 1)
        ones_v = jnp.concatenate([ones_blk] * (tk // TQ), axis=0)

        def body(j, carry):
            r = pl.multiple_of(j * tk, tk)
            bias = jnp.where(mask_fn(r, r + kofs, qpos), 0.0, NEG)
            kb = k_ref[0, 0, pl.ds(r, tk), :]
            vb = jnp.concatenate([v_ref[0, 0, pl.ds(r, tk), :], ones_v], axis=1)
            for c in range(HP // HC):
                rows = slice(c * HC * TQ, (c + 1) * HC * TQ)
                s = _bdot_nt(q_ref[0, c * HC:(c + 1) * HC].reshape(HC * TQ, HEAD_DIM), kb)
                s = (s.reshape(HC, TQ, tk) + bias[None]).reshape(HC * TQ, tk)
                m_prev = m_scr[rows, :]
                m_new = jnp.maximum(m_prev, jnp.max(s, axis=1, keepdims=True))
                alpha = jnp.exp(m_prev - m_new)
                p = jnp.exp(s - jnp.concatenate([m_new] * (tk // LANES), axis=1)).astype(BF16)
                pv = jnp.dot(p, vb, preferred_element_type=F32)
                l_scr[rows, :] = alpha * l_scr[rows, :] + pv[:, HEAD_DIM:]
                acc_scr[rows, :] = alpha * acc_scr[rows, :] + pv[:, :HEAD_DIM]
                m_scr[rows, :] = m_new
            return carry

        lax.fori_loop(lo, hi, body, 0)

    tks = min(NSA_TK_SEL, s_len)

    def sel_mask(r, kpos, qpos):
        chosen = jnp.dot(sel, e_ref[:, pl.ds(r, tks)], preferred_element_type=F32)
        return jnp.logical_and(chosen > 0.5, kpos <= qpos)

    flash(ks_ref, vs_ref, tks, 0, (s0 + TQ - 1) // tks + 1, sel_mask)
    os_scr[...] = acc_scr[...] / l_scr[...]

    def win_mask(r, kpos, qpos):
        return jnp.logical_and(kpos <= qpos, kpos > qpos - WIN)

    tkw = min(NSA_TK_WIN, s_len)
    flash(kw_ref, vw_ref, tkw, jnp.maximum(s0 - WIN + 1, 0) // tkw, (s0 + TQ - 1) // tkw + 1, win_mask)

    gates = _sigmoid(gate_ref[0, 0])
    for hh in range(HP):
        rows = slice(hh * TQ, (hh + 1) * TQ)
        o_w = acc_scr[rows, :] / l_scr[rows, :]
        out = (gates[:, 3 * hh:3 * hh + 1] * o_c[rows, :]
               + gates[:, 3 * hh + 1:3 * hh + 2] * os_scr[rows, :]
               + gates[:, 3 * hh + 2:3 * hh + 3] * o_w)
        o_ref[0, :, hh * HEAD_DIM:(hh + 1) * HEAD_DIM] = out.astype(BF16)


def _nsa_attend(rp, cmp_kv, P, ovt, expand):
    b, _, s, _ = rp.shape
    TQ, HP, G = NSA_TQ, NSA_HPG, NSA_GROUPS
    ncp = cmp_kv.shape[2]
    R = HP * TQ
    full = lambda j0: pl.BlockSpec((1, 1, s, HEAD_DIM), lambda bi, g, i: (bi, j0 + g, 0, 0))
    cmp_spec = lambda j0: pl.BlockSpec((1, 1, ncp, HEAD_DIM), lambda bi, g, i: (bi, j0 + g, 0, 0))
    return pl.pallas_call(
        functools.partial(_nsa_kernel, s_len=s),
        grid=(b, G, s // TQ),
        in_specs=[pl.BlockSpec((1, HP, TQ, HEAD_DIM), lambda bi, g, i: (bi, g, i, 0)),
                  cmp_spec(0), cmp_spec(2),
                  full(R_KS), full(R_VS), full(R_KW), full(R_VW),
                  pl.BlockSpec((1, 1, TQ, LANES), lambda bi, g, i: (bi, J_GATE + g, i, 0)),
                  pl.BlockSpec(ovt.shape, lambda bi, g, i: (0, 0)),
                  pl.BlockSpec(expand.shape, lambda bi, g, i: (0, 0))],
        out_specs=pl.BlockSpec((1, TQ, HP * HEAD_DIM), lambda bi, g, i: (bi, i, g)),
        out_shape=jax.ShapeDtypeStruct((b, s, NSA_HEADS * HEAD_DIM), BF16),
        scratch_shapes=[pltpu.VMEM((R, LANES), F32), pltpu.VMEM((R, LANES), F32),
                        pltpu.VMEM((R, HEAD_DIM), F32), pltpu.VMEM((R, HEAD_DIM), F32)],
        compiler_params=_cparams(("parallel", "parallel", "arbitrary")),
        name="nsa_attend",
    )(rp, cmp_kv, cmp_kv, rp, rp, rp, rp, P, ovt, expand)


def _nsa(P, positions, cmp_pos_k, cmp_w1_k, cmp_w2_k, cmp_pos_v, cmp_w1_v, cmp_w2_v):
    b, _, s, _ = P.shape
    lanes = np.arange(LANES)
    invf = np.where(lanes < ROPE_DIM, ROPE_THETA ** (-(lanes % ROPE_HALF) / ROPE_HALF), 0.0)
    rp = _rope(P, positions.astype(F32).reshape(b, s, 1), jnp.asarray(invf, F32).reshape(1, LANES))

    ncp = s // CMP_STRIDE
    rows = rp[:, R_KC:R_KC + 4].reshape(b, 4, ncp, CMP_STRIDE * HEAD_DIM)
    w1 = jnp.stack([cmp_w1_k, cmp_w1_v]).astype(BF16)
    w2 = jnp.stack([cmp_w2_k, cmp_w2_v]).astype(BF16)
    pos = jnp.stack([cmp_pos_k, cmp_pos_v]).reshape(2, 1, CMP_LEN * HEAD_DIM)
    cmp_kv = _compress(rows, w1, w2, pos)

    nsb = s // SLC_LEN
    cmp_start = np.arange(ncp) * CMP_STRIDE
    slc_start = np.arange(nsb) * SLC_LEN
    ov = ((cmp_start[:, None] < slc_start[None, :] + SLC_LEN)
          & (cmp_start[:, None] + CMP_LEN > slc_start[None, :])
          & (np.arange(ncp)[:, None] < ncp - 1))
    ovt = jnp.asarray(ov.T.astype(np.float32))
    expand = (np.arange(LANES)[:, None] == (np.arange(s)[None, :] // SLC_LEN))
    expand = jnp.asarray(expand.astype(np.float32), BF16)
    return _nsa_attend(rp, cmp_kv, P, ovt, expand)


MERGE_TN = 512


def _merge_kernel(oa_ref, ob_ref, wg_ref, wn_ref, ma_ref, mb_ref, y_ref):
    ya = jnp.dot(oa_ref[0], wg_ref[...], preferred_element_type=F32)
    yb = jnp.dot(ob_ref[0], wn_ref[...], preferred_element_type=F32)
    for k in range(MERGE_TN // LANES):
        cols = slice(k * LANES, (k + 1) * LANES)
        y = _sigmoid(ma_ref[0, k]) * ya[:, cols] + _sigmoid(mb_ref[0, k]) * yb[:, cols]
        y_ref[0, :, cols] = y.astype(BF16)


def _merge(o_a, o_b, wg, wn, P):
    b, s, da = o_a.shape
    db = o_b.shape[2]
    d = wg.shape[1]
    tm, tn = min(1024, s), MERGE_TN
    nk = tn // LANES
    return pl.pallas_call(
        _merge_kernel,
        grid=(b, s // tm, d // tn),
        in_specs=[pl.BlockSpec((1, tm, da), lambda bi, i, j: (bi, i, 0)),
                  pl.BlockSpec((1, tm, db), lambda bi, i, j: (bi, i, 0)),
                  pl.BlockSpec((da, tn), lambda bi, i, j: (0, j)),
                  pl.BlockSpec((db, tn), lambda bi, i, j: (0, j)),
                  pl.BlockSpec((1, nk, tm, LANES), lambda bi, i, j: (bi, J_MA // nk + j, i, 0)),
                  pl.BlockSpec((1, nk, tm, LANES), lambda bi, i, j: (bi, J_MB // nk + j, i, 0))],
        out_specs=pl.BlockSpec((1, tm, tn), lambda bi, i, j: (bi, i, j)),
        out_shape=jax.ShapeDtypeStruct((b, s, d), BF16),
        compiler_params=_cparams(("parallel", "parallel", "arbitrary")),
        name="merge",
    )(o_a, o_b, wg, wn, P, P)


def _outproj_kernel(y_ref, w_ref, x_ref, g_ref, o_ref):
    o_ref[0] = x_ref[0] + g_ref[0] * jnp.dot(y_ref[0], w_ref[...], preferred_element_type=F32)


def _outproj(y, w, x, g):
    b, s, d = x.shape
    tm, tn = min(1024, s), 512
    return pl.pallas_call(
        _outproj_kernel,
        grid=(b, s // tm, d // tn),
        in_specs=[pl.BlockSpec((1, tm, d), lambda bi, i, j: (bi, i, 0)),
                  pl.BlockSpec((d, tn), lambda bi, i, j: (0, j)),
                  pl.BlockSpec((1, tm, tn), lambda bi, i, j: (bi, i, j)),
                  pl.BlockSpec((1, 1, tn), lambda bi, i, j: (bi, 0, j))],
        out_specs=pl.BlockSpec((1, tm, tn), lambda bi, i, j: (bi, i, j)),
        out_shape=jax.ShapeDtypeStruct((b, s, d), F32),
        compiler_params=_cparams(("parallel", "parallel", "arbitrary")),
        name="outproj",
    )(y, w, x, g.reshape(b, 1, d))


PEER_TB = 512
PEER_TG = 128
PEER_AHEAD = 28
PEER_SLOTS = 32
PEER_SEL = PEER_HEADS * PEER_TOPK
MIN_F32 = -3.0e38


def _topk_rows(vals, payload=None):
    nrow = vals.shape[0]
    rid = lax.broadcasted_iota(jnp.int32, vals.shape, 0)
    out_v, out_i = [], []
    for _ in range(PEER_TOPK):
        m = jnp.max(vals, axis=0, keepdims=True)
        idx = jnp.min(jnp.where(vals == m, rid, nrow), axis=0, keepdims=True)
        hit = rid == idx
        out_v.append(m)
        if payload is None:
            out_i.append(idx)
        else:
            out_i.append(jnp.sum(jnp.where(hit, payload, 0), axis=0, keepdims=True))
        vals = jnp.where(hit, MIN_F32, vals)
    return jnp.concatenate(out_v, axis=0), jnp.concatenate(out_i, axis=0)


def _peer_topk_kernel(qh_ref, k1_ref, k2_ref, eidx_ref, gw_ref):
    def head(h, carry):
        s1 = _fdot_nt(k1_ref[h], qh_ref[0, 2 * h])
        s2 = _fdot_nt(k2_ref[h], qh_ref[0, 2 * h + 1])
        v1, i1 = _topk_rows(s1)
        v2, i2 = _topk_rows(s2)
        keep = [PEER_TOPK // (a + 1) for a in range(PEER_TOPK)]
        npad = (-sum(keep)) % 8
        cand = jnp.concatenate([v1[a:a + 1, :] + v2[0:keep[a], :] for a in range(PEER_TOPK)]
                               + [jnp.full((npad, v1.shape[1]), MIN_F32, F32)], axis=0)
        cidx = jnp.concatenate([i1[a:a + 1, :] * PEER_NKEYS + i2[0:keep[a], :] for a in range(PEER_TOPK)]
                               + [jnp.zeros((npad, v1.shape[1]), jnp.int32)], axis=0)
        sc, eidx = _topk_rows(cand, cidx)
        ex = jnp.exp(sc - sc[0:1, :])
        eidx_ref[h] = eidx
        gw_ref[h] = ex / jnp.sum(ex, axis=0, keepdims=True)
        return carry

    lax.fori_loop(0, PEER_HEADS, head, 0)


def _peer_topk(qh, keys1, keys2):
    b, nb, s, _ = qh.shape
    tb = min(PEER_TB, s)
    nt = s // tb
    kspec = pl.BlockSpec(keys1.shape, lambda bi, i: (0, 0, 0))
    ospec = pl.BlockSpec((PEER_HEADS, PEER_TOPK, tb), lambda bi, i: (0, 0, bi * nt + i))
    return pl.pallas_call(
        _peer_topk_kernel,
        grid=(b, nt),
        in_specs=[pl.BlockSpec((1, nb, tb, LANES), lambda bi, i: (bi, 0, i, 0)), kspec, kspec],
        out_specs=[ospec, ospec],
        out_shape=[jax.ShapeDtypeStruct((PEER_HEADS, PEER_TOPK, b * s), jnp.int32),
                   jax.ShapeDtypeStruct((PEER_HEADS, PEER_TOPK, b * s), F32)],
        compiler_params=_cparams(("parallel", "parallel")),
        name="peer_topk",
    )(qh, keys1, keys2)


def _peer_gather_kernel(idx_ref, idxn_ref, gw_ref, x_ref, uv_hbm, o_ref, buf, a_scr, sem, *, tg):
    NS, D, NSLOT = PEER_SEL, PEER_AHEAD, PEER_SLOTS
    RPE = uv_hbm.shape[1]
    SUBW = RPE // 2
    half = SUBW * LANES

    def wait_rows(t):
        slot = t & (NSLOT - 1)
        pltpu.make_async_copy(uv_hbm.at[pl.ds(0, NS)], buf.at[slot], sem.at[slot]).wait()

    rid = lax.broadcasted_iota(jnp.int32, (NS, NS), 0)
    cid = lax.broadcasted_iota(jnp.int32, (NS, NS), 1)
    eye = rid == cid
    ones = jnp.ones((LANES, LANES), BF16)

    def lane_sum_rep(m):
        hi, lo = _split_bf16(m)
        return (jnp.dot(hi, ones, preferred_element_type=F32)
                + jnp.dot(lo, ones, preferred_element_type=F32))

    def halves(slot, tab, sg):
        w = buf[slot, tab * SUBW + sg]
        return pltpu.bitcast(w << 16, F32), pltpu.bitcast(w & jnp.uint32(0xFFFF0000), F32)

    def phase(t, issue_from, do_dot, do_prev):
        if do_dot:
            wait_rows(t)
            xrow = x_ref[pl.ds(t, 1), :]
            acc = jnp.zeros((NS, LANES), F32)
        if do_prev:
            act = lane_sum_rep(a_scr[(t - 1) & 1])
            gw_rep = lane_sum_rep(jnp.where(eye, gw_ref[pl.ds(t - 1, 1), :], 0.0))
            coef = gw_rep * _gelu(act)
            out_lo, out_hi = [], []
        if issue_from is not None:
            slot_i = (t + D) & (NSLOT - 1)
            base_i = (t + D) * NS if issue_from is idx_ref else (t + D - tg) * NS
        for sg in range(SUBW):
            if issue_from is not None:
                for j in range(sg * NS // SUBW, (sg + 1) * NS // SUBW):
                    pltpu.make_async_copy(uv_hbm.at[issue_from[base_i + j]],
                                          buf.at[slot_i, :, j, :], sem.at[slot_i]).start(priority=j % 2)
            if do_dot:
                lo, hi = halves(t & (NSLOT - 1), 0, sg)
                acc = acc + lo * xrow[:, sg * LANES:(sg + 1) * LANES]
                acc = acc + hi * xrow[:, half + sg * LANES:half + (sg + 1) * LANES]
            if do_prev:
                lo, hi = halves((t - 1) & (NSLOT - 1), 1, sg)
                out_lo.append(jnp.sum(lo * coef, axis=0, keepdims=True))
                out_hi.append(jnp.sum(hi * coef, axis=0, keepdims=True))
        if do_dot:
            a_scr[t & 1] = acc
        if do_prev:
            o_ref[pl.ds(t - 1, 1), :] = jnp.concatenate(out_lo + out_hi, axis=1)

    def ramp(t, carry):
        slot = t & (NSLOT - 1)

        def body(j, c):
            pltpu.make_async_copy(uv_hbm.at[idx_ref[t * NS + j]], buf.at[slot, :, j, :], sem.at[slot]).start()
            return c
        return lax.fori_loop(0, NS, body, carry)

    def looped(issue_from):
        def f(t, carry):
            phase(t, issue_from, True, True)
            return carry
        return f

    step = pl.program_id(0)
    last = pl.num_programs(0) - 1

    @pl.when(step == 0)
    def _():
        lax.fori_loop(0, D, ramp, 0)

    phase(0, idx_ref, True, False)
    lax.fori_loop(1, tg - D, looped(idx_ref), 0)

    @pl.when(step < last)
    def _():
        lax.fori_loop(tg - D, tg, looped(idxn_ref), 0)

    @pl.when(step == last)
    def _():
        lax.fori_loop(tg - D, tg, looped(None), 0)

    phase(tg, None, False, True)


def _peer_gather(eidx, gw, h2r, uvc):
    t = eidx.shape[0]
    tg = min(PEER_TG, t)
    d = h2r.shape[1]
    rpe = uvc.shape[1]
    return pl.pallas_call(
        functools.partial(_peer_gather_kernel, tg=tg),
        grid=(t // tg,),
        in_specs=[pl.BlockSpec((tg * PEER_SEL,), lambda i: (i,), memory_space=pltpu.SMEM),
                  pl.BlockSpec((tg * PEER_SEL,), lambda i: (jnp.minimum(i + 1, t // tg - 1),),
                               memory_space=pltpu.SMEM),
                  pl.BlockSpec((tg, PEER_SEL), lambda i: (i, 0)),
                  pl.BlockSpec((tg, d), lambda i: (i, 0)),
                  pl.BlockSpec(memory_space=pl.ANY)],
        out_specs=pl.BlockSpec((tg, d), lambda i: (i, 0)),
        out_shape=jax.ShapeDtypeStruct((t, d), F32),
        scratch_shapes=[pltpu.VMEM((PEER_SLOTS, rpe, PEER_SEL, LANES), jnp.uint32),
                        pltpu.VMEM((2, PEER_SEL, LANES), F32),
                        pltpu.SemaphoreType.DMA((PEER_SLOTS,))],
        compiler_params=_cparams(("arbitrary",)),
        name="peer_gather",
    )(eidx.reshape(-1), eidx.reshape(-1), gw, h2r, uvc)


PACK_EB = 256


def _pack_kernel(u_ref, v_ref, o_ref):
    half = u_ref.shape[1] // 2
    for tab, ref in enumerate((u_ref, v_ref)):
        for sg in range(half // LANES):
            lo = ref[:, sg * LANES:(sg + 1) * LANES].astype(BF16).astype(F32)
            hi = ref[:, half + sg * LANES:half + (sg + 1) * LANES].astype(BF16).astype(F32)
            word = (pltpu.bitcast(lo, jnp.uint32) >> 16) | (pltpu.bitcast(hi, jnp.uint32) & jnp.uint32(0xFFFF0000))
            o_ref[:, tab * (half // LANES) + sg, :] = word


def _pack_tables(u, v):
    e, d = u.shape
    eb = min(PACK_EB, e)
    return pl.pallas_call(
        _pack_kernel,
        grid=(e // eb,),
        in_specs=[pl.BlockSpec((eb, d), lambda i: (i, 0)), pl.BlockSpec((eb, d), lambda i: (i, 0))],
        out_specs=pl.BlockSpec((eb, d // LANES, LANES), lambda i: (i, 0, 0)),
        out_shape=jax.ShapeDtypeStruct((e, d // LANES, LANES), jnp.uint32),
        compiler_params=_cparams(("parallel",)),
        name="peer_pack",
    )(u, v)


def _peer(x1, norm_w, sc, sh, wq, keys1, keys2, u, v):
    b, s, d = x1.shape
    qh, h2 = _normproj(x1, norm_w, sc, sh, wq.astype(BF16), tm=1024, tn=512, emit_h=True)
    eidx, gw = _peer_topk(qh, keys1, keys2)
    eidx = jnp.transpose(eidx, (2, 0, 1)).reshape(b * s, PEER_SEL)
    gw = jnp.transpose(gw, (2, 0, 1)).reshape(b * s, PEER_SEL)
    uvc = _pack_tables(u, v)
    out = _peer_gather(eidx, gw, h2.reshape(b * s, d), uvc)
    return out.reshape(b, s, d)


def _final_kernel(x_ref, p_ref, g_ref, w_ref, o_ref):
    x = x_ref[0] + g_ref[0] * p_ref[0]
    o_ref[0] = x * lax.rsqrt(jnp.mean(x * x, axis=-1, keepdims=True) + EPS) * w_ref[...]


def _final(x1, peer, g2, wf):
    b, s, d = x1.shape
    tm = min(512, s)
    blk = pl.BlockSpec((1, tm, d), lambda bi, i: (bi, i, 0))
    return pl.pallas_call(
        _final_kernel,
        grid=(b, s // tm),
        in_specs=[blk, blk, pl.BlockSpec((1, 1, d), lambda bi, i: (bi, 0, 0)),
                  pl.BlockSpec((1, d), lambda bi, i: (0, 0))],
        out_specs=blk,
        out_shape=jax.ShapeDtypeStruct((b, s, d), F32),
        compiler_params=_cparams(("parallel", "parallel")),
        name="final_norm",
    )(x1, peer, g2.reshape(b, 1, d), wf.reshape(1, d))


def _pad_rows(a, mult=8):
    pad = (-a.shape[0]) % mult
    return jnp.pad(a, ((0, pad), (0, 0)))


def _permute_w_in(w):
    o_aa, o_bq, o_bg, o_ma, o_end = 4096, 4112, 7696, 7744, 11840
    d = w.shape[0]
    z = lambda n: jnp.zeros((d, n), w.dtype)
    hg = 3 * NSA_HPG
    cols = [w[:, 0:o_aa], w[:, o_bq:o_bg], w[:, o_ma:o_end],
            w[:, o_aa:o_bq], z(LANES - 16),
            w[:, o_bg:o_bg + hg], z(LANES - hg),
            w[:, o_bg + hg:o_ma], z(LANES - hg),
            z(LANES)]
    return jnp.concatenate(cols, axis=1).astype(BF16)


def kernel(x, c, positions, ada_w, ada_b, norm1_w, norm2_w, w_in, gdn_conv_w, gdn_A_log, gdn_dt_bias, gdn_norm_w, cmp_pos_k, cmp_w1_k, cmp_w2_k, cmp_pos_v, cmp_w1_v, cmp_w2_v, w_branch_gdn, w_branch_nsa, w_out, peer_wq, peer_keys1, peer_keys2, peer_u, peer_v, final_norm_w):
    b, s, d = x.shape
    l = 0
    mod = _ada(_pad_rows(c), ada_w[l], ada_b[l])[:b]
    sh1, sc1, g1, sh2, sc2, g2 = jnp.split(mod, 6, axis=-1)
    P = _normproj(x, norm1_w[l], sc1, sh1, _permute_w_in(w_in[l]), tm=1024, tn=1024, emit_h=False)
    o_a = _gdn(P, gdn_conv_w[l], gdn_A_log[l], gdn_dt_bias[l], gdn_norm_w[l])
    o_b = _nsa(P, positions, cmp_pos_k[l], cmp_w1_k[l], cmp_w2_k[l], cmp_pos_v[l], cmp_w1_v[l], cmp_w2_v[l])
    y = _merge(o_a, o_b, w_branch_gdn[l].astype(BF16), w_branch_nsa[l].astype(BF16), P)
    x1 = _outproj(y, w_out[l].astype(BF16), x, g1)
    peer = _peer(x1, norm2_w[l], sc2, sh2, peer_wq[l], peer_keys1[l], peer_keys2[l], peer_u[l], peer_v[l])
    return _final(x1, peer, g2, final_norm_w)
```

```python
import functools
import math

import numpy as np
import jax
import jax.numpy as jnp
from jax import lax
from jax.experimental import pallas as pl
from jax.experimental.pallas import tpu as pltpu

F32 = jnp.float32
BF16 = jnp.bfloat16
HI = lax.Precision.HIGHEST

LANES = 128
VMEM_LIMIT = 56 * 1024 * 1024

EPS = 1e-6
ROPE_THETA = 500000.0
HEAD_DIM = 128
ROPE_DIM = HEAD_DIM // 4
ROPE_HALF = ROPE_DIM // 2

GDN_HEADS = 8
GDN_CONV = 4
GDN_CHUNK = 64

NSA_HEADS = 16
NSA_GROUPS = 2
NSA_HPG = NSA_HEADS // NSA_GROUPS
CMP_LEN = 32
CMP_STRIDE = 16
CMP_HIDDEN = 256
SLC_LEN = 64
SLC_TOPK = 16
WIN = 512

PEER_HEADS = 8
PEER_NKEYS = 128
PEER_QDIM = 256
PEER_TOPK = 16

NEG = -1e30
BIG = 1e9

J_AQ, J_AK, J_AV, J_AZ = 0, 8, 16, 24
J_BQ = 32
N_NSA_BLOCKS = 28
J_MA, J_MB = 60, 76
J_SMALL = 92
J_GATE = 93
NJ = 96


def _cparams(sem):
    return pltpu.CompilerParams(dimension_semantics=sem, vmem_limit_bytes=VMEM_LIMIT)


def _bdot(a, b):
    return jnp.dot(a.astype(BF16), b.astype(BF16), preferred_element_type=F32)


def _bdot_nt(a, b):
    return lax.dot_general(a.astype(BF16), b.astype(BF16), (((1,), (1,)), ((), ())),
                           preferred_element_type=F32)


def _split_bf16(a):
    hi = a.astype(BF16)
    return hi, (a - hi.astype(F32)).astype(BF16)


def _dot3(a, b, exact_a=False):
    bh, bl = _split_bf16(b)
    dot = functools.partial(jnp.dot, preferred_element_type=F32)
    if exact_a:
        return dot(a, bh) + dot(a, bl)
    ah, al = _split_bf16(a)
    return dot(ah, bh) + dot(al, bh) + dot(ah, bl)


def _dot3_nt(a, b):
    ah, al = _split_bf16(a)
    bh, bl = _split_bf16(b)
    dot = functools.partial(lax.dot_general, dimension_numbers=(((1,), (1,)), ((), ())),
                            preferred_element_type=F32)
    return dot(ah, bh) + dot(al, bh) + dot(ah, bl)


def _fdot(a, b):
    return jnp.dot(a, b, precision=HI, preferred_element_type=F32)


def _fdot_nt(a, b):
    return lax.dot_general(a, b, (((1,), (1,)), ((), ())), precision=HI,
                           preferred_element_type=F32)


def _sigmoid(x):
    return 1.0 / (1.0 + jnp.exp(-x))


def _silu(x):
    return x * _sigmoid(x)


def _gelu(x):
    return 0.5 * x * (1.0 + jnp.tanh(math.sqrt(2.0 / math.pi) * (x + 0.044715 * (x * x * x))))


def _softplus(x):
    return jnp.maximum(x, 0.0) + jnp.log(1.0 + jnp.exp(-jnp.abs(x)))


def _ada_kernel(c_ref, w_ref, b_ref, o_ref):
    o_ref[...] = _fdot(_silu(c_ref[...]), w_ref[...]) + b_ref[...]


def _ada(c_pad, ada_w, ada_b):
    m, d = c_pad.shape
    n = ada_w.shape[1]
    tn = 1024
    return pl.pallas_call(
        _ada_kernel,
        grid=(n // tn,),
        in_specs=[pl.BlockSpec((m, d), lambda j: (0, 0)),
                  pl.BlockSpec((d, tn), lambda j: (0, j)),
                  pl.BlockSpec((1, tn), lambda j: (0, j))],
        out_specs=pl.BlockSpec((m, tn), lambda j: (0, j)),
        out_shape=jax.ShapeDtypeStruct((m, n), F32),
        compiler_params=_cparams(("parallel",)),
        name="ada_mod",
    )(c_pad, ada_w, ada_b.reshape(1, n))


def _normproj_kernel(x_ref, nw_ref, sc_ref, sh_ref, w_ref, *rest, nk, emit_h):
    if emit_h:
        o_ref, h_out_ref, h_scr = rest
    else:
        o_ref, h_scr = rest

    @pl.when(pl.program_id(2) == 0)
    def _():
        x = x_ref[0]
        ms = jnp.mean(x * x, axis=-1, keepdims=True)
        y = x * lax.rsqrt(ms + EPS) * nw_ref[...]
        h = y * (1.0 + sc_ref[0]) + sh_ref[0]
        h_scr[...] = h.astype(BF16)
        if emit_h:
            h_out_ref[0] = h

    acc = jnp.dot(h_scr[...], w_ref[...], preferred_element_type=F32)
    for k in range(nk):
        o_ref[0, k] = acc[:, k * LANES:(k + 1) * LANES]


def _normproj(x, nw, sc, sh, w_bf16, *, tm, tn, emit_h):
    b, s, d = x.shape
    n = w_bf16.shape[1]
    tm = min(tm, s)
    nk = tn // LANES
    out_shape = [jax.ShapeDtypeStruct((b, n // LANES, s, LANES), F32)]
    out_specs = [pl.BlockSpec((1, nk, tm, LANES), lambda bi, i, j: (bi, j, i, 0))]
    if emit_h:
        out_shape.append(jax.ShapeDtypeStruct((b, s, d), F32))
        out_specs.append(pl.BlockSpec((1, tm, d), lambda bi, i, j: (bi, i, 0)))
    res = pl.pallas_call(
        functools.partial(_normproj_kernel, nk=nk, emit_h=emit_h),
        grid=(b, s // tm, n // tn),
        in_specs=[pl.BlockSpec((1, tm, d), lambda bi, i, j: (bi, i, 0)),
                  pl.BlockSpec((1, d), lambda bi, i, j: (0, 0)),
                  pl.BlockSpec((1, 1, d), lambda bi, i, j: (bi, 0, 0)),
                  pl.BlockSpec((1, 1, d), lambda bi, i, j: (bi, 0, 0)),
                  pl.BlockSpec((d, tn), lambda bi, i, j: (0, j))],
        out_specs=out_specs,
        out_shape=out_shape,
        scratch_shapes=[pltpu.VMEM((tm, d), BF16)],
        compiler_params=_cparams(("parallel", "parallel", "arbitrary")),
        name="normproj_h" if emit_h else "normproj",
    )(x, nw.reshape(1, d), sc.reshape(b, 1, d), sh.reshape(b, 1, d), w_bf16)
    return res if emit_h else res[0]


GDN_CB = 1024
GDN_GB = 256


def _gdn_local_kernel(q_ref, qh_ref, k_ref, kh_ref, v_ref, vh_ref, sm_ref,
                      cwq_ref, cwk_ref, cwv_ref, alog_ref, dtb_ref,
                      w_ref, u0_ref, qk_ref, qg_ref, kd_ref, ge_ref, *, cb):
    h = pl.program_id(1)
    i = pl.program_id(2)
    C, G = GDN_CHUNK, GDN_GB
    first = (i == 0)

    def conv_act(main_ref, halo_ref, cw_ref, r0):
        if r0 == 0:
            prev = jnp.where(first, 0.0, halo_ref[0, 0])
        else:
            prev = main_ref[0, 0, r0 - 8:r0, :]
        ext = jnp.concatenate([prev, main_ref[0, 0, r0:r0 + G, :]], axis=0)
        w = cw_ref[0]
        y = w[0:1, :] * ext[5:5 + G, :]
        for j in range(1, GDN_CONV):
            y = y + w[j:j + 1, :] * ext[5 + j:5 + j + G, :]
        return _silu(y)

    rid = lax.broadcasted_iota(jnp.int32, (G, G), 0)
    cid = lax.broadcasted_iota(jnp.int32, (G, G), 1)
    same = (rid // C) == (cid // C)
    incl = jnp.logical_and(same, rid >= cid)
    strict = jnp.logical_and(same, rid > cid)
    eye = rid == cid
    is_last = cid == (rid // C) * C + (C - 1)
    joins = []
    bs = 1
    while bs < C:
        joins.append(jnp.logical_and(jnp.logical_and((rid // (2 * bs)) == (cid // (2 * bs)),
                                                     (rid & (2 * bs - 1)) >= bs),
                                     (cid & (2 * bs - 1)) < bs))
        bs *= 2
    tri = jnp.where(incl, 1.0, 0.0).astype(BF16)
    eye_f = jnp.where(eye, 1.0, 0.0).astype(F32)
    lane = lax.broadcasted_iota(jnp.int32, (G, LANES), 1)
    neg_a = -jnp.exp(alog_ref[...])
    dtb = dtb_ref[...]

    for grp in range(cb // G):
        r0 = grp * G
        q = conv_act(q_ref, qh_ref, cwq_ref, r0)
        k = conv_act(k_ref, kh_ref, cwk_ref, r0)
        v = conv_act(v_ref, vh_ref, cwv_ref, r0)
        q = q * lax.rsqrt(jnp.sum(q * q, axis=-1, keepdims=True) + EPS) * (HEAD_DIM ** -0.5)
        k = k * lax.rsqrt(jnp.sum(k * k, axis=-1, keepdims=True) + EPS)

        sm = sm_ref[0, 0, r0:r0 + G, :]
        g_all = neg_a * _softplus(sm + dtb)
        gc_all = _dot3(tri, g_all, exact_a=True)
        gc = jnp.sum(jnp.where(lane == h, gc_all, 0.0), axis=1, keepdims=True)
        beta = jnp.sum(jnp.where(lane == GDN_HEADS + h, _sigmoid(sm), 0.0), axis=1, keepdims=True)
        gc_row = jnp.sum(jnp.where(eye, gc, 0.0), axis=0, keepdims=True)
        gc_last = jnp.sum(jnp.where(is_last, gc_row, 0.0), axis=1, keepdims=True)
        decay = jnp.where(incl, jnp.exp(jnp.where(incl, gc - gc_row, 0.0)), 0.0)
        gamma = jnp.exp(gc)

        kk = _dot3_nt(k, k)
        lmat = jnp.where(strict, decay * kk, 0.0) * beta
        tinv = eye_f - jnp.where(joins[0], lmat, 0.0)
        for lvl in range(1, len(joins)):
            tinv = tinv - _bdot(_bdot(tinv, jnp.where(joins[lvl], lmat, 0.0)), tinv)
        resid = (eye_f - tinv) - _dot3(lmat, tinv)
        tinv = tinv + _bdot(tinv, resid)
        rhs = jnp.concatenate([(beta * gamma) * k, beta * v], axis=1)
        wu = _dot3(tinv, rhs)
        qk = decay * _bdot_nt(q, k)

        rows = slice(r0, r0 + G)
        w_ref[0, 0, rows, :] = wu[:, :HEAD_DIM].astype(BF16)
        u0_ref[0, 0, rows, :] = wu[:, HEAD_DIM:]
        qg_ref[0, 0, rows, :] = (gamma * q).astype(BF16)
        kd_ref[0, 0, rows, :] = (jnp.exp(gc_last - gc) * k).astype(BF16)
        ge_all = jnp.broadcast_to(jnp.exp(gc_last), (G, LANES))
        for c in range(G // C):
            cr = slice(c * C, (c + 1) * C)
            qk_ref[0, 0, r0 + c * C:r0 + (c + 1) * C, :] = qk[cr, cr].astype(BF16)
            ge_ref[0, 0, pl.ds((i * (cb // G) + grp) * (G // C) + c, 1), :] = ge_all[c * C:c * C + 1, :]


def _gdn_local(P, cw, alog_pad, dtb_pad):
    b, _, s, _ = P.shape
    H, CB, C = GDN_HEADS, min(GDN_CB, s), GDN_CHUNK
    n = s // C

    def main(j0):
        return pl.BlockSpec((1, 1, CB, LANES), lambda bi, h, i: (bi, j0 + h, i, 0))

    def halo(j0):
        return pl.BlockSpec((1, 1, 8, LANES),
                            lambda bi, h, i: (bi, j0 + h, jnp.maximum(i * (CB // 8) - 1, 0), 0))

    def cws(j0):
        return pl.BlockSpec((1, GDN_CONV, LANES), lambda bi, h, i: (j0 + h, 0, 0))

    row = pl.BlockSpec((1, LANES), lambda bi, h, i: (0, 0))
    hs = lambda width: pl.BlockSpec((1, 1, CB, width), lambda bi, h, i: (bi, h, i, 0))
    return pl.pallas_call(
        functools.partial(_gdn_local_kernel, cb=CB),
        grid=(b, H, s // CB),
        in_specs=[main(J_AQ), halo(J_AQ), main(J_AK), halo(J_AK), main(J_AV), halo(J_AV),
                  pl.BlockSpec((1, 1, CB, LANES), lambda bi, h, i: (bi, J_SMALL, i, 0)),
                  cws(0), cws(8), cws(16), row, row],
        out_specs=[hs(LANES), hs(LANES), hs(C), hs(LANES), hs(LANES),
                   pl.BlockSpec((1, 1, n, LANES), lambda bi, h, i: (bi, h, 0, 0))],
        out_shape=[jax.ShapeDtypeStruct((b, H, s, LANES), BF16),
                   jax.ShapeDtypeStruct((b, H, s, LANES), F32),
                   jax.ShapeDtypeStruct((b, H, s, C), BF16),
                   jax.ShapeDtypeStruct((b, H, s, LANES), BF16),
                   jax.ShapeDtypeStruct((b, H, s, LANES), BF16),
                   jax.ShapeDtypeStruct((b, H, n, LANES), F32)],
        compiler_params=_cparams(("parallel", "parallel", "arbitrary")),
        name="gdn_local",
    )(P, P, P, P, P, P, P, cw, cw, cw, alog_pad, dtb_pad)


GDN_HB = 4


def _gdn_scan_kernel(w_ref, u0_ref, qk_ref, qg_ref, kd_ref, ge_ref, z_ref, nw_ref, o_ref, s_scr,
                     *, sb):
    C = GDN_CHUNK

    @pl.when(pl.program_id(2) == 0)
    def _():
        s_scr[...] = jnp.zeros_like(s_scr)

    nw = nw_ref[...]

    def body(n, carry):
        r = pl.multiple_of(n * C, C)
        for hh in range(GDN_HB):
            st = s_scr[hh]
            stb = st.astype(BF16)
            u = u0_ref[0, hh, pl.ds(r, C), :] - jnp.dot(w_ref[0, hh, pl.ds(r, C), :], stb,
                                                       preferred_element_type=F32)
            ub = u.astype(BF16)
            o = (jnp.dot(qg_ref[0, hh, pl.ds(r, C), :], stb, preferred_element_type=F32)
                 + jnp.dot(qk_ref[0, hh, pl.ds(r, C), :], ub, preferred_element_type=F32))
            ge = ge_ref[0, hh, pl.ds(n, 1), :]
            s_scr[hh] = ge * st + lax.dot_general(kd_ref[0, hh, pl.ds(r, C), :], ub,
                                                  (((0,), (0,)), ((), ())),
                                                  preferred_element_type=F32)
            on = o * lax.rsqrt(jnp.mean(o * o, axis=-1, keepdims=True) + EPS) * nw
            z = z_ref[0, hh, pl.ds(r, C), :]
            o_ref[0, pl.ds(r, C), hh * LANES:(hh + 1) * LANES] = (on * _silu(z)).astype(BF16)
        return carry

    lax.fori_loop(0, sb // C, body, 0)


def _gdn_scan(w, u0, qk, qg, kd, ge, P, norm_w):
    b, H, s, _ = w.shape
    C, HB = GDN_CHUNK, GDN_HB
    sb = min(1024, s)
    hs = lambda width: pl.BlockSpec((1, HB, sb, width), lambda bi, hb, i: (bi, hb, i, 0))
    return pl.pallas_call(
        functools.partial(_gdn_scan_kernel, sb=sb),
        grid=(b, H // HB, s // sb),
        in_specs=[hs(LANES), hs(LANES), hs(C), hs(LANES), hs(LANES),
                  pl.BlockSpec((1, HB, sb // C, LANES), lambda bi, hb, i: (bi, hb, i, 0)),
                  pl.BlockSpec((1, HB, sb, LANES), lambda bi, hb, i: (bi, J_AZ // HB + hb, i, 0)),
                  pl.BlockSpec((1, LANES), lambda bi, hb, i: (0, 0))],
        out_specs=pl.BlockSpec((1, sb, HB * LANES), lambda bi, hb, i: (bi, i, hb)),
        out_shape=jax.ShapeDtypeStruct((b, s, H * LANES), BF16),
        scratch_shapes=[pltpu.VMEM((HB, HEAD_DIM, HEAD_DIM), F32)],
        compiler_params=_cparams(("parallel", "parallel", "arbitrary")),
        name="gdn_scan",
    )(w, u0, qk, qg, kd, ge, P, norm_w.reshape(1, LANES))


def _pad_lanes_row(v):
    return jnp.pad(v.astype(F32), (0, LANES - v.shape[0])).reshape(1, LANES)


def _gdn(P, conv_w, a_log, dt_bias, norm_w):
    cw = jnp.transpose(conv_w.reshape(GDN_CONV, 3 * GDN_HEADS, LANES), (1, 0, 2))
    w, u0, qk, qg, kd, ge = _gdn_local(P, cw, _pad_lanes_row(a_log), _pad_lanes_row(dt_bias))
    return _gdn_scan(w, u0, qk, qg, kd, ge, P, norm_w)


NSA_TQ = 128
NSA_TK_SEL = 512
NSA_TK_WIN = 256
NSA_HC = 1
R_Q, R_KC, R_VC, R_KS, R_VS, R_KW, R_VW = 0, 16, 18, 20, 22, 24, 26
ROPE_NB = 4


def _rope_kernel(x_ref, pos_ref, invf_ref, o_ref, cos_scr, sin_scr):
    j = pl.program_id(2)
    lane = lax.broadcasted_iota(jnp.int32, cos_scr.shape, 1)

    @pl.when(j == 0)
    def _():
        ang = pos_ref[0] * invf_ref[...]
        sn = jnp.sin(ang)
        cos_scr[...] = jnp.where(lane < ROPE_DIM, jnp.cos(ang), 1.0)
        sin_scr[...] = jnp.where(lane < ROPE_HALF, -sn, jnp.where(lane < ROPE_DIM, sn, 0.0))

    is_q = j < NSA_HEADS // ROPE_NB
    scale = jnp.where(is_q, HEAD_DIM ** -0.5, 1.0)
    for k in range(ROPE_NB):
        x = x_ref[0, k]
        swapped = jnp.where(lane < ROPE_HALF, pltpu.roll(x, LANES - ROPE_HALF, axis=1),
                            pltpu.roll(x, ROPE_HALF, axis=1))
        rot = x * cos_scr[...] + swapped * sin_scr[...]
        out = rot * scale if k < 2 else jnp.where(is_q, rot, x) * scale
        o_ref[0, k] = out.astype(BF16)


def _rope(P, pos_f32, invf):
    b, _, s, _ = P.shape
    tr = min(1024, s)
    nb = ROPE_NB
    return pl.pallas_call(
        _rope_kernel,
        grid=(b, s // tr, N_NSA_BLOCKS // nb),
        in_specs=[pl.BlockSpec((1, nb, tr, LANES), lambda bi, i, j: (bi, J_BQ // nb + j, i, 0)),
                  pl.BlockSpec((1, tr, 1), lambda bi, i, j: (bi, i, 0)),
                  pl.BlockSpec((1, LANES), lambda bi, i, j: (0, 0))],
        out_specs=pl.BlockSpec((1, nb, tr, LANES), lambda bi, i, j: (bi, j, i, 0)),
        out_shape=jax.ShapeDtypeStruct((b, N_NSA_BLOCKS, s, LANES), BF16),
        scratch_shapes=[pltpu.VMEM((tr, LANES), F32), pltpu.VMEM((tr, LANES), F32)],
        compiler_params=_cparams(("parallel", "parallel", "arbitrary")),
        name="nsa_rope",
    )(P, pos_f32, invf)


def _compress_kernel(r_ref, w1_ref, w2_ref, pos_ref, o_ref):
    r = r_ref[0, 0]
    nr = r.shape[0]
    half = CMP_STRIDE * HEAD_DIM
    a = jnp.dot(r, w1_ref[0, :half, :], preferred_element_type=F32)
    bm = jnp.dot(r, w1_ref[0, half:, :], preferred_element_type=F32)
    pos8 = jnp.broadcast_to(pos_ref[0], (8, CMP_LEN * HEAD_DIM)).astype(BF16)
    pb = jnp.dot(pos8, w1_ref[0], preferred_element_type=F32)[0:1, :]
    hid = a + pltpu.roll(bm, nr - 1, axis=0) + pb
    out = jnp.dot(_gelu(hid).astype(BF16), w2_ref[0], preferred_element_type=F32)
    row = lax.broadcasted_iota(jnp.int32, out.shape, 0)
    o_ref[0, 0] = jnp.where(row < nr - 1, out, 0.0).astype(BF16)


def _compress(rows, w1, w2, pos):
    b, _, nr, width = rows.shape
    return pl.pallas_call(
        _compress_kernel,
        grid=(b, 4),
        in_specs=[pl.BlockSpec((1, 1, nr, width), lambda bi, j: (bi, j, 0, 0)),
                  pl.BlockSpec((1, CMP_LEN * HEAD_DIM, CMP_HIDDEN), lambda bi, j: (j // 2, 0, 0)),
                  pl.BlockSpec((1, CMP_HIDDEN, HEAD_DIM), lambda bi, j: (j // 2, 0, 0)),
                  pl.BlockSpec((1, 1, CMP_LEN * HEAD_DIM), lambda bi, j: (j // 2, 0, 0))],
        out_specs=pl.BlockSpec((1, 1, nr, HEAD_DIM), lambda bi, j: (bi, j, 0, 0)),
        out_shape=jax.ShapeDtypeStruct((b, 4, nr, HEAD_DIM), BF16),
        compiler_params=_cparams(("parallel", "arbitrary")),
        name="nsa_compress",
    )(rows, w1, w2, pos)


def _nsa_kernel(q_ref, kc_ref, vc_ref, ks_ref, vs_ref, kw_ref, vw_ref, gate_ref, ovt_ref, e_ref,
                o_ref, m_scr, l_scr, acc_scr, os_scr, *, s_len):
    i = pl.program_id(2)
    TQ, HP = NSA_TQ, NSA_HPG
    R = HP * TQ
    nsb = s_len // SLC_LEN
    ncp = s_len // CMP_STRIDE
    n_sel = min(SLC_TOPK, nsb)
    s0 = i * TQ
    q2 = q_ref[0].reshape(R, HEAD_DIM)

    sc = _bdot_nt(q2, kc_ref[0, 0])
    tq_r = s0 + (lax.broadcasted_iota(jnp.int32, (R, ncp), 0) & (TQ - 1))
    ncol = lax.broadcasted_iota(jnp.int32, (R, ncp), 1)
    valid = jnp.logical_and(ncol * CMP_STRIDE + (CMP_LEN - 1) <= tq_r, ncol < ncp - 1)
    scm = jnp.where(valid, sc, NEG)
    e = jnp.where(valid, jnp.exp(scm - jnp.max(scm, axis=1, keepdims=True)), 0.0)
    p_c = e / jnp.maximum(jnp.sum(e, axis=1, keepdims=True), 1e-30)
    o_c = _bdot(p_c, vc_ref[0, 0])
    psum = jnp.sum(p_c.reshape(HP, TQ, ncp), axis=0)
    imp = _fdot_nt(ovt_ref[...], psum)
    blk = lax.broadcasted_iota(jnp.int32, (nsb, TQ), 0)
    tq_l = s0 + lax.broadcasted_iota(jnp.int32, (nsb, TQ), 1)
    cur = tq_l // SLC_LEN
    forced = jnp.logical_or(blk == 0, jnp.logical_or(blk == cur, blk == cur - 1))
    imp = jnp.where(forced, BIG, imp)
    imp = jnp.where(blk * SLC_LEN <= tq_l, imp, NEG)
    rank = jnp.zeros((nsb, TQ), jnp.int32)
    for j in range(nsb):
        rj = imp[j:j + 1, :]
        beats = jnp.logical_or(rj > imp, jnp.logical_and(rj == imp, blk > j))
        rank = rank + jnp.where(beats, 1, 0)
    sel_t = jnp.where(rank < n_sel, 1.0, 0.0).astype(F32)
    if nsb < LANES:
        sel_t = jnp.concatenate([sel_t, jnp.zeros((LANES - nsb, TQ), F32)], axis=0)
    sel = jnp.transpose(sel_t).astype(BF16)

    ones_blk = jnp.ones((TQ, HEAD_DIM), BF16)
    HC = NSA_HC

    def flash(k_ref, v_ref, tk, lo, hi, mask_fn):
        m_scr[...] = jnp.full(m_scr.shape, NEG, F32)
        l_scr[...] = jnp.zeros(l_scr.shape, F32)
        acc_scr[...] = jnp.zeros(acc_scr.shape, F32)
        qpos = s0 + lax.broadcasted_iota(jnp.int32, (TQ, tk), 0)
        kofs = lax.broadcasted_iota(jnp.int32, (TQ, tk), 1)
        ones_v = jnp.concatenate([ones_blk] * (tk // TQ), axis=0)

        def body(j, carry):
            r = pl.multiple_of(j * tk, tk)
            bias = jnp.where(mask_fn(r, r + kofs, qpos), 0.0, NEG)
            kb = k_ref[0, 0, pl.ds(r, tk), :]
            vb = jnp.concatenate([v_ref[0, 0, pl.ds(r, tk), :], ones_v], axis=1)
            for c in range(HP // HC):
                rows = slice(c * HC * TQ, (c + 1) * HC * TQ)
                s = _bdot_nt(q_ref[0, c * HC:(c + 1) * HC].reshape(HC * TQ, HEAD_DIM), kb)
                s = (s.reshape(HC, TQ, tk) + bias[None]).reshape(HC * TQ, tk)
                m_prev = m_scr[rows, :]
                m_new = jnp.maximum(m_prev, jnp.max(s, axis=1, keepdims=True))
                alpha = jnp.exp(m_prev - m_new)
                p = jnp.exp(s - jnp.concatenate([m_new] * (tk // LANES), axis=1)).astype(BF16)
                pv = jnp.dot(p, vb, preferred_element_type=F32)
                l_scr[rows, :] = alpha * l_scr[rows, :] + pv[:, HEAD_DIM:]
                acc_scr[rows, :] = alpha * acc_scr[rows, :] + pv[:, :HEAD_DIM]
                m_scr[rows, :] = m_new
            return carry

        lax.fori_loop(lo, hi, body, 0)

    tks = min(NSA_TK_SEL, s_len)

    def sel_mask(r, kpos, qpos):
        chosen = jnp.dot(sel, e_ref[:, pl.ds(r, tks)], preferred_element_type=F32)
        return jnp.logical_and(chosen > 0.5, kpos <= qpos)

    flash(ks_ref, vs_ref, tks, 0, (s0 + TQ - 1) // tks + 1, sel_mask)
    os_scr[...] = acc_scr[...] / l_scr[...]

    def win_mask(r, kpos, qpos):
        return jnp.logical_and(kpos <= qpos, kpos > qpos - WIN)

    tkw = min(NSA_TK_WIN, s_len)
    flash(kw_ref, vw_ref, tkw, jnp.maximum(s0 - WIN + 1, 0) // tkw, (s0 + TQ - 1) // tkw + 1, win_mask)

    gates = _sigmoid(gate_ref[0, 0])
    for hh in range(HP):
        rows = slice(hh * TQ, (hh + 1) * TQ)
        o_w = acc_scr[rows, :] / l_scr[rows, :]
        out = (gates[:, 3 * hh:3 * hh + 1] * o_c[rows, :]
               + gates[:, 3 * hh + 1:3 * hh + 2] * os_scr[rows, :]
               + gates[:, 3 * hh + 2:3 * hh + 3] * o_w)
        o_ref[0, :, hh * HEAD_DIM:(hh + 1) * HEAD_DIM] = out.astype(BF16)


def _nsa_attend(rp, cmp_kv, P, ovt, expand):
    b, _, s, _ = rp.shape
    TQ, HP, G = NSA_TQ, NSA_HPG, NSA_GROUPS
    ncp = cmp_kv.shape[2]
    R = HP * TQ
    full = lambda j0: pl.BlockSpec((1, 1, s, HEAD_DIM), lambda bi, g, i: (bi, j0 + g, 0, 0))
    cmp_spec = lambda j0: pl.BlockSpec((1, 1, ncp, HEAD_DIM), lambda bi, g, i: (bi, j0 + g, 0, 0))
    return pl.pallas_call(
        functools.partial(_nsa_kernel, s_len=s),
        grid=(b, G, s // TQ),
        in_specs=[pl.BlockSpec((1, HP, TQ, HEAD_DIM), lambda bi, g, i: (bi, g, i, 0)),
                  cmp_spec(0), cmp_spec(2),
                  full(R_KS), full(R_VS), full(R_KW), full(R_VW),
                  pl.BlockSpec((1, 1, TQ, LANES), lambda bi, g, i: (bi, J_GATE + g, i, 0)),
                  pl.BlockSpec(ovt.shape, lambda bi, g, i: (0, 0)),
                  pl.BlockSpec(expand.shape, lambda bi, g, i: (0, 0))],
        out_specs=pl.BlockSpec((1, TQ, HP * HEAD_DIM), lambda bi, g, i: (bi, i, g)),
        out_shape=jax.ShapeDtypeStruct((b, s, NSA_HEADS * HEAD_DIM), BF16),
        scratch_shapes=[pltpu.VMEM((R, LANES), F32), pltpu.VMEM((R, LANES), F32),
                        pltpu.VMEM((R, HEAD_DIM), F32), pltpu.VMEM((R, HEAD_DIM), F32)],
        compiler_params=_cparams(("parallel", "parallel", "arbitrary")),
        name="nsa_attend",
    )(rp, cmp_kv, cmp_kv, rp, rp, rp, rp, P, ovt, expand)


def _nsa(P, positions, cmp_pos_k, cmp_w1_k, cmp_w2_k, cmp_pos_v, cmp_w1_v, cmp_w2_v):
    b, _, s, _ = P.shape
    lanes = np.arange(LANES)
    invf = np.where(lanes < ROPE_DIM, ROPE_THETA ** (-(lanes % ROPE_HALF) / ROPE_HALF), 0.0)
    rp = _rope(P, positions.astype(F32).reshape(b, s, 1), jnp.asarray(invf, F32).reshape(1, LANES))

    ncp = s // CMP_STRIDE
    rows = rp[:, R_KC:R_KC + 4].reshape(b, 4, ncp, CMP_STRIDE * HEAD_DIM)
    w1 = jnp.stack([cmp_w1_k, cmp_w1_v]).astype(BF16)
    w2 = jnp.stack([cmp_w2_k, cmp_w2_v]).astype(BF16)
    pos = jnp.stack([cmp_pos_k, cmp_pos_v]).reshape(2, 1, CMP_LEN * HEAD_DIM)
    cmp_kv = _compress(rows, w1, w2, pos)

    nsb = s // SLC_LEN
    cmp_start = np.arange(ncp) * CMP_STRIDE
    slc_start = np.arange(nsb) * SLC_LEN
    ov = ((cmp_start[:, None] < slc_start[None, :] + SLC_LEN)
          & (cmp_start[:, None] + CMP_LEN > slc_start[None, :])
          & (np.arange(ncp)[:, None] < ncp - 1))
    ovt = jnp.asarray(ov.T.astype(np.float32))
    expand = (np.arange(LANES)[:, None] == (np.arange(s)[None, :] // SLC_LEN))
    expand = jnp.asarray(expand.astype(np.float32), BF16)
    return _nsa_attend(rp, cmp_kv, P, ovt, expand)


MERGE_TN = 512


def _merge_kernel(oa_ref, ob_ref, wg_ref, wn_ref, ma_ref, mb_ref, y_ref):
    ya = jnp.dot(oa_ref[0], wg_ref[...], preferred_element_type=F32)
    yb = jnp.dot(ob_ref[0], wn_ref[...], preferred_element_type=F32)
    for k in range(MERGE_TN // LANES):
        cols = slice(k * LANES, (k + 1) * LANES)
        y = _sigmoid(ma_ref[0, k]) * ya[:, cols] + _sigmoid(mb_ref[0, k]) * yb[:, cols]
        y_ref[0, :, cols] = y.astype(BF16)


def _merge(o_a, o_b, wg, wn, P):
    b, s, da = o_a.shape
    db = o_b.shape[2]
    d = wg.shape[1]
    tm, tn = min(1024, s), MERGE_TN
    nk = tn // LANES
    return pl.pallas_call(
        _merge_kernel,
        grid=(b, s // tm, d // tn),
        in_specs=[pl.BlockSpec((1, tm, da), lambda bi, i, j: (bi, i, 0)),
                  pl.BlockSpec((1, tm, db), lambda bi, i, j: (bi, i, 0)),
                  pl.BlockSpec((da, tn), lambda bi, i, j: (0, j)),
                  pl.BlockSpec((db, tn), lambda bi, i, j: (0, j)),
                  pl.BlockSpec((1, nk, tm, LANES), lambda bi, i, j: (bi, J_MA // nk + j, i, 0)),
                  pl.BlockSpec((1, nk, tm, LANES), lambda bi, i, j: (bi, J_MB // nk + j, i, 0))],
        out_specs=pl.BlockSpec((1, tm, tn), lambda bi, i, j: (bi, i, j)),
        out_shape=jax.ShapeDtypeStruct((b, s, d), BF16),
        compiler_params=_cparams(("parallel", "parallel", "arbitrary")),
        name="merge",
    )(o_a, o_b, wg, wn, P, P)


def _outproj_kernel(y_ref, w_ref, x_ref, g_ref, o_ref):
    o_ref[0] = x_ref[0] + g_ref[0] * jnp.dot(y_ref[0], w_ref[...], preferred_element_type=F32)


def _outproj(y, w, x, g):
    b, s, d = x.shape
    tm, tn = min(1024, s), 512
    return pl.pallas_call(
        _outproj_kernel,
        grid=(b, s // tm, d // tn),
        in_specs=[pl.BlockSpec((1, tm, d), lambda bi, i, j: (bi, i, 0)),
                  pl.BlockSpec((d, tn), lambda bi, i, j: (0, j)),
                  pl.BlockSpec((1, tm, tn), lambda bi, i, j: (bi, i, j)),
                  pl.BlockSpec((1, 1, tn), lambda bi, i, j: (bi, 0, j))],
        out_specs=pl.BlockSpec((1, tm, tn), lambda bi, i, j: (bi, i, j)),
        out_shape=jax.ShapeDtypeStruct((b, s, d), F32),
        compiler_params=_cparams(("parallel", "parallel", "arbitrary")),
        name="outproj",
    )(y, w, x, g.reshape(b, 1, d))


PEER_TB = 512
PEER_TG = 128
PEER_AHEAD = 26
PEER_SLOTS = 32
PEER_SEL = PEER_HEADS * PEER_TOPK
MIN_F32 = -3.0e38


def _topk_rows(vals, payload=None):
    nrow = vals.shape[0]
    rid = lax.broadcasted_iota(jnp.int32, vals.shape, 0)
    out_v, out_i = [], []
    for _ in range(PEER_TOPK):
        m = jnp.max(vals, axis=0, keepdims=True)
        idx = jnp.min(jnp.where(vals == m, rid, nrow), axis=0, keepdims=True)
        hit = rid == idx
        out_v.append(m)
        if payload is None:
            out_i.append(idx)
        else:
            out_i.append(jnp.sum(jnp.where(hit, payload, 0), axis=0, keepdims=True))
        vals = jnp.where(hit, MIN_F32, vals)
    return jnp.concatenate(out_v, axis=0), jnp.concatenate(out_i, axis=0)


def _peer_topk_kernel(qh_ref, k1_ref, k2_ref, eidx_ref, gw_ref):
    def head(h, carry):
        s1 = _fdot_nt(k1_ref[h], qh_ref[0, 2 * h])
        s2 = _fdot_nt(k2_ref[h], qh_ref[0, 2 * h + 1])
        v1, i1 = _topk_rows(s1)
        v2, i2 = _topk_rows(s2)
        keep = [PEER_TOPK // (a + 1) for a in range(PEER_TOPK)]
        npad = (-sum(keep)) % 8
        cand = jnp.concatenate([v1[a:a + 1, :] + v2[0:keep[a], :] for a in range(PEER_TOPK)]
                               + [jnp.full((npad, v1.shape[1]), MIN_F32, F32)], axis=0)
        cidx = jnp.concatenate([i1[a:a + 1, :] * PEER_NKEYS + i2[0:keep[a], :] for a in range(PEER_TOPK)]
                               + [jnp.zeros((npad, v1.shape[1]), jnp.int32)], axis=0)
        sc, eidx = _topk_rows(cand, cidx)
        ex = jnp.exp(sc - sc[0:1, :])
        eidx_ref[h] = eidx
        gw_ref[h] = ex / jnp.sum(ex, axis=0, keepdims=True)
        return carry

    lax.fori_loop(0, PEER_HEADS, head, 0)


def _peer_topk(qh, keys1, keys2):
    b, nb, s, _ = qh.shape
    tb = min(PEER_TB, s)
    nt = s // tb
    kspec = pl.BlockSpec(keys1.shape, lambda bi, i: (0, 0, 0))
    ospec = pl.BlockSpec((PEER_HEADS, PEER_TOPK, tb), lambda bi, i: (0, 0, bi * nt + i))
    return pl.pallas_call(
        _peer_topk_kernel,
        grid=(b, nt),
        in_specs=[pl.BlockSpec((1, nb, tb, LANES), lambda bi, i: (bi, 0, i, 0)), kspec, kspec],
        out_specs=[ospec, ospec],
        out_shape=[jax.ShapeDtypeStruct((PEER_HEADS, PEER_TOPK, b * s), jnp.int32),
                   jax.ShapeDtypeStruct((PEER_HEADS, PEER_TOPK, b * s), F32)],
        compiler_params=_cparams(("parallel", "parallel")),
        name="peer_topk",
    )(qh, keys1, keys2)


PEER_BANKS = 4


def _peer_gather_kernel(idx_ref, idxs_ref, gw_ref, x_ref, uv_hbm, o_ref, b0, b1, b2, b3, a_scr, sem, *, tg):
    NS, D, NB = PEER_SEL, PEER_AHEAD, PEER_BANKS
    banks = (b0, b1, b2, b3)
    Q = b0.shape[0]
    RPE = uv_hbm.shape[1]
    SUBW = RPE // 2
    half = SUBW * LANES
    RT = 8

    def place(g, k, off):
        kk = k + off
        return banks[kk % NB], kk % NB, (g + kk // NB) & (Q - 1)

    def wait_rows(g, k):
        bank, kb, q = place(g, k, 0)
        pltpu.make_async_copy(uv_hbm.at[pl.ds(0, NS)], bank.at[q], sem.at[kb, q]).wait()

    rid = lax.broadcasted_iota(jnp.int32, (NS, NS), 0)
    cid = lax.broadcasted_iota(jnp.int32, (NS, NS), 1)
    eye = rid == cid
    ones = jnp.ones((LANES, LANES), BF16)

    def lane_sum_rep(m):
        hi, lo = _split_bf16(m)
        return (jnp.dot(hi, ones, preferred_element_type=F32)
                + jnp.dot(lo, ones, preferred_element_type=F32))

    def low(w):
        return pltpu.bitcast(w << 16, F32)

    def high(w):
        return pltpu.bitcast(w & jnp.uint32(0xFFFF0000), F32)

    def phase(g, k, do_dot, do_prev):
        t = g * NB + k
        if do_dot:
            wait_rows(g, k)
            xrow = x_ref[pl.ds(t, 1), :]
            bank_d, _, q_d = place(g, k, 0)
            acc = [jnp.zeros((RT, LANES), F32) for _ in range(NS // RT)]
            bank_i, k_i, q_i = place(g, k, D)
            base_i = t * NS
        if do_prev:
            bank_s, _, q_s = place(g, k, -1)
            par_s = (t - 1) & 1
            out_lo, out_hi = [], []
        for sg in range(SUBW):
            if do_dot:
                x_lo = jnp.broadcast_to(xrow[:, sg * LANES:(sg + 1) * LANES], (RT, LANES))
                x_hi = jnp.broadcast_to(xrow[:, half + sg * LANES:half + (sg + 1) * LANES], (RT, LANES))
            if do_prev:
                p_lo = jnp.zeros((RT, LANES), F32)
                p_hi = jnp.zeros((RT, LANES), F32)
            for rt in range(NS // RT):
                rows = slice(rt * RT, (rt + 1) * RT)
                if do_dot:
                    j = sg * (NS // RT) + rt
                    pltpu.make_async_copy(uv_hbm.at[idxs_ref[base_i + j]],
                                          bank_i.at[q_i, :, j, :], sem.at[k_i, q_i]).start(priority=j % 2)
                    w = bank_d[q_d, sg, rows, :]
                    acc[rt] = acc[rt] + low(w) * x_lo + high(w) * x_hi
                if do_prev:
                    w = bank_s[q_s, SUBW + sg, rows, :]
                    c = a_scr[par_s, rows, :]
                    p_lo = p_lo + low(w) * c
                    p_hi = p_hi + high(w) * c
            if do_prev:
                out_lo.append(jnp.sum(p_lo, axis=0, keepdims=True))
                out_hi.append(jnp.sum(p_hi, axis=0, keepdims=True))
        if do_dot:
            gw_rep = lane_sum_rep(jnp.where(eye, gw_ref[pl.ds(t, 1), :], 0.0))
            a_scr[t & 1] = gw_rep * _gelu(lane_sum_rep(jnp.concatenate(acc, axis=0)))
        if do_prev:
            o_ref[pl.ds(t - 1, 1), :] = jnp.concatenate(out_lo + out_hi, axis=1)

    step = pl.program_id(0)
    last = pl.num_programs(0) - 1

    @pl.when(step == 0)
    def _():
        for k in range(NB):
            def ramp(g, carry, k=k):
                bank, kb, q = place(g, k, 0)

                def body(j, c):
                    pltpu.make_async_copy(uv_hbm.at[idx_ref[(g * NB + k) * NS + j]],
                                          bank.at[q, :, j, :], sem.at[kb, q]).start()
                    return c
                return lax.fori_loop(0, NS, body, carry)
            lax.fori_loop(0, (D - k + NB - 1) // NB, ramp, 0)

    phase(0, 0, True, False)
    for k in range(1, NB):
        phase(0, k, True, True)

    def group(g, carry):
        for k in range(NB):
            phase(g, k, True, True)
        return carry
    lax.fori_loop(1, tg // NB, group, 0)
    phase(tg // NB, 0, False, True)

    @pl.when(step == last)
    def _():
        for r in range(D):
            wait_rows(tg // NB, r)


def _peer_gather(eidx, gw, h2r, uvc):
    t = eidx.shape[0]
    tg = min(PEER_TG, t)
    d = h2r.shape[1]
    rpe = uvc.shape[1]
    nq = PEER_SLOTS // PEER_BANKS
    flat = eidx.reshape(-1)
    shifted = jnp.roll(flat, -PEER_AHEAD * PEER_SEL)
    bank = pltpu.VMEM((nq, rpe, PEER_SEL, LANES), jnp.uint32)
    return pl.pallas_call(
        functools.partial(_peer_gather_kernel, tg=tg),
        grid=(t // tg,),
        in_specs=[pl.BlockSpec((tg * PEER_SEL,), lambda i: (0,), memory_space=pltpu.SMEM),
                  pl.BlockSpec((tg * PEER_SEL,), lambda i: (i,), memory_space=pltpu.SMEM),
                  pl.BlockSpec((tg, PEER_SEL), lambda i: (i, 0)),
                  pl.BlockSpec((tg, d), lambda i: (i, 0)),
                  pl.BlockSpec(memory_space=pl.ANY)],
        out_specs=pl.BlockSpec((tg, d), lambda i: (i, 0)),
        out_shape=jax.ShapeDtypeStruct((t, d), F32),
        scratch_shapes=[bank, bank, bank, bank,
                        pltpu.VMEM((2, PEER_SEL, LANES), F32),
                        pltpu.SemaphoreType.DMA((PEER_BANKS, nq))],
        compiler_params=_cparams(("arbitrary",)),
        name="peer_gather",
    )(flat, shifted, gw, h2r, uvc)


PACK_EB = 256


def _pack_kernel(u_ref, v_ref, o_ref):
    half = u_ref.shape[1] // 2
    for tab, ref in enumerate((u_ref, v_ref)):
        for sg in range(half // LANES):
            lo = ref[:, sg * LANES:(sg + 1) * LANES].astype(BF16).astype(F32)
            hi = ref[:, half + sg * LANES:half + (sg + 1) * LANES].astype(BF16).astype(F32)
            word = (pltpu.bitcast(lo, jnp.uint32) >> 16) | (pltpu.bitcast(hi, jnp.uint32) & jnp.uint32(0xFFFF0000))
            o_ref[:, tab * (half // LANES) + sg, :] = word


def _pack_tables(u, v):
    e, d = u.shape
    eb = min(PACK_EB, e)
    return pl.pallas_call(
        _pack_kernel,
        grid=(e // eb,),
        in_specs=[pl.BlockSpec((eb, d), lambda i: (i, 0)), pl.BlockSpec((eb, d), lambda i: (i, 0))],
        out_specs=pl.BlockSpec((eb, d // LANES, LANES), lambda i: (i, 0, 0)),
        out_shape=jax.ShapeDtypeStruct((e, d // LANES, LANES), jnp.uint32),
        compiler_params=_cparams(("parallel",)),
        name="peer_pack",
    )(u, v)


def _peer(x1, norm_w, sc, sh, wq, keys1, keys2, u, v):
    b, s, d = x1.shape
    qh, h2 = _normproj(x1, norm_w, sc, sh, wq.astype(BF16), tm=1024, tn=512, emit_h=True)
    eidx, gw = _peer_topk(qh, keys1, keys2)
    eidx = jnp.transpose(eidx, (2, 0, 1)).reshape(b * s, PEER_SEL)
    gw = jnp.transpose(gw, (2, 0, 1)).reshape(b * s, PEER_SEL)
    uvc = _pack_tables(u, v)
    out = _peer_gather(eidx, gw, h2.reshape(b * s, d), uvc)
    return out.reshape(b, s, d)


def _final_kernel(x_ref, p_ref, g_ref, w_ref, o_ref):
    x = x_ref[0] + g_ref[0] * p_ref[0]
    o_ref[0] = x * lax.rsqrt(jnp.mean(x * x, axis=-1, keepdims=True) + EPS) * w_ref[...]


def _final(x1, peer, g2, wf):
    b, s, d = x1.shape
    tm = min(512, s)
    blk = pl.BlockSpec((1, tm, d), lambda bi, i: (bi, i, 0))
    return pl.pallas_call(
        _final_kernel,
        grid=(b, s // tm),
        in_specs=[blk, blk, pl.BlockSpec((1, 1, d), lambda bi, i: (bi, 0, 0)),
                  pl.BlockSpec((1, d), lambda bi, i: (0, 0))],
        out_specs=blk,
        out_shape=jax.ShapeDtypeStruct((b, s, d), F32),
        compiler_params=_cparams(("parallel", "parallel")),
        name="final_norm",
    )(x1, peer, g2.reshape(b, 1, d), wf.reshape(1, d))


def _pad_rows(a, mult=8):
    pad = (-a.shape[0]) % mult
    return jnp.pad(a, ((0, pad), (0, 0)))


def _permute_w_in(w):
    o_aa, o_bq, o_bg, o_ma, o_end = 4096, 4112, 7696, 7744, 11840
    d = w.shape[0]
    z = lambda n: jnp.zeros((d, n), w.dtype)
    hg = 3 * NSA_HPG
    cols = [w[:, 0:o_aa], w[:, o_bq:o_bg], w[:, o_ma:o_end],
            w[:, o_aa:o_bq], z(LANES - 16),
            w[:, o_bg:o_bg + hg], z(LANES - hg),
            w[:, o_bg + hg:o_ma], z(LANES - hg),
            z(LANES)]
    return jnp.concatenate(cols, axis=1).astype(BF16)


def kernel(x, c, positions, ada_w, ada_b, norm1_w, norm2_w, w_in, gdn_conv_w, gdn_A_log, gdn_dt_bias, gdn_norm_w, cmp_pos_k, cmp_w1_k, cmp_w2_k, cmp_pos_v, cmp_w1_v, cmp_w2_v, w_branch_gdn, w_branch_nsa, w_out, peer_wq, peer_keys1, peer_keys2, peer_u, peer_v, final_norm_w):
    b, s, d = x.shape
    l = 0
    mod = _ada(_pad_rows(c), ada_w[l], ada_b[l])[:b]
    sh1, sc1, g1, sh2, sc2, g2 = jnp.split(mod, 6, axis=-1)
    P = _normproj(x, norm1_w[l], sc1, sh1, _permute_w_in(w_in[l]), tm=1024, tn=1024, emit_h=False)
    o_a = _gdn(P, gdn_conv_w[l], gdn_A_log[l], gdn_dt_bias[l], gdn_norm_w[l])
    o_b = _nsa(P, positions, cmp_pos_k[l], cmp_w1_k[l], cmp_w2_k[l], cmp_pos_v[l], cmp_w1_v[l], cmp_w2_v[l])
    y = _merge(o_a, o_b, w_branch_gdn[l].astype(BF16), w_branch_nsa[l].astype(BF16), P)
    x1 = _outproj(y, w_out[l].astype(BF16), x, g1)
    peer = _peer(x1, norm2_w[l], sc2, sh2, peer_wq[l], peer_keys1[l], peer_keys2[l], peer_u[l], peer_v[l])
    return _final(x1, peer, g2, final_norm_w)
```

```python
import functools
import math

import numpy as np
import jax
import jax.numpy as jnp
from jax import lax
from jax.experimental import pallas as pl
from jax.experimental.pallas import tpu as pltpu

F32 = jnp.float32
BF16 = jnp.bfloat16
HI = lax.Precision.HIGHEST

LANES = 128
VMEM_LIMIT = 56 * 1024 * 1024

EPS = 1e-6
ROPE_THETA = 500000.0
HEAD_DIM = 128
ROPE_DIM = HEAD_DIM // 4
ROPE_HALF = ROPE_DIM // 2

GDN_HEADS = 8
GDN_CONV = 4
GDN_CHUNK = 64

NSA_HEADS = 16
NSA_GROUPS = 2
NSA_HPG = NSA_HEADS // NSA_GROUPS
CMP_LEN = 32
CMP_STRIDE = 16
CMP_HIDDEN = 256
SLC_LEN = 64
SLC_TOPK = 16
WIN = 512

PEER_HEADS = 8
PEER_NKEYS = 128
PEER_QDIM = 256
PEER_TOPK = 16

NEG = -1e30
BIG = 1e9

J_AQ, J_AK, J_AV, J_AZ = 0, 8, 16, 24
J_BQ = 32
N_NSA_BLOCKS = 28
J_MA, J_MB = 60, 76
J_SMALL = 92
J_GATE = 93
NJ = 96


def _cparams(sem):
    return pltpu.CompilerParams(dimension_semantics=sem, vmem_limit_bytes=VMEM_LIMIT)


def _bdot(a, b):
    return jnp.dot(a.astype(BF16), b.astype(BF16), preferred_element_type=F32)


def _bdot_nt(a, b):
    return lax.dot_general(a.astype(BF16), b.astype(BF16), (((1,), (1,)), ((), ())),
                           preferred_element_type=F32)


def _split_bf16(a):
    hi = a.astype(BF16)
    return hi, (a - hi.astype(F32)).astype(BF16)


def _dot3(a, b, exact_a=False):
    bh, bl = _split_bf16(b)
    dot = functools.partial(jnp.dot, preferred_element_type=F32)
    if exact_a:
        return dot(a, bh) + dot(a, bl)
    ah, al = _split_bf16(a)
    return dot(ah, bh) + dot(al, bh) + dot(ah, bl)


def _dot3_nt(a, b):
    ah, al = _split_bf16(a)
    bh, bl = _split_bf16(b)
    dot = functools.partial(lax.dot_general, dimension_numbers=(((1,), (1,)), ((), ())),
                            preferred_element_type=F32)
    return dot(ah, bh) + dot(al, bh) + dot(ah, bl)


def _fdot(a, b):
    return jnp.dot(a, b, precision=HI, preferred_element_type=F32)


def _fdot_nt(a, b):
    return lax.dot_general(a, b, (((1,), (1,)), ((), ())), precision=HI,
                           preferred_element_type=F32)


def _sigmoid(x):
    return 1.0 / (1.0 + jnp.exp(-x))


def _silu(x):
    return x * _sigmoid(x)


def _gelu(x):
    return 0.5 * x * (1.0 + jnp.tanh(math.sqrt(2.0 / math.pi) * (x + 0.044715 * (x * x * x))))


def _softplus(x):
    return jnp.maximum(x, 0.0) + jnp.log(1.0 + jnp.exp(-jnp.abs(x)))


def _ada_kernel(c_ref, w_ref, b_ref, o_ref):
    o_ref[...] = _fdot(_silu(c_ref[...]), w_ref[...]) + b_ref[...]


def _ada(c_pad, ada_w, ada_b):
    m, d = c_pad.shape
    n = ada_w.shape[1]
    tn = 1024
    return pl.pallas_call(
        _ada_kernel,
        grid=(n // tn,),
        in_specs=[pl.BlockSpec((m, d), lambda j: (0, 0)),
                  pl.BlockSpec((d, tn), lambda j: (0, j)),
                  pl.BlockSpec((1, tn), lambda j: (0, j))],
        out_specs=pl.BlockSpec((m, tn), lambda j: (0, j)),
        out_shape=jax.ShapeDtypeStruct((m, n), F32),
        compiler_params=_cparams(("parallel",)),
        name="ada_mod",
    )(c_pad, ada_w, ada_b.reshape(1, n))


def _normproj_kernel(x_ref, nw_ref, sc_ref, sh_ref, w_ref, *rest, nk, emit_h):
    if emit_h:
        o_ref, h_out_ref, h_scr = rest
    else:
        o_ref, h_scr = rest

    @pl.when(pl.program_id(2) == 0)
    def _():
        x = x_ref[0]
        ms = jnp.mean(x * x, axis=-1, keepdims=True)
        y = x * lax.rsqrt(ms + EPS) * nw_ref[...]
        h = y * (1.0 + sc_ref[0]) + sh_ref[0]
        h_scr[...] = h.astype(BF16)
        if emit_h:
            h_out_ref[0] = h

    acc = jnp.dot(h_scr[...], w_ref[...], preferred_element_type=F32)
    for k in range(nk):
        o_ref[0, k] = acc[:, k * LANES:(k + 1) * LANES]


def _normproj(x, nw, sc, sh, w_bf16, *, tm, tn, emit_h):
    b, s, d = x.shape
    n = w_bf16.shape[1]
    tm = min(tm, s)
    nk = tn // LANES
    out_shape = [jax.ShapeDtypeStruct((b, n // LANES, s, LANES), F32)]
    out_specs = [pl.BlockSpec((1, nk, tm, LANES), lambda bi, i, j: (bi, j, i, 0))]
    if emit_h:
        out_shape.append(jax.ShapeDtypeStruct((b, s, d), F32))
        out_specs.append(pl.BlockSpec((1, tm, d), lambda bi, i, j: (bi, i, 0)))
    res = pl.pallas_call(
        functools.partial(_normproj_kernel, nk=nk, emit_h=emit_h),
        grid=(b, s // tm, n // tn),
        in_specs=[pl.BlockSpec((1, tm, d), lambda bi, i, j: (bi, i, 0)),
                  pl.BlockSpec((1, d), lambda bi, i, j: (0, 0)),
                  pl.BlockSpec((1, 1, d), lambda bi, i, j: (bi, 0, 0)),
                  pl.BlockSpec((1, 1, d), lambda bi, i, j: (bi, 0, 0)),
                  pl.BlockSpec((d, tn), lambda bi, i, j: (0, j))],
        out_specs=out_specs,
        out_shape=out_shape,
        scratch_shapes=[pltpu.VMEM((tm, d), BF16)],
        compiler_params=_cparams(("parallel", "parallel", "arbitrary")),
        name="normproj_h" if emit_h else "normproj",
    )(x, nw.reshape(1, d), sc.reshape(b, 1, d), sh.reshape(b, 1, d), w_bf16)
    return res if emit_h else res[0]


GDN_CB = 1024
GDN_GB = 256


def _gdn_local_kernel(q_ref, qh_ref, k_ref, kh_ref, v_ref, vh_ref, sm_ref,
                      cwq_ref, cwk_ref, cwv_ref, alog_ref, dtb_ref,
                      w_ref, u0_ref, qk_ref, qg_ref, kd_ref, ge_ref, *, cb):
    h = pl.program_id(1)
    i = pl.program_id(2)
    C, G = GDN_CHUNK, GDN_GB
    first = (i == 0)

    def conv_act(main_ref, halo_ref, cw_ref, r0):
        if r0 == 0:
            prev = jnp.where(first, 0.0, halo_ref[0, 0])
        else:
            prev = main_ref[0, 0, r0 - 8:r0, :]
        ext = jnp.concatenate([prev, main_ref[0, 0, r0:r0 + G, :]], axis=0)
        w = cw_ref[0]
        y = w[0:1, :] * ext[5:5 + G, :]
        for j in range(1, GDN_CONV):
            y = y + w[j:j + 1, :] * ext[5 + j:5 + j + G, :]
        return _silu(y)

    rid = lax.broadcasted_iota(jnp.int32, (G, G), 0)
    cid = lax.broadcasted_iota(jnp.int32, (G, G), 1)
    same = (rid // C) == (cid // C)
    incl = jnp.logical_and(same, rid >= cid)
    strict = jnp.logical_and(same, rid > cid)
    eye = rid == cid
    is_last = cid == (rid // C) * C + (C - 1)
    joins = []
    bs = 1
    while bs < C:
        joins.append(jnp.logical_and(jnp.logical_and((rid // (2 * bs)) == (cid // (2 * bs)),
                                                     (rid & (2 * bs - 1)) >= bs),
                                     (cid & (2 * bs - 1)) < bs))
        bs *= 2
    tri = jnp.where(incl, 1.0, 0.0).astype(BF16)
    eye_f = jnp.where(eye, 1.0, 0.0).astype(F32)
    lane = lax.broadcasted_iota(jnp.int32, (G, LANES), 1)
    neg_a = -jnp.exp(alog_ref[...])
    dtb = dtb_ref[...]

    for grp in range(cb // G):
        r0 = grp * G
        q = conv_act(q_ref, qh_ref, cwq_ref, r0)
        k = conv_act(k_ref, kh_ref, cwk_ref, r0)
        v = conv_act(v_ref, vh_ref, cwv_ref, r0)
        q = q * lax.rsqrt(jnp.sum(q * q, axis=-1, keepdims=True) + EPS) * (HEAD_DIM ** -0.5)
        k = k * lax.rsqrt(jnp.sum(k * k, axis=-1, keepdims=True) + EPS)

        sm = sm_ref[0, 0, r0:r0 + G, :]
        g_all = neg_a * _softplus(sm + dtb)
        gc_all = _dot3(tri, g_all, exact_a=True)
        gc = jnp.sum(jnp.where(lane == h, gc_all, 0.0), axis=1, keepdims=True)
        beta = jnp.sum(jnp.where(lane == GDN_HEADS + h, _sigmoid(sm), 0.0), axis=1, keepdims=True)
        gc_row = jnp.sum(jnp.where(eye, gc, 0.0), axis=0, keepdims=True)
        gc_last = jnp.sum(jnp.where(is_last, gc_row, 0.0), axis=1, keepdims=True)
        decay = jnp.where(incl, jnp.exp(jnp.where(incl, gc - gc_row, 0.0)), 0.0)
        gamma = jnp.exp(gc)

        kk = _dot3_nt(k, k)
        lmat = jnp.where(strict, decay * kk, 0.0) * beta
        tinv = eye_f - jnp.where(joins[0], lmat, 0.0)
        for lvl in range(1, len(joins)):
            tinv = tinv - _bdot(_bdot(tinv, jnp.where(joins[lvl], lmat, 0.0)), tinv)
        resid = (eye_f - tinv) - _dot3(lmat, tinv)
        tinv = tinv + _bdot(tinv, resid)
        rhs = jnp.concatenate([(beta * gamma) * k, beta * v], axis=1)
        wu = _dot3(tinv, rhs)
        qk = decay * _bdot_nt(q, k)

        rows = slice(r0, r0 + G)
        w_ref[0, 0, rows, :] = wu[:, :HEAD_DIM].astype(BF16)
        u0_ref[0, 0, rows, :] = wu[:, HEAD_DIM:]
        qg_ref[0, 0, rows, :] = (gamma * q).astype(BF16)
        kd_ref[0, 0, rows, :] = (jnp.exp(gc_last - gc) * k).astype(BF16)
        ge_all = jnp.broadcast_to(jnp.exp(gc_last), (G, LANES))
        for c in range(G // C):
            cr = slice(c * C, (c + 1) * C)
            qk_ref[0, 0, r0 + c * C:r0 + (c + 1) * C, :] = qk[cr, cr].astype(BF16)
            ge_ref[0, 0, pl.ds((i * (cb // G) + grp) * (G // C) + c, 1), :] = ge_all[c * C:c * C + 1, :]


def _gdn_local(P, cw, alog_pad, dtb_pad):
    b, _, s, _ = P.shape
    H, CB, C = GDN_HEADS, min(GDN_CB, s), GDN_CHUNK
    n = s // C

    def main(j0):
        return pl.BlockSpec((1, 1, CB, LANES), lambda bi, h, i: (bi, j0 + h, i, 0))

    def halo(j0):
        return pl.BlockSpec((1, 1, 8, LANES),
                            lambda bi, h, i: (bi, j0 + h, jnp.maximum(i * (CB // 8) - 1, 0), 0))

    def cws(j0):
        return pl.BlockSpec((1, GDN_CONV, LANES), lambda bi, h, i: (j0 + h, 0, 0))

    row = pl.BlockSpec((1, LANES), lambda bi, h, i: (0, 0))
    hs = lambda width: pl.BlockSpec((1, 1, CB, width), lambda bi, h, i: (bi, h, i, 0))
    return pl.pallas_call(
        functools.partial(_gdn_local_kernel, cb=CB),
        grid=(b, H, s // CB),
        in_specs=[main(J_AQ), halo(J_AQ), main(J_AK), halo(J_AK), main(J_AV), halo(J_AV),
                  pl.BlockSpec((1, 1, CB, LANES), lambda bi, h, i: (bi, J_SMALL, i, 0)),
                  cws(0), cws(8), cws(16), row, row],
        out_specs=[hs(LANES), hs(LANES), hs(C), hs(LANES), hs(LANES),
                   pl.BlockSpec((1, 1, n, LANES), lambda bi, h, i: (bi, h, 0, 0))],
        out_shape=[jax.ShapeDtypeStruct((b, H, s, LANES), BF16),
                   jax.ShapeDtypeStruct((b, H, s, LANES), F32),
                   jax.ShapeDtypeStruct((b, H, s, C), BF16),
                   jax.ShapeDtypeStruct((b, H, s, LANES), BF16),
                   jax.ShapeDtypeStruct((b, H, s, LANES), BF16),
                   jax.ShapeDtypeStruct((b, H, n, LANES), F32)],
        compiler_params=_cparams(("parallel", "parallel", "arbitrary")),
        name="gdn_local",
    )(P, P, P, P, P, P, P, cw, cw, cw, alog_pad, dtb_pad)


GDN_HB = 8


def _gdn_scan_kernel(w_ref, u0_ref, qk_ref, qg_ref, kd_ref, ge_ref, z_ref, nw_ref, o_ref, s_scr,
                     *, sb):
    C = GDN_CHUNK

    @pl.when(pl.program_id(2) == 0)
    def _():
        s_scr[...] = jnp.zeros_like(s_scr)

    nw = nw_ref[...]

    def body(n, carry):
        r = pl.multiple_of(n * C, C)
        for hh in range(GDN_HB):
            st = s_scr[hh]
            stb = st.astype(BF16)
            u = u0_ref[0, hh, pl.ds(r, C), :] - jnp.dot(w_ref[0, hh, pl.ds(r, C), :], stb,
                                                       preferred_element_type=F32)
            ub = u.astype(BF16)
            o = (jnp.dot(qg_ref[0, hh, pl.ds(r, C), :], stb, preferred_element_type=F32)
                 + jnp.dot(qk_ref[0, hh, pl.ds(r, C), :], ub, preferred_element_type=F32))
            ge = ge_ref[0, hh, pl.ds(n, 1), :]
            s_scr[hh] = ge * st + lax.dot_general(kd_ref[0, hh, pl.ds(r, C), :], ub,
                                                  (((0,), (0,)), ((), ())),
                                                  preferred_element_type=F32)
            on = o * lax.rsqrt(jnp.mean(o * o, axis=-1, keepdims=True) + EPS) * nw
            z = z_ref[0, hh, pl.ds(r, C), :]
            o_ref[0, pl.ds(r, C), hh * LANES:(hh + 1) * LANES] = (on * _silu(z)).astype(BF16)
        return carry

    lax.fori_loop(0, sb // C, body, 0)


def _gdn_scan(w, u0, qk, qg, kd, ge, P, norm_w):
    b, H, s, _ = w.shape
    C, HB = GDN_CHUNK, GDN_HB
    sb = min(1024, s)
    hs = lambda width: pl.BlockSpec((1, HB, sb, width), lambda bi, hb, i: (bi, hb, i, 0))
    return pl.pallas_call(
        functools.partial(_gdn_scan_kernel, sb=sb),
        grid=(b, H // HB, s // sb),
        in_specs=[hs(LANES), hs(LANES), hs(C), hs(LANES), hs(LANES),
                  pl.BlockSpec((1, HB, sb // C, LANES), lambda bi, hb, i: (bi, hb, i, 0)),
                  pl.BlockSpec((1, HB, sb, LANES), lambda bi, hb, i: (bi, J_AZ // HB + hb, i, 0)),
                  pl.BlockSpec((1, LANES), lambda bi, hb, i: (0, 0))],
        out_specs=pl.BlockSpec((1, sb, HB * LANES), lambda bi, hb, i: (bi, i, hb)),
        out_shape=jax.ShapeDtypeStruct((b, s, H * LANES), BF16),
        scratch_shapes=[pltpu.VMEM((HB, HEAD_DIM, HEAD_DIM), F32)],
        compiler_params=_cparams(("parallel", "parallel", "arbitrary")),
        name="gdn_scan",
    )(w, u0, qk, qg, kd, ge, P, norm_w.reshape(1, LANES))


def _pad_lanes_row(v):
    return jnp.pad(v.astype(F32), (0, LANES - v.shape[0])).reshape(1, LANES)


def _gdn(P, conv_w, a_log, dt_bias, norm_w):
    cw = jnp.transpose(conv_w.reshape(GDN_CONV, 3 * GDN_HEADS, LANES), (1, 0, 2))
    w, u0, qk, qg, kd, ge = _gdn_local(P, cw, _pad_lanes_row(a_log), _pad_lanes_row(dt_bias))
    return _gdn_scan(w, u0, qk, qg, kd, ge, P, norm_w)


NSA_TQ = 128
NSA_TK_SEL = 512
NSA_TK_WIN = 256
NSA_HC = 1
R_Q, R_KC, R_VC, R_KS, R_VS, R_KW, R_VW = 0, 16, 18, 20, 22, 24, 26
ROPE_NB = 4


def _rope_kernel(x_ref, pos_ref, invf_ref, o_ref, cos_scr, sin_scr):
    j = pl.program_id(2)
    lane = lax.broadcasted_iota(jnp.int32, cos_scr.shape, 1)

    @pl.when(j == 0)
    def _():
        ang = pos_ref[0] * invf_ref[...]
        sn = jnp.sin(ang)
        cos_scr[...] = jnp.where(lane < ROPE_DIM, jnp.cos(ang), 1.0)
        sin_scr[...] = jnp.where(lane < ROPE_HALF, -sn, jnp.where(lane < ROPE_DIM, sn, 0.0))

    is_q = j < NSA_HEADS // ROPE_NB
    scale = jnp.where(is_q, HEAD_DIM ** -0.5, 1.0)
    for k in range(ROPE_NB):
        x = x_ref[0, k]
        swapped = jnp.where(lane < ROPE_HALF, pltpu.roll(x, LANES - ROPE_HALF, axis=1),
                            pltpu.roll(x, ROPE_HALF, axis=1))
        rot = x * cos_scr[...] + swapped * sin_scr[...]
        out = rot * scale if k < 2 else jnp.where(is_q, rot, x) * scale
        o_ref[0, k] = out.astype(BF16)


def _rope(P, pos_f32, invf):
    b, _, s, _ = P.shape
    tr = min(1024, s)
    nb = ROPE_NB
    return pl.pallas_call(
        _rope_kernel,
        grid=(b, s // tr, N_NSA_BLOCKS // nb),
        in_specs=[pl.BlockSpec((1, nb, tr, LANES), lambda bi, i, j: (bi, J_BQ // nb + j, i, 0)),
                  pl.BlockSpec((1, tr, 1), lambda bi, i, j: (bi, i, 0)),
                  pl.BlockSpec((1, LANES), lambda bi, i, j: (0, 0))],
        out_specs=pl.BlockSpec((1, nb, tr, LANES), lambda bi, i, j: (bi, j, i, 0)),
        out_shape=jax.ShapeDtypeStruct((b, N_NSA_BLOCKS, s, LANES), BF16),
        scratch_shapes=[pltpu.VMEM((tr, LANES), F32), pltpu.VMEM((tr, LANES), F32)],
        compiler_params=_cparams(("parallel", "parallel", "arbitrary")),
        name="nsa_rope",
    )(P, pos_f32, invf)


def _compress_kernel(r_ref, w1_ref, w2_ref, pos_ref, o_ref):
    r = r_ref[0, 0]
    nr = r.shape[0]
    half = CMP_STRIDE * HEAD_DIM
    a = jnp.dot(r, w1_ref[0, :half, :], preferred_element_type=F32)
    bm = jnp.dot(r, w1_ref[0, half:, :], preferred_element_type=F32)
    pos8 = jnp.broadcast_to(pos_ref[0], (8, CMP_LEN * HEAD_DIM)).astype(BF16)
    pb = jnp.dot(pos8, w1_ref[0], preferred_element_type=F32)[0:1, :]
    hid = a + pltpu.roll(bm, nr - 1, axis=0) + pb
    out = jnp.dot(_gelu(hid).astype(BF16), w2_ref[0], preferred_element_type=F32)
    row = lax.broadcasted_iota(jnp.int32, out.shape, 0)
    o_ref[0, 0] = jnp.where(row < nr - 1, out, 0.0).astype(BF16)


def _compress(rows, w1, w2, pos):
    b, _, nr, width = rows.shape
    return pl.pallas_call(
        _compress_kernel,
        grid=(b, 4),
        in_specs=[pl.BlockSpec((1, 1, nr, width), lambda bi, j: (bi, j, 0, 0)),
                  pl.BlockSpec((1, CMP_LEN * HEAD_DIM, CMP_HIDDEN), lambda bi, j: (j // 2, 0, 0)),
                  pl.BlockSpec((1, CMP_HIDDEN, HEAD_DIM), lambda bi, j: (j // 2, 0, 0)),
                  pl.BlockSpec((1, 1, CMP_LEN * HEAD_DIM), lambda bi, j: (j // 2, 0, 0))],
        out_specs=pl.BlockSpec((1, 1, nr, HEAD_DIM), lambda bi, j: (bi, j, 0, 0)),
        out_shape=jax.ShapeDtypeStruct((b, 4, nr, HEAD_DIM), BF16),
        compiler_params=_cparams(("parallel", "arbitrary")),
        name="nsa_compress",
    )(rows, w1, w2, pos)


def _nsa_kernel(q_ref, kc_ref, vc_ref, ks_ref, vs_ref, kw_ref, vw_ref, gate_ref, ovt_ref, e_ref,
                o_ref, m_scr, l_scr, acc_scr, os_scr, *, s_len):
    i = pl.program_id(2)
    TQ, HP = NSA_TQ, NSA_HPG
    R = HP * TQ
    nsb = s_len // SLC_LEN
    ncp = s_len // CMP_STRIDE
    n_sel = min(SLC_TOPK, nsb)
    s0 = i * TQ
    q2 = q_ref[0].reshape(R, HEAD_DIM)

    sc = _bdot_nt(q2, kc_ref[0, 0])
    tq_r = s0 + (lax.broadcasted_iota(jnp.int32, (R, ncp), 0) & (TQ - 1))
    ncol = lax.broadcasted_iota(jnp.int32, (R, ncp), 1)
    valid = jnp.logical_and(ncol * CMP_STRIDE + (CMP_LEN - 1) <= tq_r, ncol < ncp - 1)
    scm = jnp.where(valid, sc, NEG)
    e = jnp.where(valid, jnp.exp(scm - jnp.max(scm, axis=1, keepdims=True)), 0.0)
    p_c = e / jnp.maximum(jnp.sum(e, axis=1, keepdims=True), 1e-30)
    o_c = _bdot(p_c, vc_ref[0, 0])
    psum = jnp.sum(p_c.reshape(HP, TQ, ncp), axis=0)
    imp = _fdot_nt(ovt_ref[...], psum)
    blk = lax.broadcasted_iota(jnp.int32, (nsb, TQ), 0)
    tq_l = s0 + lax.broadcasted_iota(jnp.int32, (nsb, TQ), 1)
    cur = tq_l // SLC_LEN
    forced = jnp.logical_or(blk == 0, jnp.logical_or(blk == cur, blk == cur - 1))
    imp = jnp.where(forced, BIG, imp)
    imp = jnp.where(blk * SLC_LEN <= tq_l, imp, NEG)
    rank = jnp.zeros((nsb, TQ), jnp.int32)
    for j in range(nsb):
        rj = imp[j:j + 1, :]
        beats = jnp.logical_or(rj > imp, jnp.logical_and(rj == imp, blk > j))
        rank = rank + jnp.where(beats, 1, 0)
    sel_t = jnp.where(rank < n_sel, 1.0, 0.0).astype(F32)
    if nsb < LANES:
        sel_t = jnp.concatenate([sel_t, jnp.zeros((LANES - nsb, TQ), F32)], axis=0)
    sel = jnp.transpose(sel_t).astype(BF16)

    ones_blk = jnp.ones((TQ, HEAD_DIM), BF16)
    HC = NSA_HC

    def flash(k_ref, v_ref, tk, lo, hi, mask_fn):
        m_scr[...] = jnp.full(m_scr.shape, NEG, F32)
        l_scr[...] = jnp.zeros(l_scr.shape, F32)
        acc_scr[...] = jnp.zeros(acc_scr.shape, F32)
        qpos = s0 + lax.broadcasted_iota(jnp.int32, (TQ, tk), 0)
        kofs = lax.broadcasted_iota(jnp.int32, (TQ, tk), 1)
        ones_v = jnp.concatenate([ones_blk] * (tk // TQ), axis=0)

        def body(j, carry):
            r = pl.multiple_of(j * tk, tk)
            bias = jnp.where(mask_fn(r, r + kofs, qpos), 0.0, NEG)
            kb = k_ref[0, 0, pl.ds(r, tk), :]
            vb = jnp.concatenate([v_ref[0, 0, pl.ds(r, tk), :], ones_v], axis=1)
            for c in range(HP // HC):
                rows = slice(c * HC * TQ, (c + 1) * HC * TQ)
                s = _bdot_nt(q_ref[0, c * HC:(c + 1) * HC].reshape(HC * TQ, HEAD_DIM), kb)
                s = (s.reshape(HC, TQ, tk) + bias[None]).reshape(HC * TQ, tk)
                m_prev = m_scr[rows, :]
                m_new = jnp.maximum(m_prev, jnp.max(s, axis=1, keepdims=True))
                alpha = jnp.exp(m_prev - m_new)
                p = jnp.exp(s - jnp.concatenate([m_new] * (tk // LANES), axis=1)).astype(BF16)
                pv = jnp.dot(p, vb, preferred_element_type=F32)
                l_scr[rows, :] = alpha * l_scr[rows, :] + pv[:, HEAD_DIM:]
                acc_scr[rows, :] = alpha * acc_scr[rows, :] + pv[:, :HEAD_DIM]
                m_scr[rows, :] = m_new
            return carry

        lax.fori_loop(lo, hi, body, 0)

    tks = min(NSA_TK_SEL, s_len)

    def sel_mask(r, kpos, qpos):
        chosen = jnp.dot(sel, e_ref[:, pl.ds(r, tks)], preferred_element_type=F32)
        return jnp.logical_and(chosen > 0.5, kpos <= qpos)

    flash(ks_ref, vs_ref, tks, 0, (s0 + TQ - 1) // tks + 1, sel_mask)
    os_scr[...] = acc_scr[...] / l_scr[...]

    def win_mask(r, kpos, qpos):
        return jnp.logical_and(kpos <= qpos, kpos > qpos - WIN)

    tkw = min(NSA_TK_WIN, s_len)
    flash(kw_ref, vw_ref, tkw, jnp.maximum(s0 - WIN + 1, 0) // tkw, (s0 + TQ - 1) // tkw + 1, win_mask)

    gates = _sigmoid(gate_ref[0, 0])
    for hh in range(HP):
        rows = slice(hh * TQ, (hh + 1) * TQ)
        o_w = acc_scr[rows, :] / l_scr[rows, :]
        out = (gates[:, 3 * hh:3 * hh + 1] * o_c[rows, :]
               + gates[:, 3 * hh + 1:3 * hh + 2] * os_scr[rows, :]
               + gates[:, 3 * hh + 2:3 * hh + 3] * o_w)
        o_ref[0, :, hh * HEAD_DIM:(hh + 1) * HEAD_DIM] = out.astype(BF16)


def _nsa_attend(rp, cmp_kv, P, ovt, expand):
    b, _, s, _ = rp.shape
    TQ, HP, G = NSA_TQ, NSA_HPG, NSA_GROUPS
    ncp = cmp_kv.shape[2]
    R = HP * TQ
    full = lambda j0: pl.BlockSpec((1, 1, s, HEAD_DIM), lambda bi, g, i: (bi, j0 + g, 0, 0))
    cmp_spec = lambda j0: pl.BlockSpec((1, 1, ncp, HEAD_DIM), lambda bi, g, i: (bi, j0 + g, 0, 0))
    return pl.pallas_call(
        functools.partial(_nsa_kernel, s_len=s),
        grid=(b, G, s // TQ),
        in_specs=[pl.BlockSpec((1, HP, TQ, HEAD_DIM), lambda bi, g, i: (bi, g, i, 0)),
                  cmp_spec(0), cmp_spec(2),
                  full(R_KS), full(R_VS), full(R_KW), full(R_VW),
                  pl.BlockSpec((1, 1, TQ, LANES), lambda bi, g, i: (bi, J_GATE + g, i, 0)),
                  pl.BlockSpec(ovt.shape, lambda bi, g, i: (0, 0)),
                  pl.BlockSpec(expand.shape, lambda bi, g, i: (0, 0))],
        out_specs=pl.BlockSpec((1, TQ, HP * HEAD_DIM), lambda bi, g, i: (bi, i, g)),
        out_shape=jax.ShapeDtypeStruct((b, s, NSA_HEADS * HEAD_DIM), BF16),
        scratch_shapes=[pltpu.VMEM((R, LANES), F32), pltpu.VMEM((R, LANES), F32),
                        pltpu.VMEM((R, HEAD_DIM), F32), pltpu.VMEM((R, HEAD_DIM), F32)],
        compiler_params=_cparams(("parallel", "parallel", "arbitrary")),
        name="nsa_attend",
    )(rp, cmp_kv, cmp_kv, rp, rp, rp, rp, P, ovt, expand)


def _nsa(P, positions, cmp_pos_k, cmp_w1_k, cmp_w2_k, cmp_pos_v, cmp_w1_v, cmp_w2_v):
    b, _, s, _ = P.shape
    lanes = np.arange(LANES)
    invf = np.where(lanes < ROPE_DIM, ROPE_THETA ** (-(lanes % ROPE_HALF) / ROPE_HALF), 0.0)
    rp = _rope(P, positions.astype(F32).reshape(b, s, 1), jnp.asarray(invf, F32).reshape(1, LANES))

    ncp = s // CMP_STRIDE
    rows = rp[:, R_KC:R_KC + 4].reshape(b, 4, ncp, CMP_STRIDE * HEAD_DIM)
    w1 = jnp.stack([cmp_w1_k, cmp_w1_v]).astype(BF16)
    w2 = jnp.stack([cmp_w2_k, cmp_w2_v]).astype(BF16)
    pos = jnp.stack([cmp_pos_k, cmp_pos_v]).reshape(2, 1, CMP_LEN * HEAD_DIM)
    cmp_kv = _compress(rows, w1, w2, pos)

    nsb = s // SLC_LEN
    cmp_start = np.arange(ncp) * CMP_STRIDE
    slc_start = np.arange(nsb) * SLC_LEN
    ov = ((cmp_start[:, None] < slc_start[None, :] + SLC_LEN)
          & (cmp_start[:, None] + CMP_LEN > slc_start[None, :])
          & (np.arange(ncp)[:, None] < ncp - 1))
    ovt = jnp.asarray(ov.T.astype(np.float32))
    expand = (np.arange(LANES)[:, None] == (np.arange(s)[None, :] // SLC_LEN))
    expand = jnp.asarray(expand.astype(np.float32), BF16)
    return _nsa_attend(rp, cmp_kv, P, ovt, expand)


MERGE_TN = 512


def _merge_kernel(oa_ref, ob_ref, wg_ref, wn_ref, ma_ref, mb_ref, y_ref):
    ya = jnp.dot(oa_ref[0], wg_ref[...], preferred_element_type=F32)
    yb = jnp.dot(ob_ref[0], wn_ref[...], preferred_element_type=F32)
    for k in range(MERGE_TN // LANES):
        cols = slice(k * LANES, (k + 1) * LANES)
        y = _sigmoid(ma_ref[0, k]) * ya[:, cols] + _sigmoid(mb_ref[0, k]) * yb[:, cols]
        y_ref[0, :, cols] = y.astype(BF16)


def _merge(o_a, o_b, wg, wn, P):
    b, s, da = o_a.shape
    db = o_b.shape[2]
    d = wg.shape[1]
    tm, tn = min(1024, s), MERGE_TN
    nk = tn // LANES
    return pl.pallas_call(
        _merge_kernel,
        grid=(b, s // tm, d // tn),
        in_specs=[pl.BlockSpec((1, tm, da), lambda bi, i, j: (bi, i, 0)),
                  pl.BlockSpec((1, tm, db), lambda bi, i, j: (bi, i, 0)),
                  pl.BlockSpec((da, tn), lambda bi, i, j: (0, j)),
                  pl.BlockSpec((db, tn), lambda bi, i, j: (0, j)),
                  pl.BlockSpec((1, nk, tm, LANES), lambda bi, i, j: (bi, J_MA // nk + j, i, 0)),
                  pl.BlockSpec((1, nk, tm, LANES), lambda bi, i, j: (bi, J_MB // nk + j, i, 0))],
        out_specs=pl.BlockSpec((1, tm, tn), lambda bi, i, j: (bi, i, j)),
        out_shape=jax.ShapeDtypeStruct((b, s, d), BF16),
        compiler_params=_cparams(("parallel", "parallel", "arbitrary")),
        name="merge",
    )(o_a, o_b, wg, wn, P, P)


def _outproj_kernel(y_ref, w_ref, x_ref, g_ref, o_ref):
    o_ref[0] = x_ref[0] + g_ref[0] * jnp.dot(y_ref[0], w_ref[...], preferred_element_type=F32)


def _outproj(y, w, x, g):
    b, s, d = x.shape
    tm, tn = min(1024, s), 512
    return pl.pallas_call(
        _outproj_kernel,
        grid=(b, s // tm, d // tn),
        in_specs=[pl.BlockSpec((1, tm, d), lambda bi, i, j: (bi, i, 0)),
                  pl.BlockSpec((d, tn), lambda bi, i, j: (0, j)),
                  pl.BlockSpec((1, tm, tn), lambda bi, i, j: (bi, i, j)),
                  pl.BlockSpec((1, 1, tn), lambda bi, i, j: (bi, 0, j))],
        out_specs=pl.BlockSpec((1, tm, tn), lambda bi, i, j: (bi, i, j)),
        out_shape=jax.ShapeDtypeStruct((b, s, d), F32),
        compiler_params=_cparams(("parallel", "parallel", "arbitrary")),
        name="outproj",
    )(y, w, x, g.reshape(b, 1, d))


PEER_TB = 512
PEER_TG = 128
PEER_AHEAD = 26
PEER_SLOTS = 32
PEER_SEL = PEER_HEADS * PEER_TOPK
MIN_F32 = -3.0e38


def _topk_rows(vals, payload=None):
    nrow = vals.shape[0]
    rid = lax.broadcasted_iota(jnp.int32, vals.shape, 0)
    out_v, out_i = [], []
    for _ in range(PEER_TOPK):
        m = jnp.max(vals, axis=0, keepdims=True)
        idx = jnp.min(jnp.where(vals == m, rid, nrow), axis=0, keepdims=True)
        hit = rid == idx
        out_v.append(m)
        if payload is None:
            out_i.append(idx)
        else:
            out_i.append(jnp.sum(jnp.where(hit, payload, 0), axis=0, keepdims=True))
        vals = jnp.where(hit, MIN_F32, vals)
    return jnp.concatenate(out_v, axis=0), jnp.concatenate(out_i, axis=0)


def _peer_topk_kernel(qh_ref, k1_ref, k2_ref, eidx_ref, gw_ref):
    def head(h, carry):
        s1 = _fdot_nt(k1_ref[h], qh_ref[0, 2 * h])
        s2 = _fdot_nt(k2_ref[h], qh_ref[0, 2 * h + 1])
        v1, i1 = _topk_rows(s1)
        v2, i2 = _topk_rows(s2)
        keep = [PEER_TOPK // (a + 1) for a in range(PEER_TOPK)]
        npad = (-sum(keep)) % 8
        cand = jnp.concatenate([v1[a:a + 1, :] + v2[0:keep[a], :] for a in range(PEER_TOPK)]
                               + [jnp.full((npad, v1.shape[1]), MIN_F32, F32)], axis=0)
        cidx = jnp.concatenate([i1[a:a + 1, :] * PEER_NKEYS + i2[0:keep[a], :] for a in range(PEER_TOPK)]
                               + [jnp.zeros((npad, v1.shape[1]), jnp.int32)], axis=0)
        sc, eidx = _topk_rows(cand, cidx)
        ex = jnp.exp(sc - sc[0:1, :])
        eidx_ref[h] = eidx
        gw_ref[h] = ex / jnp.sum(ex, axis=0, keepdims=True)
        return carry

    lax.fori_loop(0, PEER_HEADS, head, 0)


def _peer_topk(qh, keys1, keys2):
    b, nb, s, _ = qh.shape
    tb = min(PEER_TB, s)
    nt = s // tb
    kspec = pl.BlockSpec(keys1.shape, lambda bi, i: (0, 0, 0))
    ospec = pl.BlockSpec((PEER_HEADS, PEER_TOPK, tb), lambda bi, i: (0, 0, bi * nt + i))
    return pl.pallas_call(
        _peer_topk_kernel,
        grid=(b, nt),
        in_specs=[pl.BlockSpec((1, nb, tb, LANES), lambda bi, i: (bi, 0, i, 0)), kspec, kspec],
        out_specs=[ospec, ospec],
        out_shape=[jax.ShapeDtypeStruct((PEER_HEADS, PEER_TOPK, b * s), jnp.int32),
                   jax.ShapeDtypeStruct((PEER_HEADS, PEER_TOPK, b * s), F32)],
        compiler_params=_cparams(("parallel", "parallel")),
        name="peer_topk",
    )(qh, keys1, keys2)


PEER_BANKS = 4


def _peer_gather_kernel(idx_ref, idxs_ref, gw_ref, x_ref, uv_hbm, o_ref, b0, b1, b2, b3, a_scr, sem, *, tg):
    NS, D, NB = PEER_SEL, PEER_AHEAD, PEER_BANKS
    banks = (b0, b1, b2, b3)
    Q = b0.shape[0]
    RPE = uv_hbm.shape[1]
    SUBW = RPE // 2
    half = SUBW * LANES
    RT = 8

    def place(g, k, off):
        kk = k + off
        return banks[kk % NB], kk % NB, (g + kk // NB) & (Q - 1)

    def wait_rows(g, k):
        bank, kb, q = place(g, k, 0)
        pltpu.make_async_copy(uv_hbm.at[pl.ds(0, NS)], bank.at[q], sem.at[kb, q]).wait()

    rid = lax.broadcasted_iota(jnp.int32, (NS, NS), 0)
    cid = lax.broadcasted_iota(jnp.int32, (NS, NS), 1)
    eye = rid == cid
    ones = jnp.ones((LANES, LANES), BF16)

    def lane_sum_rep(m):
        hi, lo = _split_bf16(m)
        return (jnp.dot(hi, ones, preferred_element_type=F32)
                + jnp.dot(lo, ones, preferred_element_type=F32))

    def low(w):
        return pltpu.bitcast(w << 16, F32)

    def high(w):
        return pltpu.bitcast(w & jnp.uint32(0xFFFF0000), F32)

    def phase(g, k, do_dot, do_prev):
        t = g * NB + k
        if do_dot:
            wait_rows(g, k)
            xrow = x_ref[pl.ds(t, 1), :]
            bank_d, _, q_d = place(g, k, 0)
            acc = [jnp.zeros((RT, LANES), F32) for _ in range(NS // RT)]
            bank_i, k_i, q_i = place(g, k, D)
            base_i = t * NS
        if do_prev:
            bank_s, _, q_s = place(g, k, -1)
            par_s = (t - 1) & 1
            out_lo, out_hi = [], []
        for sg in range(SUBW):
            if do_dot:
                x_lo = jnp.broadcast_to(xrow[:, sg * LANES:(sg + 1) * LANES], (RT, LANES))
                x_hi = jnp.broadcast_to(xrow[:, half + sg * LANES:half + (sg + 1) * LANES], (RT, LANES))
            if do_prev:
                p_lo = jnp.zeros((RT, LANES), F32)
                p_hi = jnp.zeros((RT, LANES), F32)
            for rt in range(NS // RT):
                rows = slice(rt * RT, (rt + 1) * RT)
                if do_dot:
                    j = sg * (NS // RT) + rt
                    pltpu.make_async_copy(uv_hbm.at[idxs_ref[base_i + j]],
                                          bank_i.at[q_i, :, j, :], sem.at[k_i, q_i]).start(priority=j % 2)
                    w = bank_d[q_d, sg, rows, :]
                    acc[rt] = acc[rt] + low(w) * x_lo + high(w) * x_hi
                if do_prev:
                    w = bank_s[q_s, SUBW + sg, rows, :]
                    c = a_scr[par_s, rows, :]
                    p_lo = p_lo + low(w) * c
                    p_hi = p_hi + high(w) * c
            if do_prev:
                out_lo.append(jnp.sum(p_lo, axis=0, keepdims=True))
                out_hi.append(jnp.sum(p_hi, axis=0, keepdims=True))
        if do_dot:
            gw_rep = lane_sum_rep(jnp.where(eye, gw_ref[pl.ds(t, 1), :], 0.0))
            a_scr[t & 1] = gw_rep * _gelu(lane_sum_rep(jnp.concatenate(acc, axis=0)))
        if do_prev:
            o_ref[pl.ds(t - 1, 1), :] = jnp.concatenate(out_lo + out_hi, axis=1)

    step = pl.program_id(0)
    last = pl.num_programs(0) - 1

    @pl.when(step == 0)
    def _():
        for k in range(NB):
            def ramp(g, carry, k=k):
                bank, kb, q = place(g, k, 0)

                def body(j, c):
                    pltpu.make_async_copy(uv_hbm.at[idx_ref[(g * NB + k) * NS + j]],
                                          bank.at[q, :, j, :], sem.at[kb, q]).start()
                    return c
                return lax.fori_loop(0, NS, body, carry)
            lax.fori_loop(0, (D - k + NB - 1) // NB, ramp, 0)

    phase(0, 0, True, False)
    for k in range(1, NB):
        phase(0, k, True, True)

    def group(g, carry):
        for k in range(NB):
            phase(g, k, True, True)
        return carry
    lax.fori_loop(1, tg // NB, group, 0)
    phase(tg // NB, 0, False, True)

    @pl.when(step == last)
    def _():
        for r in range(D):
            wait_rows(tg // NB, r)


def _peer_gather(eidx, gw, h2r, uvc):
    t = eidx.shape[0]
    tg = min(PEER_TG, t)
    d = h2r.shape[1]
    rpe = uvc.shape[1]
    nq = PEER_SLOTS // PEER_BANKS
    flat = eidx.reshape(-1)
    shifted = jnp.roll(flat, -PEER_AHEAD * PEER_SEL)
    bank = pltpu.VMEM((nq, rpe, PEER_SEL, LANES), jnp.uint32)
    return pl.pallas_call(
        functools.partial(_peer_gather_kernel, tg=tg),
        grid=(t // tg,),
        in_specs=[pl.BlockSpec((tg * PEER_SEL,), lambda i: (0,), memory_space=pltpu.SMEM),
                  pl.BlockSpec((tg * PEER_SEL,), lambda i: (i,), memory_space=pltpu.SMEM),
                  pl.BlockSpec((tg, PEER_SEL), lambda i: (i, 0)),
                  pl.BlockSpec((tg, d), lambda i: (i, 0)),
                  pl.BlockSpec(memory_space=pl.ANY)],
        out_specs=pl.BlockSpec((tg, d), lambda i: (i, 0)),
        out_shape=jax.ShapeDtypeStruct((t, d), F32),
        scratch_shapes=[bank, bank, bank, bank,
                        pltpu.VMEM((2, PEER_SEL, LANES), F32),
                        pltpu.SemaphoreType.DMA((PEER_BANKS, nq))],
        compiler_params=_cparams(("arbitrary",)),
        name="peer_gather",
    )(flat, shifted, gw, h2r, uvc)


PACK_EB = 256


def _pack_kernel(u_ref, v_ref, o_ref):
    half = u_ref.shape[1] // 2
    for tab, ref in enumerate((u_ref, v_ref)):
        for sg in range(half // LANES):
            lo = ref[:, sg * LANES:(sg + 1) * LANES].astype(BF16).astype(F32)
            hi = ref[:, half + sg * LANES:half + (sg + 1) * LANES].astype(BF16).astype(F32)
            word = (pltpu.bitcast(lo, jnp.uint32) >> 16) | (pltpu.bitcast(hi, jnp.uint32) & jnp.uint32(0xFFFF0000))
            o_ref[:, tab * (half // LANES) + sg, :] = word


def _pack_tables(u, v):
    e, d = u.shape
    eb = min(PACK_EB, e)
    return pl.pallas_call(
        _pack_kernel,
        grid=(e // eb,),
        in_specs=[pl.BlockSpec((eb, d), lambda i: (i, 0)), pl.BlockSpec((eb, d), lambda i: (i, 0))],
        out_specs=pl.BlockSpec((eb, d // LANES, LANES), lambda i: (i, 0, 0)),
        out_shape=jax.ShapeDtypeStruct((e, d // LANES, LANES), jnp.uint32),
        compiler_params=_cparams(("parallel",)),
        name="peer_pack",
    )(u, v)


def _peer(x1, norm_w, sc, sh, wq, keys1, keys2, u, v):
    b, s, d = x1.shape
    qh, h2 = _normproj(x1, norm_w, sc, sh, wq.astype(BF16), tm=1024, tn=512, emit_h=True)
    eidx, gw = _peer_topk(qh, keys1, keys2)
    eidx = jnp.transpose(eidx, (2, 0, 1)).reshape(b * s, PEER_SEL)
    gw = jnp.transpose(gw, (2, 0, 1)).reshape(b * s, PEER_SEL)
    uvc = _pack_tables(u, v)
    out = _peer_gather(eidx, gw, h2.reshape(b * s, d), uvc)
    return out.reshape(b, s, d)


def _final_kernel(x_ref, p_ref, g_ref, w_ref, o_ref):
    x = x_ref[0] + g_ref[0] * p_ref[0]
    o_ref[0] = x * lax.rsqrt(jnp.mean(x * x, axis=-1, keepdims=True) + EPS) * w_ref[...]


def _final(x1, peer, g2, wf):
    b, s, d = x1.shape
    tm = min(512, s)
    blk = pl.BlockSpec((1, tm, d), lambda bi, i: (bi, i, 0))
    return pl.pallas_call(
        _final_kernel,
        grid=(b, s // tm),
        in_specs=[blk, blk, pl.BlockSpec((1, 1, d), lambda bi, i: (bi, 0, 0)),
                  pl.BlockSpec((1, d), lambda bi, i: (0, 0))],
        out_specs=blk,
        out_shape=jax.ShapeDtypeStruct((b, s, d), F32),
        compiler_params=_cparams(("parallel", "parallel")),
        name="final_norm",
    )(x1, peer, g2.reshape(b, 1, d), wf.reshape(1, d))


def _pad_rows(a, mult=8):
    pad = (-a.shape[0]) % mult
    return jnp.pad(a, ((0, pad), (0, 0)))


def _permute_w_in(w):
    o_aa, o_bq, o_bg, o_ma, o_end = 4096, 4112, 7696, 7744, 11840
    d = w.shape[0]
    z = lambda n: jnp.zeros((d, n), w.dtype)
    hg = 3 * NSA_HPG
    cols = [w[:, 0:o_aa], w[:, o_bq:o_bg], w[:, o_ma:o_end],
            w[:, o_aa:o_bq], z(LANES - 16),
            w[:, o_bg:o_bg + hg], z(LANES - hg),
            w[:, o_bg + hg:o_ma], z(LANES - hg),
            z(LANES)]
    return jnp.concatenate(cols, axis=1).astype(BF16)


def kernel(x, c, positions, ada_w, ada_b, norm1_w, norm2_w, w_in, gdn_conv_w, gdn_A_log, gdn_dt_bias, gdn_norm_w, cmp_pos_k, cmp_w1_k, cmp_w2_k, cmp_pos_v, cmp_w1_v, cmp_w2_v, w_branch_gdn, w_branch_nsa, w_out, peer_wq, peer_keys1, peer_keys2, peer_u, peer_v, final_norm_w):
    b, s, d = x.shape
    l = 0
    mod = _ada(_pad_rows(c), ada_w[l], ada_b[l])[:b]
    sh1, sc1, g1, sh2, sc2, g2 = jnp.split(mod, 6, axis=-1)
    P = _normproj(x, norm1_w[l], sc1, sh1, _permute_w_in(w_in[l]), tm=1024, tn=1024, emit_h=False)
    o_a = _gdn(P, gdn_conv_w[l], gdn_A_log[l], gdn_dt_bias[l], gdn_norm_w[l])
    o_b = _nsa(P, positions, cmp_pos_k[l], cmp_w1_k[l], cmp_w2_k[l], cmp_pos_v[l], cmp_w1_v[l], cmp_w2_v[l])
    y = _merge(o_a, o_b, w_branch_gdn[l].astype(BF16), w_branch_nsa[l].astype(BF16), P)
    x1 = _outproj(y, w_out[l].astype(BF16), x, g1)
    peer = _peer(x1, norm2_w[l], sc2, sh2, peer_wq[l], peer_keys1[l], peer_keys2[l], peer_u[l], peer_v[l])
    return _final(x1, peer, g2, final_norm_w)
```

```python
import functools
import math

import numpy as np
import jax
import jax.numpy as jnp
from jax import lax
from jax.experimental import pallas as pl
from jax.experimental.pallas import tpu as pltpu

F32 = jnp.float32
BF16 = jnp.bfloat16
HI = lax.Precision.HIGHEST

LANES = 128
VMEM_LIMIT = 56 * 1024 * 1024

EPS = 1e-6
ROPE_THETA = 500000.0
HEAD_DIM = 128
ROPE_DIM = HEAD_DIM // 4
ROPE_HALF = ROPE_DIM // 2

GDN_HEADS = 8
GDN_CONV = 4
GDN_CHUNK = 64

NSA_HEADS = 16
NSA_GROUPS = 2
NSA_HPG = NSA_HEADS // NSA_GROUPS
CMP_LEN = 32
CMP_STRIDE = 16
CMP_HIDDEN = 256
SLC_LEN = 64
SLC_TOPK = 16
WIN = 512

PEER_HEADS = 8
PEER_NKEYS = 128
PEER_TOPK = 16

NEG = -1e30
BIG = 1e9

J_AQ, J_AK, J_AV, J_AZ = 0, 8, 16, 24
J_BQ = 32
N_NSA_BLOCKS = 28
J_MA, J_MB = 60, 76
J_SMALL = 92
J_GATE = 93


def _cparams(sem):
    return pltpu.CompilerParams(dimension_semantics=sem, vmem_limit_bytes=VMEM_LIMIT)


def _bdot(a, b):
    return jnp.dot(a.astype(BF16), b.astype(BF16), preferred_element_type=F32)


def _bdot_nt(a, b):
    return lax.dot_general(a.astype(BF16), b.astype(BF16), (((1,), (1,)), ((), ())),
                           preferred_element_type=F32)


def _split_bf16(a):
    hi = a.astype(BF16)
    return hi, (a - hi.astype(F32)).astype(BF16)


def _dot3(a, b, exact_a=False):
    bh, bl = _split_bf16(b)
    dot = functools.partial(jnp.dot, preferred_element_type=F32)
    if exact_a:
        return dot(a, bh) + dot(a, bl)
    ah, al = _split_bf16(a)
    return dot(ah, bh) + dot(al, bh) + dot(ah, bl)


def _dot3_nt(a, b):
    ah, al = _split_bf16(a)
    bh, bl = _split_bf16(b)
    dot = functools.partial(lax.dot_general, dimension_numbers=(((1,), (1,)), ((), ())),
                            preferred_element_type=F32)
    return dot(ah, bh) + dot(al, bh) + dot(ah, bl)


def _fdot(a, b):
    return jnp.dot(a, b, precision=HI, preferred_element_type=F32)


def _fdot_nt(a, b):
    return lax.dot_general(a, b, (((1,), (1,)), ((), ())), precision=HI,
                           preferred_element_type=F32)


def _sigmoid(x):
    return 1.0 / (1.0 + jnp.exp(-x))


def _silu(x):
    return x * _sigmoid(x)


def _gelu(x):
    return 0.5 * x * (1.0 + jnp.tanh(math.sqrt(2.0 / math.pi) * (x + 0.044715 * (x * x * x))))


def _softplus(x):
    return jnp.maximum(x, 0.0) + jnp.log(1.0 + jnp.exp(-jnp.abs(x)))


def _ada_kernel(c_ref, w_ref, b_ref, o_ref):
    o_ref[...] = _fdot(_silu(c_ref[...]), w_ref[...]) + b_ref[...]


def _ada(c_pad, ada_w, ada_b):
    m, d = c_pad.shape
    n = ada_w.shape[1]
    tn = 1024
    return pl.pallas_call(
        _ada_kernel,
        grid=(n // tn,),
        in_specs=[pl.BlockSpec((m, d), lambda j: (0, 0)),
                  pl.BlockSpec((d, tn), lambda j: (0, j)),
                  pl.BlockSpec((1, tn), lambda j: (0, j))],
        out_specs=pl.BlockSpec((m, tn), lambda j: (0, j)),
        out_shape=jax.ShapeDtypeStruct((m, n), F32),
        compiler_params=_cparams(("parallel",)),
        name="ada_mod",
    )(c_pad, ada_w, ada_b.reshape(1, n))


def _normproj_kernel(x_ref, nw_ref, sc_ref, sh_ref, w_ref, *rest, nk, emit_h):
    if emit_h:
        o_ref, h_out_ref, h_scr = rest
    else:
        o_ref, h_scr = rest

    @pl.when(pl.program_id(2) == 0)
    def _():
        x = x_ref[0]
        ms = jnp.mean(x * x, axis=-1, keepdims=True)
        y = x * lax.rsqrt(ms + EPS) * nw_ref[...]
        h = y * (1.0 + sc_ref[0]) + sh_ref[0]
        h_scr[...] = h.astype(BF16)
        if emit_h:
            h_out_ref[0] = h

    acc = jnp.dot(h_scr[...], w_ref[...], preferred_element_type=F32)
    for k in range(nk):
        o_ref[0, k] = acc[:, k * LANES:(k + 1) * LANES]


def _normproj(x, nw, sc, sh, w_bf16, *, tm, tn, emit_h):
    b, s, d = x.shape
    n = w_bf16.shape[1]
    tm = min(tm, s)
    nk = tn // LANES
    out_shape = [jax.ShapeDtypeStruct((b, n // LANES, s, LANES), F32)]
    out_specs = [pl.BlockSpec((1, nk, tm, LANES), lambda bi, i, j: (bi, j, i, 0))]
    if emit_h:
        out_shape.append(jax.ShapeDtypeStruct((b, s, d), F32))
        out_specs.append(pl.BlockSpec((1, tm, d), lambda bi, i, j: (bi, i, 0)))
    res = pl.pallas_call(
        functools.partial(_normproj_kernel, nk=nk, emit_h=emit_h),
        grid=(b, s // tm, n // tn),
        in_specs=[pl.BlockSpec((1, tm, d), lambda bi, i, j: (bi, i, 0)),
                  pl.BlockSpec((1, d), lambda bi, i, j: (0, 0)),
                  pl.BlockSpec((1, 1, d), lambda bi, i, j: (bi, 0, 0)),
                  pl.BlockSpec((1, 1, d), lambda bi, i, j: (bi, 0, 0)),
                  pl.BlockSpec((d, tn), lambda bi, i, j: (0, j))],
        out_specs=out_specs,
        out_shape=out_shape,
        scratch_shapes=[pltpu.VMEM((tm, d), BF16)],
        compiler_params=_cparams(("parallel", "parallel", "arbitrary")),
        name="normproj_h" if emit_h else "normproj",
    )(x, nw.reshape(1, d), sc.reshape(b, 1, d), sh.reshape(b, 1, d), w_bf16)
    return res if emit_h else res[0]


GDN_CB = 1024
GDN_GB = 256


def _gdn_local_kernel(q_ref, qh_ref, k_ref, kh_ref, v_ref, vh_ref, sm_ref,
                      cwq_ref, cwk_ref, cwv_ref, alog_ref, dtb_ref,
                      w_ref, u0_ref, qk_ref, qg_ref, kd_ref, ge_ref, *, cb):
    h = pl.program_id(1)
    i = pl.program_id(2)
    C, G = GDN_CHUNK, GDN_GB
    first = (i == 0)

    def conv_act(main_ref, halo_ref, cw_ref, r0):
        if r0 == 0:
            prev = jnp.where(first, 0.0, halo_ref[0, 0])
        else:
            prev = main_ref[0, 0, r0 - 8:r0, :]
        ext = jnp.concatenate([prev, main_ref[0, 0, r0:r0 + G, :]], axis=0)
        w = cw_ref[0]
        y = w[0:1, :] * ext[5:5 + G, :]
        for j in range(1, GDN_CONV):
            y = y + w[j:j + 1, :] * ext[5 + j:5 + j + G, :]
        return _silu(y)

    rid = lax.broadcasted_iota(jnp.int32, (G, G), 0)
    cid = lax.broadcasted_iota(jnp.int32, (G, G), 1)
    same = (rid // C) == (cid // C)
    incl = jnp.logical_and(same, rid >= cid)
    strict = jnp.logical_and(same, rid > cid)
    eye = rid == cid
    is_last = cid == (rid // C) * C + (C - 1)
    joins = []
    bs = 1
    while bs < C:
        joins.append(jnp.logical_and(jnp.logical_and((rid // (2 * bs)) == (cid // (2 * bs)),
                                                     (rid & (2 * bs - 1)) >= bs),
                                     (cid & (2 * bs - 1)) < bs))
        bs *= 2
    tri = jnp.where(incl, 1.0, 0.0).astype(BF16)
    eye_f = jnp.where(eye, 1.0, 0.0).astype(F32)
    lane = lax.broadcasted_iota(jnp.int32, (G, LANES), 1)
    neg_a = -jnp.exp(alog_ref[...])
    dtb = dtb_ref[...]

    for grp in range(cb // G):
        r0 = grp * G
        q = conv_act(q_ref, qh_ref, cwq_ref, r0)
        k = conv_act(k_ref, kh_ref, cwk_ref, r0)
        v = conv_act(v_ref, vh_ref, cwv_ref, r0)
        q = q * lax.rsqrt(jnp.sum(q * q, axis=-1, keepdims=True) + EPS) * (HEAD_DIM ** -0.5)
        k = k * lax.rsqrt(jnp.sum(k * k, axis=-1, keepdims=True) + EPS)

        sm = sm_ref[0, 0, r0:r0 + G, :]
        g_all = neg_a * _softplus(sm + dtb)
        gc_all = _dot3(tri, g_all, exact_a=True)
        gc = jnp.sum(jnp.where(lane == h, gc_all, 0.0), axis=1, keepdims=True)
        beta = jnp.sum(jnp.where(lane == GDN_HEADS + h, _sigmoid(sm), 0.0), axis=1, keepdims=True)
        gc_row = jnp.sum(jnp.where(eye, gc, 0.0), axis=0, keepdims=True)
        gc_last = jnp.sum(jnp.where(is_last, gc_row, 0.0), axis=1, keepdims=True)
        decay = jnp.where(incl, jnp.exp(jnp.where(incl, gc - gc_row, 0.0)), 0.0)
        gamma = jnp.exp(gc)

        kk = _dot3_nt(k, k)
        lmat = jnp.where(strict, decay * kk, 0.0) * beta
        tinv = eye_f - jnp.where(joins[0], lmat, 0.0)
        for lvl in range(1, len(joins)):
            tinv = tinv - _bdot(_bdot(tinv, jnp.where(joins[lvl], lmat, 0.0)), tinv)
        resid = (eye_f - tinv) - _dot3(lmat, tinv)
        tinv = tinv + _bdot(tinv, resid)
        rhs = jnp.concatenate([(beta * gamma) * k, beta * v], axis=1)
        wu = _dot3(tinv, rhs)
        qk = decay * _bdot_nt(q, k)

        rows = slice(r0, r0 + G)
        w_ref[0, 0, rows, :] = wu[:, :HEAD_DIM].astype(BF16)
        u0_ref[0, 0, rows, :] = wu[:, HEAD_DIM:]
        qg_ref[0, 0, rows, :] = (gamma * q).astype(BF16)
        kd_ref[0, 0, rows, :] = (jnp.exp(gc_last - gc) * k).astype(BF16)
        ge_all = jnp.broadcast_to(jnp.exp(gc_last), (G, LANES))
        for c in range(G // C):
            cr = slice(c * C, (c + 1) * C)
            qk_ref[0, 0, r0 + c * C:r0 + (c + 1) * C, :] = qk[cr, cr].astype(BF16)
            ge_ref[0, 0, pl.ds((i * (cb // G) + grp) * (G // C) + c, 1), :] = ge_all[c * C:c * C + 1, :]


def _gdn_local(P, cw, alog_pad, dtb_pad):
    b, _, s, _ = P.shape
    H, CB, C = GDN_HEADS, min(GDN_CB, s), GDN_CHUNK
    n = s // C

    def main(j0):
        return pl.BlockSpec((1, 1, CB, LANES), lambda bi, h, i: (bi, j0 + h, i, 0))

    def halo(j0):
        return pl.BlockSpec((1, 1, 8, LANES),
                            lambda bi, h, i: (bi, j0 + h, jnp.maximum(i * (CB // 8) - 1, 0), 0))

    def cws(j0):
        return pl.BlockSpec((1, GDN_CONV, LANES), lambda bi, h, i: (j0 + h, 0, 0))

    row = pl.BlockSpec((1, LANES), lambda bi, h, i: (0, 0))
    hs = lambda width: pl.BlockSpec((1, 1, CB, width), lambda bi, h, i: (bi, h, i, 0))
    return pl.pallas_call(
        functools.partial(_gdn_local_kernel, cb=CB),
        grid=(b, H, s // CB),
        in_specs=[main(J_AQ), halo(J_AQ), main(J_AK), halo(J_AK), main(J_AV), halo(J_AV),
                  pl.BlockSpec((1, 1, CB, LANES), lambda bi, h, i: (bi, J_SMALL, i, 0)),
                  cws(0), cws(8), cws(16), row, row],
        out_specs=[hs(LANES), hs(LANES), hs(C), hs(LANES), hs(LANES),
                   pl.BlockSpec((1, 1, n, LANES), lambda bi, h, i: (bi, h, 0, 0))],
        out_shape=[jax.ShapeDtypeStruct((b, H, s, LANES), BF16),
                   jax.ShapeDtypeStruct((b, H, s, LANES), F32),
                   jax.ShapeDtypeStruct((b, H, s, C), BF16),
                   jax.ShapeDtypeStruct((b, H, s, LANES), BF16),
                   jax.ShapeDtypeStruct((b, H, s, LANES), BF16),
                   jax.ShapeDtypeStruct((b, H, n, LANES), F32)],
        compiler_params=_cparams(("parallel", "parallel", "arbitrary")),
        name="gdn_local",
    )(P, P, P, P, P, P, P, cw, cw, cw, alog_pad, dtb_pad)


GDN_HB = 8


def _gdn_scan_kernel(w_ref, u0_ref, qk_ref, qg_ref, kd_ref, ge_ref, z_ref, nw_ref, o_ref, s_scr,
                     *, sb):
    C = GDN_CHUNK

    @pl.when(pl.program_id(2) == 0)
    def _():
        s_scr[...] = jnp.zeros_like(s_scr)

    nw = nw_ref[...]

    def body(n, carry):
        r = pl.multiple_of(n * C, C)
        for hh in range(GDN_HB):
            st = s_scr[hh]
            stb = st.astype(BF16)
            u = u0_ref[0, hh, pl.ds(r, C), :] - jnp.dot(w_ref[0, hh, pl.ds(r, C), :], stb,
                                                       preferred_element_type=F32)
            ub = u.astype(BF16)
            o = (jnp.dot(qg_ref[0, hh, pl.ds(r, C), :], stb, preferred_element_type=F32)
                 + jnp.dot(qk_ref[0, hh, pl.ds(r, C), :], ub, preferred_element_type=F32))
            ge = ge_ref[0, hh, pl.ds(n, 1), :]
            s_scr[hh] = ge * st + lax.dot_general(kd_ref[0, hh, pl.ds(r, C), :], ub,
                                                  (((0,), (0,)), ((), ())),
                                                  preferred_element_type=F32)
            on = o * lax.rsqrt(jnp.mean(o * o, axis=-1, keepdims=True) + EPS) * nw
            z = z_ref[0, hh, pl.ds(r, C), :]
            o_ref[0, pl.ds(r, C), hh * LANES:(hh + 1) * LANES] = (on * _silu(z)).astype(BF16)
        return carry

    lax.fori_loop(0, sb // C, body, 0)


def _gdn_scan(w, u0, qk, qg, kd, ge, P, norm_w):
    b, H, s, _ = w.shape
    C, HB = GDN_CHUNK, GDN_HB
    sb = min(1024, s)
    hs = lambda width: pl.BlockSpec((1, HB, sb, width), lambda bi, hb, i: (bi, hb, i, 0))
    return pl.pallas_call(
        functools.partial(_gdn_scan_kernel, sb=sb),
        grid=(b, H // HB, s // sb),
        in_specs=[hs(LANES), hs(LANES), hs(C), hs(LANES), hs(LANES),
                  pl.BlockSpec((1, HB, sb // C, LANES), lambda bi, hb, i: (bi, hb, i, 0)),
                  pl.BlockSpec((1, HB, sb, LANES), lambda bi, hb, i: (bi, J_AZ // HB + hb, i, 0)),
                  pl.BlockSpec((1, LANES), lambda bi, hb, i: (0, 0))],
        out_specs=pl.BlockSpec((1, sb, HB * LANES), lambda bi, hb, i: (bi, i, hb)),
        out_shape=jax.ShapeDtypeStruct((b, s, H * LANES), BF16),
        scratch_shapes=[pltpu.VMEM((HB, HEAD_DIM, HEAD_DIM), F32)],
        compiler_params=_cparams(("parallel", "parallel", "arbitrary")),
        name="gdn_scan",
    )(w, u0, qk, qg, kd, ge, P, norm_w.reshape(1, LANES))


def _pad_lanes_row(v):
    return jnp.pad(v.astype(F32), (0, LANES - v.shape[0])).reshape(1, LANES)


def _gdn(P, conv_w, a_log, dt_bias, norm_w):
    cw = jnp.transpose(conv_w.reshape(GDN_CONV, 3 * GDN_HEADS, LANES), (1, 0, 2))
    w, u0, qk, qg, kd, ge = _gdn_local(P, cw, _pad_lanes_row(a_log), _pad_lanes_row(dt_bias))
    return _gdn_scan(w, u0, qk, qg, kd, ge, P, norm_w)


NSA_TQ = 128
NSA_TK_SEL = 512
NSA_TK_WIN = 256
NSA_HC = 1
R_KC, R_KS, R_VS, R_KW, R_VW = 16, 20, 22, 24, 26
ROPE_NB = 4


def _rope_kernel(x_ref, pos_ref, invf_ref, o_ref, cos_scr, sin_scr):
    j = pl.program_id(2)
    lane = lax.broadcasted_iota(jnp.int32, cos_scr.shape, 1)

    @pl.when(j == 0)
    def _():
        ang = pos_ref[0] * invf_ref[...]
        sn = jnp.sin(ang)
        cos_scr[...] = jnp.where(lane < ROPE_DIM, jnp.cos(ang), 1.0)
        sin_scr[...] = jnp.where(lane < ROPE_HALF, -sn, jnp.where(lane < ROPE_DIM, sn, 0.0))

    is_q = j < NSA_HEADS // ROPE_NB
    scale = jnp.where(is_q, HEAD_DIM ** -0.5, 1.0)
    for k in range(ROPE_NB):
        x = x_ref[0, k]
        swapped = jnp.where(lane < ROPE_HALF, pltpu.roll(x, LANES - ROPE_HALF, axis=1),
                            pltpu.roll(x, ROPE_HALF, axis=1))
        rot = x * cos_scr[...] + swapped * sin_scr[...]
        out = rot * scale if k < 2 else jnp.where(is_q, rot, x) * scale
        o_ref[0, k] = out.astype(BF16)


def _rope(P, pos_f32, invf):
    b, _, s, _ = P.shape
    tr = min(1024, s)
    nb = ROPE_NB
    return pl.pallas_call(
        _rope_kernel,
        grid=(b, s // tr, N_NSA_BLOCKS // nb),
        in_specs=[pl.BlockSpec((1, nb, tr, LANES), lambda bi, i, j: (bi, J_BQ // nb + j, i, 0)),
                  pl.BlockSpec((1, tr, 1), lambda bi, i, j: (bi, i, 0)),
                  pl.BlockSpec((1, LANES), lambda bi, i, j: (0, 0))],
        out_specs=pl.BlockSpec((1, nb, tr, LANES), lambda bi, i, j: (bi, j, i, 0)),
        out_shape=jax.ShapeDtypeStruct((b, N_NSA_BLOCKS, s, LANES), BF16),
        scratch_shapes=[pltpu.VMEM((tr, LANES), F32), pltpu.VMEM((tr, LANES), F32)],
        compiler_params=_cparams(("parallel", "parallel", "arbitrary")),
        name="nsa_rope",
    )(P, pos_f32, invf)


def _compress_kernel(r_ref, w1_ref, w2_ref, pos_ref, o_ref):
    r = r_ref[0, 0]
    nr = r.shape[0]
    half = CMP_STRIDE * HEAD_DIM
    a = jnp.dot(r, w1_ref[0, :half, :], preferred_element_type=F32)
    bm = jnp.dot(r, w1_ref[0, half:, :], preferred_element_type=F32)
    pos8 = jnp.broadcast_to(pos_ref[0], (8, CMP_LEN * HEAD_DIM)).astype(BF16)
    pb = jnp.dot(pos8, w1_ref[0], preferred_element_type=F32)[0:1, :]
    hid = a + pltpu.roll(bm, nr - 1, axis=0) + pb
    out = jnp.dot(_gelu(hid).astype(BF16), w2_ref[0], preferred_element_type=F32)
    row = lax.broadcasted_iota(jnp.int32, out.shape, 0)
    o_ref[0, 0] = jnp.where(row < nr - 1, out, 0.0).astype(BF16)


def _compress(rows, w1, w2, pos):
    b, _, nr, width = rows.shape
    return pl.pallas_call(
        _compress_kernel,
        grid=(b, 4),
        in_specs=[pl.BlockSpec((1, 1, nr, width), lambda bi, j: (bi, j, 0, 0)),
                  pl.BlockSpec((1, CMP_LEN * HEAD_DIM, CMP_HIDDEN), lambda bi, j: (j // 2, 0, 0)),
                  pl.BlockSpec((1, CMP_HIDDEN, HEAD_DIM), lambda bi, j: (j // 2, 0, 0)),
                  pl.BlockSpec((1, 1, CMP_LEN * HEAD_DIM), lambda bi, j: (j // 2, 0, 0))],
        out_specs=pl.BlockSpec((1, 1, nr, HEAD_DIM), lambda bi, j: (bi, j, 0, 0)),
        out_shape=jax.ShapeDtypeStruct((b, 4, nr, HEAD_DIM), BF16),
        compiler_params=_cparams(("parallel", "arbitrary")),
        name="nsa_compress",
    )(rows, w1, w2, pos)


def _nsa_kernel(q_ref, kc_ref, vc_ref, ks_ref, vs_ref, kw_ref, vw_ref, gate_ref, ovt_ref, e_ref,
                o_ref, m_scr, l_scr, acc_scr, os_scr, *, s_len):
    i = pl.program_id(2)
    TQ, HP = NSA_TQ, NSA_HPG
    R = HP * TQ
    nsb = s_len // SLC_LEN
    ncp = s_len // CMP_STRIDE
    n_sel = min(SLC_TOPK, nsb)
    s0 = i * TQ
    q2 = q_ref[0].reshape(R, HEAD_DIM)

    sc = _bdot_nt(q2, kc_ref[0, 0])
    tq_r = s0 + (lax.broadcasted_iota(jnp.int32, (R, ncp), 0) & (TQ - 1))
    ncol = lax.broadcasted_iota(jnp.int32, (R, ncp), 1)
    valid = jnp.logical_and(ncol * CMP_STRIDE + (CMP_LEN - 1) <= tq_r, ncol < ncp - 1)
    scm = jnp.where(valid, sc, NEG)
    e = jnp.where(valid, jnp.exp(scm - jnp.max(scm, axis=1, keepdims=True)), 0.0)
    p_c = e / jnp.maximum(jnp.sum(e, axis=1, keepdims=True), 1e-30)
    o_c = _bdot(p_c, vc_ref[0, 0])
    psum = jnp.sum(p_c.reshape(HP, TQ, ncp), axis=0)
    imp = _fdot_nt(ovt_ref[...], psum)
    blk = lax.broadcasted_iota(jnp.int32, (nsb, TQ), 0)
    tq_l = s0 + lax.broadcasted_iota(jnp.int32, (nsb, TQ), 1)
    cur = tq_l // SLC_LEN
    forced = jnp.logical_or(blk == 0, jnp.logical_or(blk == cur, blk == cur - 1))
    imp = jnp.where(forced, BIG, imp)
    imp = jnp.where(blk * SLC_LEN <= tq_l, imp, NEG)
    rank = jnp.zeros((nsb, TQ), jnp.int32)
    for j in range(nsb):
        rj = imp[j:j + 1, :]
        beats = jnp.logical_or(rj > imp, jnp.logical_and(rj == imp, blk > j))
        rank = rank + jnp.where(beats, 1, 0)
    sel_t = jnp.where(rank < n_sel, 1.0, 0.0).astype(F32)
    if nsb < LANES:
        sel_t = jnp.concatenate([sel_t, jnp.zeros((LANES - nsb, TQ), F32)], axis=0)
    sel = jnp.transpose(sel_t).astype(BF16)

    ones_blk = jnp.ones((TQ, HEAD_DIM), BF16)
    HC = NSA_HC

    def flash(k_ref, v_ref, tk, lo, hi, mask_fn):
        m_scr[...] = jnp.full(m_scr.shape, NEG, F32)
        l_scr[...] = jnp.zeros(l_scr.shape, F32)
        acc_scr[...] = jnp.zeros(acc_scr.shape, F32)
        qpos = s0 + lax.broadcasted_iota(jnp.int32, (TQ, tk), 0)
        kofs = lax.broadcasted_iota(jnp.int32, (TQ, tk), 1)
        ones_v = jnp.concatenate([ones_blk] * (tk // TQ), axis=0)

        def body(j, carry):
            r = pl.multiple_of(j * tk, tk)
            bias = jnp.where(mask_fn(r, r + kofs, qpos), 0.0, NEG)
            kb = k_ref[0, 0, pl.ds(r, tk), :]
            vb = jnp.concatenate([v_ref[0, 0, pl.ds(r, tk), :], ones_v], axis=1)
            for c in range(HP // HC):
                rows = slice(c * HC * TQ, (c + 1) * HC * TQ)
                s = _bdot_nt(q_ref[0, c * HC:(c + 1) * HC].reshape(HC * TQ, HEAD_DIM), kb)
                s = (s.reshape(HC, TQ, tk) + bias[None]).reshape(HC * TQ, tk)
                m_prev = m_scr[rows, :]
                m_new = jnp.maximum(m_prev, jnp.max(s, axis=1, keepdims=True))
                alpha = jnp.exp(m_prev - m_new)
                p = jnp.exp(s - jnp.concatenate([m_new] * (tk // LANES), axis=1)).astype(BF16)
                pv = jnp.dot(p, vb, preferred_element_type=F32)
                l_scr[rows, :] = alpha * l_scr[rows, :] + pv[:, HEAD_DIM:]
                acc_scr[rows, :] = alpha * acc_scr[rows, :] + pv[:, :HEAD_DIM]
                m_scr[rows, :] = m_new
            return carry

        lax.fori_loop(lo, hi, body, 0)

    tks = min(NSA_TK_SEL, s_len)

    def sel_mask(r, kpos, qpos):
        chosen = jnp.dot(sel, e_ref[:, pl.ds(r, tks)], preferred_element_type=F32)
        return jnp.logical_and(chosen > 0.5, kpos <= qpos)

    flash(ks_ref, vs_ref, tks, 0, (s0 + TQ - 1) // tks + 1, sel_mask)
    os_scr[...] = acc_scr[...] / l_scr[...]

    def win_mask(r, kpos, qpos):
        return jnp.logical_and(kpos <= qpos, kpos > qpos - WIN)

    tkw = min(NSA_TK_WIN, s_len)
    flash(kw_ref, vw_ref, tkw, jnp.maximum(s0 - WIN + 1, 0) // tkw, (s0 + TQ - 1) // tkw + 1, win_mask)

    gates = _sigmoid(gate_ref[0, 0])
    for hh in range(HP):
        rows = slice(hh * TQ, (hh + 1) * TQ)
        o_w = acc_scr[rows, :] / l_scr[rows, :]
        out = (gates[:, 3 * hh:3 * hh + 1] * o_c[rows, :]
               + gates[:, 3 * hh + 1:3 * hh + 2] * os_scr[rows, :]
               + gates[:, 3 * hh + 2:3 * hh + 3] * o_w)
        o_ref[0, :, hh * HEAD_DIM:(hh + 1) * HEAD_DIM] = out.astype(BF16)


def _nsa_attend(rp, cmp_kv, P, ovt, expand):
    b, _, s, _ = rp.shape
    TQ, HP, G = NSA_TQ, NSA_HPG, NSA_GROUPS
    ncp = cmp_kv.shape[2]
    R = HP * TQ
    full = lambda j0: pl.BlockSpec((1, 1, s, HEAD_DIM), lambda bi, g, i: (bi, j0 + g, 0, 0))
    cmp_spec = lambda j0: pl.BlockSpec((1, 1, ncp, HEAD_DIM), lambda bi, g, i: (bi, j0 + g, 0, 0))
    return pl.pallas_call(
        functools.partial(_nsa_kernel, s_len=s),
        grid=(b, G, s // TQ),
        in_specs=[pl.BlockSpec((1, HP, TQ, HEAD_DIM), lambda bi, g, i: (bi, g, i, 0)),
                  cmp_spec(0), cmp_spec(2),
                  full(R_KS), full(R_VS), full(R_KW), full(R_VW),
                  pl.BlockSpec((1, 1, TQ, LANES), lambda bi, g, i: (bi, J_GATE + g, i, 0)),
                  pl.BlockSpec(ovt.shape, lambda bi, g, i: (0, 0)),
                  pl.BlockSpec(expand.shape, lambda bi, g, i: (0, 0))],
        out_specs=pl.BlockSpec((1, TQ, HP * HEAD_DIM), lambda bi, g, i: (bi, i, g)),
        out_shape=jax.ShapeDtypeStruct((b, s, NSA_HEADS * HEAD_DIM), BF16),
        scratch_shapes=[pltpu.VMEM((R, LANES), F32), pltpu.VMEM((R, LANES), F32),
                        pltpu.VMEM((R, HEAD_DIM), F32), pltpu.VMEM((R, HEAD_DIM), F32)],
        compiler_params=_cparams(("parallel", "parallel", "arbitrary")),
        name="nsa_attend",
    )(rp, cmp_kv, cmp_kv, rp, rp, rp, rp, P, ovt, expand)


def _nsa(P, positions, cmp_pos_k, cmp_w1_k, cmp_w2_k, cmp_pos_v, cmp_w1_v, cmp_w2_v):
    b, _, s, _ = P.shape
    lanes = np.arange(LANES)
    invf = np.where(lanes < ROPE_DIM, ROPE_THETA ** (-(lanes % ROPE_HALF) / ROPE_HALF), 0.0)
    rp = _rope(P, positions.astype(F32).reshape(b, s, 1), jnp.asarray(invf, F32).reshape(1, LANES))

    ncp = s // CMP_STRIDE
    rows = rp[:, R_KC:R_KC + 4].reshape(b, 4, ncp, CMP_STRIDE * HEAD_DIM)
    w1 = jnp.stack([cmp_w1_k, cmp_w1_v]).astype(BF16)
    w2 = jnp.stack([cmp_w2_k, cmp_w2_v]).astype(BF16)
    pos = jnp.stack([cmp_pos_k, cmp_pos_v]).reshape(2, 1, CMP_LEN * HEAD_DIM)
    cmp_kv = _compress(rows, w1, w2, pos)

    nsb = s // SLC_LEN
    cmp_start = np.arange(ncp) * CMP_STRIDE
    slc_start = np.arange(nsb) * SLC_LEN
    ov = ((cmp_start[:, None] < slc_start[None, :] + SLC_LEN)
          & (cmp_start[:, None] + CMP_LEN > slc_start[None, :])
          & (np.arange(ncp)[:, None] < ncp - 1))
    ovt = jnp.asarray(ov.T.astype(np.float32))
    expand = (np.arange(LANES)[:, None] == (np.arange(s)[None, :] // SLC_LEN))
    expand = jnp.asarray(expand.astype(np.float32), BF16)
    return _nsa_attend(rp, cmp_kv, P, ovt, expand)


MERGE_TN = 512


def _merge_kernel(oa_ref, ob_ref, wg_ref, wn_ref, ma_ref, mb_ref, y_ref):
    ya = jnp.dot(oa_ref[0], wg_ref[...], preferred_element_type=F32)
    yb = jnp.dot(ob_ref[0], wn_ref[...], preferred_element_type=F32)
    for k in range(MERGE_TN // LANES):
        cols = slice(k * LANES, (k + 1) * LANES)
        y = _sigmoid(ma_ref[0, k]) * ya[:, cols] + _sigmoid(mb_ref[0, k]) * yb[:, cols]
        y_ref[0, :, cols] = y.astype(BF16)


def _merge(o_a, o_b, wg, wn, P):
    b, s, da = o_a.shape
    db = o_b.shape[2]
    d = wg.shape[1]
    tm, tn = min(1024, s), MERGE_TN
    nk = tn // LANES
    return pl.pallas_call(
        _merge_kernel,
        grid=(b, s // tm, d // tn),
        in_specs=[pl.BlockSpec((1, tm, da), lambda bi, i, j: (bi, i, 0)),
                  pl.BlockSpec((1, tm, db), lambda bi, i, j: (bi, i, 0)),
                  pl.BlockSpec((da, tn), lambda bi, i, j: (0, j)),
                  pl.BlockSpec((db, tn), lambda bi, i, j: (0, j)),
                  pl.BlockSpec((1, nk, tm, LANES), lambda bi, i, j: (bi, J_MA // nk + j, i, 0)),
                  pl.BlockSpec((1, nk, tm, LANES), lambda bi, i, j: (bi, J_MB // nk + j, i, 0))],
        out_specs=pl.BlockSpec((1, tm, tn), lambda bi, i, j: (bi, i, j)),
        out_shape=jax.ShapeDtypeStruct((b, s, d), BF16),
        compiler_params=_cparams(("parallel", "parallel", "arbitrary")),
        name="merge",
    )(o_a, o_b, wg, wn, P, P)


def _outproj_kernel(y_ref, w_ref, x_ref, g_ref, o_ref):
    o_ref[0] = x_ref[0] + g_ref[0] * jnp.dot(y_ref[0], w_ref[...], preferred_element_type=F32)


def _outproj(y, w, x, g):
    b, s, d = x.shape
    tm, tn = min(1024, s), 512
    return pl.pallas_call(
        _outproj_kernel,
        grid=(b, s // tm, d // tn),
        in_specs=[pl.BlockSpec((1, tm, d), lambda bi, i, j: (bi, i, 0)),
                  pl.BlockSpec((d, tn), lambda bi, i, j: (0, j)),
                  pl.BlockSpec((1, tm, tn), lambda bi, i, j: (bi, i, j)),
                  pl.BlockSpec((1, 1, tn), lambda bi, i, j: (bi, 0, j))],
        out_specs=pl.BlockSpec((1, tm, tn), lambda bi, i, j: (bi, i, j)),
        out_shape=jax.ShapeDtypeStruct((b, s, d), F32),
        compiler_params=_cparams(("parallel", "parallel", "arbitrary")),
        name="outproj",
    )(y, w, x, g.reshape(b, 1, d))


PEER_TB = 512
PEER_TG = 128
PEER_AHEAD = 26
PEER_SLOTS = 32
PEER_SEL = PEER_HEADS * PEER_TOPK
MIN_F32 = -3.0e38


def _topk_rows(vals, payload=None):
    nrow = vals.shape[0]
    rid = lax.broadcasted_iota(jnp.int32, vals.shape, 0)
    out_v, out_i = [], []
    for _ in range(PEER_TOPK):
        m = jnp.max(vals, axis=0, keepdims=True)
        idx = jnp.min(jnp.where(vals == m, rid, nrow), axis=0, keepdims=True)
        hit = rid == idx
        out_v.append(m)
        if payload is None:
            out_i.append(idx)
        else:
            out_i.append(jnp.sum(jnp.where(hit, payload, 0), axis=0, keepdims=True))
        vals = jnp.where(hit, MIN_F32, vals)
    return jnp.concatenate(out_v, axis=0), jnp.concatenate(out_i, axis=0)


def _peer_topk_kernel(qh_ref, k1_ref, k2_ref, eidx_ref, gw_ref):
    def head(h, carry):
        s1 = _fdot_nt(k1_ref[h], qh_ref[0, 2 * h])
        s2 = _fdot_nt(k2_ref[h], qh_ref[0, 2 * h + 1])
        v1, i1 = _topk_rows(s1)
        v2, i2 = _topk_rows(s2)
        keep = [PEER_TOPK // (a + 1) for a in range(PEER_TOPK)]
        npad = (-sum(keep)) % 8
        cand = jnp.concatenate([v1[a:a + 1, :] + v2[0:keep[a], :] for a in range(PEER_TOPK)]
                               + [jnp.full((npad, v1.shape[1]), MIN_F32, F32)], axis=0)
        cidx = jnp.concatenate([i1[a:a + 1, :] * PEER_NKEYS + i2[0:keep[a], :] for a in range(PEER_TOPK)]
                               + [jnp.zeros((npad, v1.shape[1]), jnp.int32)], axis=0)
        sc, eidx = _topk_rows(cand, cidx)
        ex = jnp.exp(sc - sc[0:1, :])
        eidx_ref[h] = eidx
        gw_ref[h] = ex / jnp.sum(ex, axis=0, keepdims=True)
        return carry

    lax.fori_loop(0, PEER_HEADS, head, 0)


def _peer_topk(qh, keys1, keys2):
    b, nb, s, _ = qh.shape
    tb = min(PEER_TB, s)
    nt = s // tb
    kspec = pl.BlockSpec(keys1.shape, lambda bi, i: (0, 0, 0))
    ospec = pl.BlockSpec((PEER_HEADS, PEER_TOPK, tb), lambda bi, i: (0, 0, bi * nt + i))
    return pl.pallas_call(
        _peer_topk_kernel,
        grid=(b, nt),
        in_specs=[pl.BlockSpec((1, nb, tb, LANES), lambda bi, i: (bi, 0, i, 0)), kspec, kspec],
        out_specs=[ospec, ospec],
        out_shape=[jax.ShapeDtypeStruct((PEER_HEADS, PEER_TOPK, b * s), jnp.int32),
                   jax.ShapeDtypeStruct((PEER_HEADS, PEER_TOPK, b * s), F32)],
        compiler_params=_cparams(("parallel", "parallel")),
        name="peer_topk",
    )(qh, keys1, keys2)


PEER_BANKS = 4


def _peer_gather_kernel(idx_ref, idxs_ref, gw_ref, x_ref, uv_hbm, o_ref, b0, b1, b2, b3, a_scr, sem, *, tg):
    NS, D, NB = PEER_SEL, PEER_AHEAD, PEER_BANKS
    banks = (b0, b1, b2, b3)
    Q = b0.shape[0]
    RPE = uv_hbm.shape[1]
    SUBW = RPE // 2
    half = SUBW * LANES
    RT = 8

    def place(g, k, off):
        kk = k + off
        return banks[kk % NB], kk % NB, (g + kk // NB) & (Q - 1)

    def wait_rows(g, k):
        bank, kb, q = place(g, k, 0)
        pltpu.make_async_copy(uv_hbm.at[pl.ds(0, NS)], bank.at[q], sem.at[kb, q]).wait()

    rid = lax.broadcasted_iota(jnp.int32, (NS, NS), 0)
    cid = lax.broadcasted_iota(jnp.int32, (NS, NS), 1)
    eye = rid == cid
    ones = jnp.ones((LANES, LANES), BF16)

    def lane_sum_rep(m):
        hi, lo = _split_bf16(m)
        return (jnp.dot(hi, ones, preferred_element_type=F32)
                + jnp.dot(lo, ones, preferred_element_type=F32))

    def low(w):
        return pltpu.bitcast(w << 16, F32)

    def high(w):
        return pltpu.bitcast(w & jnp.uint32(0xFFFF0000), F32)

    def phase(g, k, do_dot, do_prev):
        t = g * NB + k
        if do_dot:
            wait_rows(g, k)
            xrow = x_ref[pl.ds(t, 1), :]
            bank_d, _, q_d = place(g, k, 0)
            acc = [jnp.zeros((RT, LANES), F32) for _ in range(NS // RT)]
            bank_i, k_i, q_i = place(g, k, D)
            base_i = t * NS
        if do_prev:
            bank_s, _, q_s = place(g, k, -1)
            par_s = (t - 1) & 1
            out_lo, out_hi = [], []
        for sg in range(SUBW):
            if do_dot:
                x_lo = jnp.broadcast_to(xrow[:, sg * LANES:(sg + 1) * LANES], (RT, LANES))
                x_hi = jnp.broadcast_to(xrow[:, half + sg * LANES:half + (sg + 1) * LANES], (RT, LANES))
            if do_prev:
                p_lo = jnp.zeros((RT, LANES), F32)
                p_hi = jnp.zeros((RT, LANES), F32)
            for rt in range(NS // RT):
                rows = slice(rt * RT, (rt + 1) * RT)
                if do_dot:
                    j = sg * (NS // RT) + rt
                    pltpu.make_async_copy(uv_hbm.at[idxs_ref[base_i + j]],
                                          bank_i.at[q_i, :, j, :], sem.at[k_i, q_i]).start(priority=j % 2)
                    w = bank_d[q_d, sg, rows, :]
                    acc[rt] = acc[rt] + low(w) * x_lo + high(w) * x_hi
                if do_prev:
                    w = bank_s[q_s, SUBW + sg, rows, :]
                    c = a_scr[par_s, rows, :]
                    p_lo = p_lo + low(w) * c
                    p_hi = p_hi + high(w) * c
            if do_prev:
                out_lo.append(jnp.sum(p_lo, axis=0, keepdims=True))
                out_hi.append(jnp.sum(p_hi, axis=0, keepdims=True))
        if do_dot:
            gw_rep = lane_sum_rep(jnp.where(eye, gw_ref[pl.ds(t, 1), :], 0.0))
            a_scr[t & 1] = gw_rep * _gelu(lane_sum_rep(jnp.concatenate(acc, axis=0)))
        if do_prev:
            o_ref[pl.ds(t - 1, 1), :] = jnp.concatenate(out_lo + out_hi, axis=1)

    step = pl.program_id(0)
    last = pl.num_programs(0) - 1

    @pl.when(step == 0)
    def _():
        for k in range(NB):
            def ramp(g, carry, k=k):
                bank, kb, q = place(g, k, 0)

                def body(j, c):
                    pltpu.make_async_copy(uv_hbm.at[idx_ref[(g * NB + k) * NS + j]],
                                          bank.at[q, :, j, :], sem.at[kb, q]).start()
                    return c
                return lax.fori_loop(0, NS, body, carry)
            lax.fori_loop(0, (D - k + NB - 1) // NB, ramp, 0)

    phase(0, 0, True, False)
    for k in range(1, NB):
        phase(0, k, True, True)

    def group(g, carry):
        for k in range(NB):
            phase(g, k, True, True)
        return carry
    lax.fori_loop(1, tg // NB, group, 0)
    phase(tg // NB, 0, False, True)

    @pl.when(step == last)
    def _():
        for r in range(D):
            wait_rows(tg // NB, r)


def _peer_gather(eidx, gw, h2r, uvc):
    t = eidx.shape[0]
    tg = min(PEER_TG, t)
    d = h2r.shape[1]
    rpe = uvc.shape[1]
    nq = PEER_SLOTS // PEER_BANKS
    assert PEER_BANKS == 4 and PEER_AHEAD % PEER_BANKS in (1, 2) and PEER_AHEAD + 2 <= PEER_SLOTS
    assert tg % PEER_BANKS == 0 and tg > PEER_AHEAD and t % tg == 0
    flat = eidx.reshape(-1)
    shifted = jnp.roll(flat, -PEER_AHEAD * PEER_SEL)
    bank = pltpu.VMEM((nq, rpe, PEER_SEL, LANES), jnp.uint32)
    return pl.pallas_call(
        functools.partial(_peer_gather_kernel, tg=tg),
        grid=(t // tg,),
        in_specs=[pl.BlockSpec((tg * PEER_SEL,), lambda i: (0,), memory_space=pltpu.SMEM),
                  pl.BlockSpec((tg * PEER_SEL,), lambda i: (i,), memory_space=pltpu.SMEM),
                  pl.BlockSpec((tg, PEER_SEL), lambda i: (i, 0)),
                  pl.BlockSpec((tg, d), lambda i: (i, 0)),
                  pl.BlockSpec(memory_space=pl.ANY)],
        out_specs=pl.BlockSpec((tg, d), lambda i: (i, 0)),
        out_shape=jax.ShapeDtypeStruct((t, d), F32),
        scratch_shapes=[bank, bank, bank, bank,
                        pltpu.VMEM((2, PEER_SEL, LANES), F32),
                        pltpu.SemaphoreType.DMA((PEER_BANKS, nq))],
        compiler_params=_cparams(("arbitrary",)),
        name="peer_gather",
    )(flat, shifted, gw, h2r, uvc)


PACK_EB = 256


def _pack_kernel(u_ref, v_ref, o_ref):
    half = u_ref.shape[1] // 2
    for tab, ref in enumerate((u_ref, v_ref)):
        for sg in range(half // LANES):
            lo = ref[:, sg * LANES:(sg + 1) * LANES].astype(BF16).astype(F32)
            hi = ref[:, half + sg * LANES:half + (sg + 1) * LANES].astype(BF16).astype(F32)
            word = (pltpu.bitcast(lo, jnp.uint32) >> 16) | (pltpu.bitcast(hi, jnp.uint32) & jnp.uint32(0xFFFF0000))
            o_ref[:, tab * (half // LANES) + sg, :] = word


def _pack_tables(u, v):
    e, d = u.shape
    eb = min(PACK_EB, e)
    return pl.pallas_call(
        _pack_kernel,
        grid=(e // eb,),
        in_specs=[pl.BlockSpec((eb, d), lambda i: (i, 0)), pl.BlockSpec((eb, d), lambda i: (i, 0))],
        out_specs=pl.BlockSpec((eb, d // LANES, LANES), lambda i: (i, 0, 0)),
        out_shape=jax.ShapeDtypeStruct((e, d // LANES, LANES), jnp.uint32),
        compiler_params=_cparams(("parallel",)),
        name="peer_pack",
    )(u, v)


def _peer(x1, norm_w, sc, sh, wq, keys1, keys2, u, v):
    b, s, d = x1.shape
    qh, h2 = _normproj(x1, norm_w, sc, sh, wq.astype(BF16), tm=1024, tn=512, emit_h=True)
    eidx, gw = _peer_topk(qh, keys1, keys2)
    eidx = jnp.transpose(eidx, (2, 0, 1)).reshape(b * s, PEER_SEL)
    gw = jnp.transpose(gw, (2, 0, 1)).reshape(b * s, PEER_SEL)
    uvc = _pack_tables(u, v)
    out = _peer_gather(eidx, gw, h2.reshape(b * s, d), uvc)
    return out.reshape(b, s, d)


def _final_kernel(x_ref, p_ref, g_ref, w_ref, o_ref):
    x = x_ref[0] + g_ref[0] * p_ref[0]
    o_ref[0] = x * lax.rsqrt(jnp.mean(x * x, axis=-1, keepdims=True) + EPS) * w_ref[...]


def _final(x1, peer, g2, wf):
    b, s, d = x1.shape
    tm = min(512, s)
    blk = pl.BlockSpec((1, tm, d), lambda bi, i: (bi, i, 0))
    return pl.pallas_call(
        _final_kernel,
        grid=(b, s // tm),
        in_specs=[blk, blk, pl.BlockSpec((1, 1, d), lambda bi, i: (bi, 0, 0)),
                  pl.BlockSpec((1, d), lambda bi, i: (0, 0))],
        out_specs=blk,
        out_shape=jax.ShapeDtypeStruct((b, s, d), F32),
        compiler_params=_cparams(("parallel", "parallel")),
        name="final_norm",
    )(x1, peer, g2.reshape(b, 1, d), wf.reshape(1, d))


def _pad_rows(a, mult=8):
    pad = (-a.shape[0]) % mult
    return jnp.pad(a, ((0, pad), (0, 0)))


def _permute_w_in(w):
    o_aa, o_bq, o_bg, o_ma, o_end = 4096, 4112, 7696, 7744, 11840
    d = w.shape[0]
    z = lambda n: jnp.zeros((d, n), w.dtype)
    hg = 3 * NSA_HPG
    cols = [w[:, 0:o_aa], w[:, o_bq:o_bg], w[:, o_ma:o_end],
            w[:, o_aa:o_bq], z(LANES - 16),
            w[:, o_bg:o_bg + hg], z(LANES - hg),
            w[:, o_bg + hg:o_ma], z(LANES - hg),
            z(LANES)]
    return jnp.concatenate(cols, axis=1).astype(BF16)


def kernel(x, c, positions, ada_w, ada_b, norm1_w, norm2_w, w_in, gdn_conv_w, gdn_A_log, gdn_dt_bias, gdn_norm_w, cmp_pos_k, cmp_w1_k, cmp_w2_k, cmp_pos_v, cmp_w1_v, cmp_w2_v, w_branch_gdn, w_branch_nsa, w_out, peer_wq, peer_keys1, peer_keys2, peer_u, peer_v, final_norm_w):
    b, s, d = x.shape
    l = 0
    mod = _ada(_pad_rows(c), ada_w[l], ada_b[l])[:b]
    sh1, sc1, g1, sh2, sc2, g2 = jnp.split(mod, 6, axis=-1)
    P = _normproj(x, norm1_w[l], sc1, sh1, _permute_w_in(w_in[l]), tm=1024, tn=1024, emit_h=False)
    o_a = _gdn(P, gdn_conv_w[l], gdn_A_log[l], gdn_dt_bias[l], gdn_norm_w[l])
    o_b = _nsa(P, positions, cmp_pos_k[l], cmp_w1_k[l], cmp_w2_k[l], cmp_pos_v[l], cmp_w1_v[l], cmp_w2_v[l])
    y = _merge(o_a, o_b, w_branch_gdn[l].astype(BF16), w_branch_nsa[l].astype(BF16), P)
    x1 = _outproj(y, w_out[l].astype(BF16), x, g1)
    peer = _peer(x1, norm2_w[l], sc2, sh2, peer_wq[l], peer_keys1[l], peer_keys2[l], peer_u[l], peer_v[l])
    return _final(x1, peer, g2, final_norm_w)
```

```python
import functools
import math

import numpy as np
import jax
import jax.numpy as jnp
from jax import lax
from jax.experimental import pallas as pl
from jax.experimental.pallas import tpu as pltpu

F32 = jnp.float32
BF16 = jnp.bfloat16
HI = lax.Precision.HIGHEST

LANES = 128
VMEM_LIMIT = 56 * 1024 * 1024

EPS = 1e-6
ROPE_THETA = 500000.0
HEAD_DIM = 128
ROPE_DIM = HEAD_DIM // 4
ROPE_HALF = ROPE_DIM // 2

GDN_HEADS = 8
GDN_CONV = 4
GDN_CHUNK = 64

NSA_HEADS = 16
NSA_GROUPS = 2
NSA_HPG = NSA_HEADS // NSA_GROUPS
CMP_LEN = 32
CMP_STRIDE = 16
CMP_HIDDEN = 256
SLC_LEN = 64
SLC_TOPK = 16
WIN = 512

PEER_HEADS = 8
PEER_NKEYS = 128
PEER_TOPK = 16

NEG = -1e30
BIG = 1e9

J_AQ, J_AK, J_AV, J_AZ = 0, 8, 16, 24
J_BQ = 32
N_NSA_BLOCKS = 28
J_MA, J_MB = 60, 76
J_SMALL = 92
J_GATE = 93


def _cparams(sem):
    return pltpu.CompilerParams(dimension_semantics=sem, vmem_limit_bytes=VMEM_LIMIT)


def _bdot(a, b):
    return jnp.dot(a.astype(BF16), b.astype(BF16), preferred_element_type=F32)


def _bdot_nt(a, b):
    return lax.dot_general(a.astype(BF16), b.astype(BF16), (((1,), (1,)), ((), ())),
                           preferred_element_type=F32)


def _split_bf16(a):
    hi = a.astype(BF16)
    return hi, (a - hi.astype(F32)).astype(BF16)


def _dot3(a, b, exact_a=False):
    bh, bl = _split_bf16(b)
    dot = functools.partial(jnp.dot, preferred_element_type=F32)
    if exact_a:
        return dot(a, bh) + dot(a, bl)
    ah, al = _split_bf16(a)
    return dot(ah, bh) + dot(al, bh) + dot(ah, bl)


def _dot3_nt(a, b):
    ah, al = _split_bf16(a)
    bh, bl = _split_bf16(b)
    dot = functools.partial(lax.dot_general, dimension_numbers=(((1,), (1,)), ((), ())),
                            preferred_element_type=F32)
    return dot(ah, bh) + dot(al, bh) + dot(ah, bl)


def _fdot(a, b):
    return jnp.dot(a, b, precision=HI, preferred_element_type=F32)


def _fdot_nt(a, b):
    return lax.dot_general(a, b, (((1,), (1,)), ((), ())), precision=HI,
                           preferred_element_type=F32)


def _sigmoid(x):
    return 1.0 / (1.0 + jnp.exp(-x))


def _silu(x):
    return x * _sigmoid(x)


def _gelu(x):
    return 0.5 * x * (1.0 + jnp.tanh(math.sqrt(2.0 / math.pi) * (x + 0.044715 * (x * x * x))))


def _softplus(x):
    return jnp.maximum(x, 0.0) + jnp.log(1.0 + jnp.exp(-jnp.abs(x)))


def _ada_kernel(c_ref, w_ref, b_ref, o_ref):
    o_ref[...] = _fdot(_silu(c_ref[...]), w_ref[...]) + b_ref[...]


def _ada(c_pad, ada_w, ada_b):
    m, d = c_pad.shape
    n = ada_w.shape[1]
    tn = 1024
    return pl.pallas_call(
        _ada_kernel,
        grid=(n // tn,),
        in_specs=[pl.BlockSpec((m, d), lambda j: (0, 0)),
                  pl.BlockSpec((d, tn), lambda j: (0, j)),
                  pl.BlockSpec((1, tn), lambda j: (0, j))],
        out_specs=pl.BlockSpec((m, tn), lambda j: (0, j)),
        out_shape=jax.ShapeDtypeStruct((m, n), F32),
        compiler_params=_cparams(("parallel",)),
        name="ada_mod",
    )(c_pad, ada_w, ada_b.reshape(1, n))


def _normproj_kernel(x_ref, nw_ref, sc_ref, sh_ref, w_ref, *rest, nk, emit_h):
    if emit_h:
        o_ref, h_out_ref, h_scr = rest
    else:
        o_ref, h_scr = rest

    @pl.when(pl.program_id(2) == 0)
    def _():
        x = x_ref[0]
        ms = jnp.mean(x * x, axis=-1, keepdims=True)
        y = x * lax.rsqrt(ms + EPS) * nw_ref[...]
        h = y * (1.0 + sc_ref[0]) + sh_ref[0]
        h_scr[...] = h.astype(BF16)
        if emit_h:
            h_out_ref[0] = h

    acc = jnp.dot(h_scr[...], w_ref[...], preferred_element_type=F32)
    for k in range(nk):
        o_ref[0, k] = acc[:, k * LANES:(k + 1) * LANES]


def _normproj(x, nw, sc, sh, w_bf16, *, tm, tn, emit_h):
    b, s, d = x.shape
    n = w_bf16.shape[1]
    tm = min(tm, s)
    nk = tn // LANES
    out_shape = [jax.ShapeDtypeStruct((b, n // LANES, s, LANES), F32)]
    out_specs = [pl.BlockSpec((1, nk, tm, LANES), lambda bi, i, j: (bi, j, i, 0))]
    if emit_h:
        out_shape.append(jax.ShapeDtypeStruct((b, s, d), F32))
        out_specs.append(pl.BlockSpec((1, tm, d), lambda bi, i, j: (bi, i, 0)))
    res = pl.pallas_call(
        functools.partial(_normproj_kernel, nk=nk, emit_h=emit_h),
        grid=(b, s // tm, n // tn),
        in_specs=[pl.BlockSpec((1, tm, d), lambda bi, i, j: (bi, i, 0)),
                  pl.BlockSpec((1, d), lambda bi, i, j: (0, 0)),
                  pl.BlockSpec((1, 1, d), lambda bi, i, j: (bi, 0, 0)),
                  pl.BlockSpec((1, 1, d), lambda bi, i, j: (bi, 0, 0)),
                  pl.BlockSpec((d, tn), lambda bi, i, j: (0, j))],
        out_specs=out_specs,
        out_shape=out_shape,
        scratch_shapes=[pltpu.VMEM((tm, d), BF16)],
        compiler_params=_cparams(("parallel", "parallel", "arbitrary")),
        name="normproj_h" if emit_h else "normproj",
    )(x, nw.reshape(1, d), sc.reshape(b, 1, d), sh.reshape(b, 1, d), w_bf16)
    return res if emit_h else res[0]


GDN_CB = 1024
GDN_GB = 256


def _gdn_local_kernel(q_ref, qh_ref, k_ref, kh_ref, v_ref, vh_ref, sm_ref,
                      cwq_ref, cwk_ref, cwv_ref, alog_ref, dtb_ref,
                      w_ref, u0_ref, qk_ref, qg_ref, kd_ref, ge_ref, *, cb):
    h = pl.program_id(1)
    i = pl.program_id(2)
    C, G = GDN_CHUNK, GDN_GB
    first = (i == 0)

    def conv_act(main_ref, halo_ref, cw_ref, r0):
        if r0 == 0:
            prev = jnp.where(first, 0.0, halo_ref[0, 0])
        else:
            prev = main_ref[0, 0, r0 - 8:r0, :]
        ext = jnp.concatenate([prev, main_ref[0, 0, r0:r0 + G, :]], axis=0)
        w = cw_ref[0]
        y = w[0:1, :] * ext[5:5 + G, :]
        for j in range(1, GDN_CONV):
            y = y + w[j:j + 1, :] * ext[5 + j:5 + j + G, :]
        return _silu(y)

    rid = lax.broadcasted_iota(jnp.int32, (G, G), 0)
    cid = lax.broadcasted_iota(jnp.int32, (G, G), 1)
    same = (rid // C) == (cid // C)
    incl = jnp.logical_and(same, rid >= cid)
    strict = jnp.logical_and(same, rid > cid)
    eye = rid == cid
    is_last = cid == (rid // C) * C + (C - 1)
    joins = []
    bs = 1
    while bs < C:
        joins.append(jnp.logical_and(jnp.logical_and((rid // (2 * bs)) == (cid // (2 * bs)),
                                                     (rid & (2 * bs - 1)) >= bs),
                                     (cid & (2 * bs - 1)) < bs))
        bs *= 2
    tri = jnp.where(incl, 1.0, 0.0).astype(BF16)
    eye_f = jnp.where(eye, 1.0, 0.0).astype(F32)
    lane = lax.broadcasted_iota(jnp.int32, (G, LANES), 1)
    neg_a = -jnp.exp(alog_ref[...])
    dtb = dtb_ref[...]

    for grp in range(cb // G):
        r0 = grp * G
        q = conv_act(q_ref, qh_ref, cwq_ref, r0)
        k = conv_act(k_ref, kh_ref, cwk_ref, r0)
        v = conv_act(v_ref, vh_ref, cwv_ref, r0)
        q = q * lax.rsqrt(jnp.sum(q * q, axis=-1, keepdims=True) + EPS) * (HEAD_DIM ** -0.5)
        k = k * lax.rsqrt(jnp.sum(k * k, axis=-1, keepdims=True) + EPS)

        sm = sm_ref[0, 0, r0:r0 + G, :]
        g_all = neg_a * _softplus(sm + dtb)
        gc_all = _dot3(tri, g_all, exact_a=True)
        gc = jnp.sum(jnp.where(lane == h, gc_all, 0.0), axis=1, keepdims=True)
        beta = jnp.sum(jnp.where(lane == GDN_HEADS + h, _sigmoid(sm), 0.0), axis=1, keepdims=True)
        gc_row = jnp.sum(jnp.where(eye, gc, 0.0), axis=0, keepdims=True)
        gc_last = jnp.sum(jnp.where(is_last, gc_row, 0.0), axis=1, keepdims=True)
        decay = jnp.where(incl, jnp.exp(jnp.where(incl, gc - gc_row, 0.0)), 0.0)
        gamma = jnp.exp(gc)

        kk = _dot3_nt(k, k)
        lmat = jnp.where(strict, decay * kk, 0.0) * beta
        tinv = eye_f - jnp.where(joins[0], lmat, 0.0)
        for lvl in range(1, len(joins)):
            tinv = tinv - _bdot(_bdot(tinv, jnp.where(joins[lvl], lmat, 0.0)), tinv)
        resid = (eye_f - tinv) - _dot3(lmat, tinv)
        tinv = tinv + _bdot(tinv, resid)
        rhs = jnp.concatenate([(beta * gamma) * k, beta * v], axis=1)
        wu = _dot3(tinv, rhs)
        qk = decay * _bdot_nt(q, k)

        rows = slice(r0, r0 + G)
        w_ref[0, 0, rows, :] = wu[:, :HEAD_DIM].astype(BF16)
        u0_ref[0, 0, rows, :] = wu[:, HEAD_DIM:]
        qg_ref[0, 0, rows, :] = (gamma * q).astype(BF16)
        kd_ref[0, 0, rows, :] = (jnp.exp(gc_last - gc) * k).astype(BF16)
        ge_all = jnp.broadcast_to(jnp.exp(gc_last), (G, LANES))
        for c in range(G // C):
            cr = slice(c * C, (c + 1) * C)
            qk_ref[0, 0, r0 + c * C:r0 + (c + 1) * C, :] = qk[cr, cr].astype(BF16)
            ge_ref[0, 0, pl.ds((i * (cb // G) + grp) * (G // C) + c, 1), :] = ge_all[c * C:c * C + 1, :]


def _gdn_local(P, cw, alog_pad, dtb_pad):
    b, _, s, _ = P.shape
    H, CB, C = GDN_HEADS, min(GDN_CB, s), GDN_CHUNK
    n = s // C

    def main(j0):
        return pl.BlockSpec((1, 1, CB, LANES), lambda bi, h, i: (bi, j0 + h, i, 0))

    def halo(j0):
        return pl.BlockSpec((1, 1, 8, LANES),
                            lambda bi, h, i: (bi, j0 + h, jnp.maximum(i * (CB // 8) - 1, 0), 0))

    def cws(j0):
        return pl.BlockSpec((1, GDN_CONV, LANES), lambda bi, h, i: (j0 + h, 0, 0))

    row = pl.BlockSpec((1, LANES), lambda bi, h, i: (0, 0))
    hs = lambda width: pl.BlockSpec((1, 1, CB, width), lambda bi, h, i: (bi, h, i, 0))
    return pl.pallas_call(
        functools.partial(_gdn_local_kernel, cb=CB),
        grid=(b, H, s // CB),
        in_specs=[main(J_AQ), halo(J_AQ), main(J_AK), halo(J_AK), main(J_AV), halo(J_AV),
                  pl.BlockSpec((1, 1, CB, LANES), lambda bi, h, i: (bi, J_SMALL, i, 0)),
                  cws(0), cws(8), cws(16), row, row],
        out_specs=[hs(LANES), hs(LANES), hs(C), hs(LANES), hs(LANES),
                   pl.BlockSpec((1, 1, n, LANES), lambda bi, h, i: (bi, h, 0, 0))],
        out_shape=[jax.ShapeDtypeStruct((b, H, s, LANES), BF16),
                   jax.ShapeDtypeStruct((b, H, s, LANES), F32),
                   jax.ShapeDtypeStruct((b, H, s, C), BF16),
                   jax.ShapeDtypeStruct((b, H, s, LANES), BF16),
                   jax.ShapeDtypeStruct((b, H, s, LANES), BF16),
                   jax.ShapeDtypeStruct((b, H, n, LANES), F32)],
        compiler_params=_cparams(("parallel", "parallel", "arbitrary")),
        name="gdn_local",
    )(P, P, P, P, P, P, P, cw, cw, cw, alog_pad, dtb_pad)


GDN_HB = 8


def _gdn_scan_kernel(w_ref, u0_ref, qk_ref, qg_ref, kd_ref, ge_ref, z_ref, nw_ref, o_ref, s_scr,
                     *, sb):
    C = GDN_CHUNK

    @pl.when(pl.program_id(2) == 0)
    def _():
        s_scr[...] = jnp.zeros_like(s_scr)

    nw = nw_ref[...]

    def body(n, carry):
        r = pl.multiple_of(n * C, C)
        for hh in range(GDN_HB):
            st = s_scr[hh]
            stb = st.astype(BF16)
            u = u0_ref[0, hh, pl.ds(r, C), :] - jnp.dot(w_ref[0, hh, pl.ds(r, C), :], stb,
                                                       preferred_element_type=F32)
            ub = u.astype(BF16)
            o = (jnp.dot(qg_ref[0, hh, pl.ds(r, C), :], stb, preferred_element_type=F32)
                 + jnp.dot(qk_ref[0, hh, pl.ds(r, C), :], ub, preferred_element_type=F32))
            ge = ge_ref[0, hh, pl.ds(n, 1), :]
            s_scr[hh] = ge * st + lax.dot_general(kd_ref[0, hh, pl.ds(r, C), :], ub,
                                                  (((0,), (0,)), ((), ())),
                                                  preferred_element_type=F32)
            on = o * lax.rsqrt(jnp.mean(o * o, axis=-1, keepdims=True) + EPS) * nw
            z = z_ref[0, hh, pl.ds(r, C), :]
            o_ref[0, pl.ds(r, C), hh * LANES:(hh + 1) * LANES] = (on * _silu(z)).astype(BF16)
        return carry

    lax.fori_loop(0, sb // C, body, 0)


def _gdn_scan(w, u0, qk, qg, kd, ge, P, norm_w):
    b, H, s, _ = w.shape
    C, HB = GDN_CHUNK, GDN_HB
    sb = min(1024, s)
    hs = lambda width: pl.BlockSpec((1, HB, sb, width), lambda bi, hb, i: (bi, hb, i, 0))
    return pl.pallas_call(
        functools.partial(_gdn_scan_kernel, sb=sb),
        grid=(b, H // HB, s // sb),
        in_specs=[hs(LANES), hs(LANES), hs(C), hs(LANES), hs(LANES),
                  pl.BlockSpec((1, HB, sb // C, LANES), lambda bi, hb, i: (bi, hb, i, 0)),
                  pl.BlockSpec((1, HB, sb, LANES), lambda bi, hb, i: (bi, J_AZ // HB + hb, i, 0)),
                  pl.BlockSpec((1, LANES), lambda bi, hb, i: (0, 0))],
        out_specs=pl.BlockSpec((1, sb, HB * LANES), lambda bi, hb, i: (bi, i, hb)),
        out_shape=jax.ShapeDtypeStruct((b, s, H * LANES), BF16),
        scratch_shapes=[pltpu.VMEM((HB, HEAD_DIM, HEAD_DIM), F32)],
        compiler_params=_cparams(("parallel", "parallel", "arbitrary")),
        name="gdn_scan",
    )(w, u0, qk, qg, kd, ge, P, norm_w.reshape(1, LANES))


def _pad_lanes_row(v):
    return jnp.pad(v.astype(F32), (0, LANES - v.shape[0])).reshape(1, LANES)


def _gdn(P, conv_w, a_log, dt_bias, norm_w):
    cw = jnp.transpose(conv_w.reshape(GDN_CONV, 3 * GDN_HEADS, LANES), (1, 0, 2))
    w, u0, qk, qg, kd, ge = _gdn_local(P, cw, _pad_lanes_row(a_log), _pad_lanes_row(dt_bias))
    return _gdn_scan(w, u0, qk, qg, kd, ge, P, norm_w)


NSA_TQ = 128
NSA_TK_SEL = 512
NSA_TK_WIN = 256
NSA_HC = 1
R_KC, R_KS, R_VS, R_KW, R_VW = 16, 20, 22, 24, 26
ROPE_NB = 4


def _rope_kernel(x_ref, pos_ref, invf_ref, o_ref, cos_scr, sin_scr):
    j = pl.program_id(2)
    lane = lax.broadcasted_iota(jnp.int32, cos_scr.shape, 1)

    @pl.when(j == 0)
    def _():
        ang = pos_ref[0] * invf_ref[...]
        sn = jnp.sin(ang)
        cos_scr[...] = jnp.where(lane < ROPE_DIM, jnp.cos(ang), 1.0)
        sin_scr[...] = jnp.where(lane < ROPE_HALF, -sn, jnp.where(lane < ROPE_DIM, sn, 0.0))

    is_q = j < NSA_HEADS // ROPE_NB
    scale = jnp.where(is_q, HEAD_DIM ** -0.5, 1.0)
    for k in range(ROPE_NB):
        x = x_ref[0, k]
        swapped = jnp.where(lane < ROPE_HALF, pltpu.roll(x, LANES - ROPE_HALF, axis=1),
                            pltpu.roll(x, ROPE_HALF, axis=1))
        rot = x * cos_scr[...] + swapped * sin_scr[...]
        out = rot * scale if k < 2 else jnp.where(is_q, rot, x) * scale
        o_ref[0, k] = out.astype(BF16)


def _rope(P, pos_f32, invf):
    b, _, s, _ = P.shape
    tr = min(1024, s)
    nb = ROPE_NB
    return pl.pallas_call(
        _rope_kernel,
        grid=(b, s // tr, N_NSA_BLOCKS // nb),
        in_specs=[pl.BlockSpec((1, nb, tr, LANES), lambda bi, i, j: (bi, J_BQ // nb + j, i, 0)),
                  pl.BlockSpec((1, tr, 1), lambda bi, i, j: (bi, i, 0)),
                  pl.BlockSpec((1, LANES), lambda bi, i, j: (0, 0))],
        out_specs=pl.BlockSpec((1, nb, tr, LANES), lambda bi, i, j: (bi, j, i, 0)),
        out_shape=jax.ShapeDtypeStruct((b, N_NSA_BLOCKS, s, LANES), BF16),
        scratch_shapes=[pltpu.VMEM((tr, LANES), F32), pltpu.VMEM((tr, LANES), F32)],
        compiler_params=_cparams(("parallel", "parallel", "arbitrary")),
        name="nsa_rope",
    )(P, pos_f32, invf)


def _compress_kernel(r_ref, w1_ref, w2_ref, pos_ref, o_ref):
    r = r_ref[0, 0]
    nr = r.shape[0]
    half = CMP_STRIDE * HEAD_DIM
    a = jnp.dot(r, w1_ref[0, :half, :], preferred_element_type=F32)
    bm = jnp.dot(r, w1_ref[0, half:, :], preferred_element_type=F32)
    pos8 = jnp.broadcast_to(pos_ref[0], (8, CMP_LEN * HEAD_DIM)).astype(BF16)
    pb = jnp.dot(pos8, w1_ref[0], preferred_element_type=F32)[0:1, :]
    hid = a + pltpu.roll(bm, nr - 1, axis=0) + pb
    out = jnp.dot(_gelu(hid).astype(BF16), w2_ref[0], preferred_element_type=F32)
    row = lax.broadcasted_iota(jnp.int32, out.shape, 0)
    o_ref[0, 0] = jnp.where(row < nr - 1, out, 0.0).astype(BF16)


def _compress(rows, w1, w2, pos):
    b, _, nr, width = rows.shape
    return pl.pallas_call(
        _compress_kernel,
        grid=(b, 4),
        in_specs=[pl.BlockSpec((1, 1, nr, width), lambda bi, j: (bi, j, 0, 0)),
                  pl.BlockSpec((1, CMP_LEN * HEAD_DIM, CMP_HIDDEN), lambda bi, j: (j // 2, 0, 0)),
                  pl.BlockSpec((1, CMP_HIDDEN, HEAD_DIM), lambda bi, j: (j // 2, 0, 0)),
                  pl.BlockSpec((1, 1, CMP_LEN * HEAD_DIM), lambda bi, j: (j // 2, 0, 0))],
        out_specs=pl.BlockSpec((1, 1, nr, HEAD_DIM), lambda bi, j: (bi, j, 0, 0)),
        out_shape=jax.ShapeDtypeStruct((b, 4, nr, HEAD_DIM), BF16),
        compiler_params=_cparams(("parallel", "arbitrary")),
        name="nsa_compress",
    )(rows, w1, w2, pos)


def _nsa_kernel(q_ref, kc_ref, vc_ref, ks_ref, vs_ref, kw_ref, vw_ref, gate_ref, ovt_ref, e_ref,
                o_ref, m_scr, l_scr, acc_scr, os_scr, *, s_len):
    i = pl.program_id(2)
    TQ, HP = NSA_TQ, NSA_HPG
    R = HP * TQ
    nsb = s_len // SLC_LEN
    ncp = s_len // CMP_STRIDE
    n_sel = min(SLC_TOPK, nsb)
    s0 = i * TQ
    q2 = q_ref[0].reshape(R, HEAD_DIM)

    sc = _bdot_nt(q2, kc_ref[0, 0])
    tq_r = s0 + (lax.broadcasted_iota(jnp.int32, (R, ncp), 0) & (TQ - 1))
    ncol = lax.broadcasted_iota(jnp.int32, (R, ncp), 1)
    valid = jnp.logical_and(ncol * CMP_STRIDE + (CMP_LEN - 1) <= tq_r, ncol < ncp - 1)
    scm = jnp.where(valid, sc, NEG)
    e = jnp.where(valid, jnp.exp(scm - jnp.max(scm, axis=1, keepdims=True)), 0.0)
    p_c = e / jnp.maximum(jnp.sum(e, axis=1, keepdims=True), 1e-30)
    o_c = _bdot(p_c, vc_ref[0, 0])
    psum = jnp.sum(p_c.reshape(HP, TQ, ncp), axis=0)
    imp = _fdot_nt(ovt_ref[...], psum)
    blk = lax.broadcasted_iota(jnp.int32, (nsb, TQ), 0)
    tq_l = s0 + lax.broadcasted_iota(jnp.int32, (nsb, TQ), 1)
    cur = tq_l // SLC_LEN
    forced = jnp.logical_or(blk == 0, jnp.logical_or(blk == cur, blk == cur - 1))
    imp = jnp.where(forced, BIG, imp)
    imp = jnp.where(blk * SLC_LEN <= tq_l, imp, NEG)
    sel_t = jnp.zeros((nsb, TQ), F32)
    for _ in range(n_sel):
        top = jnp.max(imp, axis=0, keepdims=True)
        first = jnp.min(jnp.where(imp == top, blk, nsb), axis=0, keepdims=True)
        hit = blk == first
        sel_t = jnp.where(hit, 1.0, sel_t)
        imp = jnp.where(hit, MIN_F32, imp)
    if nsb < LANES:
        sel_t = jnp.concatenate([sel_t, jnp.zeros((LANES - nsb, TQ), F32)], axis=0)
    sel = jnp.transpose(sel_t).astype(BF16)

    ones_blk = jnp.ones((TQ, HEAD_DIM), BF16)
    HC = NSA_HC

    def flash(k_ref, v_ref, tk, lo, hi, mask_fn):
        m_scr[...] = jnp.full(m_scr.shape, NEG, F32)
        l_scr[...] = jnp.zeros(l_scr.shape, F32)
        acc_scr[...] = jnp.zeros(acc_scr.shape, F32)
        qpos = s0 + lax.broadcasted_iota(jnp.int32, (TQ, tk), 0)
        kofs = lax.broadcasted_iota(jnp.int32, (TQ, tk), 1)
        ones_v = jnp.concatenate([ones_blk] * (tk // TQ), axis=0)

        def body(j, carry):
            r = pl.multiple_of(j * tk, tk)
            bias = jnp.where(mask_fn(r, r + kofs, qpos), 0.0, NEG)
            kb = k_ref[0, 0, pl.ds(r, tk), :]
            vb = jnp.concatenate([v_ref[0, 0, pl.ds(r, tk), :], ones_v], axis=1)
            for c in range(HP // HC):
                rows = slice(c * HC * TQ, (c + 1) * HC * TQ)
                s = _bdot_nt(q_ref[0, c * HC:(c + 1) * HC].reshape(HC * TQ, HEAD_DIM), kb)
                s = (s.reshape(HC, TQ, tk) + bias[None]).reshape(HC * TQ, tk)
                m_prev = m_scr[rows, :]
                m_new = jnp.maximum(m_prev, jnp.max(s, axis=1, keepdims=True))
                alpha = jnp.exp(m_prev - m_new)
                p = jnp.exp(s - jnp.concatenate([m_new] * (tk // LANES), axis=1)).astype(BF16)
                pv = jnp.dot(p, vb, preferred_element_type=F32)
                l_scr[rows, :] = alpha * l_scr[rows, :] + pv[:, HEAD_DIM:]
                acc_scr[rows, :] = alpha * acc_scr[rows, :] + pv[:, :HEAD_DIM]
                m_scr[rows, :] = m_new
            return carry

        lax.fori_loop(lo, hi, body, 0)

    tks = min(NSA_TK_SEL, s_len)

    def sel_mask(r, kpos, qpos):
        chosen = jnp.dot(sel, e_ref[:, pl.ds(r, tks)], preferred_element_type=F32)
        return jnp.logical_and(chosen > 0.5, kpos <= qpos)

    flash(ks_ref, vs_ref, tks, 0, (s0 + TQ - 1) // tks + 1, sel_mask)
    os_scr[...] = acc_scr[...] / l_scr[...]

    def win_mask(r, kpos, qpos):
        return jnp.logical_and(kpos <= qpos, kpos > qpos - WIN)

    tkw = min(NSA_TK_WIN, s_len)
    flash(kw_ref, vw_ref, tkw, jnp.maximum(s0 - WIN + 1, 0) // tkw, (s0 + TQ - 1) // tkw + 1, win_mask)

    gates = _sigmoid(gate_ref[0, 0])
    for hh in range(HP):
        rows = slice(hh * TQ, (hh + 1) * TQ)
        o_w = acc_scr[rows, :] / l_scr[rows, :]
        out = (gates[:, 3 * hh:3 * hh + 1] * o_c[rows, :]
               + gates[:, 3 * hh + 1:3 * hh + 2] * os_scr[rows, :]
               + gates[:, 3 * hh + 2:3 * hh + 3] * o_w)
        o_ref[0, :, hh * HEAD_DIM:(hh + 1) * HEAD_DIM] = out.astype(BF16)


def _nsa_attend(rp, cmp_kv, P, ovt, expand):
    b, _, s, _ = rp.shape
    TQ, HP, G = NSA_TQ, NSA_HPG, NSA_GROUPS
    ncp = cmp_kv.shape[2]
    R = HP * TQ
    full = lambda j0: pl.BlockSpec((1, 1, s, HEAD_DIM), lambda bi, g, i: (bi, j0 + g, 0, 0))
    cmp_spec = lambda j0: pl.BlockSpec((1, 1, ncp, HEAD_DIM), lambda bi, g, i: (bi, j0 + g, 0, 0))
    return pl.pallas_call(
        functools.partial(_nsa_kernel, s_len=s),
        grid=(b, G, s // TQ),
        in_specs=[pl.BlockSpec((1, HP, TQ, HEAD_DIM), lambda bi, g, i: (bi, g, i, 0)),
                  cmp_spec(0), cmp_spec(2),
                  full(R_KS), full(R_VS), full(R_KW), full(R_VW),
                  pl.BlockSpec((1, 1, TQ, LANES), lambda bi, g, i: (bi, J_GATE + g, i, 0)),
                  pl.BlockSpec(ovt.shape, lambda bi, g, i: (0, 0)),
                  pl.BlockSpec(expand.shape, lambda bi, g, i: (0, 0))],
        out_specs=pl.BlockSpec((1, TQ, HP * HEAD_DIM), lambda bi, g, i: (bi, i, g)),
        out_shape=jax.ShapeDtypeStruct((b, s, NSA_HEADS * HEAD_DIM), BF16),
        scratch_shapes=[pltpu.VMEM((R, LANES), F32), pltpu.VMEM((R, LANES), F32),
                        pltpu.VMEM((R, HEAD_DIM), F32), pltpu.VMEM((R, HEAD_DIM), F32)],
        compiler_params=_cparams(("parallel", "parallel", "arbitrary")),
        name="nsa_attend",
    )(rp, cmp_kv, cmp_kv, rp, rp, rp, rp, P, ovt, expand)


def _nsa(P, positions, cmp_pos_k, cmp_w1_k, cmp_w2_k, cmp_pos_v, cmp_w1_v, cmp_w2_v):
    b, _, s, _ = P.shape
    lanes = np.arange(LANES)
    invf = np.where(lanes < ROPE_DIM, ROPE_THETA ** (-(lanes % ROPE_HALF) / ROPE_HALF), 0.0)
    rp = _rope(P, positions.astype(F32).reshape(b, s, 1), jnp.asarray(invf, F32).reshape(1, LANES))

    ncp = s // CMP_STRIDE
    rows = rp[:, R_KC:R_KC + 4].reshape(b, 4, ncp, CMP_STRIDE * HEAD_DIM)
    w1 = jnp.stack([cmp_w1_k, cmp_w1_v]).astype(BF16)
    w2 = jnp.stack([cmp_w2_k, cmp_w2_v]).astype(BF16)
    pos = jnp.stack([cmp_pos_k, cmp_pos_v]).reshape(2, 1, CMP_LEN * HEAD_DIM)
    cmp_kv = _compress(rows, w1, w2, pos)

    nsb = s // SLC_LEN
    cmp_start = np.arange(ncp) * CMP_STRIDE
    slc_start = np.arange(nsb) * SLC_LEN
    ov = ((cmp_start[:, None] < slc_start[None, :] + SLC_LEN)
          & (cmp_start[:, None] + CMP_LEN > slc_start[None, :])
          & (np.arange(ncp)[:, None] < ncp - 1))
    ovt = jnp.asarray(ov.T.astype(np.float32))
    expand = (np.arange(LANES)[:, None] == (np.arange(s)[None, :] // SLC_LEN))
    expand = jnp.asarray(expand.astype(np.float32), BF16)
    return _nsa_attend(rp, cmp_kv, P, ovt, expand)


MERGE_TN = 512


def _merge_kernel(oa_ref, ob_ref, wg_ref, wn_ref, ma_ref, mb_ref, y_ref):
    ya = jnp.dot(oa_ref[0], wg_ref[...], preferred_element_type=F32)
    yb = jnp.dot(ob_ref[0], wn_ref[...], preferred_element_type=F32)
    for k in range(MERGE_TN // LANES):
        cols = slice(k * LANES, (k + 1) * LANES)
        y = _sigmoid(ma_ref[0, k]) * ya[:, cols] + _sigmoid(mb_ref[0, k]) * yb[:, cols]
        y_ref[0, :, cols] = y.astype(BF16)


def _merge(o_a, o_b, wg, wn, P):
    b, s, da = o_a.shape
    db = o_b.shape[2]
    d = wg.shape[1]
    tm, tn = min(1024, s), MERGE_TN
    nk = tn // LANES
    return pl.pallas_call(
        _merge_kernel,
        grid=(b, s // tm, d // tn),
        in_specs=[pl.BlockSpec((1, tm, da), lambda bi, i, j: (bi, i, 0)),
                  pl.BlockSpec((1, tm, db), lambda bi, i, j: (bi, i, 0)),
                  pl.BlockSpec((da, tn), lambda bi, i, j: (0, j)),
                  pl.BlockSpec((db, tn), lambda bi, i, j: (0, j)),
                  pl.BlockSpec((1, nk, tm, LANES), lambda bi, i, j: (bi, J_MA // nk + j, i, 0)),
                  pl.BlockSpec((1, nk, tm, LANES), lambda bi, i, j: (bi, J_MB // nk + j, i, 0))],
        out_specs=pl.BlockSpec((1, tm, tn), lambda bi, i, j: (bi, i, j)),
        out_shape=jax.ShapeDtypeStruct((b, s, d), BF16),
        compiler_params=_cparams(("parallel", "parallel", "arbitrary")),
        name="merge",
    )(o_a, o_b, wg, wn, P, P)


def _outproj_kernel(y_ref, w_ref, x_ref, g_ref, o_ref):
    o_ref[0] = x_ref[0] + g_ref[0] * jnp.dot(y_ref[0], w_ref[...], preferred_element_type=F32)


def _outproj(y, w, x, g):
    b, s, d = x.shape
    tm, tn = min(1024, s), 512
    return pl.pallas_call(
        _outproj_kernel,
        grid=(b, s // tm, d // tn),
        in_specs=[pl.BlockSpec((1, tm, d), lambda bi, i, j: (bi, i, 0)),
                  pl.BlockSpec((d, tn), lambda bi, i, j: (0, j)),
                  pl.BlockSpec((1, tm, tn), lambda bi, i, j: (bi, i, j)),
                  pl.BlockSpec((1, 1, tn), lambda bi, i, j: (bi, 0, j))],
        out_specs=pl.BlockSpec((1, tm, tn), lambda bi, i, j: (bi, i, j)),
        out_shape=jax.ShapeDtypeStruct((b, s, d), F32),
        compiler_params=_cparams(("parallel", "parallel", "arbitrary")),
        name="outproj",
    )(y, w, x, g.reshape(b, 1, d))


PEER_TB = 1024
PEER_TG = 128
PEER_AHEAD = 26
PEER_SLOTS = 32
PEER_SEL = PEER_HEADS * PEER_TOPK
MIN_F32 = -3.0e38


def _topk_rows(vals, payload=None):
    nrow = vals.shape[0]
    rid = lax.broadcasted_iota(jnp.int32, vals.shape, 0)
    out_v, out_i = [], []
    for _ in range(PEER_TOPK):
        m = jnp.max(vals, axis=0, keepdims=True)
        idx = jnp.min(jnp.where(vals == m, rid, nrow), axis=0, keepdims=True)
        hit = rid == idx
        out_v.append(m)
        if payload is None:
            out_i.append(idx)
        else:
            out_i.append(jnp.sum(jnp.where(hit, payload, 0), axis=0, keepdims=True))
        vals = jnp.where(hit, MIN_F32, vals)
    return jnp.concatenate(out_v, axis=0), jnp.concatenate(out_i, axis=0)


def _peer_topk_kernel(qh_ref, k1_ref, k2_ref, eidx_ref, gw_ref):
    def head(h, carry):
        s1 = _fdot_nt(k1_ref[h], qh_ref[0, 2 * h])
        s2 = _fdot_nt(k2_ref[h], qh_ref[0, 2 * h + 1])
        v1, i1 = _topk_rows(s1)
        v2, i2 = _topk_rows(s2)
        keep = [PEER_TOPK // (a + 1) for a in range(PEER_TOPK)]
        npad = (-sum(keep)) % 8
        cand = jnp.concatenate([v1[a:a + 1, :] + v2[0:keep[a], :] for a in range(PEER_TOPK)]
                               + [jnp.full((npad, v1.shape[1]), MIN_F32, F32)], axis=0)
        cidx = jnp.concatenate([i1[a:a + 1, :] * PEER_NKEYS + i2[0:keep[a], :] for a in range(PEER_TOPK)]
                               + [jnp.zeros((npad, v1.shape[1]), jnp.int32)], axis=0)
        sc, eidx = _topk_rows(cand, cidx)
        ex = jnp.exp(sc - sc[0:1, :])
        eidx_ref[h] = eidx
        gw_ref[h] = ex / jnp.sum(ex, axis=0, keepdims=True)
        return carry

    lax.fori_loop(0, PEER_HEADS, head, 0)


def _peer_topk(qh, keys1, keys2):
    b, nb, s, _ = qh.shape
    tb = min(PEER_TB, s)
    nt = s // tb
    kspec = pl.BlockSpec(keys1.shape, lambda bi, i: (0, 0, 0))
    ospec = pl.BlockSpec((PEER_HEADS, PEER_TOPK, tb), lambda bi, i: (0, 0, bi * nt + i))
    return pl.pallas_call(
        _peer_topk_kernel,
        grid=(b, nt),
        in_specs=[pl.BlockSpec((1, nb, tb, LANES), lambda bi, i: (bi, 0, i, 0)), kspec, kspec],
        out_specs=[ospec, ospec],
        out_shape=[jax.ShapeDtypeStruct((PEER_HEADS, PEER_TOPK, b * s), jnp.int32),
                   jax.ShapeDtypeStruct((PEER_HEADS, PEER_TOPK, b * s), F32)],
        compiler_params=_cparams(("parallel", "parallel")),
        name="peer_topk",
    )(qh, keys1, keys2)


PEER_BANKS = 4


def _peer_gather_kernel(idx_ref, idxs_ref, gw_ref, x_ref, uv_hbm, o_ref, b0, b1, b2, b3, a_scr, sem, *, tg):
    NS, D, NB = PEER_SEL, PEER_AHEAD, PEER_BANKS
    banks = (b0, b1, b2, b3)
    Q = b0.shape[0]
    RPE = uv_hbm.shape[1]
    SUBW = RPE // 2
    half = SUBW * LANES
    RT = 8

    def place(g, k, off):
        kk = k + off
        return banks[kk % NB], kk % NB, (g + kk // NB) & (Q - 1)

    def wait_rows(g, k):
        bank, kb, q = place(g, k, 0)
        pltpu.make_async_copy(uv_hbm.at[pl.ds(0, NS)], bank.at[q], sem.at[kb, q]).wait()

    rid = lax.broadcasted_iota(jnp.int32, (NS, NS), 0)
    cid = lax.broadcasted_iota(jnp.int32, (NS, NS), 1)
    eye = rid == cid
    ones = jnp.ones((LANES, LANES), BF16)

    def lane_sum_rep(m):
        hi, lo = _split_bf16(m)
        return (jnp.dot(hi, ones, preferred_element_type=F32)
                + jnp.dot(lo, ones, preferred_element_type=F32))

    def low(w):
        return pltpu.bitcast(w << 16, F32)

    def high(w):
        return pltpu.bitcast(w & jnp.uint32(0xFFFF0000), F32)

    def phase(g, k, do_dot, do_prev):
        t = g * NB + k
        if do_dot:
            wait_rows(g, k)
            xrow = x_ref[pl.ds(t, 1), :]
            bank_d, _, q_d = place(g, k, 0)
            acc = [jnp.zeros((RT, LANES), F32) for _ in range(NS // RT)]
            bank_i, k_i, q_i = place(g, k, D)
            base_i = t * NS
        if do_prev:
            bank_s, _, q_s = place(g, k, -1)
            par_s = (t - 1) & 1
            out_lo, out_hi = [], []
        for sg in range(SUBW):
            if do_dot:
                x_lo = jnp.broadcast_to(xrow[:, sg * LANES:(sg + 1) * LANES], (RT, LANES))
                x_hi = jnp.broadcast_to(xrow[:, half + sg * LANES:half + (sg + 1) * LANES], (RT, LANES))
            if do_prev:
                p_lo = jnp.zeros((RT, LANES), F32)
                p_hi = jnp.zeros((RT, LANES), F32)
            for rt in range(NS // RT):
                rows = slice(rt * RT, (rt + 1) * RT)
                if do_dot:
                    j = sg * (NS // RT) + rt
                    pltpu.make_async_copy(uv_hbm.at[idxs_ref[base_i + j]],
                                          bank_i.at[q_i, :, j, :], sem.at[k_i, q_i]).start(priority=j % 2)
                    w = bank_d[q_d, sg, rows, :]
                    acc[rt] = acc[rt] + low(w) * x_lo + high(w) * x_hi
                if do_prev:
                    w = bank_s[q_s, SUBW + sg, rows, :]
                    c = a_scr[par_s, rows, :]
                    p_lo = p_lo + low(w) * c
                    p_hi = p_hi + high(w) * c
            if do_prev:
                out_lo.append(jnp.sum(p_lo, axis=0, keepdims=True))
                out_hi.append(jnp.sum(p_hi, axis=0, keepdims=True))
        if do_dot:
            gw_rep = lane_sum_rep(jnp.where(eye, gw_ref[pl.ds(t, 1), :], 0.0))
            a_scr[t & 1] = gw_rep * _gelu(lane_sum_rep(jnp.concatenate(acc, axis=0)))
        if do_prev:
            o_ref[pl.ds(t - 1, 1), :] = jnp.concatenate(out_lo + out_hi, axis=1)

    step = pl.program_id(0)
    last = pl.num_programs(0) - 1

    @pl.when(step == 0)
    def _():
        for k in range(NB):
            def ramp(g, carry, k=k):
                bank, kb, q = place(g, k, 0)

                def body(j, c):
                    pltpu.make_async_copy(uv_hbm.at[idx_ref[(g * NB + k) * NS + j]],
                                          bank.at[q, :, j, :], sem.at[kb, q]).start()
                    return c
                return lax.fori_loop(0, NS, body, carry)
            lax.fori_loop(0, (D - k + NB - 1) // NB, ramp, 0)

    phase(0, 0, True, False)
    for k in range(1, NB):
        phase(0, k, True, True)

    def group(g, carry):
        for k in range(NB):
            phase(g, k, True, True)
        return carry
    lax.fori_loop(1, tg // NB, group, 0)
    phase(tg // NB, 0, False, True)

    @pl.when(step == last)
    def _():
        for r in range(D):
            wait_rows(tg // NB, r)


def _peer_gather(eidx, gw, h2r, uvc):
    t = eidx.shape[0]
    tg = min(PEER_TG, t)
    d = h2r.shape[1]
    rpe = uvc.shape[1]
    nq = PEER_SLOTS // PEER_BANKS
    assert PEER_BANKS == 4 and PEER_AHEAD % PEER_BANKS in (1, 2) and PEER_AHEAD + 2 <= PEER_SLOTS
    assert tg % PEER_BANKS == 0 and tg > PEER_AHEAD and t % tg == 0
    flat = eidx.reshape(-1)
    shifted = jnp.roll(flat, -PEER_AHEAD * PEER_SEL)
    bank = pltpu.VMEM((nq, rpe, PEER_SEL, LANES), jnp.uint32)
    return pl.pallas_call(
        functools.partial(_peer_gather_kernel, tg=tg),
        grid=(t // tg,),
        in_specs=[pl.BlockSpec((tg * PEER_SEL,), lambda i: (0,), memory_space=pltpu.SMEM),
                  pl.BlockSpec((tg * PEER_SEL,), lambda i: (i,), memory_space=pltpu.SMEM),
                  pl.BlockSpec((tg, PEER_SEL), lambda i: (i, 0)),
                  pl.BlockSpec((tg, d), lambda i: (i, 0)),
                  pl.BlockSpec(memory_space=pl.ANY)],
        out_specs=pl.BlockSpec((tg, d), lambda i: (i, 0)),
        out_shape=jax.ShapeDtypeStruct((t, d), F32),
        scratch_shapes=[bank, bank, bank, bank,
                        pltpu.VMEM((2, PEER_SEL, LANES), F32),
                        pltpu.SemaphoreType.DMA((PEER_BANKS, nq))],
        compiler_params=_cparams(("arbitrary",)),
        name="peer_gather",
    )(flat, shifted, gw, h2r, uvc)


PACK_EB = 256


def _pack_kernel(u_ref, v_ref, o_ref):
    half = u_ref.shape[1] // 2
    for tab, ref in enumerate((u_ref, v_ref)):
        for sg in range(half // LANES):
            lo = ref[:, sg * LANES:(sg + 1) * LANES].astype(BF16).astype(F32)
            hi = ref[:, half + sg * LANES:half + (sg + 1) * LANES].astype(BF16).astype(F32)
            word = (pltpu.bitcast(lo, jnp.uint32) >> 16) | (pltpu.bitcast(hi, jnp.uint32) & jnp.uint32(0xFFFF0000))
            o_ref[:, tab * (half // LANES) + sg, :] = word


def _pack_tables(u, v):
    e, d = u.shape
    eb = min(PACK_EB, e)
    return pl.pallas_call(
        _pack_kernel,
        grid=(e // eb,),
        in_specs=[pl.BlockSpec((eb, d), lambda i: (i, 0)), pl.BlockSpec((eb, d), lambda i: (i, 0))],
        out_specs=pl.BlockSpec((eb, d // LANES, LANES), lambda i: (i, 0, 0)),
        out_shape=jax.ShapeDtypeStruct((e, d // LANES, LANES), jnp.uint32),
        compiler_params=_cparams(("parallel",)),
        name="peer_pack",
    )(u, v)


def _peer(x1, norm_w, sc, sh, wq, keys1, keys2, u, v):
    b, s, d = x1.shape
    qh, h2 = _normproj(x1, norm_w, sc, sh, wq.astype(BF16), tm=1024, tn=512, emit_h=True)
    eidx, gw = _peer_topk(qh, keys1, keys2)
    eidx = jnp.transpose(eidx, (2, 0, 1)).reshape(b * s, PEER_SEL)
    gw = jnp.transpose(gw, (2, 0, 1)).reshape(b * s, PEER_SEL)
    uvc = _pack_tables(u, v)
    out = _peer_gather(eidx, gw, h2.reshape(b * s, d), uvc)
    return out.reshape(b, s, d)


def _final_kernel(x_ref, p_ref, g_ref, w_ref, o_ref):
    x = x_ref[0] + g_ref[0] * p_ref[0]
    o_ref[0] = x * lax.rsqrt(jnp.mean(x * x, axis=-1, keepdims=True) + EPS) * w_ref[...]


def _final(x1, peer, g2, wf):
    b, s, d = x1.shape
    tm = min(512, s)
    blk = pl.BlockSpec((1, tm, d), lambda bi, i: (bi, i, 0))
    return pl.pallas_call(
        _final_kernel,
        grid=(b, s // tm),
        in_specs=[blk, blk, pl.BlockSpec((1, 1, d), lambda bi, i: (bi, 0, 0)),
                  pl.BlockSpec((1, d), lambda bi, i: (0, 0))],
        out_specs=blk,
        out_shape=jax.ShapeDtypeStruct((b, s, d), F32),
        compiler_params=_cparams(("parallel", "parallel")),
        name="final_norm",
    )(x1, peer, g2.reshape(b, 1, d), wf.reshape(1, d))


def _pad_rows(a, mult=8):
    pad = (-a.shape[0]) % mult
    return jnp.pad(a, ((0, pad), (0, 0)))


def _permute_w_in(w):
    o_aa, o_bq, o_bg, o_ma, o_end = 4096, 4112, 7696, 7744, 11840
    d = w.shape[0]
    z = lambda n: jnp.zeros((d, n), w.dtype)
    hg = 3 * NSA_HPG
    cols = [w[:, 0:o_aa], w[:, o_bq:o_bg], w[:, o_ma:o_end],
            w[:, o_aa:o_bq], z(LANES - 16),
            w[:, o_bg:o_bg + hg], z(LANES - hg),
            w[:, o_bg + hg:o_ma], z(LANES - hg),
            z(LANES)]
    return jnp.concatenate(cols, axis=1).astype(BF16)


def kernel(x, c, positions, ada_w, ada_b, norm1_w, norm2_w, w_in, gdn_conv_w, gdn_A_log, gdn_dt_bias, gdn_norm_w, cmp_pos_k, cmp_w1_k, cmp_w2_k, cmp_pos_v, cmp_w1_v, cmp_w2_v, w_branch_gdn, w_branch_nsa, w_out, peer_wq, peer_keys1, peer_keys2, peer_u, peer_v, final_norm_w):
    b, s, d = x.shape
    l = 0
    mod = _ada(_pad_rows(c), ada_w[l], ada_b[l])[:b]
    sh1, sc1, g1, sh2, sc2, g2 = jnp.split(mod, 6, axis=-1)
    P = _normproj(x, norm1_w[l], sc1, sh1, _permute_w_in(w_in[l]), tm=1024, tn=1024, emit_h=False)
    o_a = _gdn(P, gdn_conv_w[l], gdn_A_log[l], gdn_dt_bias[l], gdn_norm_w[l])
    o_b = _nsa(P, positions, cmp_pos_k[l], cmp_w1_k[l], cmp_w2_k[l], cmp_pos_v[l], cmp_w1_v[l], cmp_w2_v[l])
    y = _merge(o_a, o_b, w_branch_gdn[l].astype(BF16), w_branch_nsa[l].astype(BF16), P)
    x1 = _outproj(y, w_out[l].astype(BF16), x, g1)
    peer = _peer(x1, norm2_w[l], sc2, sh2, peer_wq[l], peer_keys1[l], peer_keys2[l], peer_u[l], peer_v[l])
    return _final(x1, peer, g2, final_norm_w)
```

```python
import functools
import math

import numpy as np
import jax
import jax.numpy as jnp
from jax import lax
from jax.experimental import pallas as pl
from jax.experimental.pallas import tpu as pltpu

F32 = jnp.float32
BF16 = jnp.bfloat16
HI = lax.Precision.HIGHEST

LANES = 128
VMEM_LIMIT = 56 * 1024 * 1024

EPS = 1e-6
ROPE_THETA = 500000.0
HEAD_DIM = 128
ROPE_DIM = HEAD_DIM // 4
ROPE_HALF = ROPE_DIM // 2

GDN_HEADS = 8
GDN_CONV = 4
GDN_CHUNK = 64

NSA_HEADS = 16
NSA_GROUPS = 2
NSA_HPG = NSA_HEADS // NSA_GROUPS
CMP_LEN = 32
CMP_STRIDE = 16
CMP_HIDDEN = 256
SLC_LEN = 64
SLC_TOPK = 16
WIN = 512

PEER_HEADS = 8
PEER_NKEYS = 128
PEER_TOPK = 16

NEG = -1e30
BIG = 1e9

J_AQ, J_AK, J_AV, J_AZ = 0, 8, 16, 24
J_BQ = 32
N_NSA_BLOCKS = 28
J_MA, J_MB = 60, 76
J_SMALL = 92
J_GATE = 93


def _cparams(sem):
    return pltpu.CompilerParams(dimension_semantics=sem, vmem_limit_bytes=VMEM_LIMIT)


def _bdot(a, b):
    return jnp.dot(a.astype(BF16), b.astype(BF16), preferred_element_type=F32)


def _bdot_nt(a, b):
    return lax.dot_general(a.astype(BF16), b.astype(BF16), (((1,), (1,)), ((), ())),
                           preferred_element_type=F32)


def _split_bf16(a):
    hi = a.astype(BF16)
    return hi, (a - hi.astype(F32)).astype(BF16)


def _dot3(a, b, exact_a=False):
    bh, bl = _split_bf16(b)
    dot = functools.partial(jnp.dot, preferred_element_type=F32)
    if exact_a:
        return dot(a, bh) + dot(a, bl)
    ah, al = _split_bf16(a)
    return dot(ah, bh) + dot(al, bh) + dot(ah, bl)


def _dot3_nt(a, b):
    ah, al = _split_bf16(a)
    bh, bl = _split_bf16(b)
    dot = functools.partial(lax.dot_general, dimension_numbers=(((1,), (1,)), ((), ())),
                            preferred_element_type=F32)
    return dot(ah, bh) + dot(al, bh) + dot(ah, bl)


def _fdot(a, b):
    return jnp.dot(a, b, precision=HI, preferred_element_type=F32)


def _fdot_nt(a, b):
    return lax.dot_general(a, b, (((1,), (1,)), ((), ())), precision=HI,
                           preferred_element_type=F32)


def _sigmoid(x):
    return 1.0 / (1.0 + jnp.exp(-x))


def _silu(x):
    return x * _sigmoid(x)


def _gelu(x):
    return 0.5 * x * (1.0 + jnp.tanh(math.sqrt(2.0 / math.pi) * (x + 0.044715 * (x * x * x))))


def _softplus(x):
    return jnp.maximum(x, 0.0) + jnp.log(1.0 + jnp.exp(-jnp.abs(x)))


def _ada_kernel(c_ref, w_ref, b_ref, o_ref):
    o_ref[...] = _fdot(_silu(c_ref[...]), w_ref[...]) + b_ref[...]


def _ada(c_pad, ada_w, ada_b):
    m, d = c_pad.shape
    n = ada_w.shape[1]
    tn = 1024
    return pl.pallas_call(
        _ada_kernel,
        grid=(n // tn,),
        in_specs=[pl.BlockSpec((m, d), lambda j: (0, 0)),
                  pl.BlockSpec((d, tn), lambda j: (0, j)),
                  pl.BlockSpec((1, tn), lambda j: (0, j))],
        out_specs=pl.BlockSpec((m, tn), lambda j: (0, j)),
        out_shape=jax.ShapeDtypeStruct((m, n), F32),
        compiler_params=_cparams(("parallel",)),
        name="ada_mod",
    )(c_pad, ada_w, ada_b.reshape(1, n))


def _normproj_kernel(x_ref, nw_ref, sc_ref, sh_ref, w_ref, *rest, nk, emit_h):
    if emit_h:
        o_ref, h_out_ref, h_scr = rest
    else:
        o_ref, h_scr = rest

    @pl.when(pl.program_id(2) == 0)
    def _():
        x = x_ref[0]
        ms = jnp.mean(x * x, axis=-1, keepdims=True)
        y = x * lax.rsqrt(ms + EPS) * nw_ref[...]
        h = y * (1.0 + sc_ref[0]) + sh_ref[0]
        h_scr[...] = h.astype(BF16)
        if emit_h:
            h_out_ref[0] = h

    acc = jnp.dot(h_scr[...], w_ref[...], preferred_element_type=F32)
    for k in range(nk):
        o_ref[0, k] = acc[:, k * LANES:(k + 1) * LANES]


def _normproj(x, nw, sc, sh, w_bf16, *, tm, tn, emit_h):
    b, s, d = x.shape
    n = w_bf16.shape[1]
    tm = min(tm, s)
    nk = tn // LANES
    out_shape = [jax.ShapeDtypeStruct((b, n // LANES, s, LANES), F32)]
    out_specs = [pl.BlockSpec((1, nk, tm, LANES), lambda bi, i, j: (bi, j, i, 0))]
    if emit_h:
        out_shape.append(jax.ShapeDtypeStruct((b, s, d), F32))
        out_specs.append(pl.BlockSpec((1, tm, d), lambda bi, i, j: (bi, i, 0)))
    res = pl.pallas_call(
        functools.partial(_normproj_kernel, nk=nk, emit_h=emit_h),
        grid=(b, s // tm, n // tn),
        in_specs=[pl.BlockSpec((1, tm, d), lambda bi, i, j: (bi, i, 0)),
                  pl.BlockSpec((1, d), lambda bi, i, j: (0, 0)),
                  pl.BlockSpec((1, 1, d), lambda bi, i, j: (bi, 0, 0)),
                  pl.BlockSpec((1, 1, d), lambda bi, i, j: (bi, 0, 0)),
                  pl.BlockSpec((d, tn), lambda bi, i, j: (0, j))],
        out_specs=out_specs,
        out_shape=out_shape,
        scratch_shapes=[pltpu.VMEM((tm, d), BF16)],
        compiler_params=_cparams(("parallel", "parallel", "arbitrary")),
        name="normproj_h" if emit_h else "normproj",
    )(x, nw.reshape(1, d), sc.reshape(b, 1, d), sh.reshape(b, 1, d), w_bf16)
    return res if emit_h else res[0]


GDN_CB = 1024
GDN_GB = 256


def _gdn_local_kernel(q_ref, qh_ref, k_ref, kh_ref, v_ref, vh_ref, sm_ref,
                      cwq_ref, cwk_ref, cwv_ref, alog_ref, dtb_ref,
                      w_ref, u0_ref, qk_ref, qg_ref, kd_ref, ge_ref, *, cb):
    h = pl.program_id(1)
    i = pl.program_id(2)
    C, G = GDN_CHUNK, GDN_GB
    first = (i == 0)

    def conv_act(main_ref, halo_ref, cw_ref, r0):
        if r0 == 0:
            prev = jnp.where(first, 0.0, halo_ref[0, 0])
        else:
            prev = main_ref[0, 0, r0 - 8:r0, :]
        ext = jnp.concatenate([prev, main_ref[0, 0, r0:r0 + G, :]], axis=0)
        w = cw_ref[0]
        y = w[0:1, :] * ext[5:5 + G, :]
        for j in range(1, GDN_CONV):
            y = y + w[j:j + 1, :] * ext[5 + j:5 + j + G, :]
        return _silu(y)

    rid = lax.broadcasted_iota(jnp.int32, (G, G), 0)
    cid = lax.broadcasted_iota(jnp.int32, (G, G), 1)
    same = (rid // C) == (cid // C)
    incl = jnp.logical_and(same, rid >= cid)
    strict = jnp.logical_and(same, rid > cid)
    eye = rid == cid
    is_last = cid == (rid // C) * C + (C - 1)
    joins = []
    bs = 1
    while bs < C:
        joins.append(jnp.logical_and(jnp.logical_and((rid // (2 * bs)) == (cid // (2 * bs)),
                                                     (rid & (2 * bs - 1)) >= bs),
                                     (cid & (2 * bs - 1)) < bs))
        bs *= 2
    tri = jnp.where(incl, 1.0, 0.0).astype(BF16)
    eye_f = jnp.where(eye, 1.0, 0.0).astype(F32)
    lane = lax.broadcasted_iota(jnp.int32, (G, LANES), 1)
    neg_a = -jnp.exp(alog_ref[...])
    dtb = dtb_ref[...]

    for grp in range(cb // G):
        r0 = grp * G
        q = conv_act(q_ref, qh_ref, cwq_ref, r0)
        k = conv_act(k_ref, kh_ref, cwk_ref, r0)
        v = conv_act(v_ref, vh_ref, cwv_ref, r0)
        q = q * lax.rsqrt(jnp.sum(q * q, axis=-1, keepdims=True) + EPS) * (HEAD_DIM ** -0.5)
        k = k * lax.rsqrt(jnp.sum(k * k, axis=-1, keepdims=True) + EPS)

        sm = sm_ref[0, 0, r0:r0 + G, :]
        g_all = neg_a * _softplus(sm + dtb)
        gc_all = _dot3(tri, g_all, exact_a=True)
        gc = jnp.sum(jnp.where(lane == h, gc_all, 0.0), axis=1, keepdims=True)
        beta = jnp.sum(jnp.where(lane == GDN_HEADS + h, _sigmoid(sm), 0.0), axis=1, keepdims=True)
        gc_row = jnp.sum(jnp.where(eye, gc, 0.0), axis=0, keepdims=True)
        gc_last = jnp.sum(jnp.where(is_last, gc_row, 0.0), axis=1, keepdims=True)
        decay = jnp.where(incl, jnp.exp(jnp.where(incl, gc - gc_row, 0.0)), 0.0)
        gamma = jnp.exp(gc)

        kk = _dot3_nt(k, k)
        lmat = jnp.where(strict, decay * kk, 0.0) * beta
        tinv = eye_f - jnp.where(joins[0], lmat, 0.0)
        for lvl in range(1, len(joins)):
            tinv = tinv - _bdot(_bdot(tinv, jnp.where(joins[lvl], lmat, 0.0)), tinv)
        resid = (eye_f - tinv) - _dot3(lmat, tinv)
        tinv = tinv + _bdot(tinv, resid)
        rhs = jnp.concatenate([(beta * gamma) * k, beta * v], axis=1)
        wu = _dot3(tinv, rhs)
        qk = decay * _bdot_nt(q, k)

        rows = slice(r0, r0 + G)
        w_ref[0, 0, rows, :] = wu[:, :HEAD_DIM].astype(BF16)
        u0_ref[0, 0, rows, :] = wu[:, HEAD_DIM:]
        qg_ref[0, 0, rows, :] = (gamma * q).astype(BF16)
        kd_ref[0, 0, rows, :] = (jnp.exp(gc_last - gc) * k).astype(BF16)
        ge_all = jnp.broadcast_to(jnp.exp(gc_last), (G, LANES))
        for c in range(G // C):
            cr = slice(c * C, (c + 1) * C)
            qk_ref[0, 0, r0 + c * C:r0 + (c + 1) * C, :] = qk[cr, cr].astype(BF16)
            ge_ref[0, 0, pl.ds((i * (cb // G) + grp) * (G // C) + c, 1), :] = ge_all[c * C:c * C + 1, :]


def _gdn_local(P, cw, alog_pad, dtb_pad):
    b, _, s, _ = P.shape
    H, CB, C = GDN_HEADS, min(GDN_CB, s), GDN_CHUNK
    n = s // C

    def main(j0):
        return pl.BlockSpec((1, 1, CB, LANES), lambda bi, h, i: (bi, j0 + h, i, 0))

    def halo(j0):
        return pl.BlockSpec((1, 1, 8, LANES),
                            lambda bi, h, i: (bi, j0 + h, jnp.maximum(i * (CB // 8) - 1, 0), 0))

    def cws(j0):
        return pl.BlockSpec((1, GDN_CONV, LANES), lambda bi, h, i: (j0 + h, 0, 0))

    row = pl.BlockSpec((1, LANES), lambda bi, h, i: (0, 0))
    hs = lambda width: pl.BlockSpec((1, 1, CB, width), lambda bi, h, i: (bi, h, i, 0))
    return pl.pallas_call(
        functools.partial(_gdn_local_kernel, cb=CB),
        grid=(b, H, s // CB),
        in_specs=[main(J_AQ), halo(J_AQ), main(J_AK), halo(J_AK), main(J_AV), halo(J_AV),
                  pl.BlockSpec((1, 1, CB, LANES), lambda bi, h, i: (bi, J_SMALL, i, 0)),
                  cws(0), cws(8), cws(16), row, row],
        out_specs=[hs(LANES), hs(LANES), hs(C), hs(LANES), hs(LANES),
                   pl.BlockSpec((1, 1, n, LANES), lambda bi, h, i: (bi, h, 0, 0))],
        out_shape=[jax.ShapeDtypeStruct((b, H, s, LANES), BF16),
                   jax.ShapeDtypeStruct((b, H, s, LANES), F32),
                   jax.ShapeDtypeStruct((b, H, s, C), BF16),
                   jax.ShapeDtypeStruct((b, H, s, LANES), BF16),
                   jax.ShapeDtypeStruct((b, H, s, LANES), BF16),
                   jax.ShapeDtypeStruct((b, H, n, LANES), F32)],
        compiler_params=_cparams(("parallel", "parallel", "arbitrary")),
        name="gdn_local",
    )(P, P, P, P, P, P, P, cw, cw, cw, alog_pad, dtb_pad)


GDN_HB = 8


def _gdn_scan_kernel(w_ref, u0_ref, qk_ref, qg_ref, kd_ref, ge_ref, z_ref, nw_ref, o_ref, s_scr,
                     *, sb):
    C = GDN_CHUNK

    @pl.when(pl.program_id(2) == 0)
    def _():
        s_scr[...] = jnp.zeros_like(s_scr)

    nw = nw_ref[...]

    def body(n, carry):
        r = pl.multiple_of(n * C, C)
        for hh in range(GDN_HB):
            st = s_scr[hh]
            stb = st.astype(BF16)
            u = u0_ref[0, hh, pl.ds(r, C), :] - jnp.dot(w_ref[0, hh, pl.ds(r, C), :], stb,
                                                       preferred_element_type=F32)
            ub = u.astype(BF16)
            o = (jnp.dot(qg_ref[0, hh, pl.ds(r, C), :], stb, preferred_element_type=F32)
                 + jnp.dot(qk_ref[0, hh, pl.ds(r, C), :], ub, preferred_element_type=F32))
            ge = ge_ref[0, hh, pl.ds(n, 1), :]
            s_scr[hh] = ge * st + lax.dot_general(kd_ref[0, hh, pl.ds(r, C), :], ub,
                                                  (((0,), (0,)), ((), ())),
                                                  preferred_element_type=F32)
            on = o * lax.rsqrt(jnp.mean(o * o, axis=-1, keepdims=True) + EPS) * nw
            z = z_ref[0, hh, pl.ds(r, C), :]
            o_ref[0, pl.ds(r, C), hh * LANES:(hh + 1) * LANES] = (on * _silu(z)).astype(BF16)
        return carry

    lax.fori_loop(0, sb // C, body, 0)


def _gdn_scan(w, u0, qk, qg, kd, ge, P, norm_w):
    b, H, s, _ = w.shape
    C, HB = GDN_CHUNK, GDN_HB
    sb = min(1024, s)
    hs = lambda width: pl.BlockSpec((1, HB, sb, width), lambda bi, hb, i: (bi, hb, i, 0))
    return pl.pallas_call(
        functools.partial(_gdn_scan_kernel, sb=sb),
        grid=(b, H // HB, s // sb),
        in_specs=[hs(LANES), hs(LANES), hs(C), hs(LANES), hs(LANES),
                  pl.BlockSpec((1, HB, sb // C, LANES), lambda bi, hb, i: (bi, hb, i, 0)),
                  pl.BlockSpec((1, HB, sb, LANES), lambda bi, hb, i: (bi, J_AZ // HB + hb, i, 0)),
                  pl.BlockSpec((1, LANES), lambda bi, hb, i: (0, 0))],
        out_specs=pl.BlockSpec((1, sb, HB * LANES), lambda bi, hb, i: (bi, i, hb)),
        out_shape=jax.ShapeDtypeStruct((b, s, H * LANES), BF16),
        scratch_shapes=[pltpu.VMEM((HB, HEAD_DIM, HEAD_DIM), F32)],
        compiler_params=_cparams(("parallel", "parallel", "arbitrary")),
        name="gdn_scan",
    )(w, u0, qk, qg, kd, ge, P, norm_w.reshape(1, LANES))


def _pad_lanes_row(v):
    return jnp.pad(v.astype(F32), (0, LANES - v.shape[0])).reshape(1, LANES)


def _gdn(P, conv_w, a_log, dt_bias, norm_w):
    cw = jnp.transpose(conv_w.reshape(GDN_CONV, 3 * GDN_HEADS, LANES), (1, 0, 2))
    w, u0, qk, qg, kd, ge = _gdn_local(P, cw, _pad_lanes_row(a_log), _pad_lanes_row(dt_bias))
    return _gdn_scan(w, u0, qk, qg, kd, ge, P, norm_w)


NSA_TQ = 128
NSA_TK_SEL = 512
NSA_TK_WIN = 256
NSA_HC = 1
R_KC, R_KS, R_VS, R_KW, R_VW = 16, 20, 22, 24, 26
ROPE_NB = 4


def _rope_kernel(x_ref, pos_ref, invf_ref, o_ref, cos_scr, sin_scr):
    j = pl.program_id(2)
    lane = lax.broadcasted_iota(jnp.int32, cos_scr.shape, 1)

    @pl.when(j == 0)
    def _():
        ang = pos_ref[0] * invf_ref[...]
        sn = jnp.sin(ang)
        cos_scr[...] = jnp.where(lane < ROPE_DIM, jnp.cos(ang), 1.0)
        sin_scr[...] = jnp.where(lane < ROPE_HALF, -sn, jnp.where(lane < ROPE_DIM, sn, 0.0))

    is_q = j < NSA_HEADS // ROPE_NB
    scale = jnp.where(is_q, HEAD_DIM ** -0.5, 1.0)
    for k in range(ROPE_NB):
        x = x_ref[0, k]
        swapped = jnp.where(lane < ROPE_HALF, pltpu.roll(x, LANES - ROPE_HALF, axis=1),
                            pltpu.roll(x, ROPE_HALF, axis=1))
        rot = x * cos_scr[...] + swapped * sin_scr[...]
        out = rot * scale if k < 2 else jnp.where(is_q, rot, x) * scale
        o_ref[0, k] = out.astype(BF16)


def _rope(P, pos_f32, invf):
    b, _, s, _ = P.shape
    tr = min(1024, s)
    nb = ROPE_NB
    return pl.pallas_call(
        _rope_kernel,
        grid=(b, s // tr, N_NSA_BLOCKS // nb),
        in_specs=[pl.BlockSpec((1, nb, tr, LANES), lambda bi, i, j: (bi, J_BQ // nb + j, i, 0)),
                  pl.BlockSpec((1, tr, 1), lambda bi, i, j: (bi, i, 0)),
                  pl.BlockSpec((1, LANES), lambda bi, i, j: (0, 0))],
        out_specs=pl.BlockSpec((1, nb, tr, LANES), lambda bi, i, j: (bi, j, i, 0)),
        out_shape=jax.ShapeDtypeStruct((b, N_NSA_BLOCKS, s, LANES), BF16),
        scratch_shapes=[pltpu.VMEM((tr, LANES), F32), pltpu.VMEM((tr, LANES), F32)],
        compiler_params=_cparams(("parallel", "parallel", "arbitrary")),
        name="nsa_rope",
    )(P, pos_f32, invf)


def _compress_kernel(r_ref, w1_ref, w2_ref, pos_ref, o_ref):
    r = r_ref[0, 0]
    nr = r.shape[0]
    half = CMP_STRIDE * HEAD_DIM
    a = jnp.dot(r, w1_ref[0, :half, :], preferred_element_type=F32)
    bm = jnp.dot(r, w1_ref[0, half:, :], preferred_element_type=F32)
    pos8 = jnp.broadcast_to(pos_ref[0], (8, CMP_LEN * HEAD_DIM)).astype(BF16)
    pb = jnp.dot(pos8, w1_ref[0], preferred_element_type=F32)[0:1, :]
    hid = a + pltpu.roll(bm, nr - 1, axis=0) + pb
    out = jnp.dot(_gelu(hid).astype(BF16), w2_ref[0], preferred_element_type=F32)
    row = lax.broadcasted_iota(jnp.int32, out.shape, 0)
    o_ref[0, 0] = jnp.where(row < nr - 1, out, 0.0).astype(BF16)


def _compress(rows, w1, w2, pos):
    b, _, nr, width = rows.shape
    return pl.pallas_call(
        _compress_kernel,
        grid=(b, 4),
        in_specs=[pl.BlockSpec((1, 1, nr, width), lambda bi, j: (bi, j, 0, 0)),
                  pl.BlockSpec((1, CMP_LEN * HEAD_DIM, CMP_HIDDEN), lambda bi, j: (j // 2, 0, 0)),
                  pl.BlockSpec((1, CMP_HIDDEN, HEAD_DIM), lambda bi, j: (j // 2, 0, 0)),
                  pl.BlockSpec((1, 1, CMP_LEN * HEAD_DIM), lambda bi, j: (j // 2, 0, 0))],
        out_specs=pl.BlockSpec((1, 1, nr, HEAD_DIM), lambda bi, j: (bi, j, 0, 0)),
        out_shape=jax.ShapeDtypeStruct((b, 4, nr, HEAD_DIM), BF16),
        compiler_params=_cparams(("parallel", "arbitrary")),
        name="nsa_compress",
    )(rows, w1, w2, pos)


def _nsa_kernel(q_ref, kc_ref, vc_ref, ks_ref, vs_ref, kw_ref, vw_ref, gate_ref, ovt_ref, e_ref,
                o_ref, m_scr, l_scr, acc_scr, os_scr, *, s_len):
    i = pl.program_id(2)
    TQ, HP = NSA_TQ, NSA_HPG
    R = HP * TQ
    nsb = s_len // SLC_LEN
    ncp = s_len // CMP_STRIDE
    n_sel = min(SLC_TOPK, nsb)
    s0 = i * TQ
    q2 = q_ref[0].reshape(R, HEAD_DIM)

    sc = _bdot_nt(q2, kc_ref[0, 0])
    tq_r = s0 + (lax.broadcasted_iota(jnp.int32, (R, ncp), 0) & (TQ - 1))
    ncol = lax.broadcasted_iota(jnp.int32, (R, ncp), 1)
    valid = jnp.logical_and(ncol * CMP_STRIDE + (CMP_LEN - 1) <= tq_r, ncol < ncp - 1)
    scm = jnp.where(valid, sc, NEG)
    e = jnp.where(valid, jnp.exp(scm - jnp.max(scm, axis=1, keepdims=True)), 0.0)
    p_c = e / jnp.maximum(jnp.sum(e, axis=1, keepdims=True), 1e-30)
    o_c = _bdot(p_c, vc_ref[0, 0])
    psum = jnp.sum(p_c.reshape(HP, TQ, ncp), axis=0)
    imp = _fdot_nt(ovt_ref[...], psum)
    blk = lax.broadcasted_iota(jnp.int32, (nsb, TQ), 0)
    tq_l = s0 + lax.broadcasted_iota(jnp.int32, (nsb, TQ), 1)
    cur = tq_l // SLC_LEN
    forced = jnp.logical_or(blk == 0, jnp.logical_or(blk == cur, blk == cur - 1))
    imp = jnp.where(forced, BIG, imp)
    imp = jnp.where(blk * SLC_LEN <= tq_l, imp, NEG)
    rank = jnp.zeros((nsb, TQ), jnp.int32)
    for j in range(nsb):
        rj = imp[j:j + 1, :]
        beats = jnp.logical_or(rj > imp, jnp.logical_and(rj == imp, blk > j))
        rank = rank + jnp.where(beats, 1, 0)
    sel_t = jnp.where(rank < n_sel, 1.0, 0.0).astype(F32)
    if nsb < LANES:
        sel_t = jnp.concatenate([sel_t, jnp.zeros((LANES - nsb, TQ), F32)], axis=0)
    sel = jnp.transpose(sel_t).astype(BF16)

    ones_blk = jnp.ones((TQ, HEAD_DIM), BF16)
    HC = NSA_HC

    def flash(k_ref, v_ref, tk, lo, hi, mask_fn):
        m_scr[...] = jnp.full(m_scr.shape, NEG, F32)
        l_scr[...] = jnp.zeros(l_scr.shape, F32)
        acc_scr[...] = jnp.zeros(acc_scr.shape, F32)
        qpos = s0 + lax.broadcasted_iota(jnp.int32, (TQ, tk), 0)
        kofs = lax.broadcasted_iota(jnp.int32, (TQ, tk), 1)
        ones_v = jnp.concatenate([ones_blk] * (tk // TQ), axis=0)

        def body(j, carry):
            r = pl.multiple_of(j * tk, tk)
            bias = jnp.where(mask_fn(r, r + kofs, qpos), 0.0, NEG)
            kb = k_ref[0, 0, pl.ds(r, tk), :]
            vb = jnp.concatenate([v_ref[0, 0, pl.ds(r, tk), :], ones_v], axis=1)
            for c in range(HP // HC):
                rows = slice(c * HC * TQ, (c + 1) * HC * TQ)
                s = _bdot_nt(q_ref[0, c * HC:(c + 1) * HC].reshape(HC * TQ, HEAD_DIM), kb)
                s = (s.reshape(HC, TQ, tk) + bias[None]).reshape(HC * TQ, tk)
                m_prev = m_scr[rows, :]
                m_new = jnp.maximum(m_prev, jnp.max(s, axis=1, keepdims=True))
                alpha = jnp.exp(m_prev - m_new)
                p = jnp.exp(s - jnp.concatenate([m_new] * (tk // LANES), axis=1)).astype(BF16)
                pv = jnp.dot(p, vb, preferred_element_type=F32)
                l_scr[rows, :] = alpha * l_scr[rows, :] + pv[:, HEAD_DIM:]
                acc_scr[rows, :] = alpha * acc_scr[rows, :] + pv[:, :HEAD_DIM]
                m_scr[rows, :] = m_new
            return carry

        lax.fori_loop(lo, hi, body, 0)

    tks = min(NSA_TK_SEL, s_len)

    def sel_mask(r, kpos, qpos):
        chosen = jnp.dot(sel, e_ref[:, pl.ds(r, tks)], preferred_element_type=F32)
        return jnp.logical_and(chosen > 0.5, kpos <= qpos)

    flash(ks_ref, vs_ref, tks, 0, (s0 + TQ - 1) // tks + 1, sel_mask)
    os_scr[...] = acc_scr[...] / l_scr[...]

    def win_mask(r, kpos, qpos):
        return jnp.logical_and(kpos <= qpos, kpos > qpos - WIN)

    tkw = min(NSA_TK_WIN, s_len)
    flash(kw_ref, vw_ref, tkw, jnp.maximum(s0 - WIN + 1, 0) // tkw, (s0 + TQ - 1) // tkw + 1, win_mask)

    gates = _sigmoid(gate_ref[0, 0])
    for hh in range(HP):
        rows = slice(hh * TQ, (hh + 1) * TQ)
        o_w = acc_scr[rows, :] / l_scr[rows, :]
        out = (gates[:, 3 * hh:3 * hh + 1] * o_c[rows, :]
               + gates[:, 3 * hh + 1:3 * hh + 2] * os_scr[rows, :]
               + gates[:, 3 * hh + 2:3 * hh + 3] * o_w)
        o_ref[0, :, hh * HEAD_DIM:(hh + 1) * HEAD_DIM] = out.astype(BF16)


def _nsa_attend(rp, cmp_kv, P, ovt, expand):
    b, _, s, _ = rp.shape
    TQ, HP, G = NSA_TQ, NSA_HPG, NSA_GROUPS
    ncp = cmp_kv.shape[2]
    R = HP * TQ
    full = lambda j0: pl.BlockSpec((1, 1, s, HEAD_DIM), lambda bi, g, i: (bi, j0 + g, 0, 0))
    cmp_spec = lambda j0: pl.BlockSpec((1, 1, ncp, HEAD_DIM), lambda bi, g, i: (bi, j0 + g, 0, 0))
    return pl.pallas_call(
        functools.partial(_nsa_kernel, s_len=s),
        grid=(b, G, s // TQ),
        in_specs=[pl.BlockSpec((1, HP, TQ, HEAD_DIM), lambda bi, g, i: (bi, g, i, 0)),
                  cmp_spec(0), cmp_spec(2),
                  full(R_KS), full(R_VS), full(R_KW), full(R_VW),
                  pl.BlockSpec((1, 1, TQ, LANES), lambda bi, g, i: (bi, J_GATE + g, i, 0)),
                  pl.BlockSpec(ovt.shape, lambda bi, g, i: (0, 0)),
                  pl.BlockSpec(expand.shape, lambda bi, g, i: (0, 0))],
        out_specs=pl.BlockSpec((1, TQ, HP * HEAD_DIM), lambda bi, g, i: (bi, i, g)),
        out_shape=jax.ShapeDtypeStruct((b, s, NSA_HEADS * HEAD_DIM), BF16),
        scratch_shapes=[pltpu.VMEM((R, LANES), F32), pltpu.VMEM((R, LANES), F32),
                        pltpu.VMEM((R, HEAD_DIM), F32), pltpu.VMEM((R, HEAD_DIM), F32)],
        compiler_params=_cparams(("parallel", "parallel", "arbitrary")),
        name="nsa_attend",
    )(rp, cmp_kv, cmp_kv, rp, rp, rp, rp, P, ovt, expand)


def _nsa(P, positions, cmp_pos_k, cmp_w1_k, cmp_w2_k, cmp_pos_v, cmp_w1_v, cmp_w2_v):
    b, _, s, _ = P.shape
    lanes = np.arange(LANES)
    invf = np.where(lanes < ROPE_DIM, ROPE_THETA ** (-(lanes % ROPE_HALF) / ROPE_HALF), 0.0)
    rp = _rope(P, positions.astype(F32).reshape(b, s, 1), jnp.asarray(invf, F32).reshape(1, LANES))

    ncp = s // CMP_STRIDE
    rows = rp[:, R_KC:R_KC + 4].reshape(b, 4, ncp, CMP_STRIDE * HEAD_DIM)
    w1 = jnp.stack([cmp_w1_k, cmp_w1_v]).astype(BF16)
    w2 = jnp.stack([cmp_w2_k, cmp_w2_v]).astype(BF16)
    pos = jnp.stack([cmp_pos_k, cmp_pos_v]).reshape(2, 1, CMP_LEN * HEAD_DIM)
    cmp_kv = _compress(rows, w1, w2, pos)

    nsb = s // SLC_LEN
    cmp_start = np.arange(ncp) * CMP_STRIDE
    slc_start = np.arange(nsb) * SLC_LEN
    ov = ((cmp_start[:, None] < slc_start[None, :] + SLC_LEN)
          & (cmp_start[:, None] + CMP_LEN > slc_start[None, :])
          & (np.arange(ncp)[:, None] < ncp - 1))
    ovt = jnp.asarray(ov.T.astype(np.float32))
    expand = (np.arange(LANES)[:, None] == (np.arange(s)[None, :] // SLC_LEN))
    expand = jnp.asarray(expand.astype(np.float32), BF16)
    return _nsa_attend(rp, cmp_kv, P, ovt, expand)


MERGE_TN = 512


def _merge_kernel(oa_ref, ob_ref, wg_ref, wn_ref, ma_ref, mb_ref, y_ref):
    ya = jnp.dot(oa_ref[0], wg_ref[...], preferred_element_type=F32)
    yb = jnp.dot(ob_ref[0], wn_ref[...], preferred_element_type=F32)
    for k in range(MERGE_TN // LANES):
        cols = slice(k * LANES, (k + 1) * LANES)
        y = _sigmoid(ma_ref[0, k]) * ya[:, cols] + _sigmoid(mb_ref[0, k]) * yb[:, cols]
        y_ref[0, :, cols] = y.astype(BF16)


def _merge(o_a, o_b, wg, wn, P):
    b, s, da = o_a.shape
    db = o_b.shape[2]
    d = wg.shape[1]
    tm, tn = min(1024, s), MERGE_TN
    nk = tn // LANES
    return pl.pallas_call(
        _merge_kernel,
        grid=(b, s // tm, d // tn),
        in_specs=[pl.BlockSpec((1, tm, da), lambda bi, i, j: (bi, i, 0)),
                  pl.BlockSpec((1, tm, db), lambda bi, i, j: (bi, i, 0)),
                  pl.BlockSpec((da, tn), lambda bi, i, j: (0, j)),
                  pl.BlockSpec((db, tn), lambda bi, i, j: (0, j)),
                  pl.BlockSpec((1, nk, tm, LANES), lambda bi, i, j: (bi, J_MA // nk + j, i, 0)),
                  pl.BlockSpec((1, nk, tm, LANES), lambda bi, i, j: (bi, J_MB // nk + j, i, 0))],
        out_specs=pl.BlockSpec((1, tm, tn), lambda bi, i, j: (bi, i, j)),
        out_shape=jax.ShapeDtypeStruct((b, s, d), BF16),
        compiler_params=_cparams(("parallel", "parallel", "arbitrary")),
        name="merge",
    )(o_a, o_b, wg, wn, P, P)


def _outproj_kernel(y_ref, w_ref, x_ref, g_ref, o_ref):
    o_ref[0] = x_ref[0] + g_ref[0] * jnp.dot(y_ref[0], w_ref[...], preferred_element_type=F32)


def _outproj(y, w, x, g):
    b, s, d = x.shape
    tm, tn = min(1024, s), 512
    return pl.pallas_call(
        _outproj_kernel,
        grid=(b, s // tm, d // tn),
        in_specs=[pl.BlockSpec((1, tm, d), lambda bi, i, j: (bi, i, 0)),
                  pl.BlockSpec((d, tn), lambda bi, i, j: (0, j)),
                  pl.BlockSpec((1, tm, tn), lambda bi, i, j: (bi, i, j)),
                  pl.BlockSpec((1, 1, tn), lambda bi, i, j: (bi, 0, j))],
        out_specs=pl.BlockSpec((1, tm, tn), lambda bi, i, j: (bi, i, j)),
        out_shape=jax.ShapeDtypeStruct((b, s, d), F32),
        compiler_params=_cparams(("parallel", "parallel", "arbitrary")),
        name="outproj",
    )(y, w, x, g.reshape(b, 1, d))


PEER_TB = 1024
PEER_TG = 128
PEER_AHEAD = 26
PEER_SLOTS = 32
PEER_SEL = PEER_HEADS * PEER_TOPK
MIN_F32 = -3.0e38


def _topk_rows(vals, payload=None):
    nrow = vals.shape[0]
    rid = lax.broadcasted_iota(jnp.int32, vals.shape, 0)
    out_v, out_i = [], []
    for _ in range(PEER_TOPK):
        m = jnp.max(vals, axis=0, keepdims=True)
        idx = jnp.min(jnp.where(vals == m, rid, nrow), axis=0, keepdims=True)
        hit = rid == idx
        out_v.append(m)
        if payload is None:
            out_i.append(idx)
        else:
            out_i.append(jnp.sum(jnp.where(hit, payload, 0), axis=0, keepdims=True))
        vals = jnp.where(hit, MIN_F32, vals)
    return jnp.concatenate(out_v, axis=0), jnp.concatenate(out_i, axis=0)


def _peer_topk_kernel(qh_ref, k1_ref, k2_ref, eidx_ref, gw_ref):
    def head(h, carry):
        s1 = _fdot_nt(k1_ref[h], qh_ref[0, 2 * h])
        s2 = _fdot_nt(k2_ref[h], qh_ref[0, 2 * h + 1])
        v1, i1 = _topk_rows(s1)
        v2, i2 = _topk_rows(s2)
        keep = [PEER_TOPK // (a + 1) for a in range(PEER_TOPK)]
        npad = (-sum(keep)) % 8
        cand = jnp.concatenate([v1[a:a + 1, :] + v2[0:keep[a], :] for a in range(PEER_TOPK)]
                               + [jnp.full((npad, v1.shape[1]), MIN_F32, F32)], axis=0)
        cidx = jnp.concatenate([i1[a:a + 1, :] * PEER_NKEYS + i2[0:keep[a], :] for a in range(PEER_TOPK)]
                               + [jnp.zeros((npad, v1.shape[1]), jnp.int32)], axis=0)
        sc, eidx = _topk_rows(cand, cidx)
        ex = jnp.exp(sc - sc[0:1, :])
        eidx_ref[h] = eidx
        gw_ref[h] = ex / jnp.sum(ex, axis=0, keepdims=True)
        return carry

    lax.fori_loop(0, PEER_HEADS, head, 0)


def _peer_topk(qh, keys1, keys2):
    b, nb, s, _ = qh.shape
    tb = min(PEER_TB, s)
    nt = s // tb
    kspec = pl.BlockSpec(keys1.shape, lambda bi, i: (0, 0, 0))
    ospec = pl.BlockSpec((PEER_HEADS, PEER_TOPK, tb), lambda bi, i: (0, 0, bi * nt + i))
    return pl.pallas_call(
        _peer_topk_kernel,
        grid=(b, nt),
        in_specs=[pl.BlockSpec((1, nb, tb, LANES), lambda bi, i: (bi, 0, i, 0)), kspec, kspec],
        out_specs=[ospec, ospec],
        out_shape=[jax.ShapeDtypeStruct((PEER_HEADS, PEER_TOPK, b * s), jnp.int32),
                   jax.ShapeDtypeStruct((PEER_HEADS, PEER_TOPK, b * s), F32)],
        compiler_params=_cparams(("parallel", "parallel")),
        name="peer_topk",
    )(qh, keys1, keys2)


PEER_BANKS = 4


def _peer_gather_kernel(idx_ref, idxs_ref, gw_ref, x_ref, uv_hbm, o_ref, b0, b1, b2, b3, a_scr, sem, *, tg):
    NS, D, NB = PEER_SEL, PEER_AHEAD, PEER_BANKS
    banks = (b0, b1, b2, b3)
    Q = b0.shape[0]
    RPE = uv_hbm.shape[1]
    SUBW = RPE // 2
    half = SUBW * LANES
    RT = 8

    def place(g, k, off):
        kk = k + off
        return banks[kk % NB], kk % NB, (g + kk // NB) & (Q - 1)

    def wait_rows(g, k):
        bank, kb, q = place(g, k, 0)
        pltpu.make_async_copy(uv_hbm.at[pl.ds(0, NS)], bank.at[q], sem.at[kb, q]).wait()

    rid = lax.broadcasted_iota(jnp.int32, (NS, NS), 0)
    cid = lax.broadcasted_iota(jnp.int32, (NS, NS), 1)
    eye = rid == cid
    ones = jnp.ones((LANES, LANES), BF16)

    def lane_sum_rep(m):
        hi, lo = _split_bf16(m)
        return (jnp.dot(hi, ones, preferred_element_type=F32)
                + jnp.dot(lo, ones, preferred_element_type=F32))

    def low(w):
        return pltpu.bitcast(w << 16, F32)

    def high(w):
        return pltpu.bitcast(w & jnp.uint32(0xFFFF0000), F32)

    def phase(g, k, do_dot, do_prev):
        t = g * NB + k
        if do_dot:
            wait_rows(g, k)
            xrow = x_ref[pl.ds(t, 1), :]
            bank_d, _, q_d = place(g, k, 0)
            acc = [jnp.zeros((RT, LANES), F32) for _ in range(NS // RT)]
            bank_i, k_i, q_i = place(g, k, D)
            base_i = t * NS
        if do_prev:
            bank_s, _, q_s = place(g, k, -1)
            par_s = (t - 1) & 1
            out_lo, out_hi = [], []
        for sg in range(SUBW):
            if do_dot:
                x_lo = jnp.broadcast_to(xrow[:, sg * LANES:(sg + 1) * LANES], (RT, LANES))
                x_hi = jnp.broadcast_to(xrow[:, half + sg * LANES:half + (sg + 1) * LANES], (RT, LANES))
            if do_prev:
                p_lo = jnp.zeros((RT, LANES), F32)
                p_hi = jnp.zeros((RT, LANES), F32)
            for rt in range(NS // RT):
                rows = slice(rt * RT, (rt + 1) * RT)
                if do_dot:
                    j = sg * (NS // RT) + rt
                    pltpu.make_async_copy(uv_hbm.at[idxs_ref[base_i + j]],
                                          bank_i.at[q_i, :, j, :], sem.at[k_i, q_i]).start(priority=j % 2)
                    w = bank_d[q_d, sg, rows, :]
                    acc[rt] = acc[rt] + low(w) * x_lo + high(w) * x_hi
                if do_prev:
                    w = bank_s[q_s, SUBW + sg, rows, :]
                    c = a_scr[par_s, rows, :]
                    p_lo = p_lo + low(w) * c
                    p_hi = p_hi + high(w) * c
            if do_prev:
                out_lo.append(jnp.sum(p_lo, axis=0, keepdims=True))
                out_hi.append(jnp.sum(p_hi, axis=0, keepdims=True))
        if do_dot:
            gw_rep = lane_sum_rep(jnp.where(eye, gw_ref[pl.ds(t, 1), :], 0.0))
            a_scr[t & 1] = gw_rep * _gelu(lane_sum_rep(jnp.concatenate(acc, axis=0)))
        if do_prev:
            o_ref[pl.ds(t - 1, 1), :] = jnp.concatenate(out_lo + out_hi, axis=1)

    step = pl.program_id(0)
    last = pl.num_programs(0) - 1

    @pl.when(step == 0)
    def _():
        for k in range(NB):
            def ramp(g, carry, k=k):
                bank, kb, q = place(g, k, 0)

                def body(j, c):
                    pltpu.make_async_copy(uv_hbm.at[idx_ref[(g * NB + k) * NS + j]],
                                          bank.at[q, :, j, :], sem.at[kb, q]).start()
                    return c
                return lax.fori_loop(0, NS, body, carry)
            lax.fori_loop(0, (D - k + NB - 1) // NB, ramp, 0)

    phase(0, 0, True, False)
    for k in range(1, NB):
        phase(0, k, True, True)

    def group(g, carry):
        for k in range(NB):
            phase(g, k, True, True)
        return carry
    lax.fori_loop(1, tg // NB, group, 0)
    phase(tg // NB, 0, False, True)

    @pl.when(step == last)
    def _():
        for r in range(D):
            wait_rows(tg // NB, r)


def _peer_gather(eidx, gw, h2r, uvc):
    t = eidx.shape[0]
    tg = min(PEER_TG, t)
    d = h2r.shape[1]
    rpe = uvc.shape[1]
    nq = PEER_SLOTS // PEER_BANKS
    assert PEER_BANKS == 4 and PEER_AHEAD % PEER_BANKS in (1, 2) and PEER_AHEAD + 2 <= PEER_SLOTS
    assert tg % PEER_BANKS == 0 and tg > PEER_AHEAD and t % tg == 0
    flat = eidx.reshape(-1)
    shifted = jnp.roll(flat, -PEER_AHEAD * PEER_SEL)
    bank = pltpu.VMEM((nq, rpe, PEER_SEL, LANES), jnp.uint32)
    return pl.pallas_call(
        functools.partial(_peer_gather_kernel, tg=tg),
        grid=(t // tg,),
        in_specs=[pl.BlockSpec((tg * PEER_SEL,), lambda i: (0,), memory_space=pltpu.SMEM),
                  pl.BlockSpec((tg * PEER_SEL,), lambda i: (i,), memory_space=pltpu.SMEM),
                  pl.BlockSpec((tg, PEER_SEL), lambda i: (i, 0)),
                  pl.BlockSpec((tg, d), lambda i: (i, 0)),
                  pl.BlockSpec(memory_space=pl.ANY)],
        out_specs=pl.BlockSpec((tg, d), lambda i: (i, 0)),
        out_shape=jax.ShapeDtypeStruct((t, d), F32),
        scratch_shapes=[bank, bank, bank, bank,
                        pltpu.VMEM((2, PEER_SEL, LANES), F32),
                        pltpu.SemaphoreType.DMA((PEER_BANKS, nq))],
        compiler_params=_cparams(("arbitrary",)),
        name="peer_gather",
    )(flat, shifted, gw, h2r, uvc)


PACK_EB = 256


def _pack_kernel(u_ref, v_ref, o_ref):
    half = u_ref.shape[1] // 2
    for tab, ref in enumerate((u_ref, v_ref)):
        for sg in range(half // LANES):
            lo = ref[:, sg * LANES:(sg + 1) * LANES].astype(BF16).astype(F32)
            hi = ref[:, half + sg * LANES:half + (sg + 1) * LANES].astype(BF16).astype(F32)
            word = (pltpu.bitcast(lo, jnp.uint32) >> 16) | (pltpu.bitcast(hi, jnp.uint32) & jnp.uint32(0xFFFF0000))
            o_ref[:, tab * (half // LANES) + sg, :] = word


def _pack_tables(u, v):
    e, d = u.shape
    eb = min(PACK_EB, e)
    return pl.pallas_call(
        _pack_kernel,
        grid=(e // eb,),
        in_specs=[pl.BlockSpec((eb, d), lambda i: (i, 0)), pl.BlockSpec((eb, d), lambda i: (i, 0))],
        out_specs=pl.BlockSpec((eb, d // LANES, LANES), lambda i: (i, 0, 0)),
        out_shape=jax.ShapeDtypeStruct((e, d // LANES, LANES), jnp.uint32),
        compiler_params=_cparams(("parallel",)),
        name="peer_pack",
    )(u, v)


def _peer(x1, norm_w, sc, sh, wq, keys1, keys2, u, v):
    b, s, d = x1.shape
    qh, h2 = _normproj(x1, norm_w, sc, sh, wq.astype(BF16), tm=1024, tn=512, emit_h=True)
    eidx, gw = _peer_topk(qh, keys1, keys2)
    eidx = jnp.transpose(eidx, (2, 0, 1)).reshape(b * s, PEER_SEL)
    gw = jnp.transpose(gw, (2, 0, 1)).reshape(b * s, PEER_SEL)
    uvc = _pack_tables(u, v)
    out = _peer_gather(eidx, gw, h2.reshape(b * s, d), uvc)
    return out.reshape(b, s, d)


def _final_kernel(x_ref, p_ref, g_ref, w_ref, o_ref):
    x = x_ref[0] + g_ref[0] * p_ref[0]
    o_ref[0] = x * lax.rsqrt(jnp.mean(x * x, axis=-1, keepdims=True) + EPS) * w_ref[...]


def _final(x1, peer, g2, wf):
    b, s, d = x1.shape
    tm = min(512, s)
    blk = pl.BlockSpec((1, tm, d), lambda bi, i: (bi, i, 0))
    return pl.pallas_call(
        _final_kernel,
        grid=(b, s // tm),
        in_specs=[blk, blk, pl.BlockSpec((1, 1, d), lambda bi, i: (bi, 0, 0)),
                  pl.BlockSpec((1, d), lambda bi, i: (0, 0))],
        out_specs=blk,
        out_shape=jax.ShapeDtypeStruct((b, s, d), F32),
        compiler_params=_cparams(("parallel", "parallel")),
        name="final_norm",
    )(x1, peer, g2.reshape(b, 1, d), wf.reshape(1, d))


def _pad_rows(a, mult=8):
    pad = (-a.shape[0]) % mult
    return jnp.pad(a, ((0, pad), (0, 0)))


def _permute_w_in(w):
    o_aa, o_bq, o_bg, o_ma, o_end = 4096, 4112, 7696, 7744, 11840
    d = w.shape[0]
    z = lambda n: jnp.zeros((d, n), w.dtype)
    hg = 3 * NSA_HPG
    cols = [w[:, 0:o_aa], w[:, o_bq:o_bg], w[:, o_ma:o_end],
            w[:, o_aa:o_bq], z(LANES - 16),
            w[:, o_bg:o_bg + hg], z(LANES - hg),
            w[:, o_bg + hg:o_ma], z(LANES - hg),
            z(LANES)]
    return jnp.concatenate(cols, axis=1).astype(BF16)


def kernel(x, c, positions, ada_w, ada_b, norm1_w, norm2_w, w_in, gdn_conv_w, gdn_A_log, gdn_dt_bias, gdn_norm_w, cmp_pos_k, cmp_w1_k, cmp_w2_k, cmp_pos_v, cmp_w1_v, cmp_w2_v, w_branch_gdn, w_branch_nsa, w_out, peer_wq, peer_keys1, peer_keys2, peer_u, peer_v, final_norm_w):
    b, s, d = x.shape
    l = 0
    mod = _ada(_pad_rows(c), ada_w[l], ada_b[l])[:b]
    sh1, sc1, g1, sh2, sc2, g2 = jnp.split(mod, 6, axis=-1)
    P = _normproj(x, norm1_w[l], sc1, sh1, _permute_w_in(w_in[l]), tm=1024, tn=1024, emit_h=False)
    o_a = _gdn(P, gdn_conv_w[l], gdn_A_log[l], gdn_dt_bias[l], gdn_norm_w[l])
    o_b = _nsa(P, positions, cmp_pos_k[l], cmp_w1_k[l], cmp_w2_k[l], cmp_pos_v[l], cmp_w1_v[l], cmp_w2_v[l])
    y = _merge(o_a, o_b, w_branch_gdn[l].astype(BF16), w_branch_nsa[l].astype(BF16), P)
    x1 = _outproj(y, w_out[l].astype(BF16), x, g1)
    peer = _peer(x1, norm2_w[l], sc2, sh2, peer_wq[l], peer_keys1[l], peer_keys2[l], peer_u[l], peer_v[l])
    return _final(x1, peer, g2, final_norm_w)
```

```python
import functools
import math

import numpy as np
import jax
import jax.numpy as jnp
from jax import lax
from jax.experimental import pallas as pl
from jax.experimental.pallas import tpu as pltpu

F32 = jnp.float32
BF16 = jnp.bfloat16
HI = lax.Precision.HIGHEST

LANES = 128
VMEM_LIMIT = 56 * 1024 * 1024

EPS = 1e-6
ROPE_THETA = 500000.0
HEAD_DIM = 128
ROPE_DIM = HEAD_DIM // 4
ROPE_HALF = ROPE_DIM // 2

GDN_HEADS = 8
GDN_CONV = 4
GDN_CHUNK = 64

NSA_HEADS = 16
NSA_GROUPS = 2
NSA_HPG = NSA_HEADS // NSA_GROUPS
CMP_LEN = 32
CMP_STRIDE = 16
CMP_HIDDEN = 256
SLC_LEN = 64
SLC_TOPK = 16
WIN = 512

PEER_HEADS = 8
PEER_NKEYS = 128
PEER_TOPK = 16

NEG = -1e30
BIG = 1e9

J_AQ, J_AK, J_AV, J_AZ = 0, 8, 16, 24
J_BQ = 32
N_NSA_BLOCKS = 28
J_MA, J_MB = 60, 76
J_SMALL = 92
J_GATE = 93


def _cparams(sem):
    return pltpu.CompilerParams(dimension_semantics=sem, vmem_limit_bytes=VMEM_LIMIT)


def _bdot(a, b):
    return jnp.dot(a.astype(BF16), b.astype(BF16), preferred_element_type=F32)


def _bdot_nt(a, b):
    return lax.dot_general(a.astype(BF16), b.astype(BF16), (((1,), (1,)), ((), ())),
                           preferred_element_type=F32)


def _split_bf16(a):
    hi = a.astype(BF16)
    return hi, (a - hi.astype(F32)).astype(BF16)


def _dot3(a, b, exact_a=False):
    bh, bl = _split_bf16(b)
    dot = functools.partial(jnp.dot, preferred_element_type=F32)
    if exact_a:
        return dot(a, bh) + dot(a, bl)
    ah, al = _split_bf16(a)
    return dot(ah, bh) + dot(al, bh) + dot(ah, bl)


def _dot3_nt(a, b):
    ah, al = _split_bf16(a)
    bh, bl = _split_bf16(b)
    dot = functools.partial(lax.dot_general, dimension_numbers=(((1,), (1,)), ((), ())),
                            preferred_element_type=F32)
    return dot(ah, bh) + dot(al, bh) + dot(ah, bl)


def _fdot(a, b):
    return jnp.dot(a, b, precision=HI, preferred_element_type=F32)


def _fdot_nt(a, b):
    return lax.dot_general(a, b, (((1,), (1,)), ((), ())), precision=HI,
                           preferred_element_type=F32)


def _sigmoid(x):
    return 1.0 / (1.0 + jnp.exp(-x))


def _silu(x):
    return x * _sigmoid(x)


def _gelu(x):
    return 0.5 * x * (1.0 + jnp.tanh(math.sqrt(2.0 / math.pi) * (x + 0.044715 * (x * x * x))))


def _softplus(x):
    return jnp.maximum(x, 0.0) + jnp.log(1.0 + jnp.exp(-jnp.abs(x)))


def _ada_kernel(c_ref, w_ref, b_ref, o_ref):
    o_ref[...] = _fdot(_silu(c_ref[...]), w_ref[...]) + b_ref[...]


def _ada(c_pad, ada_w, ada_b):
    m, d = c_pad.shape
    n = ada_w.shape[1]
    tn = 1024
    return pl.pallas_call(
        _ada_kernel,
        grid=(n // tn,),
        in_specs=[pl.BlockSpec((m, d), lambda j: (0, 0)),
                  pl.BlockSpec((d, tn), lambda j: (0, j)),
                  pl.BlockSpec((1, tn), lambda j: (0, j))],
        out_specs=pl.BlockSpec((m, tn), lambda j: (0, j)),
        out_shape=jax.ShapeDtypeStruct((m, n), F32),
        compiler_params=_cparams(("parallel",)),
        name="ada_mod",
    )(c_pad, ada_w, ada_b.reshape(1, n))


def _normproj_kernel(x_ref, nw_ref, sc_ref, sh_ref, w_ref, *rest, nk, emit_h):
    if emit_h:
        o_ref, h_out_ref, h_scr = rest
    else:
        o_ref, h_scr = rest

    @pl.when(pl.program_id(2) == 0)
    def _():
        x = x_ref[0]
        ms = jnp.mean(x * x, axis=-1, keepdims=True)
        y = x * lax.rsqrt(ms + EPS) * nw_ref[...]
        h = y * (1.0 + sc_ref[0]) + sh_ref[0]
        h_scr[...] = h.astype(BF16)
        if emit_h:
            h_out_ref[0] = h

    acc = jnp.dot(h_scr[...], w_ref[...], preferred_element_type=F32)
    for k in range(nk):
        o_ref[0, k] = acc[:, k * LANES:(k + 1) * LANES]


def _normproj(x, nw, sc, sh, w_bf16, *, tm, tn, emit_h):
    b, s, d = x.shape
    n = w_bf16.shape[1]
    tm = min(tm, s)
    nk = tn // LANES
    out_shape = [jax.ShapeDtypeStruct((b, n // LANES, s, LANES), F32)]
    out_specs = [pl.BlockSpec((1, nk, tm, LANES), lambda bi, i, j: (bi, j, i, 0))]
    if emit_h:
        out_shape.append(jax.ShapeDtypeStruct((b, s, d), F32))
        out_specs.append(pl.BlockSpec((1, tm, d), lambda bi, i, j: (bi, i, 0)))
    res = pl.pallas_call(
        functools.partial(_normproj_kernel, nk=nk, emit_h=emit_h),
        grid=(b, s // tm, n // tn),
        in_specs=[pl.BlockSpec((1, tm, d), lambda bi, i, j: (bi, i, 0)),
                  pl.BlockSpec((1, d), lambda bi, i, j: (0, 0)),
                  pl.BlockSpec((1, 1, d), lambda bi, i, j: (bi, 0, 0)),
                  pl.BlockSpec((1, 1, d), lambda bi, i, j: (bi, 0, 0)),
                  pl.BlockSpec((d, tn), lambda bi, i, j: (0, j))],
        out_specs=out_specs,
        out_shape=out_shape,
        scratch_shapes=[pltpu.VMEM((tm, d), BF16)],
        compiler_params=_cparams(("parallel", "parallel", "arbitrary")),
        name="normproj_h" if emit_h else "normproj",
    )(x, nw.reshape(1, d), sc.reshape(b, 1, d), sh.reshape(b, 1, d), w_bf16)
    return res if emit_h else res[0]


GDN_CB = 1024
GDN_GB = 256


def _gdn_local_kernel(q_ref, qh_ref, k_ref, kh_ref, v_ref, vh_ref, sm_ref,
                      cwq_ref, cwk_ref, cwv_ref, alog_ref, dtb_ref,
                      w_ref, u0_ref, qk_ref, qg_ref, kd_ref, ge_ref, *, cb):
    h = pl.program_id(1)
    i = pl.program_id(2)
    C, G = GDN_CHUNK, GDN_GB
    first = (i == 0)

    def conv_act(main_ref, halo_ref, cw_ref, r0):
        if r0 == 0:
            prev = jnp.where(first, 0.0, halo_ref[0, 0])
        else:
            prev = main_ref[0, 0, r0 - 8:r0, :]
        ext = jnp.concatenate([prev, main_ref[0, 0, r0:r0 + G, :]], axis=0)
        w = cw_ref[0]
        y = w[0:1, :] * ext[5:5 + G, :]
        for j in range(1, GDN_CONV):
            y = y + w[j:j + 1, :] * ext[5 + j:5 + j + G, :]
        return _silu(y)

    rid = lax.broadcasted_iota(jnp.int32, (G, G), 0)
    cid = lax.broadcasted_iota(jnp.int32, (G, G), 1)
    same = (rid // C) == (cid // C)
    incl = jnp.logical_and(same, rid >= cid)
    strict = jnp.logical_and(same, rid > cid)
    eye = rid == cid
    is_last = cid == (rid // C) * C + (C - 1)
    joins = []
    bs = 1
    while bs < C:
        joins.append(jnp.logical_and(jnp.logical_and((rid // (2 * bs)) == (cid // (2 * bs)),
                                                     (rid & (2 * bs - 1)) >= bs),
                                     (cid & (2 * bs - 1)) < bs))
        bs *= 2
    tri = jnp.where(incl, 1.0, 0.0).astype(BF16)
    eye_f = jnp.where(eye, 1.0, 0.0).astype(F32)
    lane = lax.broadcasted_iota(jnp.int32, (G, LANES), 1)
    neg_a = -jnp.exp(alog_ref[...])
    dtb = dtb_ref[...]

    for grp in range(cb // G):
        r0 = grp * G
        q = conv_act(q_ref, qh_ref, cwq_ref, r0)
        k = conv_act(k_ref, kh_ref, cwk_ref, r0)
        v = conv_act(v_ref, vh_ref, cwv_ref, r0)
        q = q * lax.rsqrt(jnp.sum(q * q, axis=-1, keepdims=True) + EPS) * (HEAD_DIM ** -0.5)
        k = k * lax.rsqrt(jnp.sum(k * k, axis=-1, keepdims=True) + EPS)

        sm = sm_ref[0, 0, r0:r0 + G, :]
        g_all = neg_a * _softplus(sm + dtb)
        gc_all = _dot3(tri, g_all, exact_a=True)
        gc = jnp.sum(jnp.where(lane == h, gc_all, 0.0), axis=1, keepdims=True)
        beta = jnp.sum(jnp.where(lane == GDN_HEADS + h, _sigmoid(sm), 0.0), axis=1, keepdims=True)
        gc_row = jnp.sum(jnp.where(eye, gc, 0.0), axis=0, keepdims=True)
        gc_last = jnp.sum(jnp.where(is_last, gc_row, 0.0), axis=1, keepdims=True)
        decay = jnp.where(incl, jnp.exp(jnp.where(incl, gc - gc_row, 0.0)), 0.0)
        gamma = jnp.exp(gc)

        kk = _dot3_nt(k, k)
        lmat = jnp.where(strict, decay * kk, 0.0) * beta
        tinv = eye_f - jnp.where(joins[0], lmat, 0.0)
        for lvl in range(1, len(joins)):
            tinv = tinv - _bdot(_bdot(tinv, jnp.where(joins[lvl], lmat, 0.0)), tinv)
        resid = (eye_f - tinv) - _dot3(lmat, tinv)
        tinv = tinv + _bdot(tinv, resid)
        rhs = jnp.concatenate([(beta * gamma) * k, beta * v], axis=1)
        wu = _dot3(tinv, rhs)
        qk = decay * _bdot_nt(q, k)

        rows = slice(r0, r0 + G)
        w_ref[0, 0, rows, :] = wu[:, :HEAD_DIM].astype(BF16)
        u0_ref[0, 0, rows, :] = wu[:, HEAD_DIM:]
        qg_ref[0, 0, rows, :] = (gamma * q).astype(BF16)
        kd_ref[0, 0, rows, :] = (jnp.exp(gc_last - gc) * k).astype(BF16)
        ge_all = jnp.broadcast_to(jnp.exp(gc_last), (G, LANES))
        for c in range(G // C):
            cr = slice(c * C, (c + 1) * C)
            qk_ref[0, 0, r0 + c * C:r0 + (c + 1) * C, :] = qk[cr, cr].astype(BF16)
            ge_ref[0, 0, pl.ds((i * (cb // G) + grp) * (G // C) + c, 1), :] = ge_all[c * C:c * C + 1, :]


def _gdn_local(P, cw, alog_pad, dtb_pad):
    b, _, s, _ = P.shape
    H, CB, C = GDN_HEADS, min(GDN_CB, s), GDN_CHUNK
    n = s // C

    def main(j0):
        return pl.BlockSpec((1, 1, CB, LANES), lambda bi, h, i: (bi, j0 + h, i, 0))

    def halo(j0):
        return pl.BlockSpec((1, 1, 8, LANES),
                            lambda bi, h, i: (bi, j0 + h, jnp.maximum(i * (CB // 8) - 1, 0), 0))

    def cws(j0):
        return pl.BlockSpec((1, GDN_CONV, LANES), lambda bi, h, i: (j0 + h, 0, 0))

    row = pl.BlockSpec((1, LANES), lambda bi, h, i: (0, 0))
    hs = lambda width: pl.BlockSpec((1, 1, CB, width), lambda bi, h, i: (bi, h, i, 0))
    return pl.pallas_call(
        functools.partial(_gdn_local_kernel, cb=CB),
        grid=(b, H, s // CB),
        in_specs=[main(J_AQ), halo(J_AQ), main(J_AK), halo(J_AK), main(J_AV), halo(J_AV),
                  pl.BlockSpec((1, 1, CB, LANES), lambda bi, h, i: (bi, J_SMALL, i, 0)),
                  cws(0), cws(8), cws(16), row, row],
        out_specs=[hs(LANES), hs(LANES), hs(C), hs(LANES), hs(LANES),
                   pl.BlockSpec((1, 1, n, LANES), lambda bi, h, i: (bi, h, 0, 0))],
        out_shape=[jax.ShapeDtypeStruct((b, H, s, LANES), BF16),
                   jax.ShapeDtypeStruct((b, H, s, LANES), F32),
                   jax.ShapeDtypeStruct((b, H, s, C), BF16),
                   jax.ShapeDtypeStruct((b, H, s, LANES), BF16),
                   jax.ShapeDtypeStruct((b, H, s, LANES), BF16),
                   jax.ShapeDtypeStruct((b, H, n, LANES), F32)],
        compiler_params=_cparams(("parallel", "parallel", "arbitrary")),
        name="gdn_local",
    )(P, P, P, P, P, P, P, cw, cw, cw, alog_pad, dtb_pad)


GDN_HB = 8


def _gdn_scan_kernel(w_ref, u0_ref, qk_ref, qg_ref, kd_ref, ge_ref, z_ref, nw_ref, o_ref, s_scr,
                     *, sb):
    C = GDN_CHUNK

    @pl.when(pl.program_id(2) == 0)
    def _():
        s_scr[...] = jnp.zeros_like(s_scr)

    nw = nw_ref[...]

    def body(n, carry):
        r = pl.multiple_of(n * C, C)
        for hh in range(GDN_HB):
            st = s_scr[hh]
            stb = st.astype(BF16)
            u = u0_ref[0, hh, pl.ds(r, C), :] - jnp.dot(w_ref[0, hh, pl.ds(r, C), :], stb,
                                                       preferred_element_type=F32)
            ub = u.astype(BF16)
            o = (jnp.dot(qg_ref[0, hh, pl.ds(r, C), :], stb, preferred_element_type=F32)
                 + jnp.dot(qk_ref[0, hh, pl.ds(r, C), :], ub, preferred_element_type=F32))
            ge = ge_ref[0, hh, pl.ds(n, 1), :]
            s_scr[hh] = ge * st + lax.dot_general(kd_ref[0, hh, pl.ds(r, C), :], ub,
                                                  (((0,), (0,)), ((), ())),
                                                  preferred_element_type=F32)
            on = o * lax.rsqrt(jnp.mean(o * o, axis=-1, keepdims=True) + EPS) * nw
            z = z_ref[0, hh, pl.ds(r, C), :]
            o_ref[0, pl.ds(r, C), hh * LANES:(hh + 1) * LANES] = (on * _silu(z)).astype(BF16)
        return carry

    lax.fori_loop(0, sb // C, body, 0)


def _gdn_scan(w, u0, qk, qg, kd, ge, P, norm_w):
    b, H, s, _ = w.shape
    C, HB = GDN_CHUNK, GDN_HB
    sb = min(1024, s)
    hs = lambda width: pl.BlockSpec((1, HB, sb, width), lambda bi, hb, i: (bi, hb, i, 0))
    return pl.pallas_call(
        functools.partial(_gdn_scan_kernel, sb=sb),
        grid=(b, H // HB, s // sb),
        in_specs=[hs(LANES), hs(LANES), hs(C), hs(LANES), hs(LANES),
                  pl.BlockSpec((1, HB, sb // C, LANES), lambda bi, hb, i: (bi, hb, i, 0)),
                  pl.BlockSpec((1, HB, sb, LANES), lambda bi, hb, i: (bi, J_AZ // HB + hb, i, 0)),
                  pl.BlockSpec((1, LANES), lambda bi, hb, i: (0, 0))],
        out_specs=pl.BlockSpec((1, sb, HB * LANES), lambda bi, hb, i: (bi, i, hb)),
        out_shape=jax.ShapeDtypeStruct((b, s, H * LANES), BF16),
        scratch_shapes=[pltpu.VMEM((HB, HEAD_DIM, HEAD_DIM), F32)],
        compiler_params=_cparams(("parallel", "parallel", "arbitrary")),
        name="gdn_scan",
    )(w, u0, qk, qg, kd, ge, P, norm_w.reshape(1, LANES))


def _pad_lanes_row(v):
    return jnp.pad(v.astype(F32), (0, LANES - v.shape[0])).reshape(1, LANES)


def _gdn(P, conv_w, a_log, dt_bias, norm_w):
    cw = jnp.transpose(conv_w.reshape(GDN_CONV, 3 * GDN_HEADS, LANES), (1, 0, 2))
    w, u0, qk, qg, kd, ge = _gdn_local(P, cw, _pad_lanes_row(a_log), _pad_lanes_row(dt_bias))
    return _gdn_scan(w, u0, qk, qg, kd, ge, P, norm_w)


NSA_TQ = 128
NSA_TK_SEL = 512
NSA_HC = 1
R_KC, R_KS, R_VS, R_KW, R_VW = 16, 20, 22, 24, 26
ROPE_NB = 4


def _rope_kernel(x_ref, pos_ref, invf_ref, o_ref, cos_scr, sin_scr):
    j = pl.program_id(2)
    lane = lax.broadcasted_iota(jnp.int32, cos_scr.shape, 1)

    @pl.when(j == 0)
    def _():
        ang = pos_ref[0] * invf_ref[...]
        sn = jnp.sin(ang)
        cos_scr[...] = jnp.where(lane < ROPE_DIM, jnp.cos(ang), 1.0)
        sin_scr[...] = jnp.where(lane < ROPE_HALF, -sn, jnp.where(lane < ROPE_DIM, sn, 0.0))

    is_q = j < NSA_HEADS // ROPE_NB
    scale = jnp.where(is_q, HEAD_DIM ** -0.5, 1.0)
    for k in range(ROPE_NB):
        x = x_ref[0, k]
        swapped = jnp.where(lane < ROPE_HALF, pltpu.roll(x, LANES - ROPE_HALF, axis=1),
                            pltpu.roll(x, ROPE_HALF, axis=1))
        rot = x * cos_scr[...] + swapped * sin_scr[...]
        out = rot * scale if k < 2 else jnp.where(is_q, rot, x) * scale
        o_ref[0, k] = out.astype(BF16)


def _rope(P, pos_f32, invf):
    b, _, s, _ = P.shape
    tr = min(1024, s)
    nb = ROPE_NB
    return pl.pallas_call(
        _rope_kernel,
        grid=(b, s // tr, N_NSA_BLOCKS // nb),
        in_specs=[pl.BlockSpec((1, nb, tr, LANES), lambda bi, i, j: (bi, J_BQ // nb + j, i, 0)),
                  pl.BlockSpec((1, tr, 1), lambda bi, i, j: (bi, i, 0)),
                  pl.BlockSpec((1, LANES), lambda bi, i, j: (0, 0))],
        out_specs=pl.BlockSpec((1, nb, tr, LANES), lambda bi, i, j: (bi, j, i, 0)),
        out_shape=jax.ShapeDtypeStruct((b, N_NSA_BLOCKS, s, LANES), BF16),
        scratch_shapes=[pltpu.VMEM((tr, LANES), F32), pltpu.VMEM((tr, LANES), F32)],
        compiler_params=_cparams(("parallel", "parallel", "arbitrary")),
        name="nsa_rope",
    )(P, pos_f32, invf)


def _compress_kernel(r_ref, w1_ref, w2_ref, pos_ref, o_ref):
    r = r_ref[0, 0]
    nr = r.shape[0]
    half = CMP_STRIDE * HEAD_DIM
    a = jnp.dot(r, w1_ref[0, :half, :], preferred_element_type=F32)
    bm = jnp.dot(r, w1_ref[0, half:, :], preferred_element_type=F32)
    pos8 = jnp.broadcast_to(pos_ref[0], (8, CMP_LEN * HEAD_DIM)).astype(BF16)
    pb = jnp.dot(pos8, w1_ref[0], preferred_element_type=F32)[0:1, :]
    hid = a + pltpu.roll(bm, nr - 1, axis=0) + pb
    out = jnp.dot(_gelu(hid).astype(BF16), w2_ref[0], preferred_element_type=F32)
    row = lax.broadcasted_iota(jnp.int32, out.shape, 0)
    o_ref[0, 0] = jnp.where(row < nr - 1, out, 0.0).astype(BF16)


def _compress(rows, w1, w2, pos):
    b, _, nr, width = rows.shape
    return pl.pallas_call(
        _compress_kernel,
        grid=(b, 4),
        in_specs=[pl.BlockSpec((1, 1, nr, width), lambda bi, j: (bi, j, 0, 0)),
                  pl.BlockSpec((1, CMP_LEN * HEAD_DIM, CMP_HIDDEN), lambda bi, j: (j // 2, 0, 0)),
                  pl.BlockSpec((1, CMP_HIDDEN, HEAD_DIM), lambda bi, j: (j // 2, 0, 0)),
                  pl.BlockSpec((1, 1, CMP_LEN * HEAD_DIM), lambda bi, j: (j // 2, 0, 0))],
        out_specs=pl.BlockSpec((1, 1, nr, HEAD_DIM), lambda bi, j: (bi, j, 0, 0)),
        out_shape=jax.ShapeDtypeStruct((b, 4, nr, HEAD_DIM), BF16),
        compiler_params=_cparams(("parallel", "arbitrary")),
        name="nsa_compress",
    )(rows, w1, w2, pos)


def _nsa_kernel(q_ref, kc_ref, vc_ref, ks_ref, vs_ref, kw_ref, vw_ref, gate_ref, ovt_ref, e_ref,
                o_ref, m_scr, l_scr, acc_scr, os_scr, *, s_len):
    i = pl.program_id(2)
    TQ, HP = NSA_TQ, NSA_HPG
    R = HP * TQ
    nsb = s_len // SLC_LEN
    ncp = s_len // CMP_STRIDE
    n_sel = min(SLC_TOPK, nsb)
    s0 = i * TQ
    q2 = q_ref[0].reshape(R, HEAD_DIM)

    sc = _bdot_nt(q2, kc_ref[0, 0])
    tq_r = s0 + (lax.broadcasted_iota(jnp.int32, (R, ncp), 0) & (TQ - 1))
    ncol = lax.broadcasted_iota(jnp.int32, (R, ncp), 1)
    valid = jnp.logical_and(ncol * CMP_STRIDE + (CMP_LEN - 1) <= tq_r, ncol < ncp - 1)
    scm = jnp.where(valid, sc, NEG)
    e = jnp.where(valid, jnp.exp(scm - jnp.max(scm, axis=1, keepdims=True)), 0.0)
    p_c = e / jnp.maximum(jnp.sum(e, axis=1, keepdims=True), 1e-30)
    o_c = _bdot(p_c, vc_ref[0, 0])
    psum = jnp.sum(p_c.reshape(HP, TQ, ncp), axis=0)
    imp = _fdot_nt(ovt_ref[...], psum)
    blk = lax.broadcasted_iota(jnp.int32, (nsb, TQ), 0)
    tq_l = s0 + lax.broadcasted_iota(jnp.int32, (nsb, TQ), 1)
    cur = tq_l // SLC_LEN
    forced = jnp.logical_or(blk == 0, jnp.logical_or(blk == cur, blk == cur - 1))
    imp = jnp.where(forced, BIG, imp)
    imp = jnp.where(blk * SLC_LEN <= tq_l, imp, NEG)
    rank = jnp.zeros((nsb, TQ), jnp.int32)
    for j in range(nsb):
        rj = imp[j:j + 1, :]
        beats = jnp.logical_or(rj > imp, jnp.logical_and(rj == imp, blk > j))
        rank = rank + jnp.where(beats, 1, 0)
    sel_t = jnp.where(rank < n_sel, 1.0, 0.0).astype(F32)
    if nsb < LANES:
        sel_t = jnp.concatenate([sel_t, jnp.zeros((LANES - nsb, TQ), F32)], axis=0)
    sel = jnp.transpose(sel_t).astype(BF16)

    ones_blk = jnp.ones((TQ, HEAD_DIM), BF16)
    HC = NSA_HC

    def flash(k_ref, v_ref, tk, lo, hi, mask_fn):
        m_scr[...] = jnp.full(m_scr.shape, NEG, F32)
        l_scr[...] = jnp.zeros(l_scr.shape, F32)
        acc_scr[...] = jnp.zeros(acc_scr.shape, F32)
        qpos = s0 + lax.broadcasted_iota(jnp.int32, (TQ, tk), 0)
        kofs = lax.broadcasted_iota(jnp.int32, (TQ, tk), 1)
        ones_v = jnp.concatenate([ones_blk] * (tk // TQ), axis=0)

        def body(j, carry):
            r = pl.multiple_of(j * tk, tk)
            bias = jnp.where(mask_fn(r, r + kofs, qpos), 0.0, NEG)
            kb = k_ref[0, 0, pl.ds(r, tk), :]
            vb = jnp.concatenate([v_ref[0, 0, pl.ds(r, tk), :], ones_v], axis=1)
            for c in range(HP // HC):
                rows = slice(c * HC * TQ, (c + 1) * HC * TQ)
                s = _bdot_nt(q_ref[0, c * HC:(c + 1) * HC].reshape(HC * TQ, HEAD_DIM), kb)
                s = (s.reshape(HC, TQ, tk) + bias[None]).reshape(HC * TQ, tk)
                m_prev = m_scr[rows, :]
                m_new = jnp.maximum(m_prev, jnp.max(s, axis=1, keepdims=True))
                alpha = jnp.exp(m_prev - m_new)
                p = jnp.exp(s - jnp.concatenate([m_new] * (tk // LANES), axis=1)).astype(BF16)
                pv = jnp.dot(p, vb, preferred_element_type=F32)
                l_scr[rows, :] = alpha * l_scr[rows, :] + pv[:, HEAD_DIM:]
                acc_scr[rows, :] = alpha * acc_scr[rows, :] + pv[:, :HEAD_DIM]
                m_scr[rows, :] = m_new
            return carry

        lax.fori_loop(lo, hi, body, 0)

    tks = min(NSA_TK_SEL, s_len)

    def sel_mask(r, kpos, qpos):
        chosen = jnp.dot(sel, e_ref[:, pl.ds(r, tks)], preferred_element_type=F32)
        return jnp.logical_and(chosen > 0.5, kpos <= qpos)

    flash(ks_ref, vs_ref, tks, 0, (s0 + TQ - 1) // tks + 1, sel_mask)
    os_scr[...] = acc_scr[...] / l_scr[...]

    tkw = min(WIN + TQ, s_len)
    w0 = pl.multiple_of(jnp.clip(s0 - WIN, 0, s_len - tkw), TQ)
    qpos_w = s0 + lax.broadcasted_iota(jnp.int32, (TQ, tkw), 0)
    kpos_w = w0 + lax.broadcasted_iota(jnp.int32, (TQ, tkw), 1)
    bias_w = jnp.where(jnp.logical_and(kpos_w <= qpos_w, kpos_w > qpos_w - WIN), 0.0, NEG)
    kb_w = kw_ref[0, 0, pl.ds(w0, tkw), :]
    vb_w = jnp.concatenate([vw_ref[0, 0, pl.ds(w0, tkw), :],
                            jnp.concatenate([ones_blk] * (tkw // TQ), axis=0)], axis=1)
    for c in range(HP // HC):
        rows = slice(c * HC * TQ, (c + 1) * HC * TQ)
        s = _bdot_nt(q_ref[0, c * HC:(c + 1) * HC].reshape(HC * TQ, HEAD_DIM), kb_w)
        s = (s.reshape(HC, TQ, tkw) + bias_w[None]).reshape(HC * TQ, tkw)
        p = jnp.exp(s - jnp.max(s, axis=1, keepdims=True)).astype(BF16)
        pv = jnp.dot(p, vb_w, preferred_element_type=F32)
        acc_scr[rows, :] = pv[:, :HEAD_DIM] / pv[:, HEAD_DIM:]

    gates = _sigmoid(gate_ref[0, 0])
    for hh in range(HP):
        rows = slice(hh * TQ, (hh + 1) * TQ)
        out = (gates[:, 3 * hh:3 * hh + 1] * o_c[rows, :]
               + gates[:, 3 * hh + 1:3 * hh + 2] * os_scr[rows, :]
               + gates[:, 3 * hh + 2:3 * hh + 3] * acc_scr[rows, :])
        o_ref[0, :, hh * HEAD_DIM:(hh + 1) * HEAD_DIM] = out.astype(BF16)


def _nsa_attend(rp, cmp_kv, P, ovt, expand):
    b, _, s, _ = rp.shape
    TQ, HP, G = NSA_TQ, NSA_HPG, NSA_GROUPS
    ncp = cmp_kv.shape[2]
    R = HP * TQ
    full = lambda j0: pl.BlockSpec((1, 1, s, HEAD_DIM), lambda bi, g, i: (bi, j0 + g, 0, 0))
    cmp_spec = lambda j0: pl.BlockSpec((1, 1, ncp, HEAD_DIM), lambda bi, g, i: (bi, j0 + g, 0, 0))
    return pl.pallas_call(
        functools.partial(_nsa_kernel, s_len=s),
        grid=(b, G, s // TQ),
        in_specs=[pl.BlockSpec((1, HP, TQ, HEAD_DIM), lambda bi, g, i: (bi, g, i, 0)),
                  cmp_spec(0), cmp_spec(2),
                  full(R_KS), full(R_VS), full(R_KW), full(R_VW),
                  pl.BlockSpec((1, 1, TQ, LANES), lambda bi, g, i: (bi, J_GATE + g, i, 0)),
                  pl.BlockSpec(ovt.shape, lambda bi, g, i: (0, 0)),
                  pl.BlockSpec(expand.shape, lambda bi, g, i: (0, 0))],
        out_specs=pl.BlockSpec((1, TQ, HP * HEAD_DIM), lambda bi, g, i: (bi, i, g)),
        out_shape=jax.ShapeDtypeStruct((b, s, NSA_HEADS * HEAD_DIM), BF16),
        scratch_shapes=[pltpu.VMEM((R, LANES), F32), pltpu.VMEM((R, LANES), F32),
                        pltpu.VMEM((R, HEAD_DIM), F32), pltpu.VMEM((R, HEAD_DIM), F32)],
        compiler_params=_cparams(("parallel", "parallel", "arbitrary")),
        name="nsa_attend",
    )(rp, cmp_kv, cmp_kv, rp, rp, rp, rp, P, ovt, expand)


def _nsa(P, positions, cmp_pos_k, cmp_w1_k, cmp_w2_k, cmp_pos_v, cmp_w1_v, cmp_w2_v):
    b, _, s, _ = P.shape
    lanes = np.arange(LANES)
    invf = np.where(lanes < ROPE_DIM, ROPE_THETA ** (-(lanes % ROPE_HALF) / ROPE_HALF), 0.0)
    rp = _rope(P, positions.astype(F32).reshape(b, s, 1), jnp.asarray(invf, F32).reshape(1, LANES))

    ncp = s // CMP_STRIDE
    rows = rp[:, R_KC:R_KC + 4].reshape(b, 4, ncp, CMP_STRIDE * HEAD_DIM)
    w1 = jnp.stack([cmp_w1_k, cmp_w1_v]).astype(BF16)
    w2 = jnp.stack([cmp_w2_k, cmp_w2_v]).astype(BF16)
    pos = jnp.stack([cmp_pos_k, cmp_pos_v]).reshape(2, 1, CMP_LEN * HEAD_DIM)
    cmp_kv = _compress(rows, w1, w2, pos)

    nsb = s // SLC_LEN
    cmp_start = np.arange(ncp) * CMP_STRIDE
    slc_start = np.arange(nsb) * SLC_LEN
    ov = ((cmp_start[:, None] < slc_start[None, :] + SLC_LEN)
          & (cmp_start[:, None] + CMP_LEN > slc_start[None, :])
          & (np.arange(ncp)[:, None] < ncp - 1))
    ovt = jnp.asarray(ov.T.astype(np.float32))
    expand = (np.arange(LANES)[:, None] == (np.arange(s)[None, :] // SLC_LEN))
    expand = jnp.asarray(expand.astype(np.float32), BF16)
    return _nsa_attend(rp, cmp_kv, P, ovt, expand)


MERGE_TN = 512


def _merge_kernel(oa_ref, ob_ref, wg_ref, wn_ref, ma_ref, mb_ref, y_ref):
    ya = jnp.dot(oa_ref[0], wg_ref[...], preferred_element_type=F32)
    yb = jnp.dot(ob_ref[0], wn_ref[...], preferred_element_type=F32)
    for k in range(MERGE_TN // LANES):
        cols = slice(k * LANES, (k + 1) * LANES)
        y = _sigmoid(ma_ref[0, k]) * ya[:, cols] + _sigmoid(mb_ref[0, k]) * yb[:, cols]
        y_ref[0, :, cols] = y.astype(BF16)


def _merge(o_a, o_b, wg, wn, P):
    b, s, da = o_a.shape
    db = o_b.shape[2]
    d = wg.shape[1]
    tm, tn = min(1024, s), MERGE_TN
    nk = tn // LANES
    return pl.pallas_call(
        _merge_kernel,
        grid=(b, s // tm, d // tn),
        in_specs=[pl.BlockSpec((1, tm, da), lambda bi, i, j: (bi, i, 0)),
                  pl.BlockSpec((1, tm, db), lambda bi, i, j: (bi, i, 0)),
                  pl.BlockSpec((da, tn), lambda bi, i, j: (0, j)),
                  pl.BlockSpec((db, tn), lambda bi, i, j: (0, j)),
                  pl.BlockSpec((1, nk, tm, LANES), lambda bi, i, j: (bi, J_MA // nk + j, i, 0)),
                  pl.BlockSpec((1, nk, tm, LANES), lambda bi, i, j: (bi, J_MB // nk + j, i, 0))],
        out_specs=pl.BlockSpec((1, tm, tn), lambda bi, i, j: (bi, i, j)),
        out_shape=jax.ShapeDtypeStruct((b, s, d), BF16),
        compiler_params=_cparams(("parallel", "parallel", "arbitrary")),
        name="merge",
    )(o_a, o_b, wg, wn, P, P)


def _outproj_kernel(y_ref, w_ref, x_ref, g_ref, o_ref):
    o_ref[0] = x_ref[0] + g_ref[0] * jnp.dot(y_ref[0], w_ref[...], preferred_element_type=F32)


def _outproj(y, w, x, g):
    b, s, d = x.shape
    tm, tn = min(1024, s), 512
    return pl.pallas_call(
        _outproj_kernel,
        grid=(b, s // tm, d // tn),
        in_specs=[pl.BlockSpec((1, tm, d), lambda bi, i, j: (bi, i, 0)),
                  pl.BlockSpec((d, tn), lambda bi, i, j: (0, j)),
                  pl.BlockSpec((1, tm, tn), lambda bi, i, j: (bi, i, j)),
                  pl.BlockSpec((1, 1, tn), lambda bi, i, j: (bi, 0, j))],
        out_specs=pl.BlockSpec((1, tm, tn), lambda bi, i, j: (bi, i, j)),
        out_shape=jax.ShapeDtypeStruct((b, s, d), F32),
        compiler_params=_cparams(("parallel", "parallel", "arbitrary")),
        name="outproj",
    )(y, w, x, g.reshape(b, 1, d))


PEER_TB = 1024
PEER_TG = 128
PEER_AHEAD = 26
PEER_SLOTS = 32
PEER_SEL = PEER_HEADS * PEER_TOPK
MIN_F32 = -3.0e38


def _topk_rows(vals, payload=None):
    nrow = vals.shape[0]
    rid = lax.broadcasted_iota(jnp.int32, vals.shape, 0)
    out_v, out_i = [], []
    for _ in range(PEER_TOPK):
        m = jnp.max(vals, axis=0, keepdims=True)
        idx = jnp.min(jnp.where(vals == m, rid, nrow), axis=0, keepdims=True)
        hit = rid == idx
        out_v.append(m)
        if payload is None:
            out_i.append(idx)
        else:
            out_i.append(jnp.sum(jnp.where(hit, payload, 0), axis=0, keepdims=True))
        vals = jnp.where(hit, MIN_F32, vals)
    return jnp.concatenate(out_v, axis=0), jnp.concatenate(out_i, axis=0)


def _peer_topk_kernel(qh_ref, k1_ref, k2_ref, eidx_ref, gw_ref):
    def head(h, carry):
        s1 = _fdot_nt(k1_ref[h], qh_ref[0, 2 * h])
        s2 = _fdot_nt(k2_ref[h], qh_ref[0, 2 * h + 1])
        v1, i1 = _topk_rows(s1)
        v2, i2 = _topk_rows(s2)
        keep = [PEER_TOPK // (a + 1) for a in range(PEER_TOPK)]
        npad = (-sum(keep)) % 8
        cand = jnp.concatenate([v1[a:a + 1, :] + v2[0:keep[a], :] for a in range(PEER_TOPK)]
                               + [jnp.full((npad, v1.shape[1]), MIN_F32, F32)], axis=0)
        cidx = jnp.concatenate([i1[a:a + 1, :] * PEER_NKEYS + i2[0:keep[a], :] for a in range(PEER_TOPK)]
                               + [jnp.zeros((npad, v1.shape[1]), jnp.int32)], axis=0)
        sc, eidx = _topk_rows(cand, cidx)
        ex = jnp.exp(sc - sc[0:1, :])
        eidx_ref[h] = eidx
        gw_ref[h] = ex / jnp.sum(ex, axis=0, keepdims=True)
        return carry

    lax.fori_loop(0, PEER_HEADS, head, 0)


def _peer_topk(qh, keys1, keys2):
    b, nb, s, _ = qh.shape
    tb = min(PEER_TB, s)
    nt = s // tb
    kspec = pl.BlockSpec(keys1.shape, lambda bi, i: (0, 0, 0))
    ospec = pl.BlockSpec((PEER_HEADS, PEER_TOPK, tb), lambda bi, i: (0, 0, bi * nt + i))
    return pl.pallas_call(
        _peer_topk_kernel,
        grid=(b, nt),
        in_specs=[pl.BlockSpec((1, nb, tb, LANES), lambda bi, i: (bi, 0, i, 0)), kspec, kspec],
        out_specs=[ospec, ospec],
        out_shape=[jax.ShapeDtypeStruct((PEER_HEADS, PEER_TOPK, b * s), jnp.int32),
                   jax.ShapeDtypeStruct((PEER_HEADS, PEER_TOPK, b * s), F32)],
        compiler_params=_cparams(("parallel", "parallel")),
        name="peer_topk",
    )(qh, keys1, keys2)


PEER_BANKS = 4


def _peer_gather_kernel(idx_ref, idxs_ref, gw_ref, x_ref, uv_hbm, o_ref, b0, b1, b2, b3, a_scr, sem, *, tg):
    NS, D, NB = PEER_SEL, PEER_AHEAD, PEER_BANKS
    banks = (b0, b1, b2, b3)
    Q = b0.shape[0]
    RPE = uv_hbm.shape[1]
    SUBW = RPE // 2
    half = SUBW * LANES
    RT = 8

    def place(g, k, off):
        kk = k + off
        return banks[kk % NB], kk % NB, (g + kk // NB) & (Q - 1)

    def wait_rows(g, k):
        bank, kb, q = place(g, k, 0)
        pltpu.make_async_copy(uv_hbm.at[pl.ds(0, NS)], bank.at[q], sem.at[kb, q]).wait()

    rid = lax.broadcasted_iota(jnp.int32, (NS, NS), 0)
    cid = lax.broadcasted_iota(jnp.int32, (NS, NS), 1)
    eye = rid == cid
    ones = jnp.ones((LANES, LANES), BF16)

    def lane_sum_rep(m):
        hi, lo = _split_bf16(m)
        return (jnp.dot(hi, ones, preferred_element_type=F32)
                + jnp.dot(lo, ones, preferred_element_type=F32))

    def low(w):
        return pltpu.bitcast(w << 16, F32)

    def high(w):
        return pltpu.bitcast(w & jnp.uint32(0xFFFF0000), F32)

    def phase(g, k, do_dot, do_prev):
        t = g * NB + k
        if do_dot:
            wait_rows(g, k)
            xrow = x_ref[pl.ds(t, 1), :]
            bank_d, _, q_d = place(g, k, 0)
            acc = [jnp.zeros((RT, LANES), F32) for _ in range(NS // RT)]
            bank_i, k_i, q_i = place(g, k, D)
            base_i = t * NS
        if do_prev:
            bank_s, _, q_s = place(g, k, -1)
            par_s = (t - 1) & 1
            out_lo, out_hi = [], []
        for sg in range(SUBW):
            if do_dot:
                x_lo = jnp.broadcast_to(xrow[:, sg * LANES:(sg + 1) * LANES], (RT, LANES))
                x_hi = jnp.broadcast_to(xrow[:, half + sg * LANES:half + (sg + 1) * LANES], (RT, LANES))
            if do_prev:
                p_lo = jnp.zeros((RT, LANES), F32)
                p_hi = jnp.zeros((RT, LANES), F32)
            for rt in range(NS // RT):
                rows = slice(rt * RT, (rt + 1) * RT)
                if do_dot:
                    j = sg * (NS // RT) + rt
                    pltpu.make_async_copy(uv_hbm.at[idxs_ref[base_i + j]],
                                          bank_i.at[q_i, :, j, :], sem.at[k_i, q_i]).start(priority=j % 2)
                    w = bank_d[q_d, sg, rows, :]
                    acc[rt] = acc[rt] + low(w) * x_lo + high(w) * x_hi
                if do_prev:
                    w = bank_s[q_s, SUBW + sg, rows, :]
                    c = a_scr[par_s, rows, :]
                    p_lo = p_lo + low(w) * c
                    p_hi = p_hi + high(w) * c
            if do_prev:
                out_lo.append(jnp.sum(p_lo, axis=0, keepdims=True))
                out_hi.append(jnp.sum(p_hi, axis=0, keepdims=True))
        if do_dot:
            gw_rep = lane_sum_rep(jnp.where(eye, gw_ref[pl.ds(t, 1), :], 0.0))
            a_scr[t & 1] = gw_rep * _gelu(lane_sum_rep(jnp.concatenate(acc, axis=0)))
        if do_prev:
            o_ref[pl.ds(t - 1, 1), :] = jnp.concatenate(out_lo + out_hi, axis=1)

    step = pl.program_id(0)
    last = pl.num_programs(0) - 1

    @pl.when(step == 0)
    def _():
        for k in range(NB):
            def ramp(g, carry, k=k):
                bank, kb, q = place(g, k, 0)

                def body(j, c):
                    pltpu.make_async_copy(uv_hbm.at[idx_ref[(g * NB + k) * NS + j]],
                                          bank.at[q, :, j, :], sem.at[kb, q]).start()
                    return c
                return lax.fori_loop(0, NS, body, carry)
            lax.fori_loop(0, (D - k + NB - 1) // NB, ramp, 0)

    phase(0, 0, True, False)
    for k in range(1, NB):
        phase(0, k, True, True)

    def group(g, carry):
        for k in range(NB):
            phase(g, k, True, True)
        return carry
    lax.fori_loop(1, tg // NB, group, 0)
    phase(tg // NB, 0, False, True)

    @pl.when(step == last)
    def _():
        for r in range(D):
            wait_rows(tg // NB, r)


def _peer_gather(eidx, gw, h2r, uvc):
    t = eidx.shape[0]
    tg = min(PEER_TG, t)
    d = h2r.shape[1]
    rpe = uvc.shape[1]
    nq = PEER_SLOTS // PEER_BANKS
    assert PEER_BANKS == 4 and PEER_AHEAD % PEER_BANKS in (1, 2) and PEER_AHEAD + 2 <= PEER_SLOTS
    assert tg % PEER_BANKS == 0 and tg > PEER_AHEAD and t % tg == 0
    flat = eidx.reshape(-1)
    shifted = jnp.roll(flat, -PEER_AHEAD * PEER_SEL)
    bank = pltpu.VMEM((nq, rpe, PEER_SEL, LANES), jnp.uint32)
    return pl.pallas_call(
        functools.partial(_peer_gather_kernel, tg=tg),
        grid=(t // tg,),
        in_specs=[pl.BlockSpec((tg * PEER_SEL,), lambda i: (0,), memory_space=pltpu.SMEM),
                  pl.BlockSpec((tg * PEER_SEL,), lambda i: (i,), memory_space=pltpu.SMEM),
                  pl.BlockSpec((tg, PEER_SEL), lambda i: (i, 0)),
                  pl.BlockSpec((tg, d), lambda i: (i, 0)),
                  pl.BlockSpec(memory_space=pl.ANY)],
        out_specs=pl.BlockSpec((tg, d), lambda i: (i, 0)),
        out_shape=jax.ShapeDtypeStruct((t, d), F32),
        scratch_shapes=[bank, bank, bank, bank,
                        pltpu.VMEM((2, PEER_SEL, LANES), F32),
                        pltpu.SemaphoreType.DMA((PEER_BANKS, nq))],
        compiler_params=_cparams(("arbitrary",)),
        name="peer_gather",
    )(flat, shifted, gw, h2r, uvc)


PACK_EB = 256


def _pack_kernel(u_ref, v_ref, o_ref):
    half = u_ref.shape[1] // 2
    for tab, ref in enumerate((u_ref, v_ref)):
        for sg in range(half // LANES):
            lo = ref[:, sg * LANES:(sg + 1) * LANES].astype(BF16).astype(F32)
            hi = ref[:, half + sg * LANES:half + (sg + 1) * LANES].astype(BF16).astype(F32)
            word = (pltpu.bitcast(lo, jnp.uint32) >> 16) | (pltpu.bitcast(hi, jnp.uint32) & jnp.uint32(0xFFFF0000))
            o_ref[:, tab * (half // LANES) + sg, :] = word


def _pack_tables(u, v):
    e, d = u.shape
    eb = min(PACK_EB, e)
    return pl.pallas_call(
        _pack_kernel,
        grid=(e // eb,),
        in_specs=[pl.BlockSpec((eb, d), lambda i: (i, 0)), pl.BlockSpec((eb, d), lambda i: (i, 0))],
        out_specs=pl.BlockSpec((eb, d // LANES, LANES), lambda i: (i, 0, 0)),
        out_shape=jax.ShapeDtypeStruct((e, d // LANES, LANES), jnp.uint32),
        compiler_params=_cparams(("parallel",)),
        name="peer_pack",
    )(u, v)


def _peer(x1, norm_w, sc, sh, wq, keys1, keys2, u, v):
    b, s, d = x1.shape
    qh, h2 = _normproj(x1, norm_w, sc, sh, wq.astype(BF16), tm=1024, tn=512, emit_h=True)
    eidx, gw = _peer_topk(qh, keys1, keys2)
    eidx = jnp.transpose(eidx, (2, 0, 1)).reshape(b * s, PEER_SEL)
    gw = jnp.transpose(gw, (2, 0, 1)).reshape(b * s, PEER_SEL)
    uvc = _pack_tables(u, v)
    out = _peer_gather(eidx, gw, h2.reshape(b * s, d), uvc)
    return out.reshape(b, s, d)


def _final_kernel(x_ref, p_ref, g_ref, w_ref, o_ref):
    x = x_ref[0] + g_ref[0] * p_ref[0]
    o_ref[0] = x * lax.rsqrt(jnp.mean(x * x, axis=-1, keepdims=True) + EPS) * w_ref[...]


def _final(x1, peer, g2, wf):
    b, s, d = x1.shape
    tm = min(512, s)
    blk = pl.BlockSpec((1, tm, d), lambda bi, i: (bi, i, 0))
    return pl.pallas_call(
        _final_kernel,
        grid=(b, s // tm),
        in_specs=[blk, blk, pl.BlockSpec((1, 1, d), lambda bi, i: (bi, 0, 0)),
                  pl.BlockSpec((1, d), lambda bi, i: (0, 0))],
        out_specs=blk,
        out_shape=jax.ShapeDtypeStruct((b, s, d), F32),
        compiler_params=_cparams(("parallel", "parallel")),
        name="final_norm",
    )(x1, peer, g2.reshape(b, 1, d), wf.reshape(1, d))


def _pad_rows(a, mult=8):
    pad = (-a.shape[0]) % mult
    return jnp.pad(a, ((0, pad), (0, 0)))


def _permute_w_in(w):
    o_aa, o_bq, o_bg, o_ma, o_end = 4096, 4112, 7696, 7744, 11840
    d = w.shape[0]
    z = lambda n: jnp.zeros((d, n), w.dtype)
    hg = 3 * NSA_HPG
    cols = [w[:, 0:o_aa], w[:, o_bq:o_bg], w[:, o_ma:o_end],
            w[:, o_aa:o_bq], z(LANES - 16),
            w[:, o_bg:o_bg + hg], z(LANES - hg),
            w[:, o_bg + hg:o_ma], z(LANES - hg),
            z(LANES)]
    return jnp.concatenate(cols, axis=1).astype(BF16)


def kernel(x, c, positions, ada_w, ada_b, norm1_w, norm2_w, w_in, gdn_conv_w, gdn_A_log, gdn_dt_bias, gdn_norm_w, cmp_pos_k, cmp_w1_k, cmp_w2_k, cmp_pos_v, cmp_w1_v, cmp_w2_v, w_branch_gdn, w_branch_nsa, w_out, peer_wq, peer_keys1, peer_keys2, peer_u, peer_v, final_norm_w):
    b, s, d = x.shape
    l = 0
    mod = _ada(_pad_rows(c), ada_w[l], ada_b[l])[:b]
    sh1, sc1, g1, sh2, sc2, g2 = jnp.split(mod, 6, axis=-1)
    P = _normproj(x, norm1_w[l], sc1, sh1, _permute_w_in(w_in[l]), tm=1024, tn=1024, emit_h=False)
    o_a = _gdn(P, gdn_conv_w[l], gdn_A_log[l], gdn_dt_bias[l], gdn_norm_w[l])
    o_b = _nsa(P, positions, cmp_pos_k[l], cmp_w1_k[l], cmp_w2_k[l], cmp_pos_v[l], cmp_w1_v[l], cmp_w2_v[l])
    y = _merge(o_a, o_b, w_branch_gdn[l].astype(BF16), w_branch_nsa[l].astype(BF16), P)
    x1 = _outproj(y, w_out[l].astype(BF16), x, g1)
    peer = _peer(x1, norm2_w[l], sc2, sh2, peer_wq[l], peer_keys1[l], peer_keys2[l], peer_u[l], peer_v[l])
    return _final(x1, peer, g2, final_norm_w)
```
